```python
import math
import jax, jax.numpy as jnp
from jax import lax
import numpy as np

D_MODEL = 1024
BATCH = 8
SEQ = 4096
DEPTH = 2

GRID_W = 64
CTX_LEN = 256
N_HEADS = 8
Q_LORA = 256
KV_LORA = 128
QK_NOPE = 64
QK_ROPE = 32
V_HEAD = 64
ATT_WIDTH = N_HEADS * V_HEAD
ATT_SCALE = (QK_NOPE + QK_ROPE) ** -0.5
ROPE_BASE = 10000.0
Q_BLOCK = 128
FOURIER_WIDTH = 256
POOL_WINDOWS = (2, 4, 8, 16)
POOL_GROUP = 64
POOL_WIDTH = POOL_GROUP * len(POOL_WINDOWS)
N_BRANCH = 3
IN_WIDTH = Q_LORA + KV_LORA + QK_ROPE + FOURIER_WIDTH + POOL_WIDTH + N_BRANCH * D_MODEL
SPLITS = (Q_LORA, Q_LORA + KV_LORA, Q_LORA + KV_LORA + QK_ROPE,
          Q_LORA + KV_LORA + QK_ROPE + FOURIER_WIDTH,
          Q_LORA + KV_LORA + QK_ROPE + FOURIER_WIDTH + POOL_WIDTH)
PEER_HEADS = 8
N_KEYS = 128
N_EXPERTS = N_KEYS * N_KEYS
PEER_QDIM = 256
PEER_TOPK = 16
TOKEN_BLOCK = 128
EPS = 1e-6

kernel_name = "hybrid_mla_fnet_pool_peer_dit"


def _rmsnorm(x, g):
    xf = x.astype(jnp.float32)
    y = xf * lax.rsqrt(jnp.mean(xf * xf, axis=-1, keepdims=True) + EPS)
    return (y * g.astype(jnp.float32)).astype(x.dtype)


def _modulation(cvec, w_mod, b_mod):
    m = jax.nn.silu(cvec) @ w_mod + b_mod
    return jnp.split(m[:, None, :], 6, axis=-1)


def _modulate(x, g, shift, scale):
    return _rmsnorm(x, g) * (1 + scale) + shift


def _axial_angles(L):
    rows = L // GRID_W
    r, cl = jnp.meshgrid(jnp.arange(rows, dtype=jnp.float32),
                         jnp.arange(GRID_W, dtype=jnp.float32), indexing="ij")
    half = QK_ROPE // 2
    inv_freq = ROPE_BASE ** (-jnp.arange(0, half, 2, dtype=jnp.float32) / half)
    ang_r = r.reshape(L, 1, 1) * inv_freq
    ang_c = cl.reshape(L, 1, 1) * inv_freq
    return ang_r, ang_c


def _rotate_half(x, ang):
    x1, x2 = jnp.split(x.astype(jnp.float32), 2, axis=-1)
    cos, sin = jnp.cos(ang), jnp.sin(ang)
    return jnp.concatenate([x1 * cos - x2 * sin, x1 * sin + x2 * cos], axis=-1)


def _rope_2d(x, ang):
    ang_r, ang_c = ang
    xr, xc = jnp.split(x, 2, axis=-1)
    return jnp.concatenate([_rotate_half(xr, ang_r), _rotate_half(xc, ang_c)], axis=-1).astype(x.dtype)


def _mla_q(cq, lp, ang):
    B, L, _ = cq.shape
    q = (_rmsnorm(cq, lp["q_norm_g"]) @ lp["w_uq"]).reshape(B, L, N_HEADS, QK_NOPE + QK_ROPE)
    q_nope, q_rope = q[..., :QK_NOPE], q[..., QK_NOPE:]
    if ang is not None:
        q_rope = _rope_2d(q_rope, ang)
    return jnp.concatenate([q_nope, q_rope], axis=-1)


def _mla_kv(ckv, kr, lp, ang):
    B, L, _ = ckv.shape
    kv = (_rmsnorm(ckv, lp["kv_norm_g"]) @ lp["w_ukv"]).reshape(B, L, N_HEADS, QK_NOPE + V_HEAD)
    k_nope, v = kv[..., :QK_NOPE], kv[..., QK_NOPE:]
    k_rope = kr[:, :, None, :]
    if ang is not None:
        k_rope = _rope_2d(k_rope, ang)
    k = jnp.concatenate([k_nope, jnp.broadcast_to(k_rope, (B, L, N_HEADS, QK_ROPE))], axis=-1)
    return k, v


def _attend(q, k, v):
    s = jnp.einsum("bqhd,bkhd->bhqk", q, k, preferred_element_type=jnp.float32) * ATT_SCALE
    p = jax.nn.softmax(s, axis=-1).astype(v.dtype)
    return jnp.einsum("bhqk,bkhd->bqhd", p, v)


def _latent_attention(q, k, v, kv_ctx):
    k_ctx, v_ctx = kv_ctx
    k_all = jnp.concatenate([k_ctx, k], axis=1)
    v_all = jnp.concatenate([v_ctx, v], axis=1)
    B, L, H, Dq = q.shape
    nb = L // Q_BLOCK
    qb = jnp.moveaxis(q.reshape(B, nb, Q_BLOCK, H, Dq), 1, 0)
    o = lax.map(lambda qq: _attend(qq, k_all, v_all), qb)
    return jnp.moveaxis(o, 0, 1).reshape(B, L, ATT_WIDTH)


def _fourier_mix(z):
    zf = z.astype(jnp.float32)
    return jnp.real(jnp.fft.fft2(zf, axes=(1, 2), norm="ortho")).astype(z.dtype)


def _pool_mix(z, w_grp, pool_scale):
    B, L, _ = z.shape
    zf = z.astype(jnp.float32)
    cs = jnp.concatenate([jnp.zeros((B, 1, POOL_WIDTH), jnp.float32), jnp.cumsum(zf, axis=1)], axis=1)
    t = jnp.arange(L)
    outs = []
    for gi, w in enumerate(POOL_WINDOWS):
        lo = jnp.clip(t - w // 2, 0, L)
        hi = jnp.clip(t - w // 2 + w, 0, L)
        seg = cs[:, :, gi * POOL_GROUP:(gi + 1) * POOL_GROUP]
        cnt = (hi - lo).astype(jnp.float32)[None, :, None]
        outs.append((seg[:, hi] - seg[:, lo]) / cnt - zf[:, :, gi * POOL_GROUP:(gi + 1) * POOL_GROUP])
    pooled = jnp.concatenate(outs, axis=-1).astype(z.dtype).reshape(B, L, len(POOL_WINDOWS), POOL_GROUP)
    y = jnp.einsum("blgc,gcd->blgd", pooled, w_grp).reshape(B, L, POOL_WIDTH)
    return y * pool_scale


def _token_mix(h, lp, ang, kv_ctx):
    B, L, _ = h.shape
    z = h @ lp["w_in"]
    cq, ckv, kr, zf, zp, zg = jnp.split(z, SPLITS, axis=-1)
    q = _mla_q(cq, lp, ang)
    k, v = _mla_kv(ckv, kr, lp, ang)
    if kv_ctx is None:
        att = _attend(q, k, v).reshape(B, L, ATT_WIDTH)
    else:
        att = _latent_attention(q, k, v, kv_ctx)
    y_a = att @ lp["w_oa"]
    y_b = _fourier_mix(zf) @ lp["w_ob"]
    y_c = _pool_mix(zp, lp["w_grp"], lp["pool_scale"]) @ lp["w_oc"]
    g = jax.nn.sigmoid((zg + lp["b_gate"]).astype(jnp.float32)).astype(h.dtype)
    g_a, g_b, g_c = jnp.split(g, N_BRANCH, axis=-1)
    return (g_a * y_a + g_b * y_b + g_c * y_c) @ lp["w_out"], (k, v)


def _ctx_keys_values(hc, lp):
    z = hc @ lp["w_in"][:, Q_LORA:Q_LORA + KV_LORA + QK_ROPE]
    ckv, kr = z[..., :KV_LORA], z[..., KV_LORA:]
    return _mla_kv(ckv, kr, lp, None)


def _peer(h, w_pq, peer_keys, peer_down, peer_up):
    B, L, D = h.shape
    q = (h @ w_pq).reshape(B, L, PEER_HEADS, 2, PEER_QDIM // 2)
    s = jnp.einsum("blhpd,hpkd->blhpk", q, peer_keys, preferred_element_type=jnp.float32)
    sv, si = lax.top_k(s, PEER_TOPK)
    cand = (sv[..., 0, :, None] + sv[..., 1, None, :]).reshape(B, L, PEER_HEADS, PEER_TOPK * PEER_TOPK)
    cidx = (si[..., 0, :, None] * N_KEYS + si[..., 1, None, :]).reshape(B, L, PEER_HEADS, PEER_TOPK * PEER_TOPK)
    top_s, pos = lax.top_k(cand, PEER_TOPK)
    eidx = jnp.take_along_axis(cidx, pos, axis=-1)
    gate = jax.nn.softmax(top_s, axis=-1)
    T = B * L
    nb = T // TOKEN_BLOCK
    hb = h.reshape(nb, TOKEN_BLOCK, D)
    eb = eidx.reshape(nb, TOKEN_BLOCK, PEER_HEADS * PEER_TOPK)
    gb = gate.reshape(nb, TOKEN_BLOCK, PEER_HEADS * PEER_TOPK)

    def one(args):
        hh, ee, gg = args
        a = jnp.einsum("tkd,td->tk", peer_down[ee], hh, preferred_element_type=jnp.float32)
        act = (gg * jax.nn.gelu(a)).astype(hh.dtype)
        return jnp.einsum("tk,tkd->td", act, peer_up[ee])

    return lax.map(one, (hb, eb, gb)).reshape(B, L, D)


def setup_inputs(seed: int = 0) -> dict:
    key = jax.random.key(seed)
    ks = jax.random.split(key, 25)
    D = D_MODEL

    def n(k, shape, s):
        return jax.random.normal(k, shape, jnp.float32) * s

    return {
        "x": n(ks[0], (BATCH, SEQ, D), 1.0),
        "c": n(ks[1], (BATCH, D), 1.0),
        "ctx": n(ks[2], (BATCH, CTX_LEN, D), 1.0),
        "c_ctx": n(ks[3], (D,), 1.0),
        "w_mod": n(ks[4], (DEPTH, D, 6 * D), 0.5 * D ** -0.5),
        "b_mod": n(ks[5], (DEPTH, 6 * D), 0.01),
        "norm1_g": 1.0 + n(ks[6], (DEPTH, D), 0.02),
        "norm2_g": 1.0 + n(ks[7], (DEPTH, D), 0.02),
        "w_in": n(ks[8], (DEPTH, D, IN_WIDTH), D ** -0.5),
        "b_gate": n(ks[9], (DEPTH, N_BRANCH * D), 0.01),
        "q_norm_g": 1.0 + n(ks[10], (DEPTH, Q_LORA), 0.02),
        "w_uq": n(ks[11], (DEPTH, Q_LORA, N_HEADS * (QK_NOPE + QK_ROPE)), Q_LORA ** -0.5),
        "kv_norm_g": 1.0 + n(ks[12], (DEPTH, KV_LORA), 0.02),
        "w_ukv": n(ks[13], (DEPTH, KV_LORA, N_HEADS * (QK_NOPE + V_HEAD)), KV_LORA ** -0.5),
        "w_oa": n(ks[14], (DEPTH, ATT_WIDTH, D), ATT_WIDTH ** -0.5),
        "w_ob": n(ks[15], (DEPTH, FOURIER_WIDTH, D), FOURIER_WIDTH ** -0.5),
        "w_grp": n(ks[16], (DEPTH, len(POOL_WINDOWS), POOL_GROUP, POOL_GROUP), POOL_GROUP ** -0.5),
        "pool_scale": 1.0 + n(ks[17], (DEPTH, POOL_WIDTH), 0.02),
        "w_oc": n(ks[18], (DEPTH, POOL_WIDTH, D), POOL_WIDTH ** -0.5),
        "w_out": n(ks[19], (DEPTH, D, D), D ** -0.5),
        "w_pq": n(ks[20], (DEPTH, D, PEER_HEADS * PEER_QDIM), D ** -0.5),
        "peer_keys": n(ks[21], (DEPTH, PEER_HEADS, 2, N_KEYS, PEER_QDIM // 2), (PEER_QDIM // 2) ** -0.5),
        "peer_down": n(ks[22], (DEPTH, N_EXPERTS, D), D ** -0.5),
        "peer_up": n(ks[23], (DEPTH, N_EXPERTS, D), PEER_HEADS ** -0.5),
        "final_g": 1.0 + n(ks[24], (D,), 0.02),
    }


def reference(x, c, ctx, c_ctx, w_mod, b_mod, norm1_g, norm2_g, w_in, b_gate, q_norm_g, w_uq,
              kv_norm_g, w_ukv, w_oa, w_ob, w_grp, pool_scale, w_oc, w_out, w_pq, peer_keys,
              peer_down, peer_up, final_g):
    L = x.shape[1]
    ang = _axial_angles(L)
    for l in range(DEPTH):
        last = l == DEPTH - 1
        lp = {"w_in": w_in[l], "b_gate": b_gate[l], "q_norm_g": q_norm_g[l], "w_uq": w_uq[l],
              "kv_norm_g": kv_norm_g[l], "w_ukv": w_ukv[l], "w_oa": w_oa[l], "w_ob": w_ob[l],
              "w_grp": w_grp[l], "pool_scale": pool_scale[l], "w_oc": w_oc[l], "w_out": w_out[l]}
        m_x = _modulation(c, w_mod[l], b_mod[l])
        m_c = _modulation(c_ctx[None, :], w_mod[l], b_mod[l])
        h = _modulate(x, norm1_g[l], m_x[0], m_x[1])
        hc = _modulate(ctx, norm1_g[l], m_c[0], m_c[1])
        if last:
            kv_c = _ctx_keys_values(hc, lp)
        else:
            mix_c, kv_c = _token_mix(hc, lp, None, None)
        mix_x, _ = _token_mix(h, lp, ang, kv_c)
        x = x + m_x[2] * mix_x
        h2 = _modulate(x, norm2_g[l], m_x[3], m_x[4])
        x = x + m_x[5] * _peer(h2, w_pq[l], peer_keys[l], peer_down[l], peer_up[l])
        if not last:
            ctx = ctx + m_c[2] * mix_c
            hc2 = _modulate(ctx, norm2_g[l], m_c[3], m_c[4])
            ctx = ctx + m_c[5] * _peer(hc2, w_pq[l], peer_keys[l], peer_down[l], peer_up[l])
    return _rmsnorm(x, final_g)
```

```python
import functools
import math

import jax
import jax.numpy as jnp
from jax import lax
from jax.experimental import pallas as pl
from jax.experimental.pallas import tpu as pltpu

F32 = jnp.float32
BF16 = jnp.bfloat16
I32 = jnp.int32

GRID_W = 64
N_HEADS = 8
Q_LORA = 256
KV_LORA = 128
QK_NOPE = 64
QK_ROPE = 32
V_HEAD = 64
ATT_WIDTH = N_HEADS * V_HEAD
ATT_SCALE = (QK_NOPE + QK_ROPE) ** -0.5
ROPE_BASE = 10000.0
FOURIER_WIDTH = 256
POOL_WINDOWS = (2, 4, 8, 16)
POOL_GROUP = 64
POOL_WIDTH = POOL_GROUP * len(POOL_WINDOWS)
N_BRANCH = 3
PEER_HEADS = 8
N_KEYS = 128
PEER_QDIM = 256
PEER_HALF = PEER_QDIM // 2
PEER_TOPK = 16
PEER_SLOTS = PEER_HEADS * PEER_TOPK
EPS = 1e-6

LANES = 128
SUBLANES = 8
HEAD_PAD = 128
POOL_PAD = 8
MOD_ROWS = 16
VMEM_LIMIT = 48 * 1024 * 1024

TOK_TILE = 256
ATT_Q_TILE = 256
FOUR_TILE = 512
ROUTE_TILE = 128
PEER_TOK = 8

C_CQ = 0
C_CKV = C_CQ + Q_LORA
C_ZF = C_CKV + KV_LORA
C_ZP = C_ZF + FOURIER_WIDTH
C_ZG = C_ZP + POOL_WIDTH
C_KR = None


def _params(sem=None):
    return pltpu.CompilerParams(dimension_semantics=sem, vmem_limit_bytes=VMEM_LIMIT)


def _rms(x):
    return x * lax.rsqrt(jnp.mean(x * x, axis=-1, keepdims=True) + EPS)


def _dot(a, b):
    return jnp.dot(a, b, preferred_element_type=F32)


def _dot_nt(a, b):
    return lax.dot_general(a, b, (((1,), (1,)), ((), ())), preferred_element_type=F32)


def _mod_kernel(c_ref, w_ref, b_ref, o_ref):
    c = c_ref[...]
    s = c * jax.nn.sigmoid(c)
    o_ref[0] = jnp.dot(s, w_ref[0], preferred_element_type=F32,
                       precision=lax.Precision.HIGHEST) + b_ref[0]


def _modulation(cvec, w_mod, b_mod):
    depth, d, n = w_mod.shape
    tn = 1536
    return pl.pallas_call(
        _mod_kernel,
        out_shape=jax.ShapeDtypeStruct((depth, MOD_ROWS, n), F32),
        grid=(depth, n // tn),
        in_specs=[pl.BlockSpec((MOD_ROWS, d), lambda l, j: (0, 0)),
                  pl.BlockSpec((1, d, tn), lambda l, j: (l, 0, j)),
                  pl.BlockSpec((1, 1, tn), lambda l, j: (l, 0, j))],
        out_specs=pl.BlockSpec((1, MOD_ROWS, tn), lambda l, j: (l, 0, j)),
        compiler_params=_params(("arbitrary", "arbitrary")),
        name="modulation",
    )(cvec, w_mod, b_mod.reshape(depth, 1, n))


def _in_kernel(d, full, x_ref, m_ref, n1_ref, wall_ref, qg_ref, wq_ref, wqp_ref, kvg_ref,
               wk_ref, wv_ref, fc_ref, bg_ref, cq_ref, sq_ref, ck_ref, sk_ref, *outs):
    x = x_ref[...]
    m = m_ref[0]
    h = _rms(x) * n1_ref[...] * (1.0 + m[:, d:2 * d]) + m[:, 0:d]
    hb = h.astype(BF16)
    c_kr = C_ZG + N_BRANCH * d
    if full:
        z = _dot(hb, wall_ref[...])
        k_ref, v_ref, q_ref, ab_ref, zp_ref, g_ref = outs
    else:
        z = None
        k_ref, v_ref = outs
    def col(lo, hi):
        if full:
            return z[:, lo:hi]
        return _dot(hb, wall_ref[:, lo:hi])

    ckv = col(C_CKV, C_ZF)
    ckvn = (_rms(ckv) * kvg_ref[...]).astype(BF16)
    kf = _dot(ckvn, wk_ref[...])
    kr = col(c_kr, c_kr + LANES) * ck_ref[...] + col(c_kr + LANES, c_kr + 2 * LANES) * sk_ref[...]
    for hh in range(N_HEADS):
        sl = slice(hh * HEAD_PAD, (hh + 1) * HEAD_PAD)
        k_ref[:, sl] = (kf[:, sl] + kr).astype(BF16)
    v_ref[...] = _dot(ckvn, wv_ref[...]).astype(BF16)
    if not full:
        return
    cq = z[:, C_CQ:C_CKV]
    cqn = (_rms(cq) * qg_ref[...]).astype(BF16)
    qf = _dot(cqn, wq_ref[...])
    qr = _dot(cqn, wqp_ref[...])
    cosq = cq_ref[...]
    sinq = sq_ref[...]
    for hh in range(N_HEADS):
        sl = slice(hh * HEAD_PAD, (hh + 1) * HEAD_PAD)
        q_ref[:, sl] = (qf[:, sl] * cosq + qr[:, sl] * sinq).astype(BF16)
    ab_ref[...] = _dot(z[:, C_ZF:C_ZP].astype(BF16), fc_ref[...]).astype(BF16)
    zp_ref[...] = z[:, C_ZP:C_ZG]
    g_ref[...] = jax.nn.sigmoid(z[:, C_ZG:c_kr] + bg_ref[...]).astype(BF16)


def _in_proj(x2d, mod, mod_row, seq_len, lw, tabs, full):
    t, d = x2d.shape
    tm = min(TOK_TILE, seq_len)
    tiles_per_seq = seq_len // tm
    wall = lw["wall"]
    nw = wall.shape[1]
    cq, sq, ck, sk = tabs

    def const(shape):
        return pl.BlockSpec(shape, lambda i: (0,) * len(shape))

    def pos(i):
        return (i % tiles_per_seq, 0)

    in_specs = [
        pl.BlockSpec((tm, d), lambda i: (i, 0)),
        pl.BlockSpec((1, 1, mod.shape[-1]), lambda i: (mod_row(i, tiles_per_seq), 0, 0)),
        const((1, d)), const((d, nw)), const((1, Q_LORA)),
        const(lw["wq"].shape), const(lw["wqp"].shape), const((1, KV_LORA)),
        const(lw["wk"].shape), const(lw["wv"].shape), const(lw["fc"].shape),
        const((1, N_BRANCH * d)),
        pl.BlockSpec((tm, LANES), pos), pl.BlockSpec((tm, LANES), pos),
        pl.BlockSpec((tm, LANES), pos), pl.BlockSpec((tm, LANES), pos),
    ]
    kw = N_HEADS * HEAD_PAD
    out_shape = [jax.ShapeDtypeStruct((t, kw), BF16), jax.ShapeDtypeStruct((t, ATT_WIDTH), BF16)]
    out_specs = [pl.BlockSpec((tm, kw), lambda i: (i, 0)),
                 pl.BlockSpec((tm, ATT_WIDTH), lambda i: (i, 0))]
    if full:
        out_shape += [jax.ShapeDtypeStruct((t, kw), BF16),
                      jax.ShapeDtypeStruct((t, 2 * FOURIER_WIDTH), BF16),
                      jax.ShapeDtypeStruct((t, POOL_WIDTH), F32),
                      jax.ShapeDtypeStruct((t, N_BRANCH * d), BF16)]
        out_specs += [pl.BlockSpec((tm, kw), lambda i: (i, 0)),
                      pl.BlockSpec((tm, 2 * FOURIER_WIDTH), lambda i: (i, 0)),
                      pl.BlockSpec((tm, POOL_WIDTH), lambda i: (i, 0)),
                      pl.BlockSpec((tm, N_BRANCH * d), lambda i: (i, 0))]
    return pl.pallas_call(
        functools.partial(_in_kernel, d, full),
        out_shape=out_shape,
        grid=(t // tm,),
        in_specs=in_specs,
        out_specs=out_specs,
        compiler_params=_params(("arbitrary",)),
        name="in_proj" if full else "ctx_kv_proj",
    )(x2d, mod, lw["n1"], wall, lw["qg"], lw["wq"], lw["wqp"], lw["kvg"], lw["wk"], lw["wv"],
      lw["fc"], lw["bg"], cq, sq, ck, sk)


def _attn_kernel(has_ctx, q_ref, k_ref, v_ref, *rest):
    if has_ctx:
        kc_ref, vc_ref, o_ref = rest
    else:
        (o_ref,) = rest
    outs = []
    for hh in range(2):
        sl = slice(hh * HEAD_PAD, (hh + 1) * HEAD_PAD)
        q = q_ref[:, sl]
        s = _dot_nt(q, k_ref[:, sl])
        mx = jnp.max(s, axis=-1, keepdims=True)
        if has_ctx:
            sc = _dot_nt(q, kc_ref[:, sl])
            mx = jnp.maximum(mx, jnp.max(sc, axis=-1, keepdims=True))
        p = jnp.exp(s - mx)
        den = jnp.sum(p, axis=-1, keepdims=True)
        o = _dot(p.astype(BF16), v_ref[...])
        if has_ctx:
            pc = jnp.exp(sc - mx)
            den = den + jnp.sum(pc, axis=-1, keepdims=True)
            o = o + _dot(pc.astype(BF16), vc_ref[...])
        outs.append(o / den)
    lane = lax.broadcasted_iota(I32, outs[0].shape, 1)
    o_ref[...] = jnp.where(lane < V_HEAD, outs[0], outs[1]).astype(BF16)


def _attention(q, k, v, batch, seq_len, ctx_kv):
    t = q.shape[0]
    tq = min(ATT_Q_TILE, seq_len)
    nq = seq_len // tq
    pair_w = 2 * HEAD_PAD
    in_specs = [pl.BlockSpec((tq, pair_w), lambda b, j, i: (b * nq + i, j)),
                pl.BlockSpec((seq_len, pair_w), lambda b, j, i: (b, j)),
                pl.BlockSpec((seq_len, 2 * V_HEAD), lambda b, j, i: (b, j))]
    args = [q, k, v]
    if ctx_kv is not None:
        kc, vc = ctx_kv
        lc = kc.shape[0] // batch
        in_specs += [pl.BlockSpec((lc, pair_w), lambda b, j, i: (b, j)),
                     pl.BlockSpec((lc, 2 * V_HEAD), lambda b, j, i: (b, j))]
        args += [kc, vc]
    return pl.pallas_call(
        functools.partial(_attn_kernel, ctx_kv is not None),
        out_shape=jax.ShapeDtypeStruct((t, ATT_WIDTH), BF16),
        grid=(batch, N_HEADS // 2, nq),
        in_specs=in_specs,
        out_specs=pl.BlockSpec((tq, 2 * V_HEAD), lambda b, j, i: (b * nq + i, j)),
        compiler_params=_params(("arbitrary", "arbitrary", "arbitrary")),
        name="attention" if ctx_kv is not None else "ctx_attention",
    )(*args)


def _fourier_kernel(norm, c_ref, s_ref, ab_ref, o_ref):
    a = ab_ref[:, 0:FOURIER_WIDTH]
    b = ab_ref[:, FOURIER_WIDTH:2 * FOURIER_WIDTH]
    o = _dot(c_ref[...], a) - _dot(s_ref[...], b)
    o_ref[...] = (o * norm).astype(BF16)


def _fourier(ab, batch, seq_len, dft):
    t = ab.shape[0]
    tm = min(FOUR_TILE, seq_len)
    nt = seq_len // tm
    cl, sl = dft
    norm = 1.0 / math.sqrt(seq_len * FOURIER_WIDTH)
    return pl.pallas_call(
        functools.partial(_fourier_kernel, norm),
        out_shape=jax.ShapeDtypeStruct((t, FOURIER_WIDTH), BF16),
        grid=(nt, batch),
        in_specs=[pl.BlockSpec((tm, seq_len), lambda i, b: (i, 0)),
                  pl.BlockSpec((tm, seq_len), lambda i, b: (i, 0)),
                  pl.BlockSpec((seq_len, 2 * FOURIER_WIDTH), lambda i, b: (b, 0))],
        out_specs=pl.BlockSpec((tm, FOURIER_WIDTH), lambda i, b: (b * nt + i, 0)),
        compiler_params=_params(("arbitrary", "arbitrary")),
        name="fourier",
    )(cl, sl, ab)


def _pool_kernel(seq_len, z_ref, wg_ref, ps_ref, o_ref, pad_ref, s_ref):
    n = seq_len
    p = n + 2 * POOL_PAD
    z = z_ref[...]
    zeros = jnp.zeros((POOL_PAD, POOL_WIDTH), F32)
    pad_ref[0:POOL_PAD, :] = zeros
    pad_ref[POOL_PAD + n:p, :] = zeros
    pad_ref[POOL_PAD:POOL_PAD + n, :] = z
    s_ref[0:p - 1, :] = pad_ref[0:p - 1, :] + pad_ref[1:p, :]
    w2 = s_ref[POOL_PAD - 1:POOL_PAD - 1 + n, :]
    pad_ref[0:p - 3, :] = s_ref[0:p - 3, :] + s_ref[2:p - 1, :]
    w4 = pad_ref[POOL_PAD - 2:POOL_PAD - 2 + n, :]
    s_ref[0:p - 7, :] = pad_ref[0:p - 7, :] + pad_ref[4:p - 3, :]
    w8 = s_ref[POOL_PAD - 4:POOL_PAD - 4 + n, :]
    pad_ref[0:p - 15, :] = s_ref[0:p - 15, :] + s_ref[8:p - 7, :]
    w16 = pad_ref[0:n, :]
    pos = lax.broadcasted_iota(I32, (n, POOL_WIDTH), 0)
    grp = lax.broadcasted_iota(I32, (n, POOL_WIDTH), 1) // POOL_GROUP
    win = jnp.where(grp == 0, w2, jnp.where(grp == 1, w4, jnp.where(grp == 2, w8, w16)))
    half = jnp.where(grp == 0, 1, jnp.where(grp == 1, 2, jnp.where(grp == 2, 4, 8)))
    lo = jnp.maximum(pos - half, 0)
    hi = jnp.minimum(pos + half, n)
    cnt = (hi - lo).astype(F32)
    pooled = win / cnt - z
    y = _dot(pooled.astype(BF16), wg_ref[...])
    o_ref[...] = (y * ps_ref[...]).astype(BF16)


def _pool(zp, batch, seq_len, wg_bd, pool_scale):
    t = zp.shape[0]
    return pl.pallas_call(
        functools.partial(_pool_kernel, seq_len),
        out_shape=jax.ShapeDtypeStruct((t, POOL_WIDTH), BF16),
        grid=(batch,),
        in_specs=[pl.BlockSpec((seq_len, POOL_WIDTH), lambda b: (b, 0)),
                  pl.BlockSpec((POOL_WIDTH, POOL_WIDTH), lambda b: (0, 0)),
                  pl.BlockSpec((1, POOL_WIDTH), lambda b: (0, 0))],
        out_specs=pl.BlockSpec((seq_len, POOL_WIDTH), lambda b: (b, 0)),
        scratch_shapes=[pltpu.VMEM((seq_len + 2 * POOL_PAD, POOL_WIDTH), F32),
                        pltpu.VMEM((seq_len + 2 * POOL_PAD, POOL_WIDTH), F32)],
        compiler_params=_params(("arbitrary",)),
        name="pool",
    )(zp, wg_bd, pool_scale)


def _out_kernel(d, x_ref, att_ref, four_ref, pool_ref, g_ref, m_ref, woa_ref, wob_ref, woc_ref,
                wout_ref, n2_ref, wpq_ref, x1_ref, h2_ref, pq_ref):
    m = m_ref[0]
    ya = _dot(att_ref[...], woa_ref[...])
    yb = _dot(four_ref[...], wob_ref[...])
    yc = _dot(pool_ref[...], woc_ref[...])
    mixp = (g_ref[:, 0:d].astype(F32) * ya + g_ref[:, d:2 * d].astype(F32) * yb
            + g_ref[:, 2 * d:3 * d].astype(F32) * yc)
    mix = _dot(mixp.astype(BF16), wout_ref[...])
    x1 = x_ref[...] + m[:, 2 * d:3 * d] * mix
    x1_ref[...] = x1
    h2 = _rms(x1) * n2_ref[...] * (1.0 + m[:, 4 * d:5 * d]) + m[:, 3 * d:4 * d]
    h2_ref[...] = h2
    pq = _dot(h2.astype(BF16), wpq_ref[...])
    for hp in range(2 * PEER_HEADS):
        pq_ref[hp] = pq[:, hp * PEER_HALF:(hp + 1) * PEER_HALF].astype(BF16)


def _out_proj(x2d, att, four, pool, g, mod, mod_row, seq_len, lw):
    t, d = x2d.shape
    tm = min(TOK_TILE, seq_len)
    tiles_per_seq = seq_len // tm

    def const(shape):
        return pl.BlockSpec(shape, lambda i: (0,) * len(shape))

    def row(w):
        return pl.BlockSpec((tm, w), lambda i: (i, 0))

    nhp = 2 * PEER_HEADS
    return pl.pallas_call(
        functools.partial(_out_kernel, d),
        out_shape=[jax.ShapeDtypeStruct((t, d), F32), jax.ShapeDtypeStruct((t, d), F32),
                   jax.ShapeDtypeStruct((nhp, t, PEER_HALF), BF16)],
        grid=(t // tm,),
        in_specs=[row(d), row(ATT_WIDTH), row(FOURIER_WIDTH), row(POOL_WIDTH), row(N_BRANCH * d),
                  pl.BlockSpec((1, 1, mod.shape[-1]), lambda i: (mod_row(i, tiles_per_seq), 0, 0)),
                  const(lw["woa"].shape), const(lw["wob"].shape), const(lw["woc"].shape),
                  const(lw["wout"].shape), const((1, d)), const(lw["wpq"].shape)],
        out_specs=[row(d), row(d), pl.BlockSpec((nhp, tm, PEER_HALF), lambda i: (0, i, 0))],
        compiler_params=_params(("arbitrary",)),
        name="out_proj",
    )(x2d, att, four, pool, g, mod, lw["woa"], lw["wob"], lw["woc"], lw["wout"], lw["n2"],
      lw["wpq"])


def _top_rounds(s, iota, n, emit):
    neg = jnp.float32(-jnp.inf)
    for r in range(PEER_TOPK):
        mx = jnp.max(s, axis=0, keepdims=True)
        idx = jnp.min(jnp.where(s == mx, iota, n), axis=0, keepdims=True)
        hit = iota == idx
        emit(r, mx, idx, hit)
        s = jnp.where(hit, neg, s)


def _route_kernel(pq_ref, keys_ref, eidx_ref, gate_ref, sv_ref, si_ref, cand_ref, cidx_ref,
                  ts_ref):
    tm = pq_ref.shape[1]
    iota_k = lax.broadcasted_iota(I32, (N_KEYS, tm), 0)
    iota_c = lax.broadcasted_iota(I32, (PEER_TOPK * PEER_TOPK, tm), 0)

    def head(hd, carry):
        for p in range(2):
            s = _dot_nt(keys_ref[2 * hd + p], pq_ref[2 * hd + p])

            def emit(r, mx, idx, hit, p=p):
                sv_ref[p, r:r + 1, :] = mx
                si_ref[p, r:r + 1, :] = idx

            _top_rounds(s, iota_k, N_KEYS, emit)
        for i in range(PEER_TOPK):
            rows = slice(i * PEER_TOPK, (i + 1) * PEER_TOPK)
            cand_ref[rows, :] = sv_ref[0, i:i + 1, :] + sv_ref[1]
            cidx_ref[rows, :] = si_ref[0, i:i + 1, :] * N_KEYS + si_ref[1]
        cidx = cidx_ref[...]
        base = pl.multiple_of(hd * PEER_TOPK, PEER_TOPK)

        def emit2(r, mx, idx, hit):
            ts_ref[r:r + 1, :] = mx
            eidx_ref[pl.ds(base + r, 1), :] = jnp.sum(jnp.where(hit, cidx, 0), axis=0,
                                                      keepdims=True)

        _top_rounds(cand_ref[...], iota_c, PEER_TOPK * PEER_TOPK, emit2)
        ts = ts_ref[...]
        ex = jnp.exp(ts - ts[0:1, :])
        gate_ref[pl.ds(base, PEER_TOPK), :] = ex / jnp.sum(ex, axis=0, keepdims=True)
        return carry

    lax.fori_loop(0, PEER_HEADS, head, 0)


def _route(pq, keys):
    nhp, t, _ = pq.shape
    tm = ROUTE_TILE
    return pl.pallas_call(
        _route_kernel,
        out_shape=[jax.ShapeDtypeStruct((PEER_SLOTS, t), I32),
                   jax.ShapeDtypeStruct((PEER_SLOTS, t), F32)],
        grid=(t // tm,),
        in_specs=[pl.BlockSpec((nhp, tm, PEER_HALF), lambda i: (0, i, 0)),
                  pl.BlockSpec((nhp, N_KEYS, PEER_HALF), lambda i: (0, 0, 0))],
        out_specs=[pl.BlockSpec((PEER_SLOTS, tm), lambda i: (0, i)),
                   pl.BlockSpec((PEER_SLOTS, tm), lambda i: (0, i))],
        scratch_shapes=[pltpu.VMEM((2, PEER_TOPK, tm), F32), pltpu.VMEM((2, PEER_TOPK, tm), I32),
                        pltpu.VMEM((PEER_TOPK * PEER_TOPK, tm), F32),
                        pltpu.VMEM((PEER_TOPK * PEER_TOPK, tm), I32),
                        pltpu.VMEM((PEER_TOPK, tm), F32)],
        compiler_params=_params(("arbitrary",)),
        name="peer_route",
    )(pq, keys)


def _peer_kernel(final, idx_ref, idxn_ref, gate_ref, h_ref, x_ref, g2_ref, fg_ref, down_ref,
                 up_ref, o_ref, dbuf, ubuf, act_ref, sem):
    i = pl.program_id(0)
    n = pl.num_programs(0)
    slot = i % 2
    rows = PEER_TOK * PEER_SLOTS

    def row_copy(table_ref, buf, which, e, sl, r):
        return pltpu.make_async_copy(table_ref.at[e], buf.at[sl, r], sem.at[which, sl])

    def issue(ids, sl):
        for t in range(PEER_TOK):
            def body(k, carry, t=t):
                e = ids[t, k]
                r = t * PEER_SLOTS + k
                row_copy(down_ref, dbuf, 0, e, sl, r).start()
                row_copy(up_ref, ubuf, 1, e, sl, r).start()
                return carry
            lax.fori_loop(0, PEER_SLOTS, body, 0)

    @pl.when(i == 0)
    def _():
        issue(idx_ref, slot)

    @pl.when(i + 1 < n)
    def _():
        issue(idxn_ref, 1 - slot)

    pltpu.make_async_copy(down_ref.at[pl.ds(0, rows)], dbuf.at[slot], sem.at[0, slot]).wait()
    pltpu.make_async_copy(up_ref.at[pl.ds(0, rows)], ubuf.at[slot], sem.at[1, slot]).wait()

    g2 = g2_ref[0, 0]
    for t in range(PEER_TOK):
        r0 = t * PEER_SLOTS
        dn = dbuf[slot, r0:r0 + PEER_SLOTS]
        hv = h_ref[t]
        part = jnp.sum(dn * hv[None], axis=1)
        a = jnp.sum(part, axis=1, keepdims=True)
        act = gate_ref[0][:, t:t + 1] * jax.nn.gelu(a)
        act_ref[...] = jnp.broadcast_to(act, (PEER_SLOTS, LANES))
        acc = jnp.zeros((SUBLANES, LANES), F32)
        for k in range(PEER_SLOTS):
            acc = acc + act_ref[k:k + 1, :] * ubuf[slot, r0 + k]
        xr = x_ref[t] + g2 * acc
        if final:
            ms = jnp.sum(jnp.sum(xr * xr, axis=1, keepdims=True), axis=0, keepdims=True)
            xr = xr * lax.rsqrt(ms / (SUBLANES * LANES) + EPS) * fg_ref[...]
        o_ref[t] = xr


def _peer(eidx, gate3, h3, x3, g2, mod_row, seq_len, fg, down3, up3, final):
    t = h3.shape[0]
    tb = PEER_TOK
    nb = t // tb
    steps_per_seq = seq_len // tb
    rows = tb * PEER_SLOTS
    tile = (SUBLANES, LANES)
    return pl.pallas_call(
        functools.partial(_peer_kernel, final),
        out_shape=jax.ShapeDtypeStruct((t,) + tile, F32),
        grid=(nb,),
        in_specs=[pl.BlockSpec((tb, PEER_SLOTS), lambda i: (i, 0), memory_space=pltpu.SMEM),
                  pl.BlockSpec((tb, PEER_SLOTS), lambda i: (jnp.minimum(i + 1, nb - 1), 0),
                               memory_space=pltpu.SMEM),
                  pl.BlockSpec((1, PEER_SLOTS, tb), lambda i: (i, 0, 0)),
                  pl.BlockSpec((tb,) + tile, lambda i: (i, 0, 0)),
                  pl.BlockSpec((tb,) + tile, lambda i: (i, 0, 0)),
                  pl.BlockSpec((1, 1) + tile, lambda i: (mod_row(i, steps_per_seq), 0, 0, 0)),
                  pl.BlockSpec(tile, lambda i: (0, 0)),
                  pl.BlockSpec(memory_space=pl.ANY),
                  pl.BlockSpec(memory_space=pl.ANY)],
        out_specs=pl.BlockSpec((tb,) + tile, lambda i: (i, 0, 0)),
        scratch_shapes=[pltpu.VMEM((2, rows) + tile, F32), pltpu.VMEM((2, rows) + tile, F32),
                        pltpu.VMEM((PEER_SLOTS, LANES), F32),
                        pltpu.SemaphoreType.DMA((2, 2))],
        compiler_params=_params(("arbitrary",)),
        name="peer_experts",
    )(eidx, eidx, gate3, h3, x3, g2, fg, down3, up3)


def _rope_perm(w):
    q = QK_ROPE // 4
    a1, a2, b1, b2 = (w[..., j * q:(j + 1) * q] for j in range(4))
    return jnp.concatenate([-a2, a1, -b2, b1], axis=-1)


def _prep_layer(l, d, w_in, b_gate, q_norm_g, w_uq, kv_norm_g, w_ukv, w_oa, w_ob, w_grp,
                pool_scale, w_oc, w_out, w_pq, norm1_g, norm2_g):
    wi = w_in[l]
    s0 = Q_LORA
    s1 = s0 + KV_LORA
    s2 = s1 + QK_ROPE
    s3 = s2 + FOURIER_WIDTH
    s4 = s3 + POOL_WIDTH
    w_kr = wi[:, s1:s2]
    zl = jnp.zeros((d, QK_NOPE), F32)
    zr = jnp.zeros((d, HEAD_PAD - QK_NOPE - QK_ROPE), F32)
    wall = jnp.concatenate([wi[:, 0:s1], wi[:, s2:], zl, w_kr, zr, zl, _rope_perm(w_kr), zr],
                           axis=1).astype(BF16)
    wq = w_uq[l].reshape(Q_LORA, N_HEADS, QK_NOPE + QK_ROPE)
    qpad = jnp.zeros((Q_LORA, N_HEADS, HEAD_PAD - QK_NOPE - QK_ROPE), F32)
    wq_full = jnp.concatenate([wq, qpad], axis=-1).reshape(Q_LORA, N_HEADS * HEAD_PAD)
    wq_perm = jnp.concatenate([jnp.zeros((Q_LORA, N_HEADS, QK_NOPE), F32),
                               _rope_perm(wq[..., QK_NOPE:]), qpad],
                              axis=-1).reshape(Q_LORA, N_HEADS * HEAD_PAD)
    wkv = w_ukv[l].reshape(KV_LORA, N_HEADS, QK_NOPE + V_HEAD)
    wk = jnp.concatenate([wkv[..., :QK_NOPE],
                          jnp.zeros((KV_LORA, N_HEADS, HEAD_PAD - QK_NOPE), F32)],
                         axis=-1).reshape(KV_LORA, N_HEADS * HEAD_PAD)
    wv = wkv[..., QK_NOPE:].reshape(KV_LORA, ATT_WIDTH)
    cidx = jnp.arange(FOURIER_WIDTH, dtype=I32)
    ang = (2.0 * math.pi / FOURIER_WIDTH) * ((cidx[:, None] * cidx[None, :]) % FOURIER_WIDTH
                                             ).astype(F32)
    fc = jnp.concatenate([jnp.cos(ang), jnp.sin(ang)], axis=1).astype(BF16)
    ng = len(POOL_WINDOWS)
    wg = jnp.zeros((ng, POOL_GROUP, ng, POOL_GROUP), F32)
    for gi in range(ng):
        wg = wg.at[gi, :, gi, :].set(w_grp[l, gi])
    return {
        "wall": wall, "n1": norm1_g[l][None, :], "n2": norm2_g[l][None, :],
        "qg": q_norm_g[l][None, :], "kvg": kv_norm_g[l][None, :],
        "wq": wq_full.astype(BF16), "wqp": wq_perm.astype(BF16),
        "wk": wk.astype(BF16), "wv": wv.astype(BF16), "fc": fc,
        "bg": b_gate[l][None, :],
        "wg": wg.reshape(POOL_WIDTH, POOL_WIDTH).astype(BF16),
        "ps": pool_scale[l][None, :],
        "woa": w_oa[l].astype(BF16), "wob": w_ob[l].astype(BF16), "woc": w_oc[l].astype(BF16),
        "wout": w_out[l].astype(BF16), "wpq": w_pq[l].astype(BF16),
    }


def _rope_tables(seq_len, rope):
    zeros_n = jnp.zeros((seq_len, QK_NOPE), F32)
    zeros_p = jnp.zeros((seq_len, HEAD_PAD - QK_NOPE - QK_ROPE), F32)
    ones_n = jnp.ones((seq_len, QK_NOPE), F32)
    if rope:
        pos = jnp.arange(seq_len, dtype=I32)
        half = QK_ROPE // 2
        inv_freq = ROPE_BASE ** (-jnp.arange(0, half, 2, dtype=F32) / half)
        ang_r = (pos // GRID_W).astype(F32)[:, None] * inv_freq
        ang_c = (pos % GRID_W).astype(F32)[:, None] * inv_freq
        cos = jnp.concatenate([jnp.cos(ang_r)] * 2 + [jnp.cos(ang_c)] * 2, axis=1)
        sin = jnp.concatenate([jnp.sin(ang_r)] * 2 + [jnp.sin(ang_c)] * 2, axis=1)
    else:
        cos = jnp.ones((seq_len, QK_ROPE), F32)
        sin = jnp.zeros((seq_len, QK_ROPE), F32)
    cq = jnp.concatenate([ones_n, cos, zeros_p], axis=1) * ATT_SCALE
    sq = jnp.concatenate([zeros_n, sin, zeros_p], axis=1) * ATT_SCALE
    ck = jnp.concatenate([zeros_n, cos, zeros_p], axis=1)
    sk = jnp.concatenate([zeros_n, sin, zeros_p], axis=1)
    return cq, sq, ck, sk


def _dft(seq_len):
    idx = jnp.arange(seq_len, dtype=I32)
    ang = (2.0 * math.pi / seq_len) * ((idx[:, None] * idx[None, :]) % seq_len).astype(F32)
    return jnp.cos(ang).astype(BF16), jnp.sin(ang).astype(BF16)


def _peer_block(x1, h2, pq, keys, mod_tiles, mod_row, seq_len, fg, down3, up3, final):
    t, d = x1.shape
    eidx_n, gate_n = _route(pq, keys)
    eidx = eidx_n.T
    gate3 = gate_n.reshape(PEER_SLOTS, t // PEER_TOK, PEER_TOK).transpose(1, 0, 2)
    tile = (SUBLANES, LANES)
    out = _peer(eidx, gate3, h2.reshape((t,) + tile), x1.reshape((t,) + tile), mod_tiles,
                mod_row, seq_len, fg, down3, up3, final)
    return out.reshape(t, d)


def kernel(x, c, ctx, c_ctx, w_mod, b_mod, norm1_g, norm2_g, w_in, b_gate, q_norm_g, w_uq,
           kv_norm_g, w_ukv, w_oa, w_ob, w_grp, pool_scale, w_oc, w_out, w_pq, peer_keys,
           peer_down, peer_up, final_g):
    batch, seq_len, d = x.shape
    ctx_len = ctx.shape[1]
    depth = w_mod.shape[0]
    assert d == SUBLANES * LANES and batch + 1 <= MOD_ROWS
    assert seq_len % TOK_TILE == 0 and ctx_len % PEER_TOK == 0 and ctx_len % LANES == 0
    tile = (SUBLANES, LANES)

    cvec = jnp.concatenate([c, c_ctx[None, :], jnp.zeros((MOD_ROWS - batch - 1, d), F32)], axis=0)
    mod_all = _modulation(cvec, w_mod, b_mod)

    def x_row(i, per_seq):
        return i // per_seq

    def c_row(i, per_seq):
        return batch

    tabs_x = _rope_tables(seq_len, True)
    tabs_c = _rope_tables(ctx_len, False)
    dft_x = _dft(seq_len)
    dft_c = _dft(ctx_len)
    fg = final_g.reshape(tile)

    xs = x.reshape(batch * seq_len, d)
    cs = ctx.reshape(batch * ctx_len, d)
    for l in range(depth):
        last = l == depth - 1
        lw = _prep_layer(l, d, w_in, b_gate, q_norm_g, w_uq, kv_norm_g, w_ukv, w_oa, w_ob, w_grp,
                         pool_scale, w_oc, w_out, w_pq, norm1_g, norm2_g)
        mod = mod_all[l].reshape(MOD_ROWS, 1, 6 * d)
        mod_tiles = mod_all[l].reshape(MOD_ROWS, 6, SUBLANES, LANES)[:, 5:6]
        keys = peer_keys[l].reshape(2 * PEER_HEADS, N_KEYS, PEER_HALF).astype(BF16)
        down3 = peer_down[l].reshape((-1,) + tile)
        up3 = peer_up[l].reshape((-1,) + tile)

        if last:
            kc, vc = _in_proj(cs, mod, c_row, ctx_len, lw, tabs_c, False)
        else:
            kc, vc, qc, abc, zpc, gc = _in_proj(cs, mod, c_row, ctx_len, lw, tabs_c, True)
        kx, vx, qx, abx, zpx, gx = _in_proj(xs, mod, x_row, seq_len, lw, tabs_x, True)

        att_x = _attention(qx, kx, vx, batch, seq_len, (kc, vc))
        four_x = _fourier(abx, batch, seq_len, dft_x)
        pool_x = _pool(zpx, batch, seq_len, lw["wg"], lw["ps"])
        x1, h2, pq = _out_proj(xs, att_x, four_x, pool_x, gx, mod, x_row, seq_len, lw)
        xs = _peer_block(x1, h2, pq, keys, mod_tiles, x_row, seq_len, fg, down3, up3, last)

        if not last:
            att_c = _attention(qc, kc, vc, batch, ctx_len, None)
            four_c = _fourier(abc, batch, ctx_len, dft_c)
            pool_c = _pool(zpc, batch, ctx_len, lw["wg"], lw["ps"])
            c1, hc2, pqc = _out_proj(cs, att_c, four_c, pool_c, gc, mod, c_row, ctx_len, lw)
            cs = _peer_block(c1, hc2, pqc, keys, mod_tiles, c_row, ctx_len, fg, down3, up3, False)
    return xs.reshape(batch, seq_len, d)
```

```python
import functools
import math

import jax
import jax.numpy as jnp
from jax import lax
from jax.experimental import pallas as pl
from jax.experimental.pallas import tpu as pltpu

F32 = jnp.float32
BF16 = jnp.bfloat16
I32 = jnp.int32

GRID_W = 64
N_HEADS = 8
Q_LORA = 256
KV_LORA = 128
QK_NOPE = 64
QK_ROPE = 32
V_HEAD = 64
ATT_WIDTH = N_HEADS * V_HEAD
ATT_SCALE = (QK_NOPE + QK_ROPE) ** -0.5
ROPE_BASE = 10000.0
FOURIER_WIDTH = 256
POOL_WINDOWS = (2, 4, 8, 16)
POOL_GROUP = 64
POOL_WIDTH = POOL_GROUP * len(POOL_WINDOWS)
N_BRANCH = 3
PEER_HEADS = 8
N_KEYS = 128
PEER_QDIM = 256
PEER_HALF = PEER_QDIM // 2
PEER_TOPK = 16
PEER_SLOTS = PEER_HEADS * PEER_TOPK
EPS = 1e-6

LANES = 128
SUBLANES = 8
HEAD_PAD = 128
POOL_PAD = 8
MOD_ROWS = 16
VMEM_LIMIT = 48 * 1024 * 1024

TOK_TILE = 256
ATT_Q_TILE = 256
FOUR_TILE = 512
ROUTE_TILE = 128
PEER_TOK = 16
PEER_VMEM_LIMIT = 56 * 1024 * 1024

C_CQ = 0
C_CKV = C_CQ + Q_LORA
C_ZF = C_CKV + KV_LORA
C_ZP = C_ZF + FOURIER_WIDTH
C_ZG = C_ZP + POOL_WIDTH
C_KR = None


def _params(sem=None):
    return pltpu.CompilerParams(dimension_semantics=sem, vmem_limit_bytes=VMEM_LIMIT)


def _rms(x):
    return x * lax.rsqrt(jnp.mean(x * x, axis=-1, keepdims=True) + EPS)


def _dot(a, b):
    return jnp.dot(a, b, preferred_element_type=F32)


def _dot_nt(a, b):
    return lax.dot_general(a, b, (((1,), (1,)), ((), ())), preferred_element_type=F32)


def _mod_kernel(c_ref, w_ref, b_ref, o_ref):
    c = c_ref[...]
    s = c * jax.nn.sigmoid(c)
    o_ref[0] = jnp.dot(s, w_ref[0], preferred_element_type=F32,
                       precision=lax.Precision.HIGHEST) + b_ref[0]


def _modulation(cvec, w_mod, b_mod):
    depth, d, n = w_mod.shape
    tn = 1536
    return pl.pallas_call(
        _mod_kernel,
        out_shape=jax.ShapeDtypeStruct((depth, MOD_ROWS, n), F32),
        grid=(depth, n // tn),
        in_specs=[pl.BlockSpec((MOD_ROWS, d), lambda l, j: (0, 0)),
                  pl.BlockSpec((1, d, tn), lambda l, j: (l, 0, j)),
                  pl.BlockSpec((1, 1, tn), lambda l, j: (l, 0, j))],
        out_specs=pl.BlockSpec((1, MOD_ROWS, tn), lambda l, j: (l, 0, j)),
        compiler_params=_params(("arbitrary", "arbitrary")),
        name="modulation",
    )(cvec, w_mod, b_mod.reshape(depth, 1, n))


def _in_kernel(d, full, x_ref, m_ref, n1_ref, wall_ref, qg_ref, wq_ref, wqp_ref, kvg_ref,
               wk_ref, wv_ref, fc_ref, bg_ref, cq_ref, sq_ref, ck_ref, sk_ref, *outs):
    x = x_ref[...]
    m = m_ref[0]
    h = _rms(x) * n1_ref[...] * (1.0 + m[:, d:2 * d]) + m[:, 0:d]
    hb = h.astype(BF16)
    c_kr = C_ZG + N_BRANCH * d
    if full:
        z = _dot(hb, wall_ref[...])
        k_ref, v_ref, q_ref, ab_ref, zp_ref, g_ref = outs
    else:
        z = None
        k_ref, v_ref = outs
    def col(lo, hi):
        if full:
            return z[:, lo:hi]
        return _dot(hb, wall_ref[:, lo:hi])

    ckv = col(C_CKV, C_ZF)
    ckvn = (_rms(ckv) * kvg_ref[...]).astype(BF16)
    kf = _dot(ckvn, wk_ref[...])
    kr = col(c_kr, c_kr + LANES) * ck_ref[...] + col(c_kr + LANES, c_kr + 2 * LANES) * sk_ref[...]
    for hh in range(N_HEADS):
        sl = slice(hh * HEAD_PAD, (hh + 1) * HEAD_PAD)
        k_ref[:, sl] = (kf[:, sl] + kr).astype(BF16)
    v_ref[...] = _dot(ckvn, wv_ref[...]).astype(BF16)
    if not full:
        return
    cq = z[:, C_CQ:C_CKV]
    cqn = (_rms(cq) * qg_ref[...]).astype(BF16)
    qf = _dot(cqn, wq_ref[...])
    qr = _dot(cqn, wqp_ref[...])
    cosq = cq_ref[...]
    sinq = sq_ref[...]
    for hh in range(N_HEADS):
        sl = slice(hh * HEAD_PAD, (hh + 1) * HEAD_PAD)
        q_ref[:, sl] = (qf[:, sl] * cosq + qr[:, sl] * sinq).astype(BF16)
    ab_ref[...] = _dot(z[:, C_ZF:C_ZP].astype(BF16), fc_ref[...]).astype(BF16)
    zp_ref[...] = z[:, C_ZP:C_ZG]
    g_ref[...] = jax.nn.sigmoid(z[:, C_ZG:c_kr] + bg_ref[...]).astype(BF16)


def _in_proj(x2d, mod, mod_row, seq_len, lw, tabs, full):
    t, d = x2d.shape
    tm = min(TOK_TILE, seq_len)
    tiles_per_seq = seq_len // tm
    wall = lw["wall"]
    nw = wall.shape[1]
    cq, sq, ck, sk = tabs

    def const(shape):
        return pl.BlockSpec(shape, lambda i: (0,) * len(shape))

    def pos(i):
        return (i % tiles_per_seq, 0)

    in_specs = [
        pl.BlockSpec((tm, d), lambda i: (i, 0)),
        pl.BlockSpec((1, 1, mod.shape[-1]), lambda i: (mod_row(i, tiles_per_seq), 0, 0)),
        const((1, d)), const((d, nw)), const((1, Q_LORA)),
        const(lw["wq"].shape), const(lw["wqp"].shape), const((1, KV_LORA)),
        const(lw["wk"].shape), const(lw["wv"].shape), const(lw["fc"].shape),
        const((1, N_BRANCH * d)),
        pl.BlockSpec((tm, LANES), pos), pl.BlockSpec((tm, LANES), pos),
        pl.BlockSpec((tm, LANES), pos), pl.BlockSpec((tm, LANES), pos),
    ]
    kw = N_HEADS * HEAD_PAD
    out_shape = [jax.ShapeDtypeStruct((t, kw), BF16), jax.ShapeDtypeStruct((t, ATT_WIDTH), BF16)]
    out_specs = [pl.BlockSpec((tm, kw), lambda i: (i, 0)),
                 pl.BlockSpec((tm, ATT_WIDTH), lambda i: (i, 0))]
    if full:
        out_shape += [jax.ShapeDtypeStruct((t, kw), BF16),
                      jax.ShapeDtypeStruct((t, 2 * FOURIER_WIDTH), BF16),
                      jax.ShapeDtypeStruct((t, POOL_WIDTH), F32),
                      jax.ShapeDtypeStruct((t, N_BRANCH * d), BF16)]
        out_specs += [pl.BlockSpec((tm, kw), lambda i: (i, 0)),
                      pl.BlockSpec((tm, 2 * FOURIER_WIDTH), lambda i: (i, 0)),
                      pl.BlockSpec((tm, POOL_WIDTH), lambda i: (i, 0)),
                      pl.BlockSpec((tm, N_BRANCH * d), lambda i: (i, 0))]
    return pl.pallas_call(
        functools.partial(_in_kernel, d, full),
        out_shape=out_shape,
        grid=(t // tm,),
        in_specs=in_specs,
        out_specs=out_specs,
        compiler_params=_params(("arbitrary",)),
        name="in_proj" if full else "ctx_kv_proj",
    )(x2d, mod, lw["n1"], wall, lw["qg"], lw["wq"], lw["wqp"], lw["kvg"], lw["wk"], lw["wv"],
      lw["fc"], lw["bg"], cq, sq, ck, sk)


def _attn_kernel(has_ctx, q_ref, k_ref, v_ref, *rest):
    if has_ctx:
        kc_ref, vc_ref, o_ref = rest
    else:
        (o_ref,) = rest
    outs = []
    for hh in range(2):
        sl = slice(hh * HEAD_PAD, (hh + 1) * HEAD_PAD)
        q = q_ref[:, sl]
        s = _dot_nt(q, k_ref[:, sl])
        mx = jnp.max(s, axis=-1, keepdims=True)
        if has_ctx:
            sc = _dot_nt(q, kc_ref[:, sl])
            mx = jnp.maximum(mx, jnp.max(sc, axis=-1, keepdims=True))
        p = jnp.exp(s - mx)
        den = jnp.sum(p, axis=-1, keepdims=True)
        o = _dot(p.astype(BF16), v_ref[...])
        if has_ctx:
            pc = jnp.exp(sc - mx)
            den = den + jnp.sum(pc, axis=-1, keepdims=True)
            o = o + _dot(pc.astype(BF16), vc_ref[...])
        outs.append(o / den)
    lane = lax.broadcasted_iota(I32, outs[0].shape, 1)
    o_ref[...] = jnp.where(lane < V_HEAD, outs[0], outs[1]).astype(BF16)


def _attention(q, k, v, batch, seq_len, ctx_kv):
    t = q.shape[0]
    tq = min(ATT_Q_TILE, seq_len)
    nq = seq_len // tq
    pair_w = 2 * HEAD_PAD
    in_specs = [pl.BlockSpec((tq, pair_w), lambda b, j, i: (b * nq + i, j)),
                pl.BlockSpec((seq_len, pair_w), lambda b, j, i: (b, j)),
                pl.BlockSpec((seq_len, 2 * V_HEAD), lambda b, j, i: (b, j))]
    args = [q, k, v]
    if ctx_kv is not None:
        kc, vc = ctx_kv
        lc = kc.shape[0] // batch
        in_specs += [pl.BlockSpec((lc, pair_w), lambda b, j, i: (b, j)),
                     pl.BlockSpec((lc, 2 * V_HEAD), lambda b, j, i: (b, j))]
        args += [kc, vc]
    return pl.pallas_call(
        functools.partial(_attn_kernel, ctx_kv is not None),
        out_shape=jax.ShapeDtypeStruct((t, ATT_WIDTH), BF16),
        grid=(batch, N_HEADS // 2, nq),
        in_specs=in_specs,
        out_specs=pl.BlockSpec((tq, 2 * V_HEAD), lambda b, j, i: (b * nq + i, j)),
        compiler_params=_params(("arbitrary", "arbitrary", "arbitrary")),
        name="attention" if ctx_kv is not None else "ctx_attention",
    )(*args)


def _fourier_kernel(norm, c_ref, s_ref, ab_ref, o_ref):
    a = ab_ref[:, 0:FOURIER_WIDTH]
    b = ab_ref[:, FOURIER_WIDTH:2 * FOURIER_WIDTH]
    o = _dot(c_ref[...], a) - _dot(s_ref[...], b)
    o_ref[...] = (o * norm).astype(BF16)


def _fourier(ab, batch, seq_len, dft):
    t = ab.shape[0]
    tm = min(FOUR_TILE, seq_len)
    nt = seq_len // tm
    cl, sl = dft
    norm = 1.0 / math.sqrt(seq_len * FOURIER_WIDTH)
    return pl.pallas_call(
        functools.partial(_fourier_kernel, norm),
        out_shape=jax.ShapeDtypeStruct((t, FOURIER_WIDTH), BF16),
        grid=(nt, batch),
        in_specs=[pl.BlockSpec((tm, seq_len), lambda i, b: (i, 0)),
                  pl.BlockSpec((tm, seq_len), lambda i, b: (i, 0)),
                  pl.BlockSpec((seq_len, 2 * FOURIER_WIDTH), lambda i, b: (b, 0))],
        out_specs=pl.BlockSpec((tm, FOURIER_WIDTH), lambda i, b: (b * nt + i, 0)),
        compiler_params=_params(("arbitrary", "arbitrary")),
        name="fourier",
    )(cl, sl, ab)


def _pool_kernel(seq_len, z_ref, wg_ref, ps_ref, o_ref, pad_ref, s_ref):
    n = seq_len
    p = n + 2 * POOL_PAD
    z = z_ref[...]
    zeros = jnp.zeros((POOL_PAD, POOL_WIDTH), F32)
    pad_ref[0:POOL_PAD, :] = zeros
    pad_ref[POOL_PAD + n:p, :] = zeros
    pad_ref[POOL_PAD:POOL_PAD + n, :] = z
    s_ref[0:p - 1, :] = pad_ref[0:p - 1, :] + pad_ref[1:p, :]
    w2 = s_ref[POOL_PAD - 1:POOL_PAD - 1 + n, :]
    pad_ref[0:p - 3, :] = s_ref[0:p - 3, :] + s_ref[2:p - 1, :]
    w4 = pad_ref[POOL_PAD - 2:POOL_PAD - 2 + n, :]
    s_ref[0:p - 7, :] = pad_ref[0:p - 7, :] + pad_ref[4:p - 3, :]
    w8 = s_ref[POOL_PAD - 4:POOL_PAD - 4 + n, :]
    pad_ref[0:p - 15, :] = s_ref[0:p - 15, :] + s_ref[8:p - 7, :]
    w16 = pad_ref[0:n, :]
    pos = lax.broadcasted_iota(I32, (n, POOL_WIDTH), 0)
    grp = lax.broadcasted_iota(I32, (n, POOL_WIDTH), 1) // POOL_GROUP
    win = jnp.where(grp == 0, w2, jnp.where(grp == 1, w4, jnp.where(grp == 2, w8, w16)))
    half = jnp.where(grp == 0, 1, jnp.where(grp == 1, 2, jnp.where(grp == 2, 4, 8)))
    lo = jnp.maximum(pos - half, 0)
    hi = jnp.minimum(pos + half, n)
    cnt = (hi - lo).astype(F32)
    pooled = win / cnt - z
    y = _dot(pooled.astype(BF16), wg_ref[...])
    o_ref[...] = (y * ps_ref[...]).astype(BF16)


def _pool(zp, batch, seq_len, wg_bd, pool_scale):
    t = zp.shape[0]
    return pl.pallas_call(
        functools.partial(_pool_kernel, seq_len),
        out_shape=jax.ShapeDtypeStruct((t, POOL_WIDTH), BF16),
        grid=(batch,),
        in_specs=[pl.BlockSpec((seq_len, POOL_WIDTH), lambda b: (b, 0)),
                  pl.BlockSpec((POOL_WIDTH, POOL_WIDTH), lambda b: (0, 0)),
                  pl.BlockSpec((1, POOL_WIDTH), lambda b: (0, 0))],
        out_specs=pl.BlockSpec((seq_len, POOL_WIDTH), lambda b: (b, 0)),
        scratch_shapes=[pltpu.VMEM((seq_len + 2 * POOL_PAD, POOL_WIDTH), F32),
                        pltpu.VMEM((seq_len + 2 * POOL_PAD, POOL_WIDTH), F32)],
        compiler_params=_params(("arbitrary",)),
        name="pool",
    )(zp, wg_bd, pool_scale)


def _out_kernel(d, x_ref, att_ref, four_ref, pool_ref, g_ref, m_ref, woa_ref, wob_ref, woc_ref,
                wout_ref, n2_ref, wpq_ref, x1_ref, h2_ref, pq_ref):
    m = m_ref[0]
    ya = _dot(att_ref[...], woa_ref[...])
    yb = _dot(four_ref[...], wob_ref[...])
    yc = _dot(pool_ref[...], woc_ref[...])
    mixp = (g_ref[:, 0:d].astype(F32) * ya + g_ref[:, d:2 * d].astype(F32) * yb
            + g_ref[:, 2 * d:3 * d].astype(F32) * yc)
    mix = _dot(mixp.astype(BF16), wout_ref[...])
    x1 = x_ref[...] + m[:, 2 * d:3 * d] * mix
    x1_ref[...] = x1
    h2 = _rms(x1) * n2_ref[...] * (1.0 + m[:, 4 * d:5 * d]) + m[:, 3 * d:4 * d]
    h2_ref[...] = h2
    pq = _dot(h2.astype(BF16), wpq_ref[...])
    for hp in range(2 * PEER_HEADS):
        pq_ref[hp] = pq[:, hp * PEER_HALF:(hp + 1) * PEER_HALF].astype(BF16)


def _out_proj(x2d, att, four, pool, g, mod, mod_row, seq_len, lw):
    t, d = x2d.shape
    tm = min(TOK_TILE, seq_len)
    tiles_per_seq = seq_len // tm

    def const(shape):
        return pl.BlockSpec(shape, lambda i: (0,) * len(shape))

    def row(w):
        return pl.BlockSpec((tm, w), lambda i: (i, 0))

    nhp = 2 * PEER_HEADS
    return pl.pallas_call(
        functools.partial(_out_kernel, d),
        out_shape=[jax.ShapeDtypeStruct((t, d), F32), jax.ShapeDtypeStruct((t, d), F32),
                   jax.ShapeDtypeStruct((nhp, t, PEER_HALF), BF16)],
        grid=(t // tm,),
        in_specs=[row(d), row(ATT_WIDTH), row(FOURIER_WIDTH), row(POOL_WIDTH), row(N_BRANCH * d),
                  pl.BlockSpec((1, 1, mod.shape[-1]), lambda i: (mod_row(i, tiles_per_seq), 0, 0)),
                  const(lw["woa"].shape), const(lw["wob"].shape), const(lw["woc"].shape),
                  const(lw["wout"].shape), const((1, d)), const(lw["wpq"].shape)],
        out_specs=[row(d), row(d), pl.BlockSpec((nhp, tm, PEER_HALF), lambda i: (0, i, 0))],
        compiler_params=_params(("arbitrary",)),
        name="out_proj",
    )(x2d, att, four, pool, g, mod, lw["woa"], lw["wob"], lw["woc"], lw["wout"], lw["n2"],
      lw["wpq"])


def _top_rounds(s, iota, n, emit):
    neg = jnp.float32(-jnp.inf)
    for r in range(PEER_TOPK):
        mx = jnp.max(s, axis=0, keepdims=True)
        idx = jnp.min(jnp.where(s == mx, iota, n), axis=0, keepdims=True)
        hit = iota == idx
        emit(r, mx, idx, hit)
        s = jnp.where(hit, neg, s)


def _route_kernel(pq_ref, keys_ref, eidx_ref, gate_ref, sv_ref, si_ref, cand_ref, cidx_ref,
                  ts_ref):
    tm = pq_ref.shape[1]
    iota_k = lax.broadcasted_iota(I32, (N_KEYS, tm), 0)
    iota_c = lax.broadcasted_iota(I32, (PEER_TOPK * PEER_TOPK, tm), 0)

    def head(hd, carry):
        for p in range(2):
            s = _dot_nt(keys_ref[2 * hd + p], pq_ref[2 * hd + p])

            def emit(r, mx, idx, hit, p=p):
                sv_ref[p, r:r + 1, :] = mx
                si_ref[p, r:r + 1, :] = idx

            _top_rounds(s, iota_k, N_KEYS, emit)
        for i in range(PEER_TOPK):
            rows = slice(i * PEER_TOPK, (i + 1) * PEER_TOPK)
            cand_ref[rows, :] = sv_ref[0, i:i + 1, :] + sv_ref[1]
            cidx_ref[rows, :] = si_ref[0, i:i + 1, :] * N_KEYS + si_ref[1]
        cidx = cidx_ref[...]
        base = pl.multiple_of(hd * PEER_TOPK, PEER_TOPK)

        def emit2(r, mx, idx, hit):
            ts_ref[r:r + 1, :] = mx
            eidx_ref[pl.ds(base + r, 1), :] = jnp.sum(jnp.where(hit, cidx, 0), axis=0,
                                                      keepdims=True)

        _top_rounds(cand_ref[...], iota_c, PEER_TOPK * PEER_TOPK, emit2)
        ts = ts_ref[...]
        ex = jnp.exp(ts - ts[0:1, :])
        gate_ref[pl.ds(base, PEER_TOPK), :] = ex / jnp.sum(ex, axis=0, keepdims=True)
        return carry

    lax.fori_loop(0, PEER_HEADS, head, 0)


def _route(pq, keys):
    nhp, t, _ = pq.shape
    tm = ROUTE_TILE
    return pl.pallas_call(
        _route_kernel,
        out_shape=[jax.ShapeDtypeStruct((PEER_SLOTS, t), I32),
                   jax.ShapeDtypeStruct((PEER_SLOTS, t), F32)],
        grid=(t // tm,),
        in_specs=[pl.BlockSpec((nhp, tm, PEER_HALF), lambda i: (0, i, 0)),
                  pl.BlockSpec((nhp, N_KEYS, PEER_HALF), lambda i: (0, 0, 0))],
        out_specs=[pl.BlockSpec((PEER_SLOTS, tm), lambda i: (0, i)),
                   pl.BlockSpec((PEER_SLOTS, tm), lambda i: (0, i))],
        scratch_shapes=[pltpu.VMEM((2, PEER_TOPK, tm), F32), pltpu.VMEM((2, PEER_TOPK, tm), I32),
                        pltpu.VMEM((PEER_TOPK * PEER_TOPK, tm), F32),
                        pltpu.VMEM((PEER_TOPK * PEER_TOPK, tm), I32),
                        pltpu.VMEM((PEER_TOPK, tm), F32)],
        compiler_params=_params(("arbitrary",)),
        name="peer_route",
    )(pq, keys)


def _peer_kernel(final, idx_ref, idxn_ref, gate_ref, h_ref, x_ref, g2_ref, fg_ref, tab_ref,
                 o_ref, buf, act_ref, sem):
    i = pl.program_id(0)
    n = pl.num_programs(0)
    slot = i % 2
    rows = PEER_TOK * PEER_SLOTS

    def issue(ids, sl):
        def body(t, carry):
            r0 = t * PEER_SLOTS
            for k in range(PEER_SLOTS):
                pltpu.make_async_copy(tab_ref.at[ids[t, k]], buf.at[sl, r0 + k],
                                      sem.at[sl]).start(priority=k % 2)
            return carry
        lax.fori_loop(0, PEER_TOK, body, 0)

    @pl.when(i == 0)
    def _():
        issue(idx_ref, slot)

    @pl.when(i + 1 < n)
    def _():
        issue(idxn_ref, 1 - slot)

    pltpu.make_async_copy(tab_ref.at[pl.ds(0, rows)], buf.at[slot], sem.at[slot]).wait()

    g2 = g2_ref[0, 0]
    for t in range(PEER_TOK):
        r0 = t * PEER_SLOTS
        dn = buf[slot, r0:r0 + PEER_SLOTS, 0:SUBLANES, :]
        hv = h_ref[t]
        part = jnp.sum(dn * hv[None], axis=1)
        a = jnp.sum(part, axis=1, keepdims=True)
        act = gate_ref[0][:, t:t + 1] * jax.nn.gelu(a)
        act_ref[...] = jnp.broadcast_to(act, (PEER_SLOTS, LANES))
        acc = jnp.zeros((SUBLANES, LANES), F32)
        for k in range(PEER_SLOTS):
            acc = acc + act_ref[k:k + 1, :] * buf[slot, r0 + k, SUBLANES:2 * SUBLANES, :]
        xr = x_ref[t] + g2 * acc
        if final:
            ms = jnp.sum(jnp.sum(xr * xr, axis=1, keepdims=True), axis=0, keepdims=True)
            xr = xr * lax.rsqrt(ms / (SUBLANES * LANES) + EPS) * fg_ref[...]
        o_ref[t] = xr


def _peer(eidx, gate3, h3, x3, g2, mod_row, seq_len, fg, table, final):
    t = h3.shape[0]
    tb = PEER_TOK
    nb = t // tb
    steps_per_seq = seq_len // tb
    rows = tb * PEER_SLOTS
    tile = (SUBLANES, LANES)
    pair = (2 * SUBLANES, LANES)
    return pl.pallas_call(
        functools.partial(_peer_kernel, final),
        out_shape=jax.ShapeDtypeStruct((t,) + tile, F32),
        grid=(nb,),
        in_specs=[pl.BlockSpec((tb, PEER_SLOTS), lambda i: (i, 0), memory_space=pltpu.SMEM),
                  pl.BlockSpec((tb, PEER_SLOTS), lambda i: (jnp.minimum(i + 1, nb - 1), 0),
                               memory_space=pltpu.SMEM),
                  pl.BlockSpec((1, PEER_SLOTS, tb), lambda i: (i, 0, 0)),
                  pl.BlockSpec((tb,) + tile, lambda i: (i, 0, 0)),
                  pl.BlockSpec((tb,) + tile, lambda i: (i, 0, 0)),
                  pl.BlockSpec((1, 1) + tile, lambda i: (mod_row(i, steps_per_seq), 0, 0, 0)),
                  pl.BlockSpec(tile, lambda i: (0, 0)),
                  pl.BlockSpec(memory_space=pl.ANY)],
        out_specs=pl.BlockSpec((tb,) + tile, lambda i: (i, 0, 0)),
        scratch_shapes=[pltpu.VMEM((2, rows) + pair, F32),
                        pltpu.VMEM((PEER_SLOTS, LANES), F32),
                        pltpu.SemaphoreType.DMA((2,))],
        compiler_params=pltpu.CompilerParams(dimension_semantics=("arbitrary",),
                                             vmem_limit_bytes=PEER_VMEM_LIMIT),
        name="peer_experts",
    )(eidx, eidx, gate3, h3, x3, g2, fg, table)


def _rope_perm(w):
    q = QK_ROPE // 4
    a1, a2, b1, b2 = (w[..., j * q:(j + 1) * q] for j in range(4))
    return jnp.concatenate([-a2, a1, -b2, b1], axis=-1)


def _prep_layer(l, d, w_in, b_gate, q_norm_g, w_uq, kv_norm_g, w_ukv, w_oa, w_ob, w_grp,
                pool_scale, w_oc, w_out, w_pq, norm1_g, norm2_g):
    wi = w_in[l]
    s0 = Q_LORA
    s1 = s0 + KV_LORA
    s2 = s1 + QK_ROPE
    s3 = s2 + FOURIER_WIDTH
    s4 = s3 + POOL_WIDTH
    w_kr = wi[:, s1:s2]
    zl = jnp.zeros((d, QK_NOPE), F32)
    zr = jnp.zeros((d, HEAD_PAD - QK_NOPE - QK_ROPE), F32)
    wall = jnp.concatenate([wi[:, 0:s1], wi[:, s2:], zl, w_kr, zr, zl, _rope_perm(w_kr), zr],
                           axis=1).astype(BF16)
    wq = w_uq[l].reshape(Q_LORA, N_HEADS, QK_NOPE + QK_ROPE)
    qpad = jnp.zeros((Q_LORA, N_HEADS, HEAD_PAD - QK_NOPE - QK_ROPE), F32)
    wq_full = jnp.concatenate([wq, qpad], axis=-1).reshape(Q_LORA, N_HEADS * HEAD_PAD)
    wq_perm = jnp.concatenate([jnp.zeros((Q_LORA, N_HEADS, QK_NOPE), F32),
                               _rope_perm(wq[..., QK_NOPE:]), qpad],
                              axis=-1).reshape(Q_LORA, N_HEADS * HEAD_PAD)
    wkv = w_ukv[l].reshape(KV_LORA, N_HEADS, QK_NOPE + V_HEAD)
    wk = jnp.concatenate([wkv[..., :QK_NOPE],
                          jnp.zeros((KV_LORA, N_HEADS, HEAD_PAD - QK_NOPE), F32)],
                         axis=-1).reshape(KV_LORA, N_HEADS * HEAD_PAD)
    wv = wkv[..., QK_NOPE:].reshape(KV_LORA, ATT_WIDTH)
    cidx = jnp.arange(FOURIER_WIDTH, dtype=I32)
    ang = (2.0 * math.pi / FOURIER_WIDTH) * ((cidx[:, None] * cidx[None, :]) % FOURIER_WIDTH
                                             ).astype(F32)
    fc = jnp.concatenate([jnp.cos(ang), jnp.sin(ang)], axis=1).astype(BF16)
    ng = len(POOL_WINDOWS)
    wg = jnp.zeros((ng, POOL_GROUP, ng, POOL_GROUP), F32)
    for gi in range(ng):
        wg = wg.at[gi, :, gi, :].set(w_grp[l, gi])
    return {
        "wall": wall, "n1": norm1_g[l][None, :], "n2": norm2_g[l][None, :],
        "qg": q_norm_g[l][None, :], "kvg": kv_norm_g[l][None, :],
        "wq": wq_full.astype(BF16), "wqp": wq_perm.astype(BF16),
        "wk": wk.astype(BF16), "wv": wv.astype(BF16), "fc": fc,
        "bg": b_gate[l][None, :],
        "wg": wg.reshape(POOL_WIDTH, POOL_WIDTH).astype(BF16),
        "ps": pool_scale[l][None, :],
        "woa": w_oa[l].astype(BF16), "wob": w_ob[l].astype(BF16), "woc": w_oc[l].astype(BF16),
        "wout": w_out[l].astype(BF16), "wpq": w_pq[l].astype(BF16),
    }


def _rope_tables(seq_len, rope):
    zeros_n = jnp.zeros((seq_len, QK_NOPE), F32)
    zeros_p = jnp.zeros((seq_len, HEAD_PAD - QK_NOPE - QK_ROPE), F32)
    ones_n = jnp.ones((seq_len, QK_NOPE), F32)
    if rope:
        pos = jnp.arange(seq_len, dtype=I32)
        half = QK_ROPE // 2
        inv_freq = ROPE_BASE ** (-jnp.arange(0, half, 2, dtype=F32) / half)
        ang_r = (pos // GRID_W).astype(F32)[:, None] * inv_freq
        ang_c = (pos % GRID_W).astype(F32)[:, None] * inv_freq
        cos = jnp.concatenate([jnp.cos(ang_r)] * 2 + [jnp.cos(ang_c)] * 2, axis=1)
        sin = jnp.concatenate([jnp.sin(ang_r)] * 2 + [jnp.sin(ang_c)] * 2, axis=1)
    else:
        cos = jnp.ones((seq_len, QK_ROPE), F32)
        sin = jnp.zeros((seq_len, QK_ROPE), F32)
    cq = jnp.concatenate([ones_n, cos, zeros_p], axis=1) * ATT_SCALE
    sq = jnp.concatenate([zeros_n, sin, zeros_p], axis=1) * ATT_SCALE
    ck = jnp.concatenate([zeros_n, cos, zeros_p], axis=1)
    sk = jnp.concatenate([zeros_n, sin, zeros_p], axis=1)
    return cq, sq, ck, sk


def _dft(seq_len):
    idx = jnp.arange(seq_len, dtype=I32)
    ang = (2.0 * math.pi / seq_len) * ((idx[:, None] * idx[None, :]) % seq_len).astype(F32)
    return jnp.cos(ang).astype(BF16), jnp.sin(ang).astype(BF16)


def _peer_block(x1, h2, pq, keys, mod_tiles, mod_row, seq_len, fg, table, final):
    t, d = x1.shape
    eidx_n, gate_n = _route(pq, keys)
    eidx = eidx_n.T
    gate3 = gate_n.reshape(PEER_SLOTS, t // PEER_TOK, PEER_TOK).transpose(1, 0, 2)
    tile = (SUBLANES, LANES)
    out = _peer(eidx, gate3, h2.reshape((t,) + tile), x1.reshape((t,) + tile), mod_tiles,
                mod_row, seq_len, fg, table, final)
    return out.reshape(t, d)


def kernel(x, c, ctx, c_ctx, w_mod, b_mod, norm1_g, norm2_g, w_in, b_gate, q_norm_g, w_uq,
           kv_norm_g, w_ukv, w_oa, w_ob, w_grp, pool_scale, w_oc, w_out, w_pq, peer_keys,
           peer_down, peer_up, final_g):
    batch, seq_len, d = x.shape
    ctx_len = ctx.shape[1]
    depth = w_mod.shape[0]
    assert d == SUBLANES * LANES and batch + 1 <= MOD_ROWS
    assert seq_len % TOK_TILE == 0 and ctx_len % PEER_TOK == 0 and ctx_len % LANES == 0
    tile = (SUBLANES, LANES)

    cvec = jnp.concatenate([c, c_ctx[None, :], jnp.zeros((MOD_ROWS - batch - 1, d), F32)], axis=0)
    mod_all = _modulation(cvec, w_mod, b_mod)

    def x_row(i, per_seq):
        return i // per_seq

    def c_row(i, per_seq):
        return batch

    tabs_x = _rope_tables(seq_len, True)
    tabs_c = _rope_tables(ctx_len, False)
    dft_x = _dft(seq_len)
    dft_c = _dft(ctx_len)
    fg = final_g.reshape(tile)

    xs = x.reshape(batch * seq_len, d)
    cs = ctx.reshape(batch * ctx_len, d)
    for l in range(depth):
        last = l == depth - 1
        lw = _prep_layer(l, d, w_in, b_gate, q_norm_g, w_uq, kv_norm_g, w_ukv, w_oa, w_ob, w_grp,
                         pool_scale, w_oc, w_out, w_pq, norm1_g, norm2_g)
        mod = mod_all[l].reshape(MOD_ROWS, 1, 6 * d)
        mod_tiles = mod_all[l].reshape(MOD_ROWS, 6, SUBLANES, LANES)[:, 5:6]
        keys = peer_keys[l].reshape(2 * PEER_HEADS, N_KEYS, PEER_HALF).astype(BF16)
        table = jnp.concatenate([peer_down[l].reshape((-1,) + tile),
                                 peer_up[l].reshape((-1,) + tile)], axis=1)

        if last:
            kc, vc = _in_proj(cs, mod, c_row, ctx_len, lw, tabs_c, False)
        else:
            kc, vc, qc, abc, zpc, gc = _in_proj(cs, mod, c_row, ctx_len, lw, tabs_c, True)
        kx, vx, qx, abx, zpx, gx = _in_proj(xs, mod, x_row, seq_len, lw, tabs_x, True)

        att_x = _attention(qx, kx, vx, batch, seq_len, (kc, vc))
        four_x = _fourier(abx, batch, seq_len, dft_x)
        pool_x = _pool(zpx, batch, seq_len, lw["wg"], lw["ps"])
        x1, h2, pq = _out_proj(xs, att_x, four_x, pool_x, gx, mod, x_row, seq_len, lw)
        xs = _peer_block(x1, h2, pq, keys, mod_tiles, x_row, seq_len, fg, table, last)

        if not last:
            att_c = _attention(qc, kc, vc, batch, ctx_len, None)
            four_c = _fourier(abc, batch, ctx_len, dft_c)
            pool_c = _pool(zpc, batch, ctx_len, lw["wg"], lw["ps"])
            c1, hc2, pqc = _out_proj(cs, att_c, four_c, pool_c, gc, mod, c_row, ctx_len, lw)
            cs = _peer_block(c1, hc2, pqc, keys, mod_tiles, c_row, ctx_len, fg, table, False)
    return xs.reshape(batch, seq_len, d)
```

```python
import functools
import math

import jax
import jax.numpy as jnp
from jax import lax
from jax.experimental import pallas as pl
from jax.experimental.pallas import tpu as pltpu

F32 = jnp.float32
BF16 = jnp.bfloat16
I32 = jnp.int32

GRID_W = 64
N_HEADS = 8
Q_LORA = 256
KV_LORA = 128
QK_NOPE = 64
QK_ROPE = 32
V_HEAD = 64
ATT_WIDTH = N_HEADS * V_HEAD
ATT_SCALE = (QK_NOPE + QK_ROPE) ** -0.5
ROPE_BASE = 10000.0
FOURIER_WIDTH = 256
POOL_WINDOWS = (2, 4, 8, 16)
POOL_GROUP = 64
POOL_WIDTH = POOL_GROUP * len(POOL_WINDOWS)
N_BRANCH = 3
PEER_HEADS = 8
N_KEYS = 128
PEER_QDIM = 256
PEER_HALF = PEER_QDIM // 2
PEER_TOPK = 16
PEER_SLOTS = PEER_HEADS * PEER_TOPK
EPS = 1e-6

LANES = 128
SUBLANES = 8
HEAD_PAD = 128
POOL_PAD = 8
MOD_ROWS = 16
VMEM_LIMIT = 48 * 1024 * 1024

TOK_TILE = 256
ATT_Q_TILE = 256
FOUR_TILE = 512
ROUTE_TILE = 128
PEER_TOK = 16
PEER_ISSUE_SPREAD = 12
PEER_VMEM_LIMIT = 56 * 1024 * 1024

C_CQ = 0
C_CKV = C_CQ + Q_LORA
C_ZF = C_CKV + KV_LORA
C_ZP = C_ZF + FOURIER_WIDTH
C_ZG = C_ZP + POOL_WIDTH
C_KR = None


def _params(sem=None):
    return pltpu.CompilerParams(dimension_semantics=sem, vmem_limit_bytes=VMEM_LIMIT)


def _rms(x):
    return x * lax.rsqrt(jnp.mean(x * x, axis=-1, keepdims=True) + EPS)


def _dot(a, b):
    return jnp.dot(a, b, preferred_element_type=F32)


def _dot_nt(a, b):
    return lax.dot_general(a, b, (((1,), (1,)), ((), ())), preferred_element_type=F32)


def _mod_kernel(c_ref, w_ref, b_ref, o_ref):
    c = c_ref[...]
    s = c * jax.nn.sigmoid(c)
    o_ref[0] = jnp.dot(s, w_ref[0], preferred_element_type=F32,
                       precision=lax.Precision.HIGHEST) + b_ref[0]


def _modulation(cvec, w_mod, b_mod):
    depth, d, n = w_mod.shape
    tn = 1536
    return pl.pallas_call(
        _mod_kernel,
        out_shape=jax.ShapeDtypeStruct((depth, MOD_ROWS, n), F32),
        grid=(depth, n // tn),
        in_specs=[pl.BlockSpec((MOD_ROWS, d), lambda l, j: (0, 0)),
                  pl.BlockSpec((1, d, tn), lambda l, j: (l, 0, j)),
                  pl.BlockSpec((1, 1, tn), lambda l, j: (l, 0, j))],
        out_specs=pl.BlockSpec((1, MOD_ROWS, tn), lambda l, j: (l, 0, j)),
        compiler_params=_params(("arbitrary", "arbitrary")),
        name="modulation",
    )(cvec, w_mod, b_mod.reshape(depth, 1, n))


def _in_kernel(d, full, x_ref, m_ref, n1_ref, wall_ref, qg_ref, wq_ref, wqp_ref, kvg_ref,
               wk_ref, wv_ref, fc_ref, bg_ref, cq_ref, sq_ref, ck_ref, sk_ref, *outs):
    x = x_ref[...]
    m = m_ref[0]
    h = _rms(x) * n1_ref[...] * (1.0 + m[:, d:2 * d]) + m[:, 0:d]
    hb = h.astype(BF16)
    c_kr = C_ZG + N_BRANCH * d
    if full:
        z = _dot(hb, wall_ref[...])
        k_ref, v_ref, q_ref, ab_ref, zp_ref, g_ref = outs
    else:
        z = None
        k_ref, v_ref = outs
    def col(lo, hi):
        if full:
            return z[:, lo:hi]
        return _dot(hb, wall_ref[:, lo:hi])

    ckv = col(C_CKV, C_ZF)
    ckvn = (_rms(ckv) * kvg_ref[...]).astype(BF16)
    kf = _dot(ckvn, wk_ref[...])
    kr = col(c_kr, c_kr + LANES) * ck_ref[...] + col(c_kr + LANES, c_kr + 2 * LANES) * sk_ref[...]
    for hh in range(N_HEADS):
        sl = slice(hh * HEAD_PAD, (hh + 1) * HEAD_PAD)
        k_ref[:, sl] = (kf[:, sl] + kr).astype(BF16)
    v_ref[...] = _dot(ckvn, wv_ref[...]).astype(BF16)
    if not full:
        return
    cq = z[:, C_CQ:C_CKV]
    cqn = (_rms(cq) * qg_ref[...]).astype(BF16)
    qf = _dot(cqn, wq_ref[...])
    qr = _dot(cqn, wqp_ref[...])
    cosq = cq_ref[...]
    sinq = sq_ref[...]
    for hh in range(N_HEADS):
        sl = slice(hh * HEAD_PAD, (hh + 1) * HEAD_PAD)
        q_ref[:, sl] = (qf[:, sl] * cosq + qr[:, sl] * sinq).astype(BF16)
    ab_ref[...] = _dot(z[:, C_ZF:C_ZP].astype(BF16), fc_ref[...]).astype(BF16)
    zp_ref[...] = z[:, C_ZP:C_ZG]
    g_ref[...] = jax.nn.sigmoid(z[:, C_ZG:c_kr] + bg_ref[...]).astype(BF16)


def _in_proj(x2d, mod, mod_row, seq_len, lw, tabs, full):
    t, d = x2d.shape
    tm = min(TOK_TILE, seq_len)
    tiles_per_seq = seq_len // tm
    wall = lw["wall"]
    nw = wall.shape[1]
    cq, sq, ck, sk = tabs

    def const(shape):
        return pl.BlockSpec(shape, lambda i: (0,) * len(shape))

    def pos(i):
        return (i % tiles_per_seq, 0)

    in_specs = [
        pl.BlockSpec((tm, d), lambda i: (i, 0)),
        pl.BlockSpec((1, 1, mod.shape[-1]), lambda i: (mod_row(i, tiles_per_seq), 0, 0)),
        const((1, d)), const((d, nw)), const((1, Q_LORA)),
        const(lw["wq"].shape), const(lw["wqp"].shape), const((1, KV_LORA)),
        const(lw["wk"].shape), const(lw["wv"].shape), const(lw["fc"].shape),
        const((1, N_BRANCH * d)),
        pl.BlockSpec((tm, LANES), pos), pl.BlockSpec((tm, LANES), pos),
        pl.BlockSpec((tm, LANES), pos), pl.BlockSpec((tm, LANES), pos),
    ]
    kw = N_HEADS * HEAD_PAD
    out_shape = [jax.ShapeDtypeStruct((t, kw), BF16), jax.ShapeDtypeStruct((t, ATT_WIDTH), BF16)]
    out_specs = [pl.BlockSpec((tm, kw), lambda i: (i, 0)),
                 pl.BlockSpec((tm, ATT_WIDTH), lambda i: (i, 0))]
    if full:
        out_shape += [jax.ShapeDtypeStruct((t, kw), BF16),
                      jax.ShapeDtypeStruct((t, 2 * FOURIER_WIDTH), BF16),
                      jax.ShapeDtypeStruct((t, POOL_WIDTH), F32),
                      jax.ShapeDtypeStruct((t, N_BRANCH * d), BF16)]
        out_specs += [pl.BlockSpec((tm, kw), lambda i: (i, 0)),
                      pl.BlockSpec((tm, 2 * FOURIER_WIDTH), lambda i: (i, 0)),
                      pl.BlockSpec((tm, POOL_WIDTH), lambda i: (i, 0)),
                      pl.BlockSpec((tm, N_BRANCH * d), lambda i: (i, 0))]
    return pl.pallas_call(
        functools.partial(_in_kernel, d, full),
        out_shape=out_shape,
        grid=(t // tm,),
        in_specs=in_specs,
        out_specs=out_specs,
        compiler_params=_params(("arbitrary",)),
        name="in_proj" if full else "ctx_kv_proj",
    )(x2d, mod, lw["n1"], wall, lw["qg"], lw["wq"], lw["wqp"], lw["kvg"], lw["wk"], lw["wv"],
      lw["fc"], lw["bg"], cq, sq, ck, sk)


def _attn_kernel(has_ctx, q_ref, k_ref, v_ref, *rest):
    if has_ctx:
        kc_ref, vc_ref, o_ref = rest
    else:
        (o_ref,) = rest
    outs = []
    for hh in range(2):
        sl = slice(hh * HEAD_PAD, (hh + 1) * HEAD_PAD)
        q = q_ref[:, sl]
        s = _dot_nt(q, k_ref[:, sl])
        mx = jnp.max(s, axis=-1, keepdims=True)
        if has_ctx:
            sc = _dot_nt(q, kc_ref[:, sl])
            mx = jnp.maximum(mx, jnp.max(sc, axis=-1, keepdims=True))
        p = jnp.exp(s - mx)
        den = jnp.sum(p, axis=-1, keepdims=True)
        o = _dot(p.astype(BF16), v_ref[...])
        if has_ctx:
            pc = jnp.exp(sc - mx)
            den = den + jnp.sum(pc, axis=-1, keepdims=True)
            o = o + _dot(pc.astype(BF16), vc_ref[...])
        outs.append(o / den)
    lane = lax.broadcasted_iota(I32, outs[0].shape, 1)
    o_ref[...] = jnp.where(lane < V_HEAD, outs[0], outs[1]).astype(BF16)


def _attention(q, k, v, batch, seq_len, ctx_kv):
    t = q.shape[0]
    tq = min(ATT_Q_TILE, seq_len)
    nq = seq_len // tq
    pair_w = 2 * HEAD_PAD
    in_specs = [pl.BlockSpec((tq, pair_w), lambda b, j, i: (b * nq + i, j)),
                pl.BlockSpec((seq_len, pair_w), lambda b, j, i: (b, j)),
                pl.BlockSpec((seq_len, 2 * V_HEAD), lambda b, j, i: (b, j))]
    args = [q, k, v]
    if ctx_kv is not None:
        kc, vc = ctx_kv
        lc = kc.shape[0] // batch
        in_specs += [pl.BlockSpec((lc, pair_w), lambda b, j, i: (b, j)),
                     pl.BlockSpec((lc, 2 * V_HEAD), lambda b, j, i: (b, j))]
        args += [kc, vc]
    return pl.pallas_call(
        functools.partial(_attn_kernel, ctx_kv is not None),
        out_shape=jax.ShapeDtypeStruct((t, ATT_WIDTH), BF16),
        grid=(batch, N_HEADS // 2, nq),
        in_specs=in_specs,
        out_specs=pl.BlockSpec((tq, 2 * V_HEAD), lambda b, j, i: (b * nq + i, j)),
        compiler_params=_params(("arbitrary", "arbitrary", "arbitrary")),
        name="attention" if ctx_kv is not None else "ctx_attention",
    )(*args)


def _fourier_kernel(norm, c_ref, s_ref, ab_ref, o_ref):
    a = ab_ref[:, 0:FOURIER_WIDTH]
    b = ab_ref[:, FOURIER_WIDTH:2 * FOURIER_WIDTH]
    o = _dot(c_ref[...], a) - _dot(s_ref[...], b)
    o_ref[...] = (o * norm).astype(BF16)


def _fourier(ab, batch, seq_len, dft):
    t = ab.shape[0]
    tm = min(FOUR_TILE, seq_len)
    nt = seq_len // tm
    cl, sl = dft
    norm = 1.0 / math.sqrt(seq_len * FOURIER_WIDTH)
    return pl.pallas_call(
        functools.partial(_fourier_kernel, norm),
        out_shape=jax.ShapeDtypeStruct((t, FOURIER_WIDTH), BF16),
        grid=(nt, batch),
        in_specs=[pl.BlockSpec((tm, seq_len), lambda i, b: (i, 0)),
                  pl.BlockSpec((tm, seq_len), lambda i, b: (i, 0)),
                  pl.BlockSpec((seq_len, 2 * FOURIER_WIDTH), lambda i, b: (b, 0))],
        out_specs=pl.BlockSpec((tm, FOURIER_WIDTH), lambda i, b: (b * nt + i, 0)),
        compiler_params=_params(("arbitrary", "arbitrary")),
        name="fourier",
    )(cl, sl, ab)


def _pool_kernel(seq_len, z_ref, wg_ref, ps_ref, o_ref, pad_ref, s_ref):
    n = seq_len
    p = n + 2 * POOL_PAD
    z = z_ref[...]
    zeros = jnp.zeros((POOL_PAD, POOL_WIDTH), F32)
    pad_ref[0:POOL_PAD, :] = zeros
    pad_ref[POOL_PAD + n:p, :] = zeros
    pad_ref[POOL_PAD:POOL_PAD + n, :] = z
    s_ref[0:p - 1, :] = pad_ref[0:p - 1, :] + pad_ref[1:p, :]
    w2 = s_ref[POOL_PAD - 1:POOL_PAD - 1 + n, :]
    pad_ref[0:p - 3, :] = s_ref[0:p - 3, :] + s_ref[2:p - 1, :]
    w4 = pad_ref[POOL_PAD - 2:POOL_PAD - 2 + n, :]
    s_ref[0:p - 7, :] = pad_ref[0:p - 7, :] + pad_ref[4:p - 3, :]
    w8 = s_ref[POOL_PAD - 4:POOL_PAD - 4 + n, :]
    pad_ref[0:p - 15, :] = s_ref[0:p - 15, :] + s_ref[8:p - 7, :]
    w16 = pad_ref[0:n, :]
    pos = lax.broadcasted_iota(I32, (n, POOL_WIDTH), 0)
    grp = lax.broadcasted_iota(I32, (n, POOL_WIDTH), 1) // POOL_GROUP
    win = jnp.where(grp == 0, w2, jnp.where(grp == 1, w4, jnp.where(grp == 2, w8, w16)))
    half = jnp.where(grp == 0, 1, jnp.where(grp == 1, 2, jnp.where(grp == 2, 4, 8)))
    lo = jnp.maximum(pos - half, 0)
    hi = jnp.minimum(pos + half, n)
    cnt = (hi - lo).astype(F32)
    pooled = win / cnt - z
    y = _dot(pooled.astype(BF16), wg_ref[...])
    o_ref[...] = (y * ps_ref[...]).astype(BF16)


def _pool(zp, batch, seq_len, wg_bd, pool_scale):
    t = zp.shape[0]
    return pl.pallas_call(
        functools.partial(_pool_kernel, seq_len),
        out_shape=jax.ShapeDtypeStruct((t, POOL_WIDTH), BF16),
        grid=(batch,),
        in_specs=[pl.BlockSpec((seq_len, POOL_WIDTH), lambda b: (b, 0)),
                  pl.BlockSpec((POOL_WIDTH, POOL_WIDTH), lambda b: (0, 0)),
                  pl.BlockSpec((1, POOL_WIDTH), lambda b: (0, 0))],
        out_specs=pl.BlockSpec((seq_len, POOL_WIDTH), lambda b: (b, 0)),
        scratch_shapes=[pltpu.VMEM((seq_len + 2 * POOL_PAD, POOL_WIDTH), F32),
                        pltpu.VMEM((seq_len + 2 * POOL_PAD, POOL_WIDTH), F32)],
        compiler_params=_params(("arbitrary",)),
        name="pool",
    )(zp, wg_bd, pool_scale)


def _out_kernel(d, x_ref, att_ref, four_ref, pool_ref, g_ref, m_ref, woa_ref, wob_ref, woc_ref,
                wout_ref, n2_ref, wpq_ref, x1_ref, h2_ref, pq_ref):
    m = m_ref[0]
    ya = _dot(att_ref[...], woa_ref[...])
    yb = _dot(four_ref[...], wob_ref[...])
    yc = _dot(pool_ref[...], woc_ref[...])
    mixp = (g_ref[:, 0:d].astype(F32) * ya + g_ref[:, d:2 * d].astype(F32) * yb
            + g_ref[:, 2 * d:3 * d].astype(F32) * yc)
    mix = _dot(mixp.astype(BF16), wout_ref[...])
    x1 = x_ref[...] + m[:, 2 * d:3 * d] * mix
    x1_ref[...] = x1
    h2 = _rms(x1) * n2_ref[...] * (1.0 + m[:, 4 * d:5 * d]) + m[:, 3 * d:4 * d]
    h2_ref[...] = h2
    pq = _dot(h2.astype(BF16), wpq_ref[...])
    for hp in range(2 * PEER_HEADS):
        pq_ref[hp] = pq[:, hp * PEER_HALF:(hp + 1) * PEER_HALF].astype(BF16)


def _out_proj(x2d, att, four, pool, g, mod, mod_row, seq_len, lw):
    t, d = x2d.shape
    tm = min(TOK_TILE, seq_len)
    tiles_per_seq = seq_len // tm

    def const(shape):
        return pl.BlockSpec(shape, lambda i: (0,) * len(shape))

    def row(w):
        return pl.BlockSpec((tm, w), lambda i: (i, 0))

    nhp = 2 * PEER_HEADS
    return pl.pallas_call(
        functools.partial(_out_kernel, d),
        out_shape=[jax.ShapeDtypeStruct((t, d), F32), jax.ShapeDtypeStruct((t, d), F32),
                   jax.ShapeDtypeStruct((nhp, t, PEER_HALF), BF16)],
        grid=(t // tm,),
        in_specs=[row(d), row(ATT_WIDTH), row(FOURIER_WIDTH), row(POOL_WIDTH), row(N_BRANCH * d),
                  pl.BlockSpec((1, 1, mod.shape[-1]), lambda i: (mod_row(i, tiles_per_seq), 0, 0)),
                  const(lw["woa"].shape), const(lw["wob"].shape), const(lw["woc"].shape),
                  const(lw["wout"].shape), const((1, d)), const(lw["wpq"].shape)],
        out_specs=[row(d), row(d), pl.BlockSpec((nhp, tm, PEER_HALF), lambda i: (0, i, 0))],
        compiler_params=_params(("arbitrary",)),
        name="out_proj",
    )(x2d, att, four, pool, g, mod, lw["woa"], lw["wob"], lw["woc"], lw["wout"], lw["n2"],
      lw["wpq"])


def _top_rounds(s, iota, n, emit):
    neg = jnp.float32(-jnp.inf)
    for r in range(PEER_TOPK):
        mx = jnp.max(s, axis=0, keepdims=True)
        idx = jnp.min(jnp.where(s == mx, iota, n), axis=0, keepdims=True)
        hit = iota == idx
        emit(r, mx, idx, hit)
        s = jnp.where(hit, neg, s)


def _route_kernel(pq_ref, keys_ref, eidx_ref, gate_ref, sv_ref, si_ref, cand_ref, cidx_ref,
                  ts_ref):
    tm = pq_ref.shape[1]
    iota_k = lax.broadcasted_iota(I32, (N_KEYS, tm), 0)
    iota_c = lax.broadcasted_iota(I32, (PEER_TOPK * PEER_TOPK, tm), 0)

    def head(hd, carry):
        for p in range(2):
            s = _dot_nt(keys_ref[2 * hd + p], pq_ref[2 * hd + p])

            def emit(r, mx, idx, hit, p=p):
                sv_ref[p, r:r + 1, :] = mx
                si_ref[p, r:r + 1, :] = idx

            _top_rounds(s, iota_k, N_KEYS, emit)
        for i in range(PEER_TOPK):
            rows = slice(i * PEER_TOPK, (i + 1) * PEER_TOPK)
            cand_ref[rows, :] = sv_ref[0, i:i + 1, :] + sv_ref[1]
            cidx_ref[rows, :] = si_ref[0, i:i + 1, :] * N_KEYS + si_ref[1]
        cidx = cidx_ref[...]
        base = pl.multiple_of(hd * PEER_TOPK, PEER_TOPK)

        def emit2(r, mx, idx, hit):
            ts_ref[r:r + 1, :] = mx
            eidx_ref[pl.ds(base + r, 1), :] = jnp.sum(jnp.where(hit, cidx, 0), axis=0,
                                                      keepdims=True)

        _top_rounds(cand_ref[...], iota_c, PEER_TOPK * PEER_TOPK, emit2)
        ts = ts_ref[...]
        ex = jnp.exp(ts - ts[0:1, :])
        gate_ref[pl.ds(base, PEER_TOPK), :] = ex / jnp.sum(ex, axis=0, keepdims=True)
        return carry

    lax.fori_loop(0, PEER_HEADS, head, 0)


def _route(pq, keys):
    nhp, t, _ = pq.shape
    tm = ROUTE_TILE
    return pl.pallas_call(
        _route_kernel,
        out_shape=[jax.ShapeDtypeStruct((PEER_SLOTS, t), I32),
                   jax.ShapeDtypeStruct((PEER_SLOTS, t), F32)],
        grid=(t // tm,),
        in_specs=[pl.BlockSpec((nhp, tm, PEER_HALF), lambda i: (0, i, 0)),
                  pl.BlockSpec((nhp, N_KEYS, PEER_HALF), lambda i: (0, 0, 0))],
        out_specs=[pl.BlockSpec((PEER_SLOTS, tm), lambda i: (0, i)),
                   pl.BlockSpec((PEER_SLOTS, tm), lambda i: (0, i))],
        scratch_shapes=[pltpu.VMEM((2, PEER_TOPK, tm), F32), pltpu.VMEM((2, PEER_TOPK, tm), I32),
                        pltpu.VMEM((PEER_TOPK * PEER_TOPK, tm), F32),
                        pltpu.VMEM((PEER_TOPK * PEER_TOPK, tm), I32),
                        pltpu.VMEM((PEER_TOPK, tm), F32)],
        compiler_params=_params(("arbitrary",)),
        name="peer_route",
    )(pq, keys)


def _fold_rows(a, b, keep_a, shift):
    return jnp.where(keep_a, a + pltpu.roll(a, SUBLANES - shift, 0), b + pltpu.roll(b, shift, 0))


def _peer_kernel(final, idx_ref, idxn_ref, gate_ref, h_ref, x_ref, g2_ref, fg_ref, tab_ref,
                 o_ref, buf_a, buf_b, act_ref, sem):
    i = pl.program_id(0)
    n = pl.num_programs(0)
    rows = PEER_TOK * PEER_SLOTS
    share = -(-rows // PEER_ISSUE_SPREAD)
    sub = lax.broadcasted_iota(I32, (SUBLANES, LANES), 0)
    keep = {sh: (sub & sh) == 0 for sh in (4, 2, 1)}
    g2 = g2_ref[0, 0]

    def row_copy(ids, t, k, buf, s, r):
        return pltpu.make_async_copy(tab_ref.at[ids[t, k]], buf.at[r], sem.at[s])

    def issue_share(ids, t0, buf, s, j):
        for r in range(j * share, min((j + 1) * share, rows)):
            row_copy(ids, t0 + r // PEER_SLOTS, r % PEER_SLOTS, buf, s, r).start(priority=r % 2)

    def wait_buf(buf, s):
        pltpu.make_async_copy(tab_ref.at[pl.ds(0, rows)], buf, sem.at[s]).wait()

    def token(buf, tl, tb):
        r0 = tl * PEER_SLOTS
        hv = h_ref[tb]
        groups = []
        for g in range(PEER_SLOTS // SUBLANES):
            p = [buf[r0 + g * SUBLANES + j, 0:SUBLANES, :] * hv for j in range(SUBLANES)]
            for sh in (4, 2, 1):
                half = len(p) // 2
                p = [_fold_rows(p[j], p[j + half], keep[sh], sh) for j in range(half)]
            groups.append(p[0])
        part = jnp.concatenate(groups, axis=0)
        a = jnp.sum(part, axis=1, keepdims=True)
        act = gate_ref[0][:, tb:tb + 1] * jax.nn.gelu(a)
        act_ref[...] = jnp.broadcast_to(act, (PEER_SLOTS, LANES))
        accs = [jnp.zeros((SUBLANES, LANES), F32) for _ in range(4)]
        for k in range(PEER_SLOTS):
            accs[k % 4] = accs[k % 4] + act_ref[k:k + 1, :] * buf[r0 + k, SUBLANES:2 * SUBLANES, :]
        xr = x_ref[tb] + g2 * ((accs[0] + accs[1]) + (accs[2] + accs[3]))
        if final:
            ms = jnp.sum(jnp.sum(xr * xr, axis=1, keepdims=True), axis=0, keepdims=True)
            xr = xr * lax.rsqrt(ms / (SUBLANES * LANES) + EPS) * fg_ref[...]
        o_ref[tb] = xr

    @pl.when(i == 0)
    def _():
        def body(t, carry):
            for k in range(PEER_SLOTS):
                row_copy(idx_ref, t, k, buf_a, 0, t * PEER_SLOTS + k).start(priority=k % 2)
            return carry
        lax.fori_loop(0, PEER_TOK, body, 0)

    wait_buf(buf_a, 0)
    for t in range(PEER_TOK):
        if t < PEER_ISSUE_SPREAD:
            issue_share(idx_ref, PEER_TOK, buf_b, 1, t)
        token(buf_a, t, t)
    wait_buf(buf_b, 1)
    for t in range(PEER_TOK):
        if t < PEER_ISSUE_SPREAD:
            issue_share(idxn_ref, 0, buf_a, 0, t)
        token(buf_b, t, PEER_TOK + t)

    @pl.when(i == n - 1)
    def _():
        wait_buf(buf_a, 0)


def _peer(eidx, gate3, h3, x3, g2, mod_row, seq_len, fg, table, final):
    t = h3.shape[0]
    tb = 2 * PEER_TOK
    nb = t // tb
    steps_per_seq = seq_len // tb
    rows = PEER_TOK * PEER_SLOTS
    tile = (SUBLANES, LANES)
    pair = (2 * SUBLANES, LANES)
    return pl.pallas_call(
        functools.partial(_peer_kernel, final),
        out_shape=jax.ShapeDtypeStruct((t,) + tile, F32),
        grid=(nb,),
        in_specs=[pl.BlockSpec((tb, PEER_SLOTS), lambda i: (i, 0), memory_space=pltpu.SMEM),
                  pl.BlockSpec((tb, PEER_SLOTS), lambda i: (jnp.minimum(i + 1, nb - 1), 0),
                               memory_space=pltpu.SMEM),
                  pl.BlockSpec((1, PEER_SLOTS, tb), lambda i: (i, 0, 0)),
                  pl.BlockSpec((tb,) + tile, lambda i: (i, 0, 0)),
                  pl.BlockSpec((tb,) + tile, lambda i: (i, 0, 0)),
                  pl.BlockSpec((1, 1) + tile, lambda i: (mod_row(i, steps_per_seq), 0, 0, 0)),
                  pl.BlockSpec(tile, lambda i: (0, 0)),
                  pl.BlockSpec(memory_space=pl.ANY)],
        out_specs=pl.BlockSpec((tb,) + tile, lambda i: (i, 0, 0)),
        scratch_shapes=[pltpu.VMEM((rows,) + pair, F32), pltpu.VMEM((rows,) + pair, F32),
                        pltpu.VMEM((PEER_SLOTS, LANES), F32),
                        pltpu.SemaphoreType.DMA((2,))],
        compiler_params=pltpu.CompilerParams(dimension_semantics=("arbitrary",),
                                             vmem_limit_bytes=PEER_VMEM_LIMIT),
        name="peer_experts",
    )(eidx, eidx, gate3, h3, x3, g2, fg, table)


def _rope_perm(w):
    q = QK_ROPE // 4
    a1, a2, b1, b2 = (w[..., j * q:(j + 1) * q] for j in range(4))
    return jnp.concatenate([-a2, a1, -b2, b1], axis=-1)


def _prep_layer(l, d, w_in, b_gate, q_norm_g, w_uq, kv_norm_g, w_ukv, w_oa, w_ob, w_grp,
                pool_scale, w_oc, w_out, w_pq, norm1_g, norm2_g):
    wi = w_in[l]
    s0 = Q_LORA
    s1 = s0 + KV_LORA
    s2 = s1 + QK_ROPE
    s3 = s2 + FOURIER_WIDTH
    s4 = s3 + POOL_WIDTH
    w_kr = wi[:, s1:s2]
    zl = jnp.zeros((d, QK_NOPE), F32)
    zr = jnp.zeros((d, HEAD_PAD - QK_NOPE - QK_ROPE), F32)
    wall = jnp.concatenate([wi[:, 0:s1], wi[:, s2:], zl, w_kr, zr, zl, _rope_perm(w_kr), zr],
                           axis=1).astype(BF16)
    wq = w_uq[l].reshape(Q_LORA, N_HEADS, QK_NOPE + QK_ROPE)
    qpad = jnp.zeros((Q_LORA, N_HEADS, HEAD_PAD - QK_NOPE - QK_ROPE), F32)
    wq_full = jnp.concatenate([wq, qpad], axis=-1).reshape(Q_LORA, N_HEADS * HEAD_PAD)
    wq_perm = jnp.concatenate([jnp.zeros((Q_LORA, N_HEADS, QK_NOPE), F32),
                               _rope_perm(wq[..., QK_NOPE:]), qpad],
                              axis=-1).reshape(Q_LORA, N_HEADS * HEAD_PAD)
    wkv = w_ukv[l].reshape(KV_LORA, N_HEADS, QK_NOPE + V_HEAD)
    wk = jnp.concatenate([wkv[..., :QK_NOPE],
                          jnp.zeros((KV_LORA, N_HEADS, HEAD_PAD - QK_NOPE), F32)],
                         axis=-1).reshape(KV_LORA, N_HEADS * HEAD_PAD)
    wv = wkv[..., QK_NOPE:].reshape(KV_LORA, ATT_WIDTH)
    cidx = jnp.arange(FOURIER_WIDTH, dtype=I32)
    ang = (2.0 * math.pi / FOURIER_WIDTH) * ((cidx[:, None] * cidx[None, :]) % FOURIER_WIDTH
                                             ).astype(F32)
    fc = jnp.concatenate([jnp.cos(ang), jnp.sin(ang)], axis=1).astype(BF16)
    ng = len(POOL_WINDOWS)
    wg = jnp.zeros((ng, POOL_GROUP, ng, POOL_GROUP), F32)
    for gi in range(ng):
        wg = wg.at[gi, :, gi, :].set(w_grp[l, gi])
    return {
        "wall": wall, "n1": norm1_g[l][None, :], "n2": norm2_g[l][None, :],
        "qg": q_norm_g[l][None, :], "kvg": kv_norm_g[l][None, :],
        "wq": wq_full.astype(BF16), "wqp": wq_perm.astype(BF16),
        "wk": wk.astype(BF16), "wv": wv.astype(BF16), "fc": fc,
        "bg": b_gate[l][None, :],
        "wg": wg.reshape(POOL_WIDTH, POOL_WIDTH).astype(BF16),
        "ps": pool_scale[l][None, :],
        "woa": w_oa[l].astype(BF16), "wob": w_ob[l].astype(BF16), "woc": w_oc[l].astype(BF16),
        "wout": w_out[l].astype(BF16), "wpq": w_pq[l].astype(BF16),
    }


def _rope_tables(seq_len, rope):
    zeros_n = jnp.zeros((seq_len, QK_NOPE), F32)
    zeros_p = jnp.zeros((seq_len, HEAD_PAD - QK_NOPE - QK_ROPE), F32)
    ones_n = jnp.ones((seq_len, QK_NOPE), F32)
    if rope:
        pos = jnp.arange(seq_len, dtype=I32)
        half = QK_ROPE // 2
        inv_freq = ROPE_BASE ** (-jnp.arange(0, half, 2, dtype=F32) / half)
        ang_r = (pos // GRID_W).astype(F32)[:, None] * inv_freq
        ang_c = (pos % GRID_W).astype(F32)[:, None] * inv_freq
        cos = jnp.concatenate([jnp.cos(ang_r)] * 2 + [jnp.cos(ang_c)] * 2, axis=1)
        sin = jnp.concatenate([jnp.sin(ang_r)] * 2 + [jnp.sin(ang_c)] * 2, axis=1)
    else:
        cos = jnp.ones((seq_len, QK_ROPE), F32)
        sin = jnp.zeros((seq_len, QK_ROPE), F32)
    cq = jnp.concatenate([ones_n, cos, zeros_p], axis=1) * ATT_SCALE
    sq = jnp.concatenate([zeros_n, sin, zeros_p], axis=1) * ATT_SCALE
    ck = jnp.concatenate([zeros_n, cos, zeros_p], axis=1)
    sk = jnp.concatenate([zeros_n, sin, zeros_p], axis=1)
    return cq, sq, ck, sk


def _dft(seq_len):
    idx = jnp.arange(seq_len, dtype=I32)
    ang = (2.0 * math.pi / seq_len) * ((idx[:, None] * idx[None, :]) % seq_len).astype(F32)
    return jnp.cos(ang).astype(BF16), jnp.sin(ang).astype(BF16)


def _peer_block(x1, h2, pq, keys, mod_tiles, mod_row, seq_len, fg, table, final):
    t, d = x1.shape
    eidx_n, gate_n = _route(pq, keys)
    eidx = eidx_n.T
    gate3 = gate_n.reshape(PEER_SLOTS, t // (2 * PEER_TOK), 2 * PEER_TOK).transpose(1, 0, 2)
    tile = (SUBLANES, LANES)
    out = _peer(eidx, gate3, h2.reshape((t,) + tile), x1.reshape((t,) + tile), mod_tiles,
                mod_row, seq_len, fg, table, final)
    return out.reshape(t, d)


def kernel(x, c, ctx, c_ctx, w_mod, b_mod, norm1_g, norm2_g, w_in, b_gate, q_norm_g, w_uq,
           kv_norm_g, w_ukv, w_oa, w_ob, w_grp, pool_scale, w_oc, w_out, w_pq, peer_keys,
           peer_down, peer_up, final_g):
    batch, seq_len, d = x.shape
    ctx_len = ctx.shape[1]
    depth = w_mod.shape[0]
    assert d == SUBLANES * LANES and batch + 1 <= MOD_ROWS
    assert seq_len % TOK_TILE == 0 and ctx_len % (2 * PEER_TOK) == 0 and ctx_len % LANES == 0
    tile = (SUBLANES, LANES)

    cvec = jnp.concatenate([c, c_ctx[None, :], jnp.zeros((MOD_ROWS - batch - 1, d), F32)], axis=0)
    mod_all = _modulation(cvec, w_mod, b_mod)

    def x_row(i, per_seq):
        return i // per_seq

    def c_row(i, per_seq):
        return batch

    tabs_x = _rope_tables(seq_len, True)
    tabs_c = _rope_tables(ctx_len, False)
    dft_x = _dft(seq_len)
    dft_c = _dft(ctx_len)
    fg = final_g.reshape(tile)

    xs = x.reshape(batch * seq_len, d)
    cs = ctx.reshape(batch * ctx_len, d)
    for l in range(depth):
        last = l == depth - 1
        lw = _prep_layer(l, d, w_in, b_gate, q_norm_g, w_uq, kv_norm_g, w_ukv, w_oa, w_ob, w_grp,
                         pool_scale, w_oc, w_out, w_pq, norm1_g, norm2_g)
        mod = mod_all[l].reshape(MOD_ROWS, 1, 6 * d)
        mod_tiles = mod_all[l].reshape(MOD_ROWS, 6, SUBLANES, LANES)[:, 5:6]
        keys = peer_keys[l].reshape(2 * PEER_HEADS, N_KEYS, PEER_HALF).astype(BF16)
        table = jnp.concatenate([peer_down[l].reshape((-1,) + tile),
                                 peer_up[l].reshape((-1,) + tile)], axis=1)

        if last:
            kc, vc = _in_proj(cs, mod, c_row, ctx_len, lw, tabs_c, False)
        else:
            kc, vc, qc, abc, zpc, gc = _in_proj(cs, mod, c_row, ctx_len, lw, tabs_c, True)
        kx, vx, qx, abx, zpx, gx = _in_proj(xs, mod, x_row, seq_len, lw, tabs_x, True)

        att_x = _attention(qx, kx, vx, batch, seq_len, (kc, vc))
        four_x = _fourier(abx, batch, seq_len, dft_x)
        pool_x = _pool(zpx, batch, seq_len, lw["wg"], lw["ps"])
        x1, h2, pq = _out_proj(xs, att_x, four_x, pool_x, gx, mod, x_row, seq_len, lw)
        xs = _peer_block(x1, h2, pq, keys, mod_tiles, x_row, seq_len, fg, table, last)

        if not last:
            att_c = _attention(qc, kc, vc, batch, ctx_len, None)
            four_c = _fourier(abc, batch, ctx_len, dft_c)
            pool_c = _pool(zpc, batch, ctx_len, lw["wg"], lw["ps"])
            c1, hc2, pqc = _out_proj(cs, att_c, four_c, pool_c, gc, mod, c_row, ctx_len, lw)
            cs = _peer_block(c1, hc2, pqc, keys, mod_tiles, c_row, ctx_len, fg, table, False)
    return xs.reshape(batch, seq_len, d)
```

```python
import functools
import math

import jax
import jax.numpy as jnp
from jax import lax
from jax.experimental import pallas as pl
from jax.experimental.pallas import tpu as pltpu

F32 = jnp.float32
BF16 = jnp.bfloat16
I32 = jnp.int32

GRID_W = 64
N_HEADS = 8
Q_LORA = 256
KV_LORA = 128
QK_NOPE = 64
QK_ROPE = 32
V_HEAD = 64
ATT_WIDTH = N_HEADS * V_HEAD
ATT_SCALE = (QK_NOPE + QK_ROPE) ** -0.5
ROPE_BASE = 10000.0
FOURIER_WIDTH = 256
POOL_WINDOWS = (2, 4, 8, 16)
POOL_GROUP = 64
POOL_WIDTH = POOL_GROUP * len(POOL_WINDOWS)
N_BRANCH = 3
PEER_HEADS = 8
N_KEYS = 128
PEER_QDIM = 256
PEER_HALF = PEER_QDIM // 2
PEER_TOPK = 16
PEER_SLOTS = PEER_HEADS * PEER_TOPK
STAIR = tuple((i, PEER_TOPK // (i + 1)) for i in range(PEER_TOPK))
N_CAND = sum(cnt for _, cnt in STAIR)
CAND_ROWS = -(-N_CAND // 8) * 8
EPS = 1e-6

LANES = 128
SUBLANES = 8
HEAD_PAD = 128
POOL_PAD = 8
MOD_ROWS = 16
VMEM_LIMIT = 48 * 1024 * 1024

TOK_TILE = 256
ATT_Q_TILE = 256
FOUR_TILE = 512
ROUTE_TILE = 256
PEER_TOK = 16
PEER_ISSUE_SPREAD = 12
PEER_VMEM_LIMIT = 56 * 1024 * 1024

C_CQ = 0
C_CKV = C_CQ + Q_LORA
C_ZF = C_CKV + KV_LORA
C_ZP = C_ZF + FOURIER_WIDTH
C_ZG = C_ZP + POOL_WIDTH
C_KR = None


def _params(sem=None):
    return pltpu.CompilerParams(dimension_semantics=sem, vmem_limit_bytes=VMEM_LIMIT)


def _rms(x):
    return x * lax.rsqrt(jnp.mean(x * x, axis=-1, keepdims=True) + EPS)


def _dot(a, b):
    return jnp.dot(a, b, preferred_element_type=F32)


def _dot_nt(a, b):
    return lax.dot_general(a, b, (((1,), (1,)), ((), ())), preferred_element_type=F32)


def _mod_kernel(c_ref, w_ref, b_ref, o_ref):
    c = c_ref[...]
    s = c * jax.nn.sigmoid(c)
    o_ref[0] = jnp.dot(s, w_ref[0], preferred_element_type=F32,
                       precision=lax.Precision.HIGHEST) + b_ref[0]


def _modulation(cvec, w_mod, b_mod):
    depth, d, n = w_mod.shape
    tn = 1536
    return pl.pallas_call(
        _mod_kernel,
        out_shape=jax.ShapeDtypeStruct((depth, MOD_ROWS, n), F32),
        grid=(depth, n // tn),
        in_specs=[pl.BlockSpec((MOD_ROWS, d), lambda l, j: (0, 0)),
                  pl.BlockSpec((1, d, tn), lambda l, j: (l, 0, j)),
                  pl.BlockSpec((1, 1, tn), lambda l, j: (l, 0, j))],
        out_specs=pl.BlockSpec((1, MOD_ROWS, tn), lambda l, j: (l, 0, j)),
        compiler_params=_params(("arbitrary", "arbitrary")),
        name="modulation",
    )(cvec, w_mod, b_mod.reshape(depth, 1, n))


def _in_kernel(d, full, x_ref, m_ref, n1_ref, wall_ref, qg_ref, wq_ref, wqp_ref, kvg_ref,
               wk_ref, wv_ref, fc_ref, bg_ref, cq_ref, sq_ref, ck_ref, sk_ref, *outs):
    x = x_ref[...]
    m = m_ref[0]
    h = _rms(x) * n1_ref[...] * (1.0 + m[:, d:2 * d]) + m[:, 0:d]
    hb = h.astype(BF16)
    c_kr = C_ZG + N_BRANCH * d
    if full:
        z = _dot(hb, wall_ref[...])
        k_ref, v_ref, q_ref, ab_ref, zp_ref, g_ref = outs
    else:
        z = None
        k_ref, v_ref = outs
    def col(lo, hi):
        if full:
            return z[:, lo:hi]
        return _dot(hb, wall_ref[:, lo:hi])

    ckv = col(C_CKV, C_ZF)
    ckvn = (_rms(ckv) * kvg_ref[...]).astype(BF16)
    kf = _dot(ckvn, wk_ref[...])
    kr = col(c_kr, c_kr + LANES) * ck_ref[...] + col(c_kr + LANES, c_kr + 2 * LANES) * sk_ref[...]
    for hh in range(N_HEADS):
        sl = slice(hh * HEAD_PAD, (hh + 1) * HEAD_PAD)
        k_ref[:, sl] = (kf[:, sl] + kr).astype(BF16)
    v_ref[...] = _dot(ckvn, wv_ref[...]).astype(BF16)
    if not full:
        return
    cq = z[:, C_CQ:C_CKV]
    cqn = (_rms(cq) * qg_ref[...]).astype(BF16)
    qf = _dot(cqn, wq_ref[...])
    qr = _dot(cqn, wqp_ref[...])
    cosq = cq_ref[...]
    sinq = sq_ref[...]
    for hh in range(N_HEADS):
        sl = slice(hh * HEAD_PAD, (hh + 1) * HEAD_PAD)
        q_ref[:, sl] = (qf[:, sl] * cosq + qr[:, sl] * sinq).astype(BF16)
    ab_ref[...] = _dot(z[:, C_ZF:C_ZP].astype(BF16), fc_ref[...]).astype(BF16)
    zp_ref[...] = z[:, C_ZP:C_ZG]
    g_ref[...] = jax.nn.sigmoid(z[:, C_ZG:c_kr] + bg_ref[...]).astype(BF16)


def _in_proj(x2d, mod, mod_row, seq_len, lw, tabs, full):
    t, d = x2d.shape
    tm = min(TOK_TILE, seq_len)
    tiles_per_seq = seq_len // tm
    wall = lw["wall"]
    nw = wall.shape[1]
    cq, sq, ck, sk = tabs

    def const(shape):
        return pl.BlockSpec(shape, lambda i: (0,) * len(shape))

    def pos(i):
        return (i % tiles_per_seq, 0)

    in_specs = [
        pl.BlockSpec((tm, d), lambda i: (i, 0)),
        pl.BlockSpec((1, 1, mod.shape[-1]), lambda i: (mod_row(i, tiles_per_seq), 0, 0)),
        const((1, d)), const((d, nw)), const((1, Q_LORA)),
        const(lw["wq"].shape), const(lw["wqp"].shape), const((1, KV_LORA)),
        const(lw["wk"].shape), const(lw["wv"].shape), const(lw["fc"].shape),
        const((1, N_BRANCH * d)),
        pl.BlockSpec((tm, LANES), pos), pl.BlockSpec((tm, LANES), pos),
        pl.BlockSpec((tm, LANES), pos), pl.BlockSpec((tm, LANES), pos),
    ]
    kw = N_HEADS * HEAD_PAD
    out_shape = [jax.ShapeDtypeStruct((t, kw), BF16), jax.ShapeDtypeStruct((t, ATT_WIDTH), BF16)]
    out_specs = [pl.BlockSpec((tm, kw), lambda i: (i, 0)),
                 pl.BlockSpec((tm, ATT_WIDTH), lambda i: (i, 0))]
    if full:
        out_shape += [jax.ShapeDtypeStruct((t, kw), BF16),
                      jax.ShapeDtypeStruct((t, 2 * FOURIER_WIDTH), BF16),
                      jax.ShapeDtypeStruct((t, POOL_WIDTH), F32),
                      jax.ShapeDtypeStruct((t, N_BRANCH * d), BF16)]
        out_specs += [pl.BlockSpec((tm, kw), lambda i: (i, 0)),
                      pl.BlockSpec((tm, 2 * FOURIER_WIDTH), lambda i: (i, 0)),
                      pl.BlockSpec((tm, POOL_WIDTH), lambda i: (i, 0)),
                      pl.BlockSpec((tm, N_BRANCH * d), lambda i: (i, 0))]
    return pl.pallas_call(
        functools.partial(_in_kernel, d, full),
        out_shape=out_shape,
        grid=(t // tm,),
        in_specs=in_specs,
        out_specs=out_specs,
        compiler_params=_params(("arbitrary",)),
        name="in_proj" if full else "ctx_kv_proj",
    )(x2d, mod, lw["n1"], wall, lw["qg"], lw["wq"], lw["wqp"], lw["kvg"], lw["wk"], lw["wv"],
      lw["fc"], lw["bg"], cq, sq, ck, sk)


def _attn_kernel(has_ctx, q_ref, k_ref, v_ref, *rest):
    if has_ctx:
        kc_ref, vc_ref, o_ref = rest
    else:
        (o_ref,) = rest
    outs = []
    for hh in range(2):
        sl = slice(hh * HEAD_PAD, (hh + 1) * HEAD_PAD)
        q = q_ref[:, sl]
        s = _dot_nt(q, k_ref[:, sl])
        mx = jnp.max(s, axis=-1, keepdims=True)
        if has_ctx:
            sc = _dot_nt(q, kc_ref[:, sl])
            mx = jnp.maximum(mx, jnp.max(sc, axis=-1, keepdims=True))
        p = jnp.exp(s - mx)
        den = jnp.sum(p, axis=-1, keepdims=True)
        o = _dot(p.astype(BF16), v_ref[...])
        if has_ctx:
            pc = jnp.exp(sc - mx)
            den = den + jnp.sum(pc, axis=-1, keepdims=True)
            o = o + _dot(pc.astype(BF16), vc_ref[...])
        outs.append(o / den)
    lane = lax.broadcasted_iota(I32, outs[0].shape, 1)
    o_ref[...] = jnp.where(lane < V_HEAD, outs[0], outs[1]).astype(BF16)


def _attention(q, k, v, batch, seq_len, ctx_kv):
    t = q.shape[0]
    tq = min(ATT_Q_TILE, seq_len)
    nq = seq_len // tq
    pair_w = 2 * HEAD_PAD
    in_specs = [pl.BlockSpec((tq, pair_w), lambda b, j, i: (b * nq + i, j)),
                pl.BlockSpec((seq_len, pair_w), lambda b, j, i: (b, j)),
                pl.BlockSpec((seq_len, 2 * V_HEAD), lambda b, j, i: (b, j))]
    args = [q, k, v]
    if ctx_kv is not None:
        kc, vc = ctx_kv
        lc = kc.shape[0] // batch
        in_specs += [pl.BlockSpec((lc, pair_w), lambda b, j, i: (b, j)),
                     pl.BlockSpec((lc, 2 * V_HEAD), lambda b, j, i: (b, j))]
        args += [kc, vc]
    return pl.pallas_call(
        functools.partial(_attn_kernel, ctx_kv is not None),
        out_shape=jax.ShapeDtypeStruct((t, ATT_WIDTH), BF16),
        grid=(batch, N_HEADS // 2, nq),
        in_specs=in_specs,
        out_specs=pl.BlockSpec((tq, 2 * V_HEAD), lambda b, j, i: (b * nq + i, j)),
        compiler_params=_params(("arbitrary", "arbitrary", "arbitrary")),
        name="attention" if ctx_kv is not None else "ctx_attention",
    )(*args)


def _fourier_kernel(norm, c_ref, s_ref, ab_ref, o_ref):
    a = ab_ref[:, 0:FOURIER_WIDTH]
    b = ab_ref[:, FOURIER_WIDTH:2 * FOURIER_WIDTH]
    o = _dot(c_ref[...], a) - _dot(s_ref[...], b)
    o_ref[...] = (o * norm).astype(BF16)


def _fourier(ab, batch, seq_len, dft):
    t = ab.shape[0]
    tm = min(FOUR_TILE, seq_len)
    nt = seq_len // tm
    cl, sl = dft
    norm = 1.0 / math.sqrt(seq_len * FOURIER_WIDTH)
    return pl.pallas_call(
        functools.partial(_fourier_kernel, norm),
        out_shape=jax.ShapeDtypeStruct((t, FOURIER_WIDTH), BF16),
        grid=(nt, batch),
        in_specs=[pl.BlockSpec((tm, seq_len), lambda i, b: (i, 0)),
                  pl.BlockSpec((tm, seq_len), lambda i, b: (i, 0)),
                  pl.BlockSpec((seq_len, 2 * FOURIER_WIDTH), lambda i, b: (b, 0))],
        out_specs=pl.BlockSpec((tm, FOURIER_WIDTH), lambda i, b: (b * nt + i, 0)),
        compiler_params=_params(("arbitrary", "arbitrary")),
        name="fourier",
    )(cl, sl, ab)


def _pool_kernel(seq_len, z_ref, wg_ref, ps_ref, o_ref, pad_ref, s_ref):
    n = seq_len
    p = n + 2 * POOL_PAD
    z = z_ref[...]
    zeros = jnp.zeros((POOL_PAD, POOL_WIDTH), F32)
    pad_ref[0:POOL_PAD, :] = zeros
    pad_ref[POOL_PAD + n:p, :] = zeros
    pad_ref[POOL_PAD:POOL_PAD + n, :] = z
    s_ref[0:p - 1, :] = pad_ref[0:p - 1, :] + pad_ref[1:p, :]
    w2 = s_ref[POOL_PAD - 1:POOL_PAD - 1 + n, :]
    pad_ref[0:p - 3, :] = s_ref[0:p - 3, :] + s_ref[2:p - 1, :]
    w4 = pad_ref[POOL_PAD - 2:POOL_PAD - 2 + n, :]
    s_ref[0:p - 7, :] = pad_ref[0:p - 7, :] + pad_ref[4:p - 3, :]
    w8 = s_ref[POOL_PAD - 4:POOL_PAD - 4 + n, :]
    pad_ref[0:p - 15, :] = s_ref[0:p - 15, :] + s_ref[8:p - 7, :]
    w16 = pad_ref[0:n, :]
    pos = lax.broadcasted_iota(I32, (n, POOL_WIDTH), 0)
    grp = lax.broadcasted_iota(I32, (n, POOL_WIDTH), 1) // POOL_GROUP
    win = jnp.where(grp == 0, w2, jnp.where(grp == 1, w4, jnp.where(grp == 2, w8, w16)))
    half = jnp.where(grp == 0, 1, jnp.where(grp == 1, 2, jnp.where(grp == 2, 4, 8)))
    lo = jnp.maximum(pos - half, 0)
    hi = jnp.minimum(pos + half, n)
    cnt = (hi - lo).astype(F32)
    pooled = win / cnt - z
    y = _dot(pooled.astype(BF16), wg_ref[...])
    o_ref[...] = (y * ps_ref[...]).astype(BF16)


def _pool(zp, batch, seq_len, wg_bd, pool_scale):
    t = zp.shape[0]
    return pl.pallas_call(
        functools.partial(_pool_kernel, seq_len),
        out_shape=jax.ShapeDtypeStruct((t, POOL_WIDTH), BF16),
        grid=(batch,),
        in_specs=[pl.BlockSpec((seq_len, POOL_WIDTH), lambda b: (b, 0)),
                  pl.BlockSpec((POOL_WIDTH, POOL_WIDTH), lambda b: (0, 0)),
                  pl.BlockSpec((1, POOL_WIDTH), lambda b: (0, 0))],
        out_specs=pl.BlockSpec((seq_len, POOL_WIDTH), lambda b: (b, 0)),
        scratch_shapes=[pltpu.VMEM((seq_len + 2 * POOL_PAD, POOL_WIDTH), F32),
                        pltpu.VMEM((seq_len + 2 * POOL_PAD, POOL_WIDTH), F32)],
        compiler_params=_params(("arbitrary",)),
        name="pool",
    )(zp, wg_bd, pool_scale)


def _out_kernel(d, x_ref, att_ref, four_ref, pool_ref, g_ref, m_ref, woa_ref, wob_ref, woc_ref,
                wout_ref, n2_ref, wpq_ref, x1_ref, h2_ref, pq_ref):
    m = m_ref[0]
    ya = _dot(att_ref[...], woa_ref[...])
    yb = _dot(four_ref[...], wob_ref[...])
    yc = _dot(pool_ref[...], woc_ref[...])
    mixp = (g_ref[:, 0:d].astype(F32) * ya + g_ref[:, d:2 * d].astype(F32) * yb
            + g_ref[:, 2 * d:3 * d].astype(F32) * yc)
    mix = _dot(mixp.astype(BF16), wout_ref[...])
    x1 = x_ref[...] + m[:, 2 * d:3 * d] * mix
    x1_ref[...] = x1
    h2 = _rms(x1) * n2_ref[...] * (1.0 + m[:, 4 * d:5 * d]) + m[:, 3 * d:4 * d]
    h2_ref[...] = h2
    pq = _dot(h2.astype(BF16), wpq_ref[...])
    for hp in range(2 * PEER_HEADS):
        pq_ref[hp] = pq[:, hp * PEER_HALF:(hp + 1) * PEER_HALF].astype(BF16)


def _out_proj(x2d, att, four, pool, g, mod, mod_row, seq_len, lw):
    t, d = x2d.shape
    tm = min(TOK_TILE, seq_len)
    tiles_per_seq = seq_len // tm

    def const(shape):
        return pl.BlockSpec(shape, lambda i: (0,) * len(shape))

    def row(w):
        return pl.BlockSpec((tm, w), lambda i: (i, 0))

    nhp = 2 * PEER_HEADS
    return pl.pallas_call(
        functools.partial(_out_kernel, d),
        out_shape=[jax.ShapeDtypeStruct((t, d), F32), jax.ShapeDtypeStruct((t, d), F32),
                   jax.ShapeDtypeStruct((nhp, t, PEER_HALF), BF16)],
        grid=(t // tm,),
        in_specs=[row(d), row(ATT_WIDTH), row(FOURIER_WIDTH), row(POOL_WIDTH), row(N_BRANCH * d),
                  pl.BlockSpec((1, 1, mod.shape[-1]), lambda i: (mod_row(i, tiles_per_seq), 0, 0)),
                  const(lw["woa"].shape), const(lw["wob"].shape), const(lw["woc"].shape),
                  const(lw["wout"].shape), const((1, d)), const(lw["wpq"].shape)],
        out_specs=[row(d), row(d), pl.BlockSpec((nhp, tm, PEER_HALF), lambda i: (0, i, 0))],
        compiler_params=_params(("arbitrary",)),
        name="out_proj",
    )(x2d, att, four, pool, g, mod, lw["woa"], lw["wob"], lw["woc"], lw["wout"], lw["n2"],
      lw["wpq"])


def _top_rounds(s, iota, n, emit):
    neg = jnp.float32(-jnp.inf)
    for r in range(PEER_TOPK):
        mx = jnp.max(s, axis=0, keepdims=True)
        idx = jnp.min(jnp.where(s == mx, iota, n), axis=0, keepdims=True)
        hit = iota == idx
        emit(r, mx, idx, hit)
        s = jnp.where(hit, neg, s)


def _route_kernel(pq_ref, keys_ref, eidx_ref, gate_ref, sv_ref, si_ref, cand_ref, cidx_ref,
                  ts_ref):
    tm = pq_ref.shape[1]
    iota_k = lax.broadcasted_iota(I32, (N_KEYS, tm), 0)
    iota_c = lax.broadcasted_iota(I32, (CAND_ROWS, tm), 0)
    cand_ref[N_CAND:CAND_ROWS, :] = jnp.full((CAND_ROWS - N_CAND, tm), -jnp.inf, F32)
    cidx_ref[N_CAND:CAND_ROWS, :] = jnp.zeros((CAND_ROWS - N_CAND, tm), I32)

    def head(hd, carry):
        for p in range(2):
            s = _dot_nt(keys_ref[2 * hd + p], pq_ref[2 * hd + p])

            def emit(r, mx, idx, hit, p=p):
                sv_ref[p, r:r + 1, :] = mx
                si_ref[p, r:r + 1, :] = idx

            _top_rounds(s, iota_k, N_KEYS, emit)
        off = 0
        for i, cnt in STAIR:
            cand_ref[off:off + cnt, :] = sv_ref[0, i:i + 1, :] + sv_ref[1, 0:cnt, :]
            cidx_ref[off:off + cnt, :] = si_ref[0, i:i + 1, :] * N_KEYS + si_ref[1, 0:cnt, :]
            off += cnt
        cidx = cidx_ref[...]
        base = pl.multiple_of(hd * PEER_TOPK, PEER_TOPK)

        def emit2(r, mx, idx, hit):
            ts_ref[r:r + 1, :] = mx
            eidx_ref[pl.ds(base + r, 1), :] = jnp.sum(jnp.where(hit, cidx, 0), axis=0,
                                                      keepdims=True)

        _top_rounds(cand_ref[...], iota_c, CAND_ROWS, emit2)
        ts = ts_ref[...]
        ex = jnp.exp(ts - ts[0:1, :])
        gate_ref[pl.ds(base, PEER_TOPK), :] = ex / jnp.sum(ex, axis=0, keepdims=True)
        return carry

    lax.fori_loop(0, PEER_HEADS, head, 0)


def _route(pq, keys):
    nhp, t, _ = pq.shape
    tm = ROUTE_TILE if t % ROUTE_TILE == 0 else LANES
    return pl.pallas_call(
        _route_kernel,
        out_shape=[jax.ShapeDtypeStruct((PEER_SLOTS, t), I32),
                   jax.ShapeDtypeStruct((PEER_SLOTS, t), F32)],
        grid=(t // tm,),
        in_specs=[pl.BlockSpec((nhp, tm, PEER_HALF), lambda i: (0, i, 0)),
                  pl.BlockSpec((nhp, N_KEYS, PEER_HALF), lambda i: (0, 0, 0))],
        out_specs=[pl.BlockSpec((PEER_SLOTS, tm), lambda i: (0, i)),
                   pl.BlockSpec((PEER_SLOTS, tm), lambda i: (0, i))],
        scratch_shapes=[pltpu.VMEM((2, PEER_TOPK, tm), F32), pltpu.VMEM((2, PEER_TOPK, tm), I32),
                        pltpu.VMEM((CAND_ROWS, tm), F32),
                        pltpu.VMEM((CAND_ROWS, tm), I32),
                        pltpu.VMEM((PEER_TOPK, tm), F32)],
        compiler_params=_params(("arbitrary",)),
        name="peer_route",
    )(pq, keys)


def _fold_rows(a, b, keep_a, shift):
    return jnp.where(keep_a, a + pltpu.roll(a, SUBLANES - shift, 0), b + pltpu.roll(b, shift, 0))


def _peer_kernel(final, idx_ref, idxn_ref, gate_ref, h_ref, x_ref, g2_ref, fg_ref, tab_ref,
                 o_ref, buf_a, buf_b, act_ref, sem):
    i = pl.program_id(0)
    n = pl.num_programs(0)
    rows = PEER_TOK * PEER_SLOTS
    share = -(-rows // PEER_ISSUE_SPREAD)
    sub = lax.broadcasted_iota(I32, (SUBLANES, LANES), 0)
    keep = {sh: (sub & sh) == 0 for sh in (4, 2, 1)}
    g2 = g2_ref[0, 0]

    def row_copy(ids, t, k, buf, s, r):
        return pltpu.make_async_copy(tab_ref.at[ids[t, k]], buf.at[r], sem.at[s])

    def issue_share(ids, t0, buf, s, j):
        for r in range(j * share, min((j + 1) * share, rows)):
            row_copy(ids, t0 + r // PEER_SLOTS, r % PEER_SLOTS, buf, s, r).start(priority=r % 2)

    def wait_buf(buf, s):
        pltpu.make_async_copy(tab_ref.at[pl.ds(0, rows)], buf, sem.at[s]).wait()

    def token(buf, tl, tb):
        r0 = tl * PEER_SLOTS
        hv = h_ref[tb]
        groups = []
        for g in range(PEER_SLOTS // SUBLANES):
            p = [buf[r0 + g * SUBLANES + j].astype(F32)[0:SUBLANES] * hv for j in range(SUBLANES)]
            for sh in (4, 2, 1):
                half = len(p) // 2
                p = [_fold_rows(p[j], p[j + half], keep[sh], sh) for j in range(half)]
            groups.append(p[0])
        part = jnp.concatenate(groups, axis=0)
        a = jnp.sum(part, axis=1, keepdims=True)
        act = gate_ref[0][:, tb:tb + 1] * jax.nn.gelu(a)
        act_ref[...] = jnp.broadcast_to(act, (PEER_SLOTS, LANES))
        accs = [jnp.zeros((SUBLANES, LANES), F32) for _ in range(4)]
        for k in range(PEER_SLOTS):
            up = buf[r0 + k].astype(F32)[SUBLANES:2 * SUBLANES]
            accs[k % 4] = accs[k % 4] + act_ref[k:k + 1, :] * up
        xr = x_ref[tb] + g2 * ((accs[0] + accs[1]) + (accs[2] + accs[3]))
        if final:
            ms = jnp.sum(jnp.sum(xr * xr, axis=1, keepdims=True), axis=0, keepdims=True)
            xr = xr * lax.rsqrt(ms / (SUBLANES * LANES) + EPS) * fg_ref[...]
        o_ref[tb] = xr

    @pl.when(i == 0)
    def _():
        def body(t, carry):
            for k in range(PEER_SLOTS):
                row_copy(idx_ref, t, k, buf_a, 0, t * PEER_SLOTS + k).start(priority=k % 2)
            return carry
        lax.fori_loop(0, PEER_TOK, body, 0)

    wait_buf(buf_a, 0)
    for t in range(PEER_TOK):
        if t < PEER_ISSUE_SPREAD:
            issue_share(idx_ref, PEER_TOK, buf_b, 1, t)
        token(buf_a, t, t)
    wait_buf(buf_b, 1)
    for t in range(PEER_TOK):
        if t < PEER_ISSUE_SPREAD:
            issue_share(idxn_ref, 0, buf_a, 0, t)
        token(buf_b, t, PEER_TOK + t)

    @pl.when(i == n - 1)
    def _():
        wait_buf(buf_a, 0)


def _peer(eidx, gate3, h3, x3, g2, mod_row, seq_len, fg, table, final):
    t = h3.shape[0]
    tb = 2 * PEER_TOK
    nb = t // tb
    steps_per_seq = seq_len // tb
    rows = PEER_TOK * PEER_SLOTS
    tile = (SUBLANES, LANES)
    pair = (2 * SUBLANES, LANES)
    return pl.pallas_call(
        functools.partial(_peer_kernel, final),
        out_shape=jax.ShapeDtypeStruct((t,) + tile, F32),
        grid=(nb,),
        in_specs=[pl.BlockSpec((tb, PEER_SLOTS), lambda i: (i, 0), memory_space=pltpu.SMEM),
                  pl.BlockSpec((tb, PEER_SLOTS), lambda i: (jnp.minimum(i + 1, nb - 1), 0),
                               memory_space=pltpu.SMEM),
                  pl.BlockSpec((1, PEER_SLOTS, tb), lambda i: (i, 0, 0)),
                  pl.BlockSpec((tb,) + tile, lambda i: (i, 0, 0)),
                  pl.BlockSpec((tb,) + tile, lambda i: (i, 0, 0)),
                  pl.BlockSpec((1, 1) + tile, lambda i: (mod_row(i, steps_per_seq), 0, 0, 0)),
                  pl.BlockSpec(tile, lambda i: (0, 0)),
                  pl.BlockSpec(memory_space=pl.ANY)],
        out_specs=pl.BlockSpec((tb,) + tile, lambda i: (i, 0, 0)),
        scratch_shapes=[pltpu.VMEM((rows,) + pair, BF16), pltpu.VMEM((rows,) + pair, BF16),
                        pltpu.VMEM((PEER_SLOTS, LANES), F32),
                        pltpu.SemaphoreType.DMA((2,))],
        compiler_params=pltpu.CompilerParams(dimension_semantics=("arbitrary",),
                                             vmem_limit_bytes=PEER_VMEM_LIMIT),
        name="peer_experts",
    )(eidx, eidx, gate3, h3, x3, g2, fg, table)


def _rope_perm(w):
    q = QK_ROPE // 4
    a1, a2, b1, b2 = (w[..., j * q:(j + 1) * q] for j in range(4))
    return jnp.concatenate([-a2, a1, -b2, b1], axis=-1)


def _prep_layer(l, d, w_in, b_gate, q_norm_g, w_uq, kv_norm_g, w_ukv, w_oa, w_ob, w_grp,
                pool_scale, w_oc, w_out, w_pq, norm1_g, norm2_g):
    wi = w_in[l]
    s0 = Q_LORA
    s1 = s0 + KV_LORA
    s2 = s1 + QK_ROPE
    s3 = s2 + FOURIER_WIDTH
    s4 = s3 + POOL_WIDTH
    w_kr = wi[:, s1:s2]
    zl = jnp.zeros((d, QK_NOPE), F32)
    zr = jnp.zeros((d, HEAD_PAD - QK_NOPE - QK_ROPE), F32)
    wall = jnp.concatenate([wi[:, 0:s1], wi[:, s2:], zl, w_kr, zr, zl, _rope_perm(w_kr), zr],
                           axis=1).astype(BF16)
    wq = w_uq[l].reshape(Q_LORA, N_HEADS, QK_NOPE + QK_ROPE)
    qpad = jnp.zeros((Q_LORA, N_HEADS, HEAD_PAD - QK_NOPE - QK_ROPE), F32)
    wq_full = jnp.concatenate([wq, qpad], axis=-1).reshape(Q_LORA, N_HEADS * HEAD_PAD)
    wq_perm = jnp.concatenate([jnp.zeros((Q_LORA, N_HEADS, QK_NOPE), F32),
                               _rope_perm(wq[..., QK_NOPE:]), qpad],
                              axis=-1).reshape(Q_LORA, N_HEADS * HEAD_PAD)
    wkv = w_ukv[l].reshape(KV_LORA, N_HEADS, QK_NOPE + V_HEAD)
    wk = jnp.concatenate([wkv[..., :QK_NOPE],
                          jnp.zeros((KV_LORA, N_HEADS, HEAD_PAD - QK_NOPE), F32)],
                         axis=-1).reshape(KV_LORA, N_HEADS * HEAD_PAD)
    wv = wkv[..., QK_NOPE:].reshape(KV_LORA, ATT_WIDTH)
    cidx = jnp.arange(FOURIER_WIDTH, dtype=I32)
    ang = (2.0 * math.pi / FOURIER_WIDTH) * ((cidx[:, None] * cidx[None, :]) % FOURIER_WIDTH
                                             ).astype(F32)
    fc = jnp.concatenate([jnp.cos(ang), jnp.sin(ang)], axis=1).astype(BF16)
    ng = len(POOL_WINDOWS)
    wg = jnp.zeros((ng, POOL_GROUP, ng, POOL_GROUP), F32)
    for gi in range(ng):
        wg = wg.at[gi, :, gi, :].set(w_grp[l, gi])
    return {
        "wall": wall, "n1": norm1_g[l][None, :], "n2": norm2_g[l][None, :],
        "qg": q_norm_g[l][None, :], "kvg": kv_norm_g[l][None, :],
        "wq": wq_full.astype(BF16), "wqp": wq_perm.astype(BF16),
        "wk": wk.astype(BF16), "wv": wv.astype(BF16), "fc": fc,
        "bg": b_gate[l][None, :],
        "wg": wg.reshape(POOL_WIDTH, POOL_WIDTH).astype(BF16),
        "ps": pool_scale[l][None, :],
        "woa": w_oa[l].astype(BF16), "wob": w_ob[l].astype(BF16), "woc": w_oc[l].astype(BF16),
        "wout": w_out[l].astype(BF16), "wpq": w_pq[l].astype(BF16),
    }


def _rope_tables(seq_len, rope):
    zeros_n = jnp.zeros((seq_len, QK_NOPE), F32)
    zeros_p = jnp.zeros((seq_len, HEAD_PAD - QK_NOPE - QK_ROPE), F32)
    ones_n = jnp.ones((seq_len, QK_NOPE), F32)
    if rope:
        pos = jnp.arange(seq_len, dtype=I32)
        half = QK_ROPE // 2
        inv_freq = ROPE_BASE ** (-jnp.arange(0, half, 2, dtype=F32) / half)
        ang_r = (pos // GRID_W).astype(F32)[:, None] * inv_freq
        ang_c = (pos % GRID_W).astype(F32)[:, None] * inv_freq
        cos = jnp.concatenate([jnp.cos(ang_r)] * 2 + [jnp.cos(ang_c)] * 2, axis=1)
        sin = jnp.concatenate([jnp.sin(ang_r)] * 2 + [jnp.sin(ang_c)] * 2, axis=1)
    else:
        cos = jnp.ones((seq_len, QK_ROPE), F32)
        sin = jnp.zeros((seq_len, QK_ROPE), F32)
    cq = jnp.concatenate([ones_n, cos, zeros_p], axis=1) * ATT_SCALE
    sq = jnp.concatenate([zeros_n, sin, zeros_p], axis=1) * ATT_SCALE
    ck = jnp.concatenate([zeros_n, cos, zeros_p], axis=1)
    sk = jnp.concatenate([zeros_n, sin, zeros_p], axis=1)
    return cq, sq, ck, sk


def _dft(seq_len):
    idx = jnp.arange(seq_len, dtype=I32)
    ang = (2.0 * math.pi / seq_len) * ((idx[:, None] * idx[None, :]) % seq_len).astype(F32)
    return jnp.cos(ang).astype(BF16), jnp.sin(ang).astype(BF16)


def _peer_block(x1, h2, pq, keys, mod_tiles, mod_row, seq_len, fg, table, final):
    t, d = x1.shape
    eidx_n, gate_n = _route(pq, keys)
    eidx = eidx_n.T
    gate3 = gate_n.reshape(PEER_SLOTS, t // (2 * PEER_TOK), 2 * PEER_TOK).transpose(1, 0, 2)
    tile = (SUBLANES, LANES)
    out = _peer(eidx, gate3, h2.reshape((t,) + tile), x1.reshape((t,) + tile), mod_tiles,
                mod_row, seq_len, fg, table, final)
    return out.reshape(t, d)


def kernel(x, c, ctx, c_ctx, w_mod, b_mod, norm1_g, norm2_g, w_in, b_gate, q_norm_g, w_uq,
           kv_norm_g, w_ukv, w_oa, w_ob, w_grp, pool_scale, w_oc, w_out, w_pq, peer_keys,
           peer_down, peer_up, final_g):
    batch, seq_len, d = x.shape
    ctx_len = ctx.shape[1]
    depth = w_mod.shape[0]
    assert d == SUBLANES * LANES and batch + 1 <= MOD_ROWS
    assert seq_len % TOK_TILE == 0 and ctx_len % (2 * PEER_TOK) == 0 and ctx_len % LANES == 0
    tile = (SUBLANES, LANES)

    cvec = jnp.concatenate([c, c_ctx[None, :], jnp.zeros((MOD_ROWS - batch - 1, d), F32)], axis=0)
    mod_all = _modulation(cvec, w_mod, b_mod)

    def x_row(i, per_seq):
        return i // per_seq

    def c_row(i, per_seq):
        return batch

    tabs_x = _rope_tables(seq_len, True)
    tabs_c = _rope_tables(ctx_len, False)
    dft_x = _dft(seq_len)
    dft_c = _dft(ctx_len)
    fg = final_g.reshape(tile)

    xs = x.reshape(batch * seq_len, d)
    cs = ctx.reshape(batch * ctx_len, d)
    for l in range(depth):
        last = l == depth - 1
        lw = _prep_layer(l, d, w_in, b_gate, q_norm_g, w_uq, kv_norm_g, w_ukv, w_oa, w_ob, w_grp,
                         pool_scale, w_oc, w_out, w_pq, norm1_g, norm2_g)
        mod = mod_all[l].reshape(MOD_ROWS, 1, 6 * d)
        mod_tiles = mod_all[l].reshape(MOD_ROWS, 6, SUBLANES, LANES)[:, 5:6]
        keys = peer_keys[l].reshape(2 * PEER_HEADS, N_KEYS, PEER_HALF).astype(BF16)
        table = jnp.concatenate([peer_down[l].reshape((-1,) + tile),
                                 peer_up[l].reshape((-1,) + tile)], axis=1).astype(BF16)

        if last:
            kc, vc = _in_proj(cs, mod, c_row, ctx_len, lw, tabs_c, False)
        else:
            kc, vc, qc, abc, zpc, gc = _in_proj(cs, mod, c_row, ctx_len, lw, tabs_c, True)
        kx, vx, qx, abx, zpx, gx = _in_proj(xs, mod, x_row, seq_len, lw, tabs_x, True)

        att_x = _attention(qx, kx, vx, batch, seq_len, (kc, vc))
        four_x = _fourier(abx, batch, seq_len, dft_x)
        pool_x = _pool(zpx, batch, seq_len, lw["wg"], lw["ps"])
        x1, h2, pq = _out_proj(xs, att_x, four_x, pool_x, gx, mod, x_row, seq_len, lw)
        xs = _peer_block(x1, h2, pq, keys, mod_tiles, x_row, seq_len, fg, table, last)

        if not last:
            att_c = _attention(qc, kc, vc, batch, ctx_len, None)
            four_c = _fourier(abc, batch, ctx_len, dft_c)
            pool_c = _pool(zpc, batch, ctx_len, lw["wg"], lw["ps"])
            c1, hc2, pqc = _out_proj(cs, att_c, four_c, pool_c, gc, mod, c_row, ctx_len, lw)
            cs = _peer_block(c1, hc2, pqc, keys, mod_tiles, c_row, ctx_len, fg, table, False)
    return xs.reshape(batch, seq_len, d)
```

```python
import functools
import math

import jax
import jax.numpy as jnp
from jax import lax
from jax.experimental import pallas as pl
from jax.experimental.pallas import tpu as pltpu

F32 = jnp.float32
BF16 = jnp.bfloat16
I32 = jnp.int32

GRID_W = 64
N_HEADS = 8
Q_LORA = 256
KV_LORA = 128
QK_NOPE = 64
QK_ROPE = 32
V_HEAD = 64
ATT_WIDTH = N_HEADS * V_HEAD
ATT_SCALE = (QK_NOPE + QK_ROPE) ** -0.5
ROPE_BASE = 10000.0
FOURIER_WIDTH = 256
POOL_WINDOWS = (2, 4, 8, 16)
POOL_GROUP = 64
POOL_WIDTH = POOL_GROUP * len(POOL_WINDOWS)
N_BRANCH = 3
PEER_HEADS = 8
N_KEYS = 128
PEER_QDIM = 256
PEER_HALF = PEER_QDIM // 2
PEER_TOPK = 16
PEER_SLOTS = PEER_HEADS * PEER_TOPK
STAIR = tuple((i, PEER_TOPK // (i + 1)) for i in range(PEER_TOPK))
N_CAND = sum(cnt for _, cnt in STAIR)
CAND_ROWS = -(-N_CAND // 8) * 8
EPS = 1e-6

LANES = 128
SUBLANES = 8
HEAD_PAD = 128
POOL_PAD = 8
MOD_ROWS = 16
VMEM_LIMIT = 48 * 1024 * 1024

TOK_TILE = 256
ATT_Q_TILE = 256
FOUR_TILE = 512
ROUTE_TILE = 256
PEER_TOK = 16
PEER_WAIT_GROUPS = 4
PEER_VMEM_LIMIT = 56 * 1024 * 1024

C_CQ = 0
C_CKV = C_CQ + Q_LORA
C_ZF = C_CKV + KV_LORA
C_ZP = C_ZF + FOURIER_WIDTH
C_ZG = C_ZP + POOL_WIDTH
C_KR = None


def _params(sem=None):
    return pltpu.CompilerParams(dimension_semantics=sem, vmem_limit_bytes=VMEM_LIMIT)


def _rms(x):
    return x * lax.rsqrt(jnp.mean(x * x, axis=-1, keepdims=True) + EPS)


def _dot(a, b):
    return jnp.dot(a, b, preferred_element_type=F32)


def _dot_nt(a, b):
    return lax.dot_general(a, b, (((1,), (1,)), ((), ())), preferred_element_type=F32)


def _mod_kernel(c_ref, w_ref, b_ref, o_ref):
    c = c_ref[...]
    s = c * jax.nn.sigmoid(c)
    o_ref[0] = jnp.dot(s, w_ref[0], preferred_element_type=F32,
                       precision=lax.Precision.HIGHEST) + b_ref[0]


def _modulation(cvec, w_mod, b_mod):
    depth, d, n = w_mod.shape
    tn = 1536
    return pl.pallas_call(
        _mod_kernel,
        out_shape=jax.ShapeDtypeStruct((depth, MOD_ROWS, n), F32),
        grid=(depth, n // tn),
        in_specs=[pl.BlockSpec((MOD_ROWS, d), lambda l, j: (0, 0)),
                  pl.BlockSpec((1, d, tn), lambda l, j: (l, 0, j)),
                  pl.BlockSpec((1, 1, tn), lambda l, j: (l, 0, j))],
        out_specs=pl.BlockSpec((1, MOD_ROWS, tn), lambda l, j: (l, 0, j)),
        compiler_params=_params(("arbitrary", "arbitrary")),
        name="modulation",
    )(cvec, w_mod, b_mod.reshape(depth, 1, n))


def _in_kernel(d, full, x_ref, m_ref, n1_ref, wall_ref, qg_ref, wq_ref, wqp_ref, kvg_ref,
               wk_ref, wv_ref, fc_ref, bg_ref, cq_ref, sq_ref, ck_ref, sk_ref, *outs):
    x = x_ref[...]
    m = m_ref[0]
    h = _rms(x) * n1_ref[...] * (1.0 + m[:, d:2 * d]) + m[:, 0:d]
    hb = h.astype(BF16)
    c_kr = C_ZG + N_BRANCH * d
    if full:
        z = _dot(hb, wall_ref[...])
        k_ref, v_ref, q_ref, ab_ref, zp_ref, g_ref = outs
    else:
        z = None
        k_ref, v_ref = outs
    def col(lo, hi):
        if full:
            return z[:, lo:hi]
        return _dot(hb, wall_ref[:, lo:hi])

    ckv = col(C_CKV, C_ZF)
    ckvn = (_rms(ckv) * kvg_ref[...]).astype(BF16)
    kf = _dot(ckvn, wk_ref[...])
    kr = col(c_kr, c_kr + LANES) * ck_ref[...] + col(c_kr + LANES, c_kr + 2 * LANES) * sk_ref[...]
    for hh in range(N_HEADS):
        sl = slice(hh * HEAD_PAD, (hh + 1) * HEAD_PAD)
        k_ref[:, sl] = (kf[:, sl] + kr).astype(BF16)
    v_ref[...] = _dot(ckvn, wv_ref[...]).astype(BF16)
    if not full:
        return
    cq = z[:, C_CQ:C_CKV]
    cqn = (_rms(cq) * qg_ref[...]).astype(BF16)
    qf = _dot(cqn, wq_ref[...])
    qr = _dot(cqn, wqp_ref[...])
    cosq = cq_ref[...]
    sinq = sq_ref[...]
    for hh in range(N_HEADS):
        sl = slice(hh * HEAD_PAD, (hh + 1) * HEAD_PAD)
        q_ref[:, sl] = (qf[:, sl] * cosq + qr[:, sl] * sinq).astype(BF16)
    ab_ref[...] = _dot(z[:, C_ZF:C_ZP].astype(BF16), fc_ref[...]).astype(BF16)
    zp_ref[...] = z[:, C_ZP:C_ZG]
    g_ref[...] = jax.nn.sigmoid(z[:, C_ZG:c_kr] + bg_ref[...]).astype(BF16)


def _in_proj(x2d, mod, mod_row, seq_len, lw, tabs, full):
    t, d = x2d.shape
    tm = min(TOK_TILE, seq_len)
    tiles_per_seq = seq_len // tm
    wall = lw["wall"]
    nw = wall.shape[1]
    cq, sq, ck, sk = tabs

    def const(shape):
        return pl.BlockSpec(shape, lambda i: (0,) * len(shape))

    def pos(i):
        return (i % tiles_per_seq, 0)

    in_specs = [
        pl.BlockSpec((tm, d), lambda i: (i, 0)),
        pl.BlockSpec((1, 1, mod.shape[-1]), lambda i: (mod_row(i, tiles_per_seq), 0, 0)),
        const((1, d)), const((d, nw)), const((1, Q_LORA)),
        const(lw["wq"].shape), const(lw["wqp"].shape), const((1, KV_LORA)),
        const(lw["wk"].shape), const(lw["wv"].shape), const(lw["fc"].shape),
        const((1, N_BRANCH * d)),
        pl.BlockSpec((tm, LANES), pos), pl.BlockSpec((tm, LANES), pos),
        pl.BlockSpec((tm, LANES), pos), pl.BlockSpec((tm, LANES), pos),
    ]
    kw = N_HEADS * HEAD_PAD
    out_shape = [jax.ShapeDtypeStruct((t, kw), BF16), jax.ShapeDtypeStruct((t, ATT_WIDTH), BF16)]
    out_specs = [pl.BlockSpec((tm, kw), lambda i: (i, 0)),
                 pl.BlockSpec((tm, ATT_WIDTH), lambda i: (i, 0))]
    if full:
        out_shape += [jax.ShapeDtypeStruct((t, kw), BF16),
                      jax.ShapeDtypeStruct((t, 2 * FOURIER_WIDTH), BF16),
                      jax.ShapeDtypeStruct((t, POOL_WIDTH), F32),
                      jax.ShapeDtypeStruct((t, N_BRANCH * d), BF16)]
        out_specs += [pl.BlockSpec((tm, kw), lambda i: (i, 0)),
                      pl.BlockSpec((tm, 2 * FOURIER_WIDTH), lambda i: (i, 0)),
                      pl.BlockSpec((tm, POOL_WIDTH), lambda i: (i, 0)),
                      pl.BlockSpec((tm, N_BRANCH * d), lambda i: (i, 0))]
    return pl.pallas_call(
        functools.partial(_in_kernel, d, full),
        out_shape=out_shape,
        grid=(t // tm,),
        in_specs=in_specs,
        out_specs=out_specs,
        compiler_params=_params(("arbitrary",)),
        name="in_proj" if full else "ctx_kv_proj",
    )(x2d, mod, lw["n1"], wall, lw["qg"], lw["wq"], lw["wqp"], lw["kvg"], lw["wk"], lw["wv"],
      lw["fc"], lw["bg"], cq, sq, ck, sk)


def _attn_kernel(has_ctx, q_ref, k_ref, v_ref, *rest):
    if has_ctx:
        kc_ref, vc_ref, o_ref = rest
    else:
        (o_ref,) = rest
    outs = []
    for hh in range(2):
        sl = slice(hh * HEAD_PAD, (hh + 1) * HEAD_PAD)
        q = q_ref[:, sl]
        s = _dot_nt(q, k_ref[:, sl])
        mx = jnp.max(s, axis=-1, keepdims=True)
        if has_ctx:
            sc = _dot_nt(q, kc_ref[:, sl])
            mx = jnp.maximum(mx, jnp.max(sc, axis=-1, keepdims=True))
        p = jnp.exp(s - mx)
        den = jnp.sum(p, axis=-1, keepdims=True)
        o = _dot(p.astype(BF16), v_ref[...])
        if has_ctx:
            pc = jnp.exp(sc - mx)
            den = den + jnp.sum(pc, axis=-1, keepdims=True)
            o = o + _dot(pc.astype(BF16), vc_ref[...])
        outs.append(o / den)
    lane = lax.broadcasted_iota(I32, outs[0].shape, 1)
    o_ref[...] = jnp.where(lane < V_HEAD, outs[0], outs[1]).astype(BF16)


def _attention(q, k, v, batch, seq_len, ctx_kv):
    t = q.shape[0]
    tq = min(ATT_Q_TILE, seq_len)
    nq = seq_len // tq
    pair_w = 2 * HEAD_PAD
    in_specs = [pl.BlockSpec((tq, pair_w), lambda b, j, i: (b * nq + i, j)),
                pl.BlockSpec((seq_len, pair_w), lambda b, j, i: (b, j)),
                pl.BlockSpec((seq_len, 2 * V_HEAD), lambda b, j, i: (b, j))]
    args = [q, k, v]
    if ctx_kv is not None:
        kc, vc = ctx_kv
        lc = kc.shape[0] // batch
        in_specs += [pl.BlockSpec((lc, pair_w), lambda b, j, i: (b, j)),
                     pl.BlockSpec((lc, 2 * V_HEAD), lambda b, j, i: (b, j))]
        args += [kc, vc]
    return pl.pallas_call(
        functools.partial(_attn_kernel, ctx_kv is not None),
        out_shape=jax.ShapeDtypeStruct((t, ATT_WIDTH), BF16),
        grid=(batch, N_HEADS // 2, nq),
        in_specs=in_specs,
        out_specs=pl.BlockSpec((tq, 2 * V_HEAD), lambda b, j, i: (b * nq + i, j)),
        compiler_params=_params(("arbitrary", "arbitrary", "arbitrary")),
        name="attention" if ctx_kv is not None else "ctx_attention",
    )(*args)


def _fourier_kernel(norm, c_ref, s_ref, ab_ref, o_ref):
    a = ab_ref[:, 0:FOURIER_WIDTH]
    b = ab_ref[:, FOURIER_WIDTH:2 * FOURIER_WIDTH]
    o = _dot(c_ref[...], a) - _dot(s_ref[...], b)
    o_ref[...] = (o * norm).astype(BF16)


def _fourier(ab, batch, seq_len, dft):
    t = ab.shape[0]
    tm = min(FOUR_TILE, seq_len)
    nt = seq_len // tm
    cl, sl = dft
    norm = 1.0 / math.sqrt(seq_len * FOURIER_WIDTH)
    return pl.pallas_call(
        functools.partial(_fourier_kernel, norm),
        out_shape=jax.ShapeDtypeStruct((t, FOURIER_WIDTH), BF16),
        grid=(nt, batch),
        in_specs=[pl.BlockSpec((tm, seq_len), lambda i, b: (i, 0)),
                  pl.BlockSpec((tm, seq_len), lambda i, b: (i, 0)),
                  pl.BlockSpec((seq_len, 2 * FOURIER_WIDTH), lambda i, b: (b, 0))],
        out_specs=pl.BlockSpec((tm, FOURIER_WIDTH), lambda i, b: (b * nt + i, 0)),
        compiler_params=_params(("arbitrary", "arbitrary")),
        name="fourier",
    )(cl, sl, ab)


def _pool_kernel(seq_len, z_ref, wg_ref, ps_ref, o_ref, pad_ref, s_ref):
    n = seq_len
    p = n + 2 * POOL_PAD
    z = z_ref[...]
    zeros = jnp.zeros((POOL_PAD, POOL_WIDTH), F32)
    pad_ref[0:POOL_PAD, :] = zeros
    pad_ref[POOL_PAD + n:p, :] = zeros
    pad_ref[POOL_PAD:POOL_PAD + n, :] = z
    s_ref[0:p - 1, :] = pad_ref[0:p - 1, :] + pad_ref[1:p, :]
    w2 = s_ref[POOL_PAD - 1:POOL_PAD - 1 + n, :]
    pad_ref[0:p - 3, :] = s_ref[0:p - 3, :] + s_ref[2:p - 1, :]
    w4 = pad_ref[POOL_PAD - 2:POOL_PAD - 2 + n, :]
    s_ref[0:p - 7, :] = pad_ref[0:p - 7, :] + pad_ref[4:p - 3, :]
    w8 = s_ref[POOL_PAD - 4:POOL_PAD - 4 + n, :]
    pad_ref[0:p - 15, :] = s_ref[0:p - 15, :] + s_ref[8:p - 7, :]
    w16 = pad_ref[0:n, :]
    pos = lax.broadcasted_iota(I32, (n, POOL_WIDTH), 0)
    grp = lax.broadcasted_iota(I32, (n, POOL_WIDTH), 1) // POOL_GROUP
    win = jnp.where(grp == 0, w2, jnp.where(grp == 1, w4, jnp.where(grp == 2, w8, w16)))
    half = jnp.where(grp == 0, 1, jnp.where(grp == 1, 2, jnp.where(grp == 2, 4, 8)))
    lo = jnp.maximum(pos - half, 0)
    hi = jnp.minimum(pos + half, n)
    cnt = (hi - lo).astype(F32)
    pooled = win / cnt - z
    y = _dot(pooled.astype(BF16), wg_ref[...])
    o_ref[...] = (y * ps_ref[...]).astype(BF16)


def _pool(zp, batch, seq_len, wg_bd, pool_scale):
    t = zp.shape[0]
    return pl.pallas_call(
        functools.partial(_pool_kernel, seq_len),
        out_shape=jax.ShapeDtypeStruct((t, POOL_WIDTH), BF16),
        grid=(batch,),
        in_specs=[pl.BlockSpec((seq_len, POOL_WIDTH), lambda b: (b, 0)),
                  pl.BlockSpec((POOL_WIDTH, POOL_WIDTH), lambda b: (0, 0)),
                  pl.BlockSpec((1, POOL_WIDTH), lambda b: (0, 0))],
        out_specs=pl.BlockSpec((seq_len, POOL_WIDTH), lambda b: (b, 0)),
        scratch_shapes=[pltpu.VMEM((seq_len + 2 * POOL_PAD, POOL_WIDTH), F32),
                        pltpu.VMEM((seq_len + 2 * POOL_PAD, POOL_WIDTH), F32)],
        compiler_params=_params(("arbitrary",)),
        name="pool",
    )(zp, wg_bd, pool_scale)


def _out_kernel(d, x_ref, att_ref, four_ref, pool_ref, g_ref, m_ref, woa_ref, wob_ref, woc_ref,
                wout_ref, n2_ref, wpq_ref, x1_ref, h2_ref, pq_ref):
    m = m_ref[0]
    ya = _dot(att_ref[...], woa_ref[...])
    yb = _dot(four_ref[...], wob_ref[...])
    yc = _dot(pool_ref[...], woc_ref[...])
    mixp = (g_ref[:, 0:d].astype(F32) * ya + g_ref[:, d:2 * d].astype(F32) * yb
            + g_ref[:, 2 * d:3 * d].astype(F32) * yc)
    mix = _dot(mixp.astype(BF16), wout_ref[...])
    x1 = x_ref[...] + m[:, 2 * d:3 * d] * mix
    x1_ref[...] = x1
    h2 = _rms(x1) * n2_ref[...] * (1.0 + m[:, 4 * d:5 * d]) + m[:, 3 * d:4 * d]
    h2_ref[...] = h2
    pq = _dot(h2.astype(BF16), wpq_ref[...])
    for hp in range(2 * PEER_HEADS):
        pq_ref[hp] = pq[:, hp * PEER_HALF:(hp + 1) * PEER_HALF].astype(BF16)


def _out_proj(x2d, att, four, pool, g, mod, mod_row, seq_len, lw):
    t, d = x2d.shape
    tm = min(TOK_TILE, seq_len)
    tiles_per_seq = seq_len // tm

    def const(shape):
        return pl.BlockSpec(shape, lambda i: (0,) * len(shape))

    def row(w):
        return pl.BlockSpec((tm, w), lambda i: (i, 0))

    nhp = 2 * PEER_HEADS
    return pl.pallas_call(
        functools.partial(_out_kernel, d),
        out_shape=[jax.ShapeDtypeStruct((t, d), F32), jax.ShapeDtypeStruct((t, d), F32),
                   jax.ShapeDtypeStruct((nhp, t, PEER_HALF), BF16)],
        grid=(t // tm,),
        in_specs=[row(d), row(ATT_WIDTH), row(FOURIER_WIDTH), row(POOL_WIDTH), row(N_BRANCH * d),
                  pl.BlockSpec((1, 1, mod.shape[-1]), lambda i: (mod_row(i, tiles_per_seq), 0, 0)),
                  const(lw["woa"].shape), const(lw["wob"].shape), const(lw["woc"].shape),
                  const(lw["wout"].shape), const((1, d)), const(lw["wpq"].shape)],
        out_specs=[row(d), row(d), pl.BlockSpec((nhp, tm, PEER_HALF), lambda i: (0, i, 0))],
        compiler_params=_params(("arbitrary",)),
        name="out_proj",
    )(x2d, att, four, pool, g, mod, lw["woa"], lw["wob"], lw["woc"], lw["wout"], lw["n2"],
      lw["wpq"])


def _top_rounds(s, iota, n, emit):
    neg = jnp.float32(-jnp.inf)
    for r in range(PEER_TOPK):
        mx = jnp.max(s, axis=0, keepdims=True)
        idx = jnp.min(jnp.where(s == mx, iota, n), axis=0, keepdims=True)
        hit = iota == idx
        emit(r, mx, idx, hit)
        s = jnp.where(hit, neg, s)


def _route_kernel(pq_ref, keys_ref, eidx_ref, gate_ref, sv_ref, si_ref, cand_ref, cidx_ref,
                  ts_ref):
    tm = pq_ref.shape[1]
    iota_k = lax.broadcasted_iota(I32, (N_KEYS, tm), 0)
    iota_c = lax.broadcasted_iota(I32, (CAND_ROWS, tm), 0)
    cand_ref[N_CAND:CAND_ROWS, :] = jnp.full((CAND_ROWS - N_CAND, tm), -jnp.inf, F32)
    cidx_ref[N_CAND:CAND_ROWS, :] = jnp.zeros((CAND_ROWS - N_CAND, tm), I32)

    def head(hd, carry):
        for p in range(2):
            s = _dot_nt(keys_ref[2 * hd + p], pq_ref[2 * hd + p])

            def emit(r, mx, idx, hit, p=p):
                sv_ref[p, r:r + 1, :] = mx
                si_ref[p, r:r + 1, :] = idx

            _top_rounds(s, iota_k, N_KEYS, emit)
        off = 0
        for i, cnt in STAIR:
            cand_ref[off:off + cnt, :] = sv_ref[0, i:i + 1, :] + sv_ref[1, 0:cnt, :]
            cidx_ref[off:off + cnt, :] = si_ref[0, i:i + 1, :] * N_KEYS + si_ref[1, 0:cnt, :]
            off += cnt
        cidx = cidx_ref[...]
        base = pl.multiple_of(hd * PEER_TOPK, PEER_TOPK)

        def emit2(r, mx, idx, hit):
            ts_ref[r:r + 1, :] = mx
            eidx_ref[pl.ds(base + r, 1), :] = jnp.sum(jnp.where(hit, cidx, 0), axis=0,
                                                      keepdims=True)

        _top_rounds(cand_ref[...], iota_c, CAND_ROWS, emit2)
        ts = ts_ref[...]
        ex = jnp.exp(ts - ts[0:1, :])
        gate_ref[pl.ds(base, PEER_TOPK), :] = ex / jnp.sum(ex, axis=0, keepdims=True)
        return carry

    lax.fori_loop(0, PEER_HEADS, head, 0)


def _route(pq, keys):
    nhp, t, _ = pq.shape
    tm = ROUTE_TILE if t % ROUTE_TILE == 0 else LANES
    return pl.pallas_call(
        _route_kernel,
        out_shape=[jax.ShapeDtypeStruct((PEER_SLOTS, t), I32),
                   jax.ShapeDtypeStruct((PEER_SLOTS, t), F32)],
        grid=(t // tm,),
        in_specs=[pl.BlockSpec((nhp, tm, PEER_HALF), lambda i: (0, i, 0)),
                  pl.BlockSpec((nhp, N_KEYS, PEER_HALF), lambda i: (0, 0, 0))],
        out_specs=[pl.BlockSpec((PEER_SLOTS, tm), lambda i: (0, i)),
                   pl.BlockSpec((PEER_SLOTS, tm), lambda i: (0, i))],
        scratch_shapes=[pltpu.VMEM((2, PEER_TOPK, tm), F32), pltpu.VMEM((2, PEER_TOPK, tm), I32),
                        pltpu.VMEM((CAND_ROWS, tm), F32),
                        pltpu.VMEM((CAND_ROWS, tm), I32),
                        pltpu.VMEM((PEER_TOPK, tm), F32)],
        compiler_params=_params(("arbitrary",)),
        name="peer_route",
    )(pq, keys)


def _fold_rows(a, b, keep_a, shift):
    return jnp.where(keep_a, a + pltpu.roll(a, SUBLANES - shift, 0), b + pltpu.roll(b, shift, 0))


def _peer_kernel(final, idx_ref, idxn_ref, gate_ref, h_ref, x_ref, g2_ref, fg_ref, tab_ref,
                 o_ref, buf_a, buf_b, act_ref, sem):
    i = pl.program_id(0)
    n = pl.num_programs(0)
    grp_tok = PEER_TOK // PEER_WAIT_GROUPS
    grp_rows = grp_tok * PEER_SLOTS
    sub = lax.broadcasted_iota(I32, (SUBLANES, LANES), 0)
    keep = {sh: (sub & sh) == 0 for sh in (4, 2, 1)}
    g2 = g2_ref[0, 0]

    def row_copy(ids, t_src, t_dst, k, buf, s):
        return pltpu.make_async_copy(tab_ref.at[ids[t_src, k]], buf.at[t_dst * PEER_SLOTS + k],
                                     sem.at[s, t_dst // grp_tok])

    def issue_token(ids, t_src, t_dst, buf, s):
        for k in range(PEER_SLOTS):
            row_copy(ids, t_src, t_dst, k, buf, s).start(priority=k % 2)

    def wait_group(buf, s, q):
        pltpu.make_async_copy(tab_ref.at[pl.ds(0, grp_rows)],
                              buf.at[pl.ds(q * grp_rows, grp_rows)], sem.at[s, q]).wait()

    def token(buf, tl, tb):
        r0 = tl * PEER_SLOTS
        hv = h_ref[tb]
        groups = []
        for g in range(PEER_SLOTS // SUBLANES):
            p = [buf[r0 + g * SUBLANES + j].astype(F32)[0:SUBLANES] * hv for j in range(SUBLANES)]
            for sh in (4, 2, 1):
                half = len(p) // 2
                p = [_fold_rows(p[j], p[j + half], keep[sh], sh) for j in range(half)]
            groups.append(p[0])
        part = jnp.concatenate(groups, axis=0)
        a = jnp.sum(part, axis=1, keepdims=True)
        act = gate_ref[0][:, tb:tb + 1] * jax.nn.gelu(a)
        act_ref[...] = jnp.broadcast_to(act, (PEER_SLOTS, LANES))
        accs = [jnp.zeros((SUBLANES, LANES), F32) for _ in range(4)]
        for k in range(PEER_SLOTS):
            up = buf[r0 + k].astype(F32)[SUBLANES:2 * SUBLANES]
            accs[k % 4] = accs[k % 4] + act_ref[k:k + 1, :] * up
        xr = x_ref[tb] + g2 * ((accs[0] + accs[1]) + (accs[2] + accs[3]))
        if final:
            ms = jnp.sum(jnp.sum(xr * xr, axis=1, keepdims=True), axis=0, keepdims=True)
            xr = xr * lax.rsqrt(ms / (SUBLANES * LANES) + EPS) * fg_ref[...]
        o_ref[tb] = xr

    @pl.when(i == 0)
    def _():
        def body(t, carry):
            issue_token(idx_ref, t, t, buf_a, 0)
            return carry
        lax.fori_loop(0, PEER_TOK, body, 0)

    for t in range(PEER_TOK):
        if t % grp_tok == 0:
            wait_group(buf_a, 0, t // grp_tok)
        issue_token(idx_ref, PEER_TOK + t, t, buf_b, 1)
        token(buf_a, t, t)
    for t in range(PEER_TOK):
        if t % grp_tok == 0:
            wait_group(buf_b, 1, t // grp_tok)
        issue_token(idxn_ref, t, t, buf_a, 0)
        token(buf_b, t, PEER_TOK + t)

    @pl.when(i == n - 1)
    def _():
        for q in range(PEER_WAIT_GROUPS):
            wait_group(buf_a, 0, q)


def _peer(eidx, gate3, h3, x3, g2, mod_row, seq_len, fg, table, final):
    t = h3.shape[0]
    tb = 2 * PEER_TOK
    nb = t // tb
    steps_per_seq = seq_len // tb
    rows = PEER_TOK * PEER_SLOTS
    tile = (SUBLANES, LANES)
    pair = (2 * SUBLANES, LANES)
    return pl.pallas_call(
        functools.partial(_peer_kernel, final),
        out_shape=jax.ShapeDtypeStruct((t,) + tile, F32),
        grid=(nb,),
        in_specs=[pl.BlockSpec((tb, PEER_SLOTS), lambda i: (i, 0), memory_space=pltpu.SMEM),
                  pl.BlockSpec((tb, PEER_SLOTS), lambda i: (jnp.minimum(i + 1, nb - 1), 0),
                               memory_space=pltpu.SMEM),
                  pl.BlockSpec((1, PEER_SLOTS, tb), lambda i: (i, 0, 0)),
                  pl.BlockSpec((tb,) + tile, lambda i: (i, 0, 0)),
                  pl.BlockSpec((tb,) + tile, lambda i: (i, 0, 0)),
                  pl.BlockSpec((1, 1) + tile, lambda i: (mod_row(i, steps_per_seq), 0, 0, 0)),
                  pl.BlockSpec(tile, lambda i: (0, 0)),
                  pl.BlockSpec(memory_space=pl.ANY)],
        out_specs=pl.BlockSpec((tb,) + tile, lambda i: (i, 0, 0)),
        scratch_shapes=[pltpu.VMEM((rows,) + pair, BF16), pltpu.VMEM((rows,) + pair, BF16),
                        pltpu.VMEM((PEER_SLOTS, LANES), F32),
                        pltpu.SemaphoreType.DMA((2, PEER_WAIT_GROUPS))],
        compiler_params=pltpu.CompilerParams(dimension_semantics=("arbitrary",),
                                             vmem_limit_bytes=PEER_VMEM_LIMIT),
        name="peer_experts",
    )(eidx, eidx, gate3, h3, x3, g2, fg, table)


def _rope_perm(w):
    q = QK_ROPE // 4
    a1, a2, b1, b2 = (w[..., j * q:(j + 1) * q] for j in range(4))
    return jnp.concatenate([-a2, a1, -b2, b1], axis=-1)


def _prep_layer(l, d, w_in, b_gate, q_norm_g, w_uq, kv_norm_g, w_ukv, w_oa, w_ob, w_grp,
                pool_scale, w_oc, w_out, w_pq, norm1_g, norm2_g):
    wi = w_in[l]
    s0 = Q_LORA
    s1 = s0 + KV_LORA
    s2 = s1 + QK_ROPE
    s3 = s2 + FOURIER_WIDTH
    s4 = s3 + POOL_WIDTH
    w_kr = wi[:, s1:s2]
    zl = jnp.zeros((d, QK_NOPE), F32)
    zr = jnp.zeros((d, HEAD_PAD - QK_NOPE - QK_ROPE), F32)
    wall = jnp.concatenate([wi[:, 0:s1], wi[:, s2:], zl, w_kr, zr, zl, _rope_perm(w_kr), zr],
                           axis=1).astype(BF16)
    wq = w_uq[l].reshape(Q_LORA, N_HEADS, QK_NOPE + QK_ROPE)
    qpad = jnp.zeros((Q_LORA, N_HEADS, HEAD_PAD - QK_NOPE - QK_ROPE), F32)
    wq_full = jnp.concatenate([wq, qpad], axis=-1).reshape(Q_LORA, N_HEADS * HEAD_PAD)
    wq_perm = jnp.concatenate([jnp.zeros((Q_LORA, N_HEADS, QK_NOPE), F32),
                               _rope_perm(wq[..., QK_NOPE:]), qpad],
                              axis=-1).reshape(Q_LORA, N_HEADS * HEAD_PAD)
    wkv = w_ukv[l].reshape(KV_LORA, N_HEADS, QK_NOPE + V_HEAD)
    wk = jnp.concatenate([wkv[..., :QK_NOPE],
                          jnp.zeros((KV_LORA, N_HEADS, HEAD_PAD - QK_NOPE), F32)],
                         axis=-1).reshape(KV_LORA, N_HEADS * HEAD_PAD)
    wv = wkv[..., QK_NOPE:].reshape(KV_LORA, ATT_WIDTH)
    cidx = jnp.arange(FOURIER_WIDTH, dtype=I32)
    ang = (2.0 * math.pi / FOURIER_WIDTH) * ((cidx[:, None] * cidx[None, :]) % FOURIER_WIDTH
                                             ).astype(F32)
    fc = jnp.concatenate([jnp.cos(ang), jnp.sin(ang)], axis=1).astype(BF16)
    ng = len(POOL_WINDOWS)
    wg = jnp.zeros((ng, POOL_GROUP, ng, POOL_GROUP), F32)
    for gi in range(ng):
        wg = wg.at[gi, :, gi, :].set(w_grp[l, gi])
    return {
        "wall": wall, "n1": norm1_g[l][None, :], "n2": norm2_g[l][None, :],
        "qg": q_norm_g[l][None, :], "kvg": kv_norm_g[l][None, :],
        "wq": wq_full.astype(BF16), "wqp": wq_perm.astype(BF16),
        "wk": wk.astype(BF16), "wv": wv.astype(BF16), "fc": fc,
        "bg": b_gate[l][None, :],
        "wg": wg.reshape(POOL_WIDTH, POOL_WIDTH).astype(BF16),
        "ps": pool_scale[l][None, :],
        "woa": w_oa[l].astype(BF16), "wob": w_ob[l].astype(BF16), "woc": w_oc[l].astype(BF16),
        "wout": w_out[l].astype(BF16), "wpq": w_pq[l].astype(BF16),
    }


def _rope_tables(seq_len, rope):
    zeros_n = jnp.zeros((seq_len, QK_NOPE), F32)
    zeros_p = jnp.zeros((seq_len, HEAD_PAD - QK_NOPE - QK_ROPE), F32)
    ones_n = jnp.ones((seq_len, QK_NOPE), F32)
    if rope:
        pos = jnp.arange(seq_len, dtype=I32)
        half = QK_ROPE // 2
        inv_freq = ROPE_BASE ** (-jnp.arange(0, half, 2, dtype=F32) / half)
        ang_r = (pos // GRID_W).astype(F32)[:, None] * inv_freq
        ang_c = (pos % GRID_W).astype(F32)[:, None] * inv_freq
        cos = jnp.concatenate([jnp.cos(ang_r)] * 2 + [jnp.cos(ang_c)] * 2, axis=1)
        sin = jnp.concatenate([jnp.sin(ang_r)] * 2 + [jnp.sin(ang_c)] * 2, axis=1)
    else:
        cos = jnp.ones((seq_len, QK_ROPE), F32)
        sin = jnp.zeros((seq_len, QK_ROPE), F32)
    cq = jnp.concatenate([ones_n, cos, zeros_p], axis=1) * ATT_SCALE
    sq = jnp.concatenate([zeros_n, sin, zeros_p], axis=1) * ATT_SCALE
    ck = jnp.concatenate([zeros_n, cos, zeros_p], axis=1)
    sk = jnp.concatenate([zeros_n, sin, zeros_p], axis=1)
    return cq, sq, ck, sk


def _dft(seq_len):
    idx = jnp.arange(seq_len, dtype=I32)
    ang = (2.0 * math.pi / seq_len) * ((idx[:, None] * idx[None, :]) % seq_len).astype(F32)
    return jnp.cos(ang).astype(BF16), jnp.sin(ang).astype(BF16)


def _peer_block(x1, h2, pq, keys, mod_tiles, mod_row, seq_len, fg, table, final):
    t, d = x1.shape
    eidx_n, gate_n = _route(pq, keys)
    eidx = eidx_n.T
    gate3 = gate_n.reshape(PEER_SLOTS, t // (2 * PEER_TOK), 2 * PEER_TOK).transpose(1, 0, 2)
    tile = (SUBLANES, LANES)
    out = _peer(eidx, gate3, h2.reshape((t,) + tile), x1.reshape((t,) + tile), mod_tiles,
                mod_row, seq_len, fg, table, final)
    return out.reshape(t, d)


def kernel(x, c, ctx, c_ctx, w_mod, b_mod, norm1_g, norm2_g, w_in, b_gate, q_norm_g, w_uq,
           kv_norm_g, w_ukv, w_oa, w_ob, w_grp, pool_scale, w_oc, w_out, w_pq, peer_keys,
           peer_down, peer_up, final_g):
    batch, seq_len, d = x.shape
    ctx_len = ctx.shape[1]
    depth = w_mod.shape[0]
    assert d == SUBLANES * LANES and batch + 1 <= MOD_ROWS
    assert seq_len % TOK_TILE == 0 and ctx_len % (2 * PEER_TOK) == 0 and ctx_len % LANES == 0
    tile = (SUBLANES, LANES)

    cvec = jnp.concatenate([c, c_ctx[None, :], jnp.zeros((MOD_ROWS - batch - 1, d), F32)], axis=0)
    mod_all = _modulation(cvec, w_mod, b_mod)

    def x_row(i, per_seq):
        return i // per_seq

    def c_row(i, per_seq):
        return batch

    tabs_x = _rope_tables(seq_len, True)
    tabs_c = _rope_tables(ctx_len, False)
    dft_x = _dft(seq_len)
    dft_c = _dft(ctx_len)
    fg = final_g.reshape(tile)

    xs = x.reshape(batch * seq_len, d)
    cs = ctx.reshape(batch * ctx_len, d)
    for l in range(depth):
        last = l == depth - 1
        lw = _prep_layer(l, d, w_in, b_gate, q_norm_g, w_uq, kv_norm_g, w_ukv, w_oa, w_ob, w_grp,
                         pool_scale, w_oc, w_out, w_pq, norm1_g, norm2_g)
        mod = mod_all[l].reshape(MOD_ROWS, 1, 6 * d)
        mod_tiles = mod_all[l].reshape(MOD_ROWS, 6, SUBLANES, LANES)[:, 5:6]
        keys = peer_keys[l].reshape(2 * PEER_HEADS, N_KEYS, PEER_HALF).astype(BF16)
        table = jnp.concatenate([peer_down[l].reshape((-1,) + tile),
                                 peer_up[l].reshape((-1,) + tile)], axis=1).astype(BF16)

        if last:
            kc, vc = _in_proj(cs, mod, c_row, ctx_len, lw, tabs_c, False)
        else:
            kc, vc, qc, abc, zpc, gc = _in_proj(cs, mod, c_row, ctx_len, lw, tabs_c, True)
        kx, vx, qx, abx, zpx, gx = _in_proj(xs, mod, x_row, seq_len, lw, tabs_x, True)

        att_x = _attention(qx, kx, vx, batch, seq_len, (kc, vc))
        four_x = _fourier(abx, batch, seq_len, dft_x)
        pool_x = _pool(zpx, batch, seq_len, lw["wg"], lw["ps"])
        x1, h2, pq = _out_proj(xs, att_x, four_x, pool_x, gx, mod, x_row, seq_len, lw)
        xs = _peer_block(x1, h2, pq, keys, mod_tiles, x_row, seq_len, fg, table, last)

        if not last:
            att_c = _attention(qc, kc, vc, batch, ctx_len, None)
            four_c = _fourier(abc, batch, ctx_len, dft_c)
            pool_c = _pool(zpc, batch, ctx_len, lw["wg"], lw["ps"])
            c1, hc2, pqc = _out_proj(cs, att_c, four_c, pool_c, gc, mod, c_row, ctx_len, lw)
            cs = _peer_block(c1, hc2, pqc, keys, mod_tiles, c_row, ctx_len, fg, table, False)
    return xs.reshape(batch, seq_len, d)
```

```python
import functools
import math

import jax
import jax.numpy as jnp
from jax import lax
from jax.experimental import pallas as pl
from jax.experimental.pallas import tpu as pltpu
from jax.experimental.pallas import tpu_sc as plsc

F32 = jnp.float32
BF16 = jnp.bfloat16
I32 = jnp.int32

GRID_W = 64
N_HEADS = 8
Q_LORA = 256
KV_LORA = 128
QK_NOPE = 64
QK_ROPE = 32
V_HEAD = 64
ATT_WIDTH = N_HEADS * V_HEAD
ATT_SCALE = (QK_NOPE + QK_ROPE) ** -0.5
ROPE_BASE = 10000.0
FOURIER_WIDTH = 256
POOL_WINDOWS = (2, 4, 8, 16)
POOL_GROUP = 64
POOL_WIDTH = POOL_GROUP * len(POOL_WINDOWS)
N_BRANCH = 3
PEER_HEADS = 8
N_KEYS = 128
PEER_QDIM = 256
PEER_HALF = PEER_QDIM // 2
PEER_TOPK = 16
PEER_SLOTS = PEER_HEADS * PEER_TOPK
STAIR = tuple((i, PEER_TOPK // (i + 1)) for i in range(PEER_TOPK))
N_CAND = sum(cnt for _, cnt in STAIR)
CAND_ROWS = -(-N_CAND // 8) * 8
EPS = 1e-6

LANES = 128
SUBLANES = 8
HEAD_PAD = 128
POOL_PAD = 8
MOD_ROWS = 16
VMEM_LIMIT = 48 * 1024 * 1024

TOK_TILE = 256
ATT_Q_TILE = 256
FOUR_TILE = 512
ROUTE_TILE = 128
PEER_TOK = 16
PEER_WAIT_GROUPS = 4
PEER_VMEM_LIMIT = 56 * 1024 * 1024

SC_CORES = 2
SC_SUBCORES = 16
SC_WORKERS = SC_CORES * SC_SUBCORES
SC_LANES = 16
SC_CHUNK = 32
PEER_SC_TOKENS = 8192
GELU_C = math.sqrt(2.0 / math.pi)

C_CQ = 0
C_CKV = C_CQ + Q_LORA
C_ZF = C_CKV + KV_LORA
C_ZP = C_ZF + FOURIER_WIDTH
C_ZG = C_ZP + POOL_WIDTH
C_KR = None


def _params(sem=None):
    return pltpu.CompilerParams(dimension_semantics=sem, vmem_limit_bytes=VMEM_LIMIT)


def _rms(x):
    return x * lax.rsqrt(jnp.mean(x * x, axis=-1, keepdims=True) + EPS)


def _dot(a, b):
    return jnp.dot(a, b, preferred_element_type=F32)


def _dot_nt(a, b):
    return lax.dot_general(a, b, (((1,), (1,)), ((), ())), preferred_element_type=F32)


def _mod_kernel(c_ref, w_ref, b_ref, o_ref):
    c = c_ref[...]
    s = c * jax.nn.sigmoid(c)
    o_ref[0] = jnp.dot(s, w_ref[0], preferred_element_type=F32,
                       precision=lax.Precision.HIGHEST) + b_ref[0]


def _modulation(cvec, w_mod, b_mod):
    depth, d, n = w_mod.shape
    tn = 1536
    return pl.pallas_call(
        _mod_kernel,
        out_shape=jax.ShapeDtypeStruct((depth, MOD_ROWS, n), F32),
        grid=(depth, n // tn),
        in_specs=[pl.BlockSpec((MOD_ROWS, d), lambda l, j: (0, 0)),
                  pl.BlockSpec((1, d, tn), lambda l, j: (l, 0, j)),
                  pl.BlockSpec((1, 1, tn), lambda l, j: (l, 0, j))],
        out_specs=pl.BlockSpec((1, MOD_ROWS, tn), lambda l, j: (l, 0, j)),
        compiler_params=_params(("arbitrary", "arbitrary")),
        name="modulation",
    )(cvec, w_mod, b_mod.reshape(depth, 1, n))


def _in_kernel(d, full, x_ref, m_ref, n1_ref, wall_ref, qg_ref, wq_ref, wqp_ref, kvg_ref,
               wk_ref, wv_ref, fc_ref, bg_ref, cq_ref, sq_ref, ck_ref, sk_ref, *outs):
    x = x_ref[...]
    m = m_ref[0]
    h = _rms(x) * n1_ref[...] * (1.0 + m[:, d:2 * d]) + m[:, 0:d]
    hb = h.astype(BF16)
    c_kr = C_ZG + N_BRANCH * d
    if full:
        z = _dot(hb, wall_ref[...])
        k_ref, v_ref, q_ref, ab_ref, zp_ref, g_ref = outs
    else:
        z = None
        k_ref, v_ref = outs
    def col(lo, hi):
        if full:
            return z[:, lo:hi]
        return _dot(hb, wall_ref[:, lo:hi])

    ckv = col(C_CKV, C_ZF)
    ckvn = (_rms(ckv) * kvg_ref[...]).astype(BF16)
    kf = _dot(ckvn, wk_ref[...])
    kr = col(c_kr, c_kr + LANES) * ck_ref[...] + col(c_kr + LANES, c_kr + 2 * LANES) * sk_ref[...]
    for hh in range(N_HEADS):
        sl = slice(hh * HEAD_PAD, (hh + 1) * HEAD_PAD)
        k_ref[:, sl] = (kf[:, sl] + kr).astype(BF16)
    v_ref[...] = _dot(ckvn, wv_ref[...]).astype(BF16)
    if not full:
        return
    cq = z[:, C_CQ:C_CKV]
    cqn = (_rms(cq) * qg_ref[...]).astype(BF16)
    qf = _dot(cqn, wq_ref[...])
    qr = _dot(cqn, wqp_ref[...])
    cosq = cq_ref[...]
    sinq = sq_ref[...]
    for hh in range(N_HEADS):
        sl = slice(hh * HEAD_PAD, (hh + 1) * HEAD_PAD)
        q_ref[:, sl] = (qf[:, sl] * cosq + qr[:, sl] * sinq).astype(BF16)
    ab_ref[...] = _dot(z[:, C_ZF:C_ZP].astype(BF16), fc_ref[...]).astype(BF16)
    zp_ref[...] = z[:, C_ZP:C_ZG]
    g_ref[...] = jax.nn.sigmoid(z[:, C_ZG:c_kr] + bg_ref[...]).astype(BF16)


def _in_proj(x2d, mod, mod_row, seq_len, lw, tabs, full):
    t, d = x2d.shape
    tm = min(TOK_TILE, seq_len)
    tiles_per_seq = seq_len // tm
    wall = lw["wall"]
    nw = wall.shape[1]
    cq, sq, ck, sk = tabs

    def const(shape):
        return pl.BlockSpec(shape, lambda i: (0,) * len(shape))

    def pos(i):
        return (i % tiles_per_seq, 0)

    in_specs = [
        pl.BlockSpec((tm, d), lambda i: (i, 0)),
        pl.BlockSpec((1, 1, mod.shape[-1]), lambda i: (mod_row(i, tiles_per_seq), 0, 0)),
        const((1, d)), const((d, nw)), const((1, Q_LORA)),
        const(lw["wq"].shape), const(lw["wqp"].shape), const((1, KV_LORA)),
        const(lw["wk"].shape), const(lw["wv"].shape), const(lw["fc"].shape),
        const((1, N_BRANCH * d)),
        pl.BlockSpec((tm, LANES), pos), pl.BlockSpec((tm, LANES), pos),
        pl.BlockSpec((tm, LANES), pos), pl.BlockSpec((tm, LANES), pos),
    ]
    kw = N_HEADS * HEAD_PAD
    out_shape = [jax.ShapeDtypeStruct((t, kw), BF16), jax.ShapeDtypeStruct((t, ATT_WIDTH), BF16)]
    out_specs = [pl.BlockSpec((tm, kw), lambda i: (i, 0)),
                 pl.BlockSpec((tm, ATT_WIDTH), lambda i: (i, 0))]
    if full:
        out_shape += [jax.ShapeDtypeStruct((t, kw), BF16),
                      jax.ShapeDtypeStruct((t, 2 * FOURIER_WIDTH), BF16),
                      jax.ShapeDtypeStruct((t, POOL_WIDTH), F32),
                      jax.ShapeDtypeStruct((t, N_BRANCH * d), BF16)]
        out_specs += [pl.BlockSpec((tm, kw), lambda i: (i, 0)),
                      pl.BlockSpec((tm, 2 * FOURIER_WIDTH), lambda i: (i, 0)),
                      pl.BlockSpec((tm, POOL_WIDTH), lambda i: (i, 0)),
                      pl.BlockSpec((tm, N_BRANCH * d), lambda i: (i, 0))]
    return pl.pallas_call(
        functools.partial(_in_kernel, d, full),
        out_shape=out_shape,
        grid=(t // tm,),
        in_specs=in_specs,
        out_specs=out_specs,
        compiler_params=_params(("arbitrary",)),
        name="in_proj" if full else "ctx_kv_proj",
    )(x2d, mod, lw["n1"], wall, lw["qg"], lw["wq"], lw["wqp"], lw["kvg"], lw["wk"], lw["wv"],
      lw["fc"], lw["bg"], cq, sq, ck, sk)


def _attn_kernel(has_ctx, q_ref, k_ref, v_ref, *rest):
    if has_ctx:
        kc_ref, vc_ref, o_ref = rest
    else:
        (o_ref,) = rest
    outs = []
    for hh in range(2):
        sl = slice(hh * HEAD_PAD, (hh + 1) * HEAD_PAD)
        q = q_ref[:, sl]
        s = _dot_nt(q, k_ref[:, sl])
        mx = jnp.max(s, axis=-1, keepdims=True)
        if has_ctx:
            sc = _dot_nt(q, kc_ref[:, sl])
            mx = jnp.maximum(mx, jnp.max(sc, axis=-1, keepdims=True))
        p = jnp.exp(s - mx)
        den = jnp.sum(p, axis=-1, keepdims=True)
        o = _dot(p.astype(BF16), v_ref[...])
        if has_ctx:
            pc = jnp.exp(sc - mx)
            den = den + jnp.sum(pc, axis=-1, keepdims=True)
            o = o + _dot(pc.astype(BF16), vc_ref[...])
        outs.append(o / den)
    lane = lax.broadcasted_iota(I32, outs[0].shape, 1)
    o_ref[...] = jnp.where(lane < V_HEAD, outs[0], outs[1]).astype(BF16)


def _attention(q, k, v, batch, seq_len, ctx_kv):
    t = q.shape[0]
    tq = min(ATT_Q_TILE, seq_len)
    nq = seq_len // tq
    pair_w = 2 * HEAD_PAD
    in_specs = [pl.BlockSpec((tq, pair_w), lambda b, j, i: (b * nq + i, j)),
                pl.BlockSpec((seq_len, pair_w), lambda b, j, i: (b, j)),
                pl.BlockSpec((seq_len, 2 * V_HEAD), lambda b, j, i: (b, j))]
    args = [q, k, v]
    if ctx_kv is not None:
        kc, vc = ctx_kv
        lc = kc.shape[0] // batch
        in_specs += [pl.BlockSpec((lc, pair_w), lambda b, j, i: (b, j)),
                     pl.BlockSpec((lc, 2 * V_HEAD), lambda b, j, i: (b, j))]
        args += [kc, vc]
    return pl.pallas_call(
        functools.partial(_attn_kernel, ctx_kv is not None),
        out_shape=jax.ShapeDtypeStruct((t, ATT_WIDTH), BF16),
        grid=(batch, N_HEADS // 2, nq),
        in_specs=in_specs,
        out_specs=pl.BlockSpec((tq, 2 * V_HEAD), lambda b, j, i: (b * nq + i, j)),
        compiler_params=_params(("arbitrary", "arbitrary", "arbitrary")),
        name="attention" if ctx_kv is not None else "ctx_attention",
    )(*args)


def _fourier_kernel(norm, c_ref, s_ref, ab_ref, o_ref):
    a = ab_ref[:, 0:FOURIER_WIDTH]
    b = ab_ref[:, FOURIER_WIDTH:2 * FOURIER_WIDTH]
    o = _dot(c_ref[...], a) - _dot(s_ref[...], b)
    o_ref[...] = (o * norm).astype(BF16)


def _fourier(ab, batch, seq_len, dft):
    t = ab.shape[0]
    tm = min(FOUR_TILE, seq_len)
    nt = seq_len // tm
    cl, sl = dft
    norm = 1.0 / math.sqrt(seq_len * FOURIER_WIDTH)
    return pl.pallas_call(
        functools.partial(_fourier_kernel, norm),
        out_shape=jax.ShapeDtypeStruct((t, FOURIER_WIDTH), BF16),
        grid=(nt, batch),
        in_specs=[pl.BlockSpec((tm, seq_len), lambda i, b: (i, 0)),
                  pl.BlockSpec((tm, seq_len), lambda i, b: (i, 0)),
                  pl.BlockSpec((seq_len, 2 * FOURIER_WIDTH), lambda i, b: (b, 0))],
        out_specs=pl.BlockSpec((tm, FOURIER_WIDTH), lambda i, b: (b * nt + i, 0)),
        compiler_params=_params(("arbitrary", "arbitrary")),
        name="fourier",
    )(cl, sl, ab)


def _pool_kernel(seq_len, z_ref, wg_ref, ps_ref, o_ref, pad_ref, s_ref):
    n = seq_len
    p = n + 2 * POOL_PAD
    z = z_ref[...]
    zeros = jnp.zeros((POOL_PAD, POOL_WIDTH), F32)
    pad_ref[0:POOL_PAD, :] = zeros
    pad_ref[POOL_PAD + n:p, :] = zeros
    pad_ref[POOL_PAD:POOL_PAD + n, :] = z
    s_ref[0:p - 1, :] = pad_ref[0:p - 1, :] + pad_ref[1:p, :]
    w2 = s_ref[POOL_PAD - 1:POOL_PAD - 1 + n, :]
    pad_ref[0:p - 3, :] = s_ref[0:p - 3, :] + s_ref[2:p - 1, :]
    w4 = pad_ref[POOL_PAD - 2:POOL_PAD - 2 + n, :]
    s_ref[0:p - 7, :] = pad_ref[0:p - 7, :] + pad_ref[4:p - 3, :]
    w8 = s_ref[POOL_PAD - 4:POOL_PAD - 4 + n, :]
    pad_ref[0:p - 15, :] = s_ref[0:p - 15, :] + s_ref[8:p - 7, :]
    w16 = pad_ref[0:n, :]
    pos = lax.broadcasted_iota(I32, (n, POOL_WIDTH), 0)
    grp = lax.broadcasted_iota(I32, (n, POOL_WIDTH), 1) // POOL_GROUP
    win = jnp.where(grp == 0, w2, jnp.where(grp == 1, w4, jnp.where(grp == 2, w8, w16)))
    half = jnp.where(grp == 0, 1, jnp.where(grp == 1, 2, jnp.where(grp == 2, 4, 8)))
    lo = jnp.maximum(pos - half, 0)
    hi = jnp.minimum(pos + half, n)
    cnt = (hi - lo).astype(F32)
    pooled = win / cnt - z
    y = _dot(pooled.astype(BF16), wg_ref[...])
    o_ref[...] = (y * ps_ref[...]).astype(BF16)


def _pool(zp, batch, seq_len, wg_bd, pool_scale):
    t = zp.shape[0]
    return pl.pallas_call(
        functools.partial(_pool_kernel, seq_len),
        out_shape=jax.ShapeDtypeStruct((t, POOL_WIDTH), BF16),
        grid=(batch,),
        in_specs=[pl.BlockSpec((seq_len, POOL_WIDTH), lambda b: (b, 0)),
                  pl.BlockSpec((POOL_WIDTH, POOL_WIDTH), lambda b: (0, 0)),
                  pl.BlockSpec((1, POOL_WIDTH), lambda b: (0, 0))],
        out_specs=pl.BlockSpec((seq_len, POOL_WIDTH), lambda b: (b, 0)),
        scratch_shapes=[pltpu.VMEM((seq_len + 2 * POOL_PAD, POOL_WIDTH), F32),
                        pltpu.VMEM((seq_len + 2 * POOL_PAD, POOL_WIDTH), F32)],
        compiler_params=_params(("arbitrary",)),
        name="pool",
    )(zp, wg_bd, pool_scale)


def _out_kernel(d, x_ref, att_ref, four_ref, pool_ref, g_ref, m_ref, woa_ref, wob_ref, woc_ref,
                wout_ref, n2_ref, wpq_ref, x1_ref, h2_ref, pq_ref):
    m = m_ref[0]
    ya = _dot(att_ref[...], woa_ref[...])
    yb = _dot(four_ref[...], wob_ref[...])
    yc = _dot(pool_ref[...], woc_ref[...])
    mixp = (g_ref[:, 0:d].astype(F32) * ya + g_ref[:, d:2 * d].astype(F32) * yb
            + g_ref[:, 2 * d:3 * d].astype(F32) * yc)
    mix = _dot(mixp.astype(BF16), wout_ref[...])
    x1 = x_ref[...] + m[:, 2 * d:3 * d] * mix
    x1_ref[...] = x1
    h2 = _rms(x1) * n2_ref[...] * (1.0 + m[:, 4 * d:5 * d]) + m[:, 3 * d:4 * d]
    h2_ref[...] = h2
    pq = _dot(h2.astype(BF16), wpq_ref[...])
    for hp in range(2 * PEER_HEADS):
        pq_ref[hp] = pq[:, hp * PEER_HALF:(hp + 1) * PEER_HALF].astype(BF16)


def _out_proj(x2d, att, four, pool, g, mod, mod_row, seq_len, lw):
    t, d = x2d.shape
    tm = min(TOK_TILE, seq_len)
    tiles_per_seq = seq_len // tm

    def const(shape):
        return pl.BlockSpec(shape, lambda i: (0,) * len(shape))

    def row(w):
        return pl.BlockSpec((tm, w), lambda i: (i, 0))

    nhp = 2 * PEER_HEADS
    return pl.pallas_call(
        functools.partial(_out_kernel, d),
        out_shape=[jax.ShapeDtypeStruct((t, d), F32), jax.ShapeDtypeStruct((t, d), F32),
                   jax.ShapeDtypeStruct((nhp, t, PEER_HALF), BF16)],
        grid=(t // tm,),
        in_specs=[row(d), row(ATT_WIDTH), row(FOURIER_WIDTH), row(POOL_WIDTH), row(N_BRANCH * d),
                  pl.BlockSpec((1, 1, mod.shape[-1]), lambda i: (mod_row(i, tiles_per_seq), 0, 0)),
                  const(lw["woa"].shape), const(lw["wob"].shape), const(lw["woc"].shape),
                  const(lw["wout"].shape), const((1, d)), const(lw["wpq"].shape)],
        out_specs=[row(d), row(d), pl.BlockSpec((nhp, tm, PEER_HALF), lambda i: (0, i, 0))],
        compiler_params=_params(("arbitrary",)),
        name="out_proj",
    )(x2d, att, four, pool, g, mod, lw["woa"], lw["wob"], lw["woc"], lw["wout"], lw["n2"],
      lw["wpq"])


def _select_round(s, iota, n):
    mx = jnp.max(s, axis=0, keepdims=True)
    idx = jnp.min(jnp.where(s == mx, iota, n), axis=0, keepdims=True)
    hit = iota == idx
    return mx, idx, hit, jnp.where(hit, -jnp.inf, s)


def _route_kernel(pq_ref, keys_ref, eidx_ref, gate_ref, sv_a, si_a, sv_b, si_b, cand_ref, cidx_ref,
                  ts_ref):
    tm = pq_ref.shape[1]
    iota_k = lax.broadcasted_iota(I32, (N_KEYS, tm), 0).astype(F32)
    iota_c = lax.broadcasted_iota(I32, (CAND_ROWS, tm), 0).astype(F32)
    cand_ref[N_CAND:CAND_ROWS, :] = jnp.full((CAND_ROWS - N_CAND, tm), -jnp.inf, F32)
    cidx_ref[N_CAND:CAND_ROWS, :] = jnp.zeros((CAND_ROWS - N_CAND, tm), F32)

    def sub_key_topk(hd, sv_ref, si_ref):
        s = [_dot_nt(keys_ref[2 * hd + p], pq_ref[2 * hd + p]) for p in range(2)]
        for r in range(PEER_TOPK):
            for p in range(2):
                mx, idx, _, s[p] = _select_round(s[p], iota_k, float(N_KEYS))
                sv_ref[p, r:r + 1, :] = mx
                si_ref[p, r:r + 1, :] = idx

    def pair_topk(hd, sv_ref, si_ref):
        off = 0
        for i, cnt in STAIR:
            cand_ref[off:off + cnt, :] = sv_ref[0, i:i + 1, :] + sv_ref[1, 0:cnt, :]
            cidx_ref[off:off + cnt, :] = si_ref[0, i:i + 1, :] * N_KEYS + si_ref[1, 0:cnt, :]
            off += cnt
        cidx = cidx_ref[...]
        c = cand_ref[...]
        base = pl.multiple_of(hd * PEER_TOPK, PEER_TOPK)
        for r in range(PEER_TOPK):
            mx, _, hit, c = _select_round(c, iota_c, float(CAND_ROWS))
            ts_ref[r:r + 1, :] = mx
            expert = jnp.sum(jnp.where(hit, cidx, 0.0), axis=0, keepdims=True)
            eidx_ref[pl.ds(base + r, 1), :] = expert.astype(I32)
        ts = ts_ref[...]
        ex = jnp.exp(ts - ts[0:1, :])
        gate_ref[pl.ds(base, PEER_TOPK), :] = ex / jnp.sum(ex, axis=0, keepdims=True)

    sub_key_topk(0, sv_a, si_a)

    def two_heads(j, carry):
        hd = 2 * j
        sub_key_topk(hd + 1, sv_b, si_b)
        pair_topk(hd, sv_a, si_a)
        sub_key_topk(hd + 2, sv_a, si_a)
        pair_topk(hd + 1, sv_b, si_b)
        return carry

    lax.fori_loop(0, PEER_HEADS // 2 - 1, two_heads, 0)
    sub_key_topk(PEER_HEADS - 1, sv_b, si_b)
    pair_topk(PEER_HEADS - 2, sv_a, si_a)
    pair_topk(PEER_HEADS - 1, sv_b, si_b)


def _route(pq, keys):
    nhp, t, _ = pq.shape
    tm = ROUTE_TILE
    lists = pltpu.VMEM((2, PEER_TOPK, tm), F32)
    return pl.pallas_call(
        _route_kernel,
        out_shape=[jax.ShapeDtypeStruct((PEER_SLOTS, t), I32),
                   jax.ShapeDtypeStruct((PEER_SLOTS, t), F32)],
        grid=(t // tm,),
        in_specs=[pl.BlockSpec((nhp, tm, PEER_HALF), lambda i: (0, i, 0)),
                  pl.BlockSpec((nhp, N_KEYS, PEER_HALF), lambda i: (0, 0, 0))],
        out_specs=[pl.BlockSpec((PEER_SLOTS, tm), lambda i: (0, i)),
                   pl.BlockSpec((PEER_SLOTS, tm), lambda i: (0, i))],
        scratch_shapes=[lists, lists, lists, lists,
                        pltpu.VMEM((CAND_ROWS, tm), F32),
                        pltpu.VMEM((CAND_ROWS, tm), F32),
                        pltpu.VMEM((PEER_TOPK, tm), F32)],
        compiler_params=_params(("arbitrary",)),
        name="peer_route",
    )(pq, keys)


def _fold_rows(a, b, keep_a, shift):
    return jnp.where(keep_a, a + pltpu.roll(a, SUBLANES - shift, 0), b + pltpu.roll(b, shift, 0))


def _peer_kernel(final, idx_ref, idxn_ref, gate_ref, h_ref, x_ref, g2_ref, fg_ref, tab_ref,
                 o_ref, buf_a, buf_b, act_ref, sem):
    i = pl.program_id(0)
    n = pl.num_programs(0)
    grp_tok = PEER_TOK // PEER_WAIT_GROUPS
    grp_rows = grp_tok * PEER_SLOTS
    sub = lax.broadcasted_iota(I32, (SUBLANES, LANES), 0)
    keep = {sh: (sub & sh) == 0 for sh in (4, 2, 1)}
    g2 = g2_ref[0, 0]

    def row_copy(ids, t_src, t_dst, k, buf, s):
        return pltpu.make_async_copy(tab_ref.at[ids[t_src, k]], buf.at[t_dst * PEER_SLOTS + k],
                                     sem.at[s, t_dst // grp_tok])

    def issue_token(ids, t_src, t_dst, buf, s):
        for k in range(PEER_SLOTS):
            row_copy(ids, t_src, t_dst, k, buf, s).start(priority=k % 2)

    def wait_group(buf, s, q):
        pltpu.make_async_copy(tab_ref.at[pl.ds(0, grp_rows)],
                              buf.at[pl.ds(q * grp_rows, grp_rows)], sem.at[s, q]).wait()

    def token(buf, tl, tb):
        r0 = tl * PEER_SLOTS
        hv = h_ref[tb]
        groups = []
        for g in range(PEER_SLOTS // SUBLANES):
            p = [buf[r0 + g * SUBLANES + j].astype(F32)[0:SUBLANES] * hv for j in range(SUBLANES)]
            for sh in (4, 2, 1):
                half = len(p) // 2
                p = [_fold_rows(p[j], p[j + half], keep[sh], sh) for j in range(half)]
            groups.append(p[0])
        part = jnp.concatenate(groups, axis=0)
        a = jnp.sum(part, axis=1, keepdims=True)
        act = gate_ref[0][:, tb:tb + 1] * jax.nn.gelu(a)
        act_ref[...] = jnp.broadcast_to(act, (PEER_SLOTS, LANES))
        accs = [jnp.zeros((SUBLANES, LANES), F32) for _ in range(4)]
        for k in range(PEER_SLOTS):
            up = buf[r0 + k].astype(F32)[SUBLANES:2 * SUBLANES]
            accs[k % 4] = accs[k % 4] + act_ref[k:k + 1, :] * up
        xr = x_ref[tb] + g2 * ((accs[0] + accs[1]) + (accs[2] + accs[3]))
        if final:
            ms = jnp.sum(jnp.sum(xr * xr, axis=1, keepdims=True), axis=0, keepdims=True)
            xr = xr * lax.rsqrt(ms / (SUBLANES * LANES) + EPS) * fg_ref[...]
        o_ref[tb] = xr

    @pl.when(i == 0)
    def _():
        def body(t, carry):
            issue_token(idx_ref, t, t, buf_a, 0)
            return carry
        lax.fori_loop(0, PEER_TOK, body, 0)

    for t in range(PEER_TOK):
        if t % grp_tok == 0:
            wait_group(buf_a, 0, t // grp_tok)
        issue_token(idx_ref, PEER_TOK + t, t, buf_b, 1)
        token(buf_a, t, t)
    for t in range(PEER_TOK):
        if t % grp_tok == 0:
            wait_group(buf_b, 1, t // grp_tok)
        issue_token(idxn_ref, t, t, buf_a, 0)
        token(buf_b, t, PEER_TOK + t)

    @pl.when(i == n - 1)
    def _():
        for q in range(PEER_WAIT_GROUPS):
            wait_group(buf_a, 0, q)


def _peer(eidx, gate3, h3, x3, g2, mod_row, seq_len, fg, table, final):
    t = h3.shape[0]
    tb = 2 * PEER_TOK
    nb = t // tb
    steps_per_seq = seq_len // tb
    rows = PEER_TOK * PEER_SLOTS
    tile = (SUBLANES, LANES)
    pair = (2 * SUBLANES, LANES)
    return pl.pallas_call(
        functools.partial(_peer_kernel, final),
        out_shape=jax.ShapeDtypeStruct((t,) + tile, F32),
        grid=(nb,),
        in_specs=[pl.BlockSpec((tb, PEER_SLOTS), lambda i: (i, 0), memory_space=pltpu.SMEM),
                  pl.BlockSpec((tb, PEER_SLOTS), lambda i: (jnp.minimum(i + 1, nb - 1), 0),
                               memory_space=pltpu.SMEM),
                  pl.BlockSpec((1, PEER_SLOTS, tb), lambda i: (i, 0, 0)),
                  pl.BlockSpec((tb,) + tile, lambda i: (i, 0, 0)),
                  pl.BlockSpec((tb,) + tile, lambda i: (i, 0, 0)),
                  pl.BlockSpec((1, 1) + tile, lambda i: (mod_row(i, steps_per_seq), 0, 0, 0)),
                  pl.BlockSpec(tile, lambda i: (0, 0)),
                  pl.BlockSpec(memory_space=pl.ANY)],
        out_specs=pl.BlockSpec((tb,) + tile, lambda i: (i, 0, 0)),
        scratch_shapes=[pltpu.VMEM((rows,) + pair, BF16), pltpu.VMEM((rows,) + pair, BF16),
                        pltpu.VMEM((PEER_SLOTS, LANES), F32),
                        pltpu.SemaphoreType.DMA((2, PEER_WAIT_GROUPS))],
        compiler_params=pltpu.CompilerParams(dimension_semantics=("arbitrary",),
                                             vmem_limit_bytes=PEER_VMEM_LIMIT),
        name="peer_experts",
    )(eidx, eidx, gate3, h3, x3, g2, fg, table)


def _peer_sc_body(n_tok, d, eidx_hbm, gate_hbm, h_hbm, tab_hbm, y_hbm, idx_v, gate_v, h_v, rows_v,
                  out_v, sem):
    wid = lax.axis_index("s") * SC_CORES + lax.axis_index("c")
    per_worker = n_tok // SC_WORKERS
    nj = d // SC_LANES
    lane = lax.iota(I32, SC_LANES)
    zero = jnp.zeros((SC_LANES,), F32)
    hi_mask = jnp.full((SC_LANES,), 0xFFFF0000, jnp.uint32)

    def chunk(c, carry):
        row0 = pl.multiple_of(c * SC_CHUNK, SC_CHUNK)
        pltpu.async_copy(tab_hbm.at[idx_v.at[pl.ds(row0, SC_CHUNK)]], rows_v, sem).wait()

        def down_step(j, accs):
            off = pl.multiple_of(j * SC_LANES, SC_LANES)
            hj = h_v[pl.ds(off, SC_LANES)]
            out = []
            for r in range(SC_CHUNK):
                w = rows_v[r, pl.ds(off, SC_LANES)]
                dn = lax.bitcast_convert_type(w & hi_mask, F32)
                out.append(accs[r] + dn * hj)
            return tuple(out)

        accs = lax.fori_loop(0, nj, down_step, tuple(zero for _ in range(SC_CHUNK)))
        acts = []
        for g in range(SC_CHUNK // SC_LANES):
            a = zero
            for r in range(SC_LANES):
                a = jnp.where(lane == r, jnp.sum(accs[g * SC_LANES + r]), a)
            gt = gate_v[pl.ds(row0 + g * SC_LANES, SC_LANES)]
            u = GELU_C * (a + 0.044715 * (a * a * a))
            th = 1.0 - 2.0 / (jnp.exp(2.0 * u) + 1.0)
            act = gt * (0.5 * a * (1.0 + th))
            for r in range(SC_LANES):
                acts.append(jnp.sum(jnp.where(lane == r, act, 0.0)))

        def up_step(j, carry2):
            off = pl.multiple_of(j * SC_LANES, SC_LANES)
            o = out_v[pl.ds(off, SC_LANES)]
            for r in range(SC_CHUNK):
                w = rows_v[r, pl.ds(off, SC_LANES)]
                up = lax.bitcast_convert_type(w << 16, F32)
                o = o + acts[r] * up
            out_v[pl.ds(off, SC_LANES)] = o
            return carry2

        lax.fori_loop(0, nj, up_step, 0)
        return carry

    def token(ti, carry):
        t = wid * per_worker + ti
        pltpu.sync_copy(eidx_hbm.at[t], idx_v)
        pltpu.sync_copy(gate_hbm.at[t], gate_v)
        pltpu.sync_copy(h_hbm.at[t], h_v)

        def clear(j, carry2):
            out_v[pl.ds(pl.multiple_of(j * SC_LANES, SC_LANES), SC_LANES)] = zero
            return carry2

        lax.fori_loop(0, nj, clear, 0)
        lax.fori_loop(0, PEER_SLOTS // SC_CHUNK, chunk, 0)
        pltpu.sync_copy(out_v, y_hbm.at[t])
        return carry

    lax.fori_loop(0, per_worker, token, 0)


def _peer_sc(eidx, gate, h2, sc_table):
    n_tok, d = h2.shape
    mesh = plsc.VectorSubcoreMesh(core_axis_name="c", subcore_axis_name="s",
                                  num_cores=SC_CORES, num_subcores=SC_SUBCORES)
    return pl.kernel(
        functools.partial(_peer_sc_body, n_tok, d),
        out_type=jax.ShapeDtypeStruct((n_tok, d), F32),
        mesh=mesh,
        scratch_types=[pltpu.VMEM((PEER_SLOTS,), I32), pltpu.VMEM((PEER_SLOTS,), F32),
                       pltpu.VMEM((d,), F32), pltpu.VMEM((SC_CHUNK, d), jnp.uint32),
                       pltpu.VMEM((d,), F32), pltpu.SemaphoreType.DMA],
        compiler_params=pltpu.CompilerParams(needs_layout_passes=False),
        name="peer_experts_sc",
    )(eidx, gate, h2, sc_table)


def _finish_kernel(final, d, y_ref, x_ref, m_ref, fg_ref, o_ref):
    xr = x_ref[...] + m_ref[0][:, 5 * d:6 * d] * y_ref[...]
    if final:
        xr = _rms(xr) * fg_ref[...]
    o_ref[...] = xr


def _peer_finish(y, x1, mod, mod_row, seq_len, tok0, fg, final):
    t, d = x1.shape
    tm = min(TOK_TILE, seq_len)
    tiles_per_seq = seq_len // tm
    tile0 = tok0 // tm
    return pl.pallas_call(
        functools.partial(_finish_kernel, final, d),
        out_shape=jax.ShapeDtypeStruct((t, d), F32),
        grid=(t // tm,),
        in_specs=[pl.BlockSpec((tm, d), lambda i: (i, 0)),
                  pl.BlockSpec((tm, d), lambda i: (i, 0)),
                  pl.BlockSpec((1, 1, mod.shape[-1]),
                               lambda i: (mod_row(i + tile0, tiles_per_seq), 0, 0)),
                  pl.BlockSpec((1, d), lambda i: (0, 0))],
        out_specs=pl.BlockSpec((tm, d), lambda i: (i, 0)),
        compiler_params=_params(("arbitrary",)),
        name="peer_finish",
    )(y, x1, mod, fg.reshape(1, d))


def _rope_perm(w):
    q = QK_ROPE // 4
    a1, a2, b1, b2 = (w[..., j * q:(j + 1) * q] for j in range(4))
    return jnp.concatenate([-a2, a1, -b2, b1], axis=-1)


def _prep_layer(l, d, w_in, b_gate, q_norm_g, w_uq, kv_norm_g, w_ukv, w_oa, w_ob, w_grp,
                pool_scale, w_oc, w_out, w_pq, norm1_g, norm2_g):
    wi = w_in[l]
    s0 = Q_LORA
    s1 = s0 + KV_LORA
    s2 = s1 + QK_ROPE
    s3 = s2 + FOURIER_WIDTH
    s4 = s3 + POOL_WIDTH
    w_kr = wi[:, s1:s2]
    zl = jnp.zeros((d, QK_NOPE), F32)
    zr = jnp.zeros((d, HEAD_PAD - QK_NOPE - QK_ROPE), F32)
    wall = jnp.concatenate([wi[:, 0:s1], wi[:, s2:], zl, w_kr, zr, zl, _rope_perm(w_kr), zr],
                           axis=1).astype(BF16)
    wq = w_uq[l].reshape(Q_LORA, N_HEADS, QK_NOPE + QK_ROPE)
    qpad = jnp.zeros((Q_LORA, N_HEADS, HEAD_PAD - QK_NOPE - QK_ROPE), F32)
    wq_full = jnp.concatenate([wq, qpad], axis=-1).reshape(Q_LORA, N_HEADS * HEAD_PAD)
    wq_perm = jnp.concatenate([jnp.zeros((Q_LORA, N_HEADS, QK_NOPE), F32),
                               _rope_perm(wq[..., QK_NOPE:]), qpad],
                              axis=-1).reshape(Q_LORA, N_HEADS * HEAD_PAD)
    wkv = w_ukv[l].reshape(KV_LORA, N_HEADS, QK_NOPE + V_HEAD)
    wk = jnp.concatenate([wkv[..., :QK_NOPE],
                          jnp.zeros((KV_LORA, N_HEADS, HEAD_PAD - QK_NOPE), F32)],
                         axis=-1).reshape(KV_LORA, N_HEADS * HEAD_PAD)
    wv = wkv[..., QK_NOPE:].reshape(KV_LORA, ATT_WIDTH)
    cidx = jnp.arange(FOURIER_WIDTH, dtype=I32)
    ang = (2.0 * math.pi / FOURIER_WIDTH) * ((cidx[:, None] * cidx[None, :]) % FOURIER_WIDTH
                                             ).astype(F32)
    fc = jnp.concatenate([jnp.cos(ang), jnp.sin(ang)], axis=1).astype(BF16)
    ng = len(POOL_WINDOWS)
    wg = jnp.zeros((ng, POOL_GROUP, ng, POOL_GROUP), F32)
    for gi in range(ng):
        wg = wg.at[gi, :, gi, :].set(w_grp[l, gi])
    return {
        "wall": wall, "n1": norm1_g[l][None, :], "n2": norm2_g[l][None, :],
        "qg": q_norm_g[l][None, :], "kvg": kv_norm_g[l][None, :],
        "wq": wq_full.astype(BF16), "wqp": wq_perm.astype(BF16),
        "wk": wk.astype(BF16), "wv": wv.astype(BF16), "fc": fc,
        "bg": b_gate[l][None, :],
        "wg": wg.reshape(POOL_WIDTH, POOL_WIDTH).astype(BF16),
        "ps": pool_scale[l][None, :],
        "woa": w_oa[l].astype(BF16), "wob": w_ob[l].astype(BF16), "woc": w_oc[l].astype(BF16),
        "wout": w_out[l].astype(BF16), "wpq": w_pq[l].astype(BF16),
    }


def _bf16_bits(w):
    return lax.bitcast_convert_type(w.astype(BF16), jnp.uint16).astype(jnp.uint32)


def _rope_tables(seq_len, rope):
    zeros_n = jnp.zeros((seq_len, QK_NOPE), F32)
    zeros_p = jnp.zeros((seq_len, HEAD_PAD - QK_NOPE - QK_ROPE), F32)
    ones_n = jnp.ones((seq_len, QK_NOPE), F32)
    if rope:
        pos = jnp.arange(seq_len, dtype=I32)
        half = QK_ROPE // 2
        inv_freq = ROPE_BASE ** (-jnp.arange(0, half, 2, dtype=F32) / half)
        ang_r = (pos // GRID_W).astype(F32)[:, None] * inv_freq
        ang_c = (pos % GRID_W).astype(F32)[:, None] * inv_freq
        cos = jnp.concatenate([jnp.cos(ang_r)] * 2 + [jnp.cos(ang_c)] * 2, axis=1)
        sin = jnp.concatenate([jnp.sin(ang_r)] * 2 + [jnp.sin(ang_c)] * 2, axis=1)
    else:
        cos = jnp.ones((seq_len, QK_ROPE), F32)
        sin = jnp.zeros((seq_len, QK_ROPE), F32)
    cq = jnp.concatenate([ones_n, cos, zeros_p], axis=1) * ATT_SCALE
    sq = jnp.concatenate([zeros_n, sin, zeros_p], axis=1) * ATT_SCALE
    ck = jnp.concatenate([zeros_n, cos, zeros_p], axis=1)
    sk = jnp.concatenate([zeros_n, sin, zeros_p], axis=1)
    return cq, sq, ck, sk


def _dft(seq_len):
    idx = jnp.arange(seq_len, dtype=I32)
    ang = (2.0 * math.pi / seq_len) * ((idx[:, None] * idx[None, :]) % seq_len).astype(F32)
    return jnp.cos(ang).astype(BF16), jnp.sin(ang).astype(BF16)


def _peer_block(x1, h2, pq, keys, mod, mod_row, seq_len, fg, table, sc_table, n_sc, final):
    t, d = x1.shape
    tile = (SUBLANES, LANES)
    eidx_n, gate_n = _route(pq, keys)
    eidx = eidx_n.T
    n_tc = t - n_sc
    mod_tiles = mod.reshape(MOD_ROWS, 6, SUBLANES, LANES)[:, 5:6]
    gate3 = gate_n[:, :n_tc].reshape(PEER_SLOTS, n_tc // (2 * PEER_TOK), 2 * PEER_TOK)
    out = _peer(eidx[:n_tc], gate3.transpose(1, 0, 2), h2[:n_tc].reshape((n_tc,) + tile),
                x1[:n_tc].reshape((n_tc,) + tile), mod_tiles, mod_row, seq_len,
                fg.reshape(tile), table, final).reshape(n_tc, d)
    if n_sc == 0:
        return out
    y = _peer_sc(eidx[n_tc:], gate_n.T[n_tc:], h2[n_tc:], sc_table)
    out_sc = _peer_finish(y, x1[n_tc:], mod, mod_row, seq_len, n_tc, fg, final)
    return jnp.concatenate([out, out_sc], axis=0)


def kernel(x, c, ctx, c_ctx, w_mod, b_mod, norm1_g, norm2_g, w_in, b_gate, q_norm_g, w_uq,
           kv_norm_g, w_ukv, w_oa, w_ob, w_grp, pool_scale, w_oc, w_out, w_pq, peer_keys,
           peer_down, peer_up, final_g):
    batch, seq_len, d = x.shape
    ctx_len = ctx.shape[1]
    depth = w_mod.shape[0]
    assert d == SUBLANES * LANES and batch + 1 <= MOD_ROWS
    assert seq_len % TOK_TILE == 0 and ctx_len % (2 * PEER_TOK) == 0 and ctx_len % LANES == 0
    tile = (SUBLANES, LANES)

    cvec = jnp.concatenate([c, c_ctx[None, :], jnp.zeros((MOD_ROWS - batch - 1, d), F32)], axis=0)
    mod_all = _modulation(cvec, w_mod, b_mod)

    def x_row(i, per_seq):
        return i // per_seq

    def c_row(i, per_seq):
        return batch

    tabs_x = _rope_tables(seq_len, True)
    tabs_c = _rope_tables(ctx_len, False)
    dft_x = _dft(seq_len)
    dft_c = _dft(ctx_len)

    xs = x.reshape(batch * seq_len, d)
    cs = ctx.reshape(batch * ctx_len, d)
    n_sc = PEER_SC_TOKENS if batch * seq_len >= 2 * PEER_SC_TOKENS else 0
    for l in range(depth):
        last = l == depth - 1
        lw = _prep_layer(l, d, w_in, b_gate, q_norm_g, w_uq, kv_norm_g, w_ukv, w_oa, w_ob, w_grp,
                         pool_scale, w_oc, w_out, w_pq, norm1_g, norm2_g)
        mod = mod_all[l].reshape(MOD_ROWS, 1, 6 * d)
        keys = peer_keys[l].reshape(2 * PEER_HEADS, N_KEYS, PEER_HALF).astype(BF16)
        table = jnp.concatenate([peer_down[l].reshape((-1,) + tile),
                                 peer_up[l].reshape((-1,) + tile)], axis=1).astype(BF16)
        sc_table = (_bf16_bits(peer_down[l]) << 16) | _bf16_bits(peer_up[l])

        if last:
            kc, vc = _in_proj(cs, mod, c_row, ctx_len, lw, tabs_c, False)
        else:
            kc, vc, qc, abc, zpc, gc = _in_proj(cs, mod, c_row, ctx_len, lw, tabs_c, True)
        kx, vx, qx, abx, zpx, gx = _in_proj(xs, mod, x_row, seq_len, lw, tabs_x, True)

        att_x = _attention(qx, kx, vx, batch, seq_len, (kc, vc))
        four_x = _fourier(abx, batch, seq_len, dft_x)
        pool_x = _pool(zpx, batch, seq_len, lw["wg"], lw["ps"])
        x1, h2, pq = _out_proj(xs, att_x, four_x, pool_x, gx, mod, x_row, seq_len, lw)
        xs = _peer_block(x1, h2, pq, keys, mod, x_row, seq_len, final_g, table, sc_table, n_sc,
                         last)

        if not last:
            att_c = _attention(qc, kc, vc, batch, ctx_len, None)
            four_c = _fourier(abc, batch, ctx_len, dft_c)
            pool_c = _pool(zpc, batch, ctx_len, lw["wg"], lw["ps"])
            c1, hc2, pqc = _out_proj(cs, att_c, four_c, pool_c, gc, mod, c_row, ctx_len, lw)
            cs = _peer_block(c1, hc2, pqc, keys, mod, c_row, ctx_len, final_g, table, None, 0,
                             False)
    return xs.reshape(batch, seq_len, d)
```

```python
import functools
import math

import jax
import jax.numpy as jnp
from jax import lax
from jax.experimental import pallas as pl
from jax.experimental.pallas import tpu as pltpu
from jax.experimental.pallas import tpu_sc as plsc

F32 = jnp.float32
BF16 = jnp.bfloat16
I32 = jnp.int32

GRID_W = 64
N_HEADS = 8
Q_LORA = 256
KV_LORA = 128
QK_NOPE = 64
QK_ROPE = 32
V_HEAD = 64
ATT_WIDTH = N_HEADS * V_HEAD
ATT_SCALE = (QK_NOPE + QK_ROPE) ** -0.5
ROPE_BASE = 10000.0
FOURIER_WIDTH = 256
POOL_WINDOWS = (2, 4, 8, 16)
POOL_GROUP = 64
POOL_WIDTH = POOL_GROUP * len(POOL_WINDOWS)
N_BRANCH = 3
PEER_HEADS = 8
N_KEYS = 128
PEER_QDIM = 256
PEER_HALF = PEER_QDIM // 2
PEER_TOPK = 16
PEER_SLOTS = PEER_HEADS * PEER_TOPK
STAIR = tuple((i, PEER_TOPK // (i + 1)) for i in range(PEER_TOPK))
N_CAND = sum(cnt for _, cnt in STAIR)
CAND_ROWS = -(-N_CAND // 8) * 8
EPS = 1e-6

LANES = 128
SUBLANES = 8
HEAD_PAD = 128
POOL_PAD = 8
MOD_ROWS = 16
VMEM_LIMIT = 48 * 1024 * 1024

TOK_TILE = 256
ATT_Q_TILE = 256
FOUR_TILE = 512
ROUTE_TILE = 128
PEER_TOK = 16
PEER_WAIT_GROUPS = 4
PEER_VMEM_LIMIT = 56 * 1024 * 1024

SC_CORES = 2
SC_SUBCORES = 16
SC_WORKERS = SC_CORES * SC_SUBCORES
SC_LANES = 16
SC_CHUNK = 32
PEER_SC_TOKENS = 10240
GELU_C = math.sqrt(2.0 / math.pi)

C_CQ = 0
C_CKV = C_CQ + Q_LORA
C_ZF = C_CKV + KV_LORA
C_ZP = C_ZF + FOURIER_WIDTH
C_ZG = C_ZP + POOL_WIDTH
C_KR = None


def _params(sem=None):
    return pltpu.CompilerParams(dimension_semantics=sem, vmem_limit_bytes=VMEM_LIMIT)


def _rms(x):
    return x * lax.rsqrt(jnp.mean(x * x, axis=-1, keepdims=True) + EPS)


def _dot(a, b):
    return jnp.dot(a, b, preferred_element_type=F32)


def _dot_nt(a, b):
    return lax.dot_general(a, b, (((1,), (1,)), ((), ())), preferred_element_type=F32)


def _mod_kernel(c_ref, w_ref, b_ref, o_ref):
    c = c_ref[...]
    s = c * jax.nn.sigmoid(c)
    o_ref[0] = jnp.dot(s, w_ref[0], preferred_element_type=F32,
                       precision=lax.Precision.HIGHEST) + b_ref[0]


def _modulation(cvec, w_mod, b_mod):
    depth, d, n = w_mod.shape
    tn = 1536
    return pl.pallas_call(
        _mod_kernel,
        out_shape=jax.ShapeDtypeStruct((depth, MOD_ROWS, n), F32),
        grid=(depth, n // tn),
        in_specs=[pl.BlockSpec((MOD_ROWS, d), lambda l, j: (0, 0)),
                  pl.BlockSpec((1, d, tn), lambda l, j: (l, 0, j)),
                  pl.BlockSpec((1, 1, tn), lambda l, j: (l, 0, j))],
        out_specs=pl.BlockSpec((1, MOD_ROWS, tn), lambda l, j: (l, 0, j)),
        compiler_params=_params(("arbitrary", "arbitrary")),
        name="modulation",
    )(cvec, w_mod, b_mod.reshape(depth, 1, n))


def _in_kernel(d, full, x_ref, m_ref, n1_ref, wall_ref, qg_ref, wq_ref, wqp_ref, kvg_ref,
               wk_ref, wv_ref, fc_ref, bg_ref, cq_ref, sq_ref, ck_ref, sk_ref, *outs):
    x = x_ref[...]
    m = m_ref[0]
    h = _rms(x) * n1_ref[...] * (1.0 + m[:, d:2 * d]) + m[:, 0:d]
    hb = h.astype(BF16)
    c_kr = C_ZG + N_BRANCH * d
    if full:
        z = _dot(hb, wall_ref[...])
        k_ref, v_ref, q_ref, ab_ref, zp_ref, g_ref = outs
    else:
        z = None
        k_ref, v_ref = outs
    def col(lo, hi):
        if full:
            return z[:, lo:hi]
        return _dot(hb, wall_ref[:, lo:hi])

    ckv = col(C_CKV, C_ZF)
    ckvn = (_rms(ckv) * kvg_ref[...]).astype(BF16)
    kf = _dot(ckvn, wk_ref[...])
    kr = col(c_kr, c_kr + LANES) * ck_ref[...] + col(c_kr + LANES, c_kr + 2 * LANES) * sk_ref[...]
    for hh in range(N_HEADS):
        sl = slice(hh * HEAD_PAD, (hh + 1) * HEAD_PAD)
        k_ref[:, sl] = (kf[:, sl] + kr).astype(BF16)
    v_ref[...] = _dot(ckvn, wv_ref[...]).astype(BF16)
    if not full:
        return
    cq = z[:, C_CQ:C_CKV]
    cqn = (_rms(cq) * qg_ref[...]).astype(BF16)
    qf = _dot(cqn, wq_ref[...])
    qr = _dot(cqn, wqp_ref[...])
    cosq = cq_ref[...]
    sinq = sq_ref[...]
    for hh in range(N_HEADS):
        sl = slice(hh * HEAD_PAD, (hh + 1) * HEAD_PAD)
        q_ref[:, sl] = (qf[:, sl] * cosq + qr[:, sl] * sinq).astype(BF16)
    ab_ref[...] = _dot(z[:, C_ZF:C_ZP].astype(BF16), fc_ref[...]).astype(BF16)
    zp_ref[...] = z[:, C_ZP:C_ZG]
    g_ref[...] = jax.nn.sigmoid(z[:, C_ZG:c_kr] + bg_ref[...]).astype(BF16)


def _in_proj(x2d, mod, mod_row, seq_len, lw, tabs, full):
    t, d = x2d.shape
    tm = min(TOK_TILE, seq_len)
    tiles_per_seq = seq_len // tm
    wall = lw["wall"]
    nw = wall.shape[1]
    cq, sq, ck, sk = tabs

    def const(shape):
        return pl.BlockSpec(shape, lambda i: (0,) * len(shape))

    def pos(i):
        return (i % tiles_per_seq, 0)

    in_specs = [
        pl.BlockSpec((tm, d), lambda i: (i, 0)),
        pl.BlockSpec((1, 1, mod.shape[-1]), lambda i: (mod_row(i, tiles_per_seq), 0, 0)),
        const((1, d)), const((d, nw)), const((1, Q_LORA)),
        const(lw["wq"].shape), const(lw["wqp"].shape), const((1, KV_LORA)),
        const(lw["wk"].shape), const(lw["wv"].shape), const(lw["fc"].shape),
        const((1, N_BRANCH * d)),
        pl.BlockSpec((tm, LANES), pos), pl.BlockSpec((tm, LANES), pos),
        pl.BlockSpec((tm, LANES), pos), pl.BlockSpec((tm, LANES), pos),
    ]
    kw = N_HEADS * HEAD_PAD
    out_shape = [jax.ShapeDtypeStruct((t, kw), BF16), jax.ShapeDtypeStruct((t, ATT_WIDTH), BF16)]
    out_specs = [pl.BlockSpec((tm, kw), lambda i: (i, 0)),
                 pl.BlockSpec((tm, ATT_WIDTH), lambda i: (i, 0))]
    if full:
        out_shape += [jax.ShapeDtypeStruct((t, kw), BF16),
                      jax.ShapeDtypeStruct((t, 2 * FOURIER_WIDTH), BF16),
                      jax.ShapeDtypeStruct((t, POOL_WIDTH), F32),
                      jax.ShapeDtypeStruct((t, N_BRANCH * d), BF16)]
        out_specs += [pl.BlockSpec((tm, kw), lambda i: (i, 0)),
                      pl.BlockSpec((tm, 2 * FOURIER_WIDTH), lambda i: (i, 0)),
                      pl.BlockSpec((tm, POOL_WIDTH), lambda i: (i, 0)),
                      pl.BlockSpec((tm, N_BRANCH * d), lambda i: (i, 0))]
    return pl.pallas_call(
        functools.partial(_in_kernel, d, full),
        out_shape=out_shape,
        grid=(t // tm,),
        in_specs=in_specs,
        out_specs=out_specs,
        compiler_params=_params(("arbitrary",)),
        name="in_proj" if full else "ctx_kv_proj",
    )(x2d, mod, lw["n1"], wall, lw["qg"], lw["wq"], lw["wqp"], lw["kvg"], lw["wk"], lw["wv"],
      lw["fc"], lw["bg"], cq, sq, ck, sk)


def _attn_kernel(has_ctx, q_ref, k_ref, v_ref, *rest):
    if has_ctx:
        kc_ref, vc_ref, o_ref = rest
    else:
        (o_ref,) = rest
    outs = []
    for hh in range(2):
        sl = slice(hh * HEAD_PAD, (hh + 1) * HEAD_PAD)
        q = q_ref[:, sl]
        s = _dot_nt(q, k_ref[:, sl])
        mx = jnp.max(s, axis=-1, keepdims=True)
        if has_ctx:
            sc = _dot_nt(q, kc_ref[:, sl])
            mx = jnp.maximum(mx, jnp.max(sc, axis=-1, keepdims=True))
        p = jnp.exp(s - mx)
        den = jnp.sum(p, axis=-1, keepdims=True)
        o = _dot(p.astype(BF16), v_ref[...])
        if has_ctx:
            pc = jnp.exp(sc - mx)
            den = den + jnp.sum(pc, axis=-1, keepdims=True)
            o = o + _dot(pc.astype(BF16), vc_ref[...])
        outs.append(o / den)
    lane = lax.broadcasted_iota(I32, outs[0].shape, 1)
    o_ref[...] = jnp.where(lane < V_HEAD, outs[0], outs[1]).astype(BF16)


def _attention(q, k, v, batch, seq_len, ctx_kv):
    t = q.shape[0]
    tq = min(ATT_Q_TILE, seq_len)
    nq = seq_len // tq
    pair_w = 2 * HEAD_PAD
    in_specs = [pl.BlockSpec((tq, pair_w), lambda b, j, i: (b * nq + i, j)),
                pl.BlockSpec((seq_len, pair_w), lambda b, j, i: (b, j)),
                pl.BlockSpec((seq_len, 2 * V_HEAD), lambda b, j, i: (b, j))]
    args = [q, k, v]
    if ctx_kv is not None:
        kc, vc = ctx_kv
        lc = kc.shape[0] // batch
        in_specs += [pl.BlockSpec((lc, pair_w), lambda b, j, i: (b, j)),
                     pl.BlockSpec((lc, 2 * V_HEAD), lambda b, j, i: (b, j))]
        args += [kc, vc]
    return pl.pallas_call(
        functools.partial(_attn_kernel, ctx_kv is not None),
        out_shape=jax.ShapeDtypeStruct((t, ATT_WIDTH), BF16),
        grid=(batch, N_HEADS // 2, nq),
        in_specs=in_specs,
        out_specs=pl.BlockSpec((tq, 2 * V_HEAD), lambda b, j, i: (b * nq + i, j)),
        compiler_params=_params(("arbitrary", "arbitrary", "arbitrary")),
        name="attention" if ctx_kv is not None else "ctx_attention",
    )(*args)


def _fourier_kernel(norm, c_ref, s_ref, ab_ref, o_ref):
    a = ab_ref[:, 0:FOURIER_WIDTH]
    b = ab_ref[:, FOURIER_WIDTH:2 * FOURIER_WIDTH]
    o = _dot(c_ref[...], a) - _dot(s_ref[...], b)
    o_ref[...] = (o * norm).astype(BF16)


def _fourier(ab, batch, seq_len, dft):
    t = ab.shape[0]
    tm = min(FOUR_TILE, seq_len)
    nt = seq_len // tm
    cl, sl = dft
    norm = 1.0 / math.sqrt(seq_len * FOURIER_WIDTH)
    return pl.pallas_call(
        functools.partial(_fourier_kernel, norm),
        out_shape=jax.ShapeDtypeStruct((t, FOURIER_WIDTH), BF16),
        grid=(nt, batch),
        in_specs=[pl.BlockSpec((tm, seq_len), lambda i, b: (i, 0)),
                  pl.BlockSpec((tm, seq_len), lambda i, b: (i, 0)),
                  pl.BlockSpec((seq_len, 2 * FOURIER_WIDTH), lambda i, b: (b, 0))],
        out_specs=pl.BlockSpec((tm, FOURIER_WIDTH), lambda i, b: (b * nt + i, 0)),
        compiler_params=_params(("arbitrary", "arbitrary")),
        name="fourier",
    )(cl, sl, ab)


def _pool_kernel(seq_len, z_ref, wg_ref, ps_ref, o_ref, pad_ref, s_ref):
    n = seq_len
    p = n + 2 * POOL_PAD
    z = z_ref[...]
    zeros = jnp.zeros((POOL_PAD, POOL_WIDTH), F32)
    pad_ref[0:POOL_PAD, :] = zeros
    pad_ref[POOL_PAD + n:p, :] = zeros
    pad_ref[POOL_PAD:POOL_PAD + n, :] = z
    s_ref[0:p - 1, :] = pad_ref[0:p - 1, :] + pad_ref[1:p, :]
    w2 = s_ref[POOL_PAD - 1:POOL_PAD - 1 + n, :]
    pad_ref[0:p - 3, :] = s_ref[0:p - 3, :] + s_ref[2:p - 1, :]
    w4 = pad_ref[POOL_PAD - 2:POOL_PAD - 2 + n, :]
    s_ref[0:p - 7, :] = pad_ref[0:p - 7, :] + pad_ref[4:p - 3, :]
    w8 = s_ref[POOL_PAD - 4:POOL_PAD - 4 + n, :]
    pad_ref[0:p - 15, :] = s_ref[0:p - 15, :] + s_ref[8:p - 7, :]
    w16 = pad_ref[0:n, :]
    pos = lax.broadcasted_iota(I32, (n, POOL_WIDTH), 0)
    grp = lax.broadcasted_iota(I32, (n, POOL_WIDTH), 1) // POOL_GROUP
    win = jnp.where(grp == 0, w2, jnp.where(grp == 1, w4, jnp.where(grp == 2, w8, w16)))
    half = jnp.where(grp == 0, 1, jnp.where(grp == 1, 2, jnp.where(grp == 2, 4, 8)))
    lo = jnp.maximum(pos - half, 0)
    hi = jnp.minimum(pos + half, n)
    cnt = (hi - lo).astype(F32)
    pooled = win / cnt - z
    y = _dot(pooled.astype(BF16), wg_ref[...])
    o_ref[...] = (y * ps_ref[...]).astype(BF16)


def _pool(zp, batch, seq_len, wg_bd, pool_scale):
    t = zp.shape[0]
    return pl.pallas_call(
        functools.partial(_pool_kernel, seq_len),
        out_shape=jax.ShapeDtypeStruct((t, POOL_WIDTH), BF16),
        grid=(batch,),
        in_specs=[pl.BlockSpec((seq_len, POOL_WIDTH), lambda b: (b, 0)),
                  pl.BlockSpec((POOL_WIDTH, POOL_WIDTH), lambda b: (0, 0)),
                  pl.BlockSpec((1, POOL_WIDTH), lambda b: (0, 0))],
        out_specs=pl.BlockSpec((seq_len, POOL_WIDTH), lambda b: (b, 0)),
        scratch_shapes=[pltpu.VMEM((seq_len + 2 * POOL_PAD, POOL_WIDTH), F32),
                        pltpu.VMEM((seq_len + 2 * POOL_PAD, POOL_WIDTH), F32)],
        compiler_params=_params(("arbitrary",)),
        name="pool",
    )(zp, wg_bd, pool_scale)


def _out_kernel(d, x_ref, att_ref, four_ref, pool_ref, g_ref, m_ref, woa_ref, wob_ref, woc_ref,
                wout_ref, n2_ref, wpq_ref, x1_ref, h2_ref, pq_ref):
    m = m_ref[0]
    ya = _dot(att_ref[...], woa_ref[...])
    yb = _dot(four_ref[...], wob_ref[...])
    yc = _dot(pool_ref[...], woc_ref[...])
    mixp = (g_ref[:, 0:d].astype(F32) * ya + g_ref[:, d:2 * d].astype(F32) * yb
            + g_ref[:, 2 * d:3 * d].astype(F32) * yc)
    mix = _dot(mixp.astype(BF16), wout_ref[...])
    x1 = x_ref[...] + m[:, 2 * d:3 * d] * mix
    x1_ref[...] = x1
    h2 = _rms(x1) * n2_ref[...] * (1.0 + m[:, 4 * d:5 * d]) + m[:, 3 * d:4 * d]
    h2_ref[...] = h2
    pq = _dot(h2.astype(BF16), wpq_ref[...])
    for hp in range(2 * PEER_HEADS):
        pq_ref[hp] = pq[:, hp * PEER_HALF:(hp + 1) * PEER_HALF].astype(BF16)


def _out_proj(x2d, att, four, pool, g, mod, mod_row, seq_len, lw):
    t, d = x2d.shape
    tm = min(TOK_TILE, seq_len)
    tiles_per_seq = seq_len // tm

    def const(shape):
        return pl.BlockSpec(shape, lambda i: (0,) * len(shape))

    def row(w):
        return pl.BlockSpec((tm, w), lambda i: (i, 0))

    nhp = 2 * PEER_HEADS
    return pl.pallas_call(
        functools.partial(_out_kernel, d),
        out_shape=[jax.ShapeDtypeStruct((t, d), F32), jax.ShapeDtypeStruct((t, d), F32),
                   jax.ShapeDtypeStruct((nhp, t, PEER_HALF), BF16)],
        grid=(t // tm,),
        in_specs=[row(d), row(ATT_WIDTH), row(FOURIER_WIDTH), row(POOL_WIDTH), row(N_BRANCH * d),
                  pl.BlockSpec((1, 1, mod.shape[-1]), lambda i: (mod_row(i, tiles_per_seq), 0, 0)),
                  const(lw["woa"].shape), const(lw["wob"].shape), const(lw["woc"].shape),
                  const(lw["wout"].shape), const((1, d)), const(lw["wpq"].shape)],
        out_specs=[row(d), row(d), pl.BlockSpec((nhp, tm, PEER_HALF), lambda i: (0, i, 0))],
        compiler_params=_params(("arbitrary",)),
        name="out_proj",
    )(x2d, att, four, pool, g, mod, lw["woa"], lw["wob"], lw["woc"], lw["wout"], lw["n2"],
      lw["wpq"])


def _select_round(s, iota, n):
    mx = jnp.max(s, axis=0, keepdims=True)
    idx = jnp.min(jnp.where(s == mx, iota, n), axis=0, keepdims=True)
    hit = iota == idx
    return mx, idx, hit, jnp.where(hit, -jnp.inf, s)


def _route_kernel(pq_ref, keys_ref, eidx_ref, gate_ref, sv_a, si_a, sv_b, si_b, cand_ref, cidx_ref,
                  ts_ref):
    tm = pq_ref.shape[1]
    iota_k = lax.broadcasted_iota(I32, (N_KEYS, tm), 0).astype(F32)
    iota_c = lax.broadcasted_iota(I32, (CAND_ROWS, tm), 0).astype(F32)
    cand_ref[N_CAND:CAND_ROWS, :] = jnp.full((CAND_ROWS - N_CAND, tm), -jnp.inf, F32)
    cidx_ref[N_CAND:CAND_ROWS, :] = jnp.zeros((CAND_ROWS - N_CAND, tm), F32)

    def sub_key_topk(hd, sv_ref, si_ref):
        s = [_dot_nt(keys_ref[2 * hd + p], pq_ref[2 * hd + p]) for p in range(2)]
        for r in range(PEER_TOPK):
            for p in range(2):
                mx, idx, _, s[p] = _select_round(s[p], iota_k, float(N_KEYS))
                sv_ref[p, r:r + 1, :] = mx
                si_ref[p, r:r + 1, :] = idx

    def pair_topk(hd, sv_ref, si_ref):
        off = 0
        for i, cnt in STAIR:
            cand_ref[off:off + cnt, :] = sv_ref[0, i:i + 1, :] + sv_ref[1, 0:cnt, :]
            cidx_ref[off:off + cnt, :] = si_ref[0, i:i + 1, :] * N_KEYS + si_ref[1, 0:cnt, :]
            off += cnt
        cidx = cidx_ref[...]
        c = cand_ref[...]
        base = pl.multiple_of(hd * PEER_TOPK, PEER_TOPK)
        for r in range(PEER_TOPK):
            mx, _, hit, c = _select_round(c, iota_c, float(CAND_ROWS))
            ts_ref[r:r + 1, :] = mx
            expert = jnp.sum(jnp.where(hit, cidx, 0.0), axis=0, keepdims=True)
            eidx_ref[pl.ds(base + r, 1), :] = expert.astype(I32)
        ts = ts_ref[...]
        ex = jnp.exp(ts - ts[0:1, :])
        gate_ref[pl.ds(base, PEER_TOPK), :] = ex / jnp.sum(ex, axis=0, keepdims=True)

    sub_key_topk(0, sv_a, si_a)

    def two_heads(j, carry):
        hd = 2 * j
        sub_key_topk(hd + 1, sv_b, si_b)
        pair_topk(hd, sv_a, si_a)
        sub_key_topk(hd + 2, sv_a, si_a)
        pair_topk(hd + 1, sv_b, si_b)
        return carry

    lax.fori_loop(0, PEER_HEADS // 2 - 1, two_heads, 0)
    sub_key_topk(PEER_HEADS - 1, sv_b, si_b)
    pair_topk(PEER_HEADS - 2, sv_a, si_a)
    pair_topk(PEER_HEADS - 1, sv_b, si_b)


def _route(pq, keys):
    nhp, t, _ = pq.shape
    tm = ROUTE_TILE
    lists = pltpu.VMEM((2, PEER_TOPK, tm), F32)
    return pl.pallas_call(
        _route_kernel,
        out_shape=[jax.ShapeDtypeStruct((PEER_SLOTS, t), I32),
                   jax.ShapeDtypeStruct((PEER_SLOTS, t), F32)],
        grid=(t // tm,),
        in_specs=[pl.BlockSpec((nhp, tm, PEER_HALF), lambda i: (0, i, 0)),
                  pl.BlockSpec((nhp, N_KEYS, PEER_HALF), lambda i: (0, 0, 0))],
        out_specs=[pl.BlockSpec((PEER_SLOTS, tm), lambda i: (0, i)),
                   pl.BlockSpec((PEER_SLOTS, tm), lambda i: (0, i))],
        scratch_shapes=[lists, lists, lists, lists,
                        pltpu.VMEM((CAND_ROWS, tm), F32),
                        pltpu.VMEM((CAND_ROWS, tm), F32),
                        pltpu.VMEM((PEER_TOPK, tm), F32)],
        compiler_params=_params(("arbitrary",)),
        name="peer_route",
    )(pq, keys)


def _fold_rows(a, b, keep_a, shift):
    return jnp.where(keep_a, a + pltpu.roll(a, SUBLANES - shift, 0), b + pltpu.roll(b, shift, 0))


def _peer_kernel(final, idx_ref, idxn_ref, gate_ref, h_ref, x_ref, g2_ref, fg_ref, tab_ref,
                 o_ref, buf_a, buf_b, act_ref, sem):
    i = pl.program_id(0)
    n = pl.num_programs(0)
    grp_tok = PEER_TOK // PEER_WAIT_GROUPS
    grp_rows = grp_tok * PEER_SLOTS
    sub = lax.broadcasted_iota(I32, (SUBLANES, LANES), 0)
    keep = {sh: (sub & sh) == 0 for sh in (4, 2, 1)}
    g2 = g2_ref[0, 0]

    def row_copy(ids, t_src, t_dst, k, buf, s):
        return pltpu.make_async_copy(tab_ref.at[ids[t_src, k]], buf.at[t_dst * PEER_SLOTS + k],
                                     sem.at[s, t_dst // grp_tok])

    def issue_token(ids, t_src, t_dst, buf, s):
        for k in range(PEER_SLOTS):
            row_copy(ids, t_src, t_dst, k, buf, s).start(priority=k % 2)

    def wait_group(buf, s, q):
        pltpu.make_async_copy(tab_ref.at[pl.ds(0, grp_rows)],
                              buf.at[pl.ds(q * grp_rows, grp_rows)], sem.at[s, q]).wait()

    def token(buf, tl, tb):
        r0 = tl * PEER_SLOTS
        hv = h_ref[tb]
        groups = []
        for g in range(PEER_SLOTS // SUBLANES):
            p = [buf[r0 + g * SUBLANES + j].astype(F32)[0:SUBLANES] * hv for j in range(SUBLANES)]
            for sh in (4, 2, 1):
                half = len(p) // 2
                p = [_fold_rows(p[j], p[j + half], keep[sh], sh) for j in range(half)]
            groups.append(p[0])
        part = jnp.concatenate(groups, axis=0)
        a = jnp.sum(part, axis=1, keepdims=True)
        act = gate_ref[0][:, tb:tb + 1] * jax.nn.gelu(a)
        act_ref[...] = jnp.broadcast_to(act, (PEER_SLOTS, LANES))
        accs = [jnp.zeros((SUBLANES, LANES), F32) for _ in range(4)]
        for k in range(PEER_SLOTS):
            up = buf[r0 + k].astype(F32)[SUBLANES:2 * SUBLANES]
            accs[k % 4] = accs[k % 4] + act_ref[k:k + 1, :] * up
        xr = x_ref[tb] + g2 * ((accs[0] + accs[1]) + (accs[2] + accs[3]))
        if final:
            ms = jnp.sum(jnp.sum(xr * xr, axis=1, keepdims=True), axis=0, keepdims=True)
            xr = xr * lax.rsqrt(ms / (SUBLANES * LANES) + EPS) * fg_ref[...]
        o_ref[tb] = xr

    @pl.when(i == 0)
    def _():
        def body(t, carry):
            issue_token(idx_ref, t, t, buf_a, 0)
            return carry
        lax.fori_loop(0, PEER_TOK, body, 0)

    for t in range(PEER_TOK):
        if t % grp_tok == 0:
            wait_group(buf_a, 0, t // grp_tok)
        issue_token(idx_ref, PEER_TOK + t, t, buf_b, 1)
        token(buf_a, t, t)
    for t in range(PEER_TOK):
        if t % grp_tok == 0:
            wait_group(buf_b, 1, t // grp_tok)
        issue_token(idxn_ref, t, t, buf_a, 0)
        token(buf_b, t, PEER_TOK + t)

    @pl.when(i == n - 1)
    def _():
        for q in range(PEER_WAIT_GROUPS):
            wait_group(buf_a, 0, q)


def _peer(eidx, gate3, h3, x3, g2, mod_row, seq_len, fg, table, final):
    t = h3.shape[0]
    tb = 2 * PEER_TOK
    nb = t // tb
    steps_per_seq = seq_len // tb
    rows = PEER_TOK * PEER_SLOTS
    tile = (SUBLANES, LANES)
    pair = (2 * SUBLANES, LANES)
    return pl.pallas_call(
        functools.partial(_peer_kernel, final),
        out_shape=jax.ShapeDtypeStruct((t,) + tile, F32),
        grid=(nb,),
        in_specs=[pl.BlockSpec((tb, PEER_SLOTS), lambda i: (i, 0), memory_space=pltpu.SMEM),
                  pl.BlockSpec((tb, PEER_SLOTS), lambda i: (jnp.minimum(i + 1, nb - 1), 0),
                               memory_space=pltpu.SMEM),
                  pl.BlockSpec((1, PEER_SLOTS, tb), lambda i: (i, 0, 0)),
                  pl.BlockSpec((tb,) + tile, lambda i: (i, 0, 0)),
                  pl.BlockSpec((tb,) + tile, lambda i: (i, 0, 0)),
                  pl.BlockSpec((1, 1) + tile, lambda i: (mod_row(i, steps_per_seq), 0, 0, 0)),
                  pl.BlockSpec(tile, lambda i: (0, 0)),
                  pl.BlockSpec(memory_space=pl.ANY)],
        out_specs=pl.BlockSpec((tb,) + tile, lambda i: (i, 0, 0)),
        scratch_shapes=[pltpu.VMEM((rows,) + pair, BF16), pltpu.VMEM((rows,) + pair, BF16),
                        pltpu.VMEM((PEER_SLOTS, LANES), F32),
                        pltpu.SemaphoreType.DMA((2, PEER_WAIT_GROUPS))],
        compiler_params=pltpu.CompilerParams(dimension_semantics=("arbitrary",),
                                             vmem_limit_bytes=PEER_VMEM_LIMIT),
        name="peer_experts",
    )(eidx, eidx, gate3, h3, x3, g2, fg, table)


def _peer_sc_body(n_tok, d, eidx_hbm, gate_hbm, h_hbm, tab_hbm, y_hbm, idx_v, idx_n, gate_v, gate_n,
                  h_v, h_n, rows_a, rows_b, out_v, sem_a, sem_b, sem_n):
    wid = lax.axis_index("s") * SC_CORES + lax.axis_index("c")
    per_worker = n_tok // SC_WORKERS
    tok0 = wid * per_worker
    nj = d // SC_LANES
    lane = lax.iota(I32, SC_LANES)
    zero = jnp.zeros((SC_LANES,), F32)
    hi_mask = jnp.full((SC_LANES,), 0xFFFF0000, jnp.uint32)

    def gather(idx_ref, c, rows, sem):
        row0 = pl.multiple_of(c * SC_CHUNK, SC_CHUNK)
        return pltpu.make_async_copy(tab_hbm.at[idx_ref.at[pl.ds(row0, SC_CHUNK)]], rows, sem)

    def copy_words(src, dst, n):
        def step(j, carry):
            off = pl.multiple_of(j * SC_LANES, SC_LANES)
            dst[pl.ds(off, SC_LANES)] = src[pl.ds(off, SC_LANES)]
            return carry
        lax.fori_loop(0, n // SC_LANES, step, 0)

    def compute(c, rows):
        row0 = pl.multiple_of(c * SC_CHUNK, SC_CHUNK)

        def down_step(j, accs):
            off = pl.multiple_of(j * SC_LANES, SC_LANES)
            hj = h_v[pl.ds(off, SC_LANES)]
            out = []
            for r in range(SC_CHUNK):
                w = rows[r, pl.ds(off, SC_LANES)]
                dn = lax.bitcast_convert_type(w & hi_mask, F32)
                out.append(accs[r] + dn * hj)
            return tuple(out)

        accs = lax.fori_loop(0, nj, down_step, tuple(zero for _ in range(SC_CHUNK)))
        acts = []
        for g in range(SC_CHUNK // SC_LANES):
            a = zero
            for r in range(SC_LANES):
                a = jnp.where(lane == r, jnp.sum(accs[g * SC_LANES + r]), a)
            gt = gate_v[pl.ds(row0 + g * SC_LANES, SC_LANES)]
            u = GELU_C * (a + 0.044715 * (a * a * a))
            th = 1.0 - 2.0 / (jnp.exp(2.0 * u) + 1.0)
            act = gt * (0.5 * a * (1.0 + th))
            for r in range(SC_LANES):
                acts.append(jnp.sum(jnp.where(lane == r, act, 0.0)))

        def up_step(j, carry):
            off = pl.multiple_of(j * SC_LANES, SC_LANES)
            o = out_v[pl.ds(off, SC_LANES)]
            for r in range(SC_CHUNK):
                w = rows[r, pl.ds(off, SC_LANES)]
                up = lax.bitcast_convert_type(w << 16, F32)
                o = o + acts[r] * up
            out_v[pl.ds(off, SC_LANES)] = o
            return carry

        lax.fori_loop(0, nj, up_step, 0)

    def token(ti, carry):
        t = tok0 + ti
        tn = jnp.minimum(t + 1, tok0 + per_worker - 1)

        def clear(j, carry2):
            out_v[pl.ds(pl.multiple_of(j * SC_LANES, SC_LANES), SC_LANES)] = zero
            return carry2

        lax.fori_loop(0, nj, clear, 0)

        def pair(pp, carry2):
            gather(idx_v, 2 * pp + 1, rows_b, sem_b).start()
            gather(idx_v, 2 * pp, rows_a, sem_a).wait()
            compute(2 * pp, rows_a)

            @pl.when(pp == 0)
            def _():
                gather(idx_v, 2, rows_a, sem_a).start()

            @pl.when(pp == 1)
            def _():
                pltpu.sync_copy(eidx_hbm.at[tn], idx_n)
                gather(idx_n, 0, rows_a, sem_a).start()
                pltpu.make_async_copy(gate_hbm.at[tn], gate_n, sem_n).start()
                pltpu.make_async_copy(h_hbm.at[tn], h_n, sem_n).start()

            gather(idx_v, 2 * pp + 1, rows_b, sem_b).wait()
            compute(2 * pp + 1, rows_b)
            return carry2

        lax.fori_loop(0, PEER_SLOTS // (2 * SC_CHUNK), pair, 0)
        pltpu.sync_copy(out_v, y_hbm.at[t])
        pltpu.make_async_copy(gate_hbm.at[tn], gate_n, sem_n).wait()
        pltpu.make_async_copy(h_hbm.at[tn], h_n, sem_n).wait()
        copy_words(idx_n, idx_v, PEER_SLOTS)
        copy_words(gate_n, gate_v, PEER_SLOTS)
        copy_words(h_n, h_v, d)
        return carry

    pltpu.sync_copy(eidx_hbm.at[tok0], idx_v)
    pltpu.sync_copy(gate_hbm.at[tok0], gate_v)
    pltpu.sync_copy(h_hbm.at[tok0], h_v)
    gather(idx_v, 0, rows_a, sem_a).start()
    lax.fori_loop(0, per_worker, token, 0)
    gather(idx_v, 0, rows_a, sem_a).wait()


def _peer_sc(eidx, gate, h2, sc_table):
    n_tok, d = h2.shape
    mesh = plsc.VectorSubcoreMesh(core_axis_name="c", subcore_axis_name="s",
                                  num_cores=SC_CORES, num_subcores=SC_SUBCORES)
    return pl.kernel(
        functools.partial(_peer_sc_body, n_tok, d),
        out_type=jax.ShapeDtypeStruct((n_tok, d), F32),
        mesh=mesh,
        scratch_types=[pltpu.VMEM((PEER_SLOTS,), I32), pltpu.VMEM((PEER_SLOTS,), I32),
                       pltpu.VMEM((PEER_SLOTS,), F32), pltpu.VMEM((PEER_SLOTS,), F32),
                       pltpu.VMEM((d,), F32), pltpu.VMEM((d,), F32),
                       pltpu.VMEM((SC_CHUNK, d), jnp.uint32), pltpu.VMEM((SC_CHUNK, d), jnp.uint32),
                       pltpu.VMEM((d,), F32), pltpu.SemaphoreType.DMA, pltpu.SemaphoreType.DMA,
                       pltpu.SemaphoreType.DMA],
        compiler_params=pltpu.CompilerParams(needs_layout_passes=False),
        name="peer_experts_sc",
    )(eidx, gate, h2, sc_table)


def _finish_kernel(final, d, y_ref, x_ref, m_ref, fg_ref, o_ref):
    xr = x_ref[...] + m_ref[0][:, 5 * d:6 * d] * y_ref[...]
    if final:
        xr = _rms(xr) * fg_ref[...]
    o_ref[...] = xr


def _peer_finish(y, x1, mod, mod_row, seq_len, tok0, fg, final):
    t, d = x1.shape
    tm = min(TOK_TILE, seq_len)
    tiles_per_seq = seq_len // tm
    tile0 = tok0 // tm
    return pl.pallas_call(
        functools.partial(_finish_kernel, final, d),
        out_shape=jax.ShapeDtypeStruct((t, d), F32),
        grid=(t // tm,),
        in_specs=[pl.BlockSpec((tm, d), lambda i: (i, 0)),
                  pl.BlockSpec((tm, d), lambda i: (i, 0)),
                  pl.BlockSpec((1, 1, mod.shape[-1]),
                               lambda i: (mod_row(i + tile0, tiles_per_seq), 0, 0)),
                  pl.BlockSpec((1, d), lambda i: (0, 0))],
        out_specs=pl.BlockSpec((tm, d), lambda i: (i, 0)),
        compiler_params=_params(("arbitrary",)),
        name="peer_finish",
    )(y, x1, mod, fg.reshape(1, d))


def _rope_perm(w):
    q = QK_ROPE // 4
    a1, a2, b1, b2 = (w[..., j * q:(j + 1) * q] for j in range(4))
    return jnp.concatenate([-a2, a1, -b2, b1], axis=-1)


def _prep_layer(l, d, w_in, b_gate, q_norm_g, w_uq, kv_norm_g, w_ukv, w_oa, w_ob, w_grp,
                pool_scale, w_oc, w_out, w_pq, norm1_g, norm2_g):
    wi = w_in[l]
    s0 = Q_LORA
    s1 = s0 + KV_LORA
    s2 = s1 + QK_ROPE
    s3 = s2 + FOURIER_WIDTH
    s4 = s3 + POOL_WIDTH
    w_kr = wi[:, s1:s2]
    zl = jnp.zeros((d, QK_NOPE), F32)
    zr = jnp.zeros((d, HEAD_PAD - QK_NOPE - QK_ROPE), F32)
    wall = jnp.concatenate([wi[:, 0:s1], wi[:, s2:], zl, w_kr, zr, zl, _rope_perm(w_kr), zr],
                           axis=1).astype(BF16)
    wq = w_uq[l].reshape(Q_LORA, N_HEADS, QK_NOPE + QK_ROPE)
    qpad = jnp.zeros((Q_LORA, N_HEADS, HEAD_PAD - QK_NOPE - QK_ROPE), F32)
    wq_full = jnp.concatenate([wq, qpad], axis=-1).reshape(Q_LORA, N_HEADS * HEAD_PAD)
    wq_perm = jnp.concatenate([jnp.zeros((Q_LORA, N_HEADS, QK_NOPE), F32),
                               _rope_perm(wq[..., QK_NOPE:]), qpad],
                              axis=-1).reshape(Q_LORA, N_HEADS * HEAD_PAD)
    wkv = w_ukv[l].reshape(KV_LORA, N_HEADS, QK_NOPE + V_HEAD)
    wk = jnp.concatenate([wkv[..., :QK_NOPE],
                          jnp.zeros((KV_LORA, N_HEADS, HEAD_PAD - QK_NOPE), F32)],
                         axis=-1).reshape(KV_LORA, N_HEADS * HEAD_PAD)
    wv = wkv[..., QK_NOPE:].reshape(KV_LORA, ATT_WIDTH)
    cidx = jnp.arange(FOURIER_WIDTH, dtype=I32)
    ang = (2.0 * math.pi / FOURIER_WIDTH) * ((cidx[:, None] * cidx[None, :]) % FOURIER_WIDTH
                                             ).astype(F32)
    fc = jnp.concatenate([jnp.cos(ang), jnp.sin(ang)], axis=1).astype(BF16)
    ng = len(POOL_WINDOWS)
    wg = jnp.zeros((ng, POOL_GROUP, ng, POOL_GROUP), F32)
    for gi in range(ng):
        wg = wg.at[gi, :, gi, :].set(w_grp[l, gi])
    return {
        "wall": wall, "n1": norm1_g[l][None, :], "n2": norm2_g[l][None, :],
        "qg": q_norm_g[l][None, :], "kvg": kv_norm_g[l][None, :],
        "wq": wq_full.astype(BF16), "wqp": wq_perm.astype(BF16),
        "wk": wk.astype(BF16), "wv": wv.astype(BF16), "fc": fc,
        "bg": b_gate[l][None, :],
        "wg": wg.reshape(POOL_WIDTH, POOL_WIDTH).astype(BF16),
        "ps": pool_scale[l][None, :],
        "woa": w_oa[l].astype(BF16), "wob": w_ob[l].astype(BF16), "woc": w_oc[l].astype(BF16),
        "wout": w_out[l].astype(BF16), "wpq": w_pq[l].astype(BF16),
    }


def _bf16_bits(w):
    return lax.bitcast_convert_type(w.astype(BF16), jnp.uint16).astype(jnp.uint32)


def _rope_tables(seq_len, rope):
    zeros_n = jnp.zeros((seq_len, QK_NOPE), F32)
    zeros_p = jnp.zeros((seq_len, HEAD_PAD - QK_NOPE - QK_ROPE), F32)
    ones_n = jnp.ones((seq_len, QK_NOPE), F32)
    if rope:
        pos = jnp.arange(seq_len, dtype=I32)
        half = QK_ROPE // 2
        inv_freq = ROPE_BASE ** (-jnp.arange(0, half, 2, dtype=F32) / half)
        ang_r = (pos // GRID_W).astype(F32)[:, None] * inv_freq
        ang_c = (pos % GRID_W).astype(F32)[:, None] * inv_freq
        cos = jnp.concatenate([jnp.cos(ang_r)] * 2 + [jnp.cos(ang_c)] * 2, axis=1)
        sin = jnp.concatenate([jnp.sin(ang_r)] * 2 + [jnp.sin(ang_c)] * 2, axis=1)
    else:
        cos = jnp.ones((seq_len, QK_ROPE), F32)
        sin = jnp.zeros((seq_len, QK_ROPE), F32)
    cq = jnp.concatenate([ones_n, cos, zeros_p], axis=1) * ATT_SCALE
    sq = jnp.concatenate([zeros_n, sin, zeros_p], axis=1) * ATT_SCALE
    ck = jnp.concatenate([zeros_n, cos, zeros_p], axis=1)
    sk = jnp.concatenate([zeros_n, sin, zeros_p], axis=1)
    return cq, sq, ck, sk


def _dft(seq_len):
    idx = jnp.arange(seq_len, dtype=I32)
    ang = (2.0 * math.pi / seq_len) * ((idx[:, None] * idx[None, :]) % seq_len).astype(F32)
    return jnp.cos(ang).astype(BF16), jnp.sin(ang).astype(BF16)


def _peer_block(x1, h2, pq, keys, mod, mod_row, seq_len, fg, table, sc_table, n_sc, final):
    t, d = x1.shape
    tile = (SUBLANES, LANES)
    eidx_n, gate_n = _route(pq, keys)
    eidx = eidx_n.T
    n_tc = t - n_sc
    mod_tiles = mod.reshape(MOD_ROWS, 6, SUBLANES, LANES)[:, 5:6]
    gate3 = gate_n[:, :n_tc].reshape(PEER_SLOTS, n_tc // (2 * PEER_TOK), 2 * PEER_TOK)
    out = _peer(eidx[:n_tc], gate3.transpose(1, 0, 2), h2[:n_tc].reshape((n_tc,) + tile),
                x1[:n_tc].reshape((n_tc,) + tile), mod_tiles, mod_row, seq_len,
                fg.reshape(tile), table, final).reshape(n_tc, d)
    if n_sc == 0:
        return out
    y = _peer_sc(eidx[n_tc:], gate_n.T[n_tc:], h2[n_tc:], sc_table)
    out_sc = _peer_finish(y, x1[n_tc:], mod, mod_row, seq_len, n_tc, fg, final)
    return jnp.concatenate([out, out_sc], axis=0)


def kernel(x, c, ctx, c_ctx, w_mod, b_mod, norm1_g, norm2_g, w_in, b_gate, q_norm_g, w_uq,
           kv_norm_g, w_ukv, w_oa, w_ob, w_grp, pool_scale, w_oc, w_out, w_pq, peer_keys,
           peer_down, peer_up, final_g):
    batch, seq_len, d = x.shape
    ctx_len = ctx.shape[1]
    depth = w_mod.shape[0]
    assert d == SUBLANES * LANES and batch + 1 <= MOD_ROWS
    assert seq_len % TOK_TILE == 0 and ctx_len % (2 * PEER_TOK) == 0 and ctx_len % LANES == 0
    tile = (SUBLANES, LANES)

    cvec = jnp.concatenate([c, c_ctx[None, :], jnp.zeros((MOD_ROWS - batch - 1, d), F32)], axis=0)
    mod_all = _modulation(cvec, w_mod, b_mod)

    def x_row(i, per_seq):
        return i // per_seq

    def c_row(i, per_seq):
        return batch

    tabs_x = _rope_tables(seq_len, True)
    tabs_c = _rope_tables(ctx_len, False)
    dft_x = _dft(seq_len)
    dft_c = _dft(ctx_len)

    xs = x.reshape(batch * seq_len, d)
    cs = ctx.reshape(batch * ctx_len, d)
    n_sc = PEER_SC_TOKENS if batch * seq_len >= 2 * PEER_SC_TOKENS else 0
    for l in range(depth):
        last = l == depth - 1
        lw = _prep_layer(l, d, w_in, b_gate, q_norm_g, w_uq, kv_norm_g, w_ukv, w_oa, w_ob, w_grp,
                         pool_scale, w_oc, w_out, w_pq, norm1_g, norm2_g)
        mod = mod_all[l].reshape(MOD_ROWS, 1, 6 * d)
        keys = peer_keys[l].reshape(2 * PEER_HEADS, N_KEYS, PEER_HALF).astype(BF16)
        table = jnp.concatenate([peer_down[l].reshape((-1,) + tile),
                                 peer_up[l].reshape((-1,) + tile)], axis=1).astype(BF16)
        sc_table = (_bf16_bits(peer_down[l]) << 16) | _bf16_bits(peer_up[l])

        if last:
            kc, vc = _in_proj(cs, mod, c_row, ctx_len, lw, tabs_c, False)
        else:
            kc, vc, qc, abc, zpc, gc = _in_proj(cs, mod, c_row, ctx_len, lw, tabs_c, True)
        kx, vx, qx, abx, zpx, gx = _in_proj(xs, mod, x_row, seq_len, lw, tabs_x, True)

        att_x = _attention(qx, kx, vx, batch, seq_len, (kc, vc))
        four_x = _fourier(abx, batch, seq_len, dft_x)
        pool_x = _pool(zpx, batch, seq_len, lw["wg"], lw["ps"])
        x1, h2, pq = _out_proj(xs, att_x, four_x, pool_x, gx, mod, x_row, seq_len, lw)
        xs = _peer_block(x1, h2, pq, keys, mod, x_row, seq_len, final_g, table, sc_table, n_sc,
                         last)

        if not last:
            att_c = _attention(qc, kc, vc, batch, ctx_len, None)
            four_c = _fourier(abc, batch, ctx_len, dft_c)
            pool_c = _pool(zpc, batch, ctx_len, lw["wg"], lw["ps"])
            c1, hc2, pqc = _out_proj(cs, att_c, four_c, pool_c, gc, mod, c_row, ctx_len, lw)
            cs = _peer_block(c1, hc2, pqc, keys, mod, c_row, ctx_len, final_g, table, None, 0,
                             False)
    return xs.reshape(batch, seq_len, d)
```

```python
import functools
import math

import jax
import jax.numpy as jnp
from jax import lax
from jax.experimental import pallas as pl
from jax.experimental.pallas import tpu as pltpu
from jax.experimental.pallas import tpu_sc as plsc

F32 = jnp.float32
BF16 = jnp.bfloat16
I32 = jnp.int32

GRID_W = 64
N_HEADS = 8
Q_LORA = 256
KV_LORA = 128
QK_NOPE = 64
QK_ROPE = 32
V_HEAD = 64
ATT_WIDTH = N_HEADS * V_HEAD
ATT_SCALE = (QK_NOPE + QK_ROPE) ** -0.5
ROPE_BASE = 10000.0
FOURIER_WIDTH = 256
POOL_WINDOWS = (2, 4, 8, 16)
POOL_GROUP = 64
POOL_WIDTH = POOL_GROUP * len(POOL_WINDOWS)
N_BRANCH = 3
PEER_HEADS = 8
N_KEYS = 128
PEER_QDIM = 256
PEER_HALF = PEER_QDIM // 2
PEER_TOPK = 16
PEER_SLOTS = PEER_HEADS * PEER_TOPK
STAIR = tuple((i, PEER_TOPK // (i + 1)) for i in range(PEER_TOPK))
N_CAND = sum(cnt for _, cnt in STAIR)
CAND_ROWS = -(-N_CAND // 8) * 8
EPS = 1e-6

LANES = 128
SUBLANES = 8
HEAD_PAD = 128
POOL_PAD = 8
MOD_ROWS = 16
VMEM_LIMIT = 48 * 1024 * 1024

TOK_TILE = 256
ATT_Q_TILE = 256
FOUR_TILE = 512
ROUTE_TILE = 128
PEER_TOK = 16
PEER_WAIT_GROUPS = 4
PEER_VMEM_LIMIT = 56 * 1024 * 1024

SC_CORES = 2
SC_SUBCORES = 16
SC_WORKERS = SC_CORES * SC_SUBCORES
SC_LANES = 16
SC_CHUNK = 32
PEER_CHAINS = 2
PEER_SC_TOKENS = 5888
GELU_C = math.sqrt(2.0 / math.pi)

C_CQ = 0
C_CKV = C_CQ + Q_LORA
C_ZF = C_CKV + KV_LORA
C_ZP = C_ZF + FOURIER_WIDTH
C_ZG = C_ZP + POOL_WIDTH
C_KR = None


def _params(sem=None):
    return pltpu.CompilerParams(dimension_semantics=sem, vmem_limit_bytes=VMEM_LIMIT)


def _rms(x):
    return x * lax.rsqrt(jnp.mean(x * x, axis=-1, keepdims=True) + EPS)


def _dot(a, b):
    return jnp.dot(a, b, preferred_element_type=F32)


def _dot_nt(a, b):
    return lax.dot_general(a, b, (((1,), (1,)), ((), ())), preferred_element_type=F32)


def _mod_kernel(c_ref, w_ref, b_ref, o_ref):
    c = c_ref[...]
    s = c * jax.nn.sigmoid(c)
    o_ref[0] = jnp.dot(s, w_ref[0], preferred_element_type=F32,
                       precision=lax.Precision.HIGHEST) + b_ref[0]


def _modulation(cvec, w_mod, b_mod):
    depth, d, n = w_mod.shape
    tn = 1536
    return pl.pallas_call(
        _mod_kernel,
        out_shape=jax.ShapeDtypeStruct((depth, MOD_ROWS, n), F32),
        grid=(depth, n // tn),
        in_specs=[pl.BlockSpec((MOD_ROWS, d), lambda l, j: (0, 0)),
                  pl.BlockSpec((1, d, tn), lambda l, j: (l, 0, j)),
                  pl.BlockSpec((1, 1, tn), lambda l, j: (l, 0, j))],
        out_specs=pl.BlockSpec((1, MOD_ROWS, tn), lambda l, j: (l, 0, j)),
        compiler_params=_params(("arbitrary", "arbitrary")),
        name="modulation",
    )(cvec, w_mod, b_mod.reshape(depth, 1, n))


def _in_kernel(d, full, x_ref, m_ref, n1_ref, wall_ref, qg_ref, wq_ref, wqp_ref, kvg_ref,
               wk_ref, wv_ref, fc_ref, bg_ref, cq_ref, sq_ref, ck_ref, sk_ref, *outs):
    x = x_ref[...]
    m = m_ref[0]
    h = _rms(x) * n1_ref[...] * (1.0 + m[:, d:2 * d]) + m[:, 0:d]
    hb = h.astype(BF16)
    c_kr = C_ZG + N_BRANCH * d
    if full:
        z = _dot(hb, wall_ref[...])
        k_ref, v_ref, q_ref, ab_ref, zp_ref, g_ref = outs
    else:
        z = None
        k_ref, v_ref = outs
    def col(lo, hi):
        if full:
            return z[:, lo:hi]
        return _dot(hb, wall_ref[:, lo:hi])

    ckv = col(C_CKV, C_ZF)
    ckvn = (_rms(ckv) * kvg_ref[...]).astype(BF16)
    kf = _dot(ckvn, wk_ref[...])
    kr = col(c_kr, c_kr + LANES) * ck_ref[...] + col(c_kr + LANES, c_kr + 2 * LANES) * sk_ref[...]
    for hh in range(N_HEADS):
        sl = slice(hh * HEAD_PAD, (hh + 1) * HEAD_PAD)
        k_ref[:, sl] = (kf[:, sl] + kr).astype(BF16)
    v_ref[...] = _dot(ckvn, wv_ref[...]).astype(BF16)
    if not full:
        return
    cq = z[:, C_CQ:C_CKV]
    cqn = (_rms(cq) * qg_ref[...]).astype(BF16)
    qf = _dot(cqn, wq_ref[...])
    qr = _dot(cqn, wqp_ref[...])
    cosq = cq_ref[...]
    sinq = sq_ref[...]
    for hh in range(N_HEADS):
        sl = slice(hh * HEAD_PAD, (hh + 1) * HEAD_PAD)
        q_ref[:, sl] = (qf[:, sl] * cosq + qr[:, sl] * sinq).astype(BF16)
    ab_ref[...] = _dot(z[:, C_ZF:C_ZP].astype(BF16), fc_ref[...]).astype(BF16)
    zp_ref[...] = z[:, C_ZP:C_ZG]
    g_ref[...] = jax.nn.sigmoid(z[:, C_ZG:c_kr] + bg_ref[...]).astype(BF16)


def _in_proj(x2d, mod, mod_row, seq_len, lw, tabs, full):
    t, d = x2d.shape
    tm = min(TOK_TILE, seq_len)
    tiles_per_seq = seq_len // tm
    wall = lw["wall"]
    nw = wall.shape[1]
    cq, sq, ck, sk = tabs

    def const(shape):
        return pl.BlockSpec(shape, lambda i: (0,) * len(shape))

    def pos(i):
        return (i % tiles_per_seq, 0)

    in_specs = [
        pl.BlockSpec((tm, d), lambda i: (i, 0)),
        pl.BlockSpec((1, 1, mod.shape[-1]), lambda i: (mod_row(i, tiles_per_seq), 0, 0)),
        const((1, d)), const((d, nw)), const((1, Q_LORA)),
        const(lw["wq"].shape), const(lw["wqp"].shape), const((1, KV_LORA)),
        const(lw["wk"].shape), const(lw["wv"].shape), const(lw["fc"].shape),
        const((1, N_BRANCH * d)),
        pl.BlockSpec((tm, LANES), pos), pl.BlockSpec((tm, LANES), pos),
        pl.BlockSpec((tm, LANES), pos), pl.BlockSpec((tm, LANES), pos),
    ]
    kw = N_HEADS * HEAD_PAD
    out_shape = [jax.ShapeDtypeStruct((t, kw), BF16), jax.ShapeDtypeStruct((t, ATT_WIDTH), BF16)]
    out_specs = [pl.BlockSpec((tm, kw), lambda i: (i, 0)),
                 pl.BlockSpec((tm, ATT_WIDTH), lambda i: (i, 0))]
    if full:
        out_shape += [jax.ShapeDtypeStruct((t, kw), BF16),
                      jax.ShapeDtypeStruct((t, 2 * FOURIER_WIDTH), BF16),
                      jax.ShapeDtypeStruct((t, POOL_WIDTH), F32),
                      jax.ShapeDtypeStruct((t, N_BRANCH * d), BF16)]
        out_specs += [pl.BlockSpec((tm, kw), lambda i: (i, 0)),
                      pl.BlockSpec((tm, 2 * FOURIER_WIDTH), lambda i: (i, 0)),
                      pl.BlockSpec((tm, POOL_WIDTH), lambda i: (i, 0)),
                      pl.BlockSpec((tm, N_BRANCH * d), lambda i: (i, 0))]
    return pl.pallas_call(
        functools.partial(_in_kernel, d, full),
        out_shape=out_shape,
        grid=(t // tm,),
        in_specs=in_specs,
        out_specs=out_specs,
        compiler_params=_params(("arbitrary",)),
        name="in_proj" if full else "ctx_kv_proj",
    )(x2d, mod, lw["n1"], wall, lw["qg"], lw["wq"], lw["wqp"], lw["kvg"], lw["wk"], lw["wv"],
      lw["fc"], lw["bg"], cq, sq, ck, sk)


def _attn_kernel(has_ctx, q_ref, k_ref, v_ref, *rest):
    if has_ctx:
        kc_ref, vc_ref, o_ref = rest
    else:
        (o_ref,) = rest
    outs = []
    for hh in range(2):
        sl = slice(hh * HEAD_PAD, (hh + 1) * HEAD_PAD)
        q = q_ref[:, sl]
        s = _dot_nt(q, k_ref[:, sl])
        mx = jnp.max(s, axis=-1, keepdims=True)
        if has_ctx:
            sc = _dot_nt(q, kc_ref[:, sl])
            mx = jnp.maximum(mx, jnp.max(sc, axis=-1, keepdims=True))
        p = jnp.exp(s - mx)
        den = jnp.sum(p, axis=-1, keepdims=True)
        o = _dot(p.astype(BF16), v_ref[...])
        if has_ctx:
            pc = jnp.exp(sc - mx)
            den = den + jnp.sum(pc, axis=-1, keepdims=True)
            o = o + _dot(pc.astype(BF16), vc_ref[...])
        outs.append(o / den)
    lane = lax.broadcasted_iota(I32, outs[0].shape, 1)
    o_ref[...] = jnp.where(lane < V_HEAD, outs[0], outs[1]).astype(BF16)


def _attention(q, k, v, batch, seq_len, ctx_kv):
    t = q.shape[0]
    tq = min(ATT_Q_TILE, seq_len)
    nq = seq_len // tq
    pair_w = 2 * HEAD_PAD
    in_specs = [pl.BlockSpec((tq, pair_w), lambda b, j, i: (b * nq + i, j)),
                pl.BlockSpec((seq_len, pair_w), lambda b, j, i: (b, j)),
                pl.BlockSpec((seq_len, 2 * V_HEAD), lambda b, j, i: (b, j))]
    args = [q, k, v]
    if ctx_kv is not None:
        kc, vc = ctx_kv
        lc = kc.shape[0] // batch
        in_specs += [pl.BlockSpec((lc, pair_w), lambda b, j, i: (b, j)),
                     pl.BlockSpec((lc, 2 * V_HEAD), lambda b, j, i: (b, j))]
        args += [kc, vc]
    return pl.pallas_call(
        functools.partial(_attn_kernel, ctx_kv is not None),
        out_shape=jax.ShapeDtypeStruct((t, ATT_WIDTH), BF16),
        grid=(batch, N_HEADS // 2, nq),
        in_specs=in_specs,
        out_specs=pl.BlockSpec((tq, 2 * V_HEAD), lambda b, j, i: (b * nq + i, j)),
        compiler_params=_params(("arbitrary", "arbitrary", "arbitrary")),
        name="attention" if ctx_kv is not None else "ctx_attention",
    )(*args)


def _fourier_kernel(norm, c_ref, s_ref, ab_ref, o_ref):
    a = ab_ref[:, 0:FOURIER_WIDTH]
    b = ab_ref[:, FOURIER_WIDTH:2 * FOURIER_WIDTH]
    o = _dot(c_ref[...], a) - _dot(s_ref[...], b)
    o_ref[...] = (o * norm).astype(BF16)


def _fourier(ab, batch, seq_len, dft):
    t = ab.shape[0]
    tm = min(FOUR_TILE, seq_len)
    nt = seq_len // tm
    cl, sl = dft
    norm = 1.0 / math.sqrt(seq_len * FOURIER_WIDTH)
    return pl.pallas_call(
        functools.partial(_fourier_kernel, norm),
        out_shape=jax.ShapeDtypeStruct((t, FOURIER_WIDTH), BF16),
        grid=(nt, batch),
        in_specs=[pl.BlockSpec((tm, seq_len), lambda i, b: (i, 0)),
                  pl.BlockSpec((tm, seq_len), lambda i, b: (i, 0)),
                  pl.BlockSpec((seq_len, 2 * FOURIER_WIDTH), lambda i, b: (b, 0))],
        out_specs=pl.BlockSpec((tm, FOURIER_WIDTH), lambda i, b: (b * nt + i, 0)),
        compiler_params=_params(("arbitrary", "arbitrary")),
        name="fourier",
    )(cl, sl, ab)


def _pool_kernel(seq_len, z_ref, wg_ref, ps_ref, o_ref, pad_ref, s_ref):
    n = seq_len
    p = n + 2 * POOL_PAD
    z = z_ref[...]
    zeros = jnp.zeros((POOL_PAD, POOL_WIDTH), F32)
    pad_ref[0:POOL_PAD, :] = zeros
    pad_ref[POOL_PAD + n:p, :] = zeros
    pad_ref[POOL_PAD:POOL_PAD + n, :] = z
    s_ref[0:p - 1, :] = pad_ref[0:p - 1, :] + pad_ref[1:p, :]
    w2 = s_ref[POOL_PAD - 1:POOL_PAD - 1 + n, :]
    pad_ref[0:p - 3, :] = s_ref[0:p - 3, :] + s_ref[2:p - 1, :]
    w4 = pad_ref[POOL_PAD - 2:POOL_PAD - 2 + n, :]
    s_ref[0:p - 7, :] = pad_ref[0:p - 7, :] + pad_ref[4:p - 3, :]
    w8 = s_ref[POOL_PAD - 4:POOL_PAD - 4 + n, :]
    pad_ref[0:p - 15, :] = s_ref[0:p - 15, :] + s_ref[8:p - 7, :]
    w16 = pad_ref[0:n, :]
    pos = lax.broadcasted_iota(I32, (n, POOL_WIDTH), 0)
    grp = lax.broadcasted_iota(I32, (n, POOL_WIDTH), 1) // POOL_GROUP
    win = jnp.where(grp == 0, w2, jnp.where(grp == 1, w4, jnp.where(grp == 2, w8, w16)))
    half = jnp.where(grp == 0, 1, jnp.where(grp == 1, 2, jnp.where(grp == 2, 4, 8)))
    lo = jnp.maximum(pos - half, 0)
    hi = jnp.minimum(pos + half, n)
    cnt = (hi - lo).astype(F32)
    pooled = win / cnt - z
    y = _dot(pooled.astype(BF16), wg_ref[...])
    o_ref[...] = (y * ps_ref[...]).astype(BF16)


def _pool(zp, batch, seq_len, wg_bd, pool_scale):
    t = zp.shape[0]
    return pl.pallas_call(
        functools.partial(_pool_kernel, seq_len),
        out_shape=jax.ShapeDtypeStruct((t, POOL_WIDTH), BF16),
        grid=(batch,),
        in_specs=[pl.BlockSpec((seq_len, POOL_WIDTH), lambda b: (b, 0)),
                  pl.BlockSpec((POOL_WIDTH, POOL_WIDTH), lambda b: (0, 0)),
                  pl.BlockSpec((1, POOL_WIDTH), lambda b: (0, 0))],
        out_specs=pl.BlockSpec((seq_len, POOL_WIDTH), lambda b: (b, 0)),
        scratch_shapes=[pltpu.VMEM((seq_len + 2 * POOL_PAD, POOL_WIDTH), F32),
                        pltpu.VMEM((seq_len + 2 * POOL_PAD, POOL_WIDTH), F32)],
        compiler_params=_params(("arbitrary",)),
        name="pool",
    )(zp, wg_bd, pool_scale)


def _out_kernel(d, x_ref, att_ref, four_ref, pool_ref, g_ref, m_ref, woa_ref, wob_ref, woc_ref,
                wout_ref, n2_ref, wpq_ref, x1_ref, h2_ref, pq_ref):
    m = m_ref[0]
    ya = _dot(att_ref[...], woa_ref[...])
    yb = _dot(four_ref[...], wob_ref[...])
    yc = _dot(pool_ref[...], woc_ref[...])
    mixp = (g_ref[:, 0:d].astype(F32) * ya + g_ref[:, d:2 * d].astype(F32) * yb
            + g_ref[:, 2 * d:3 * d].astype(F32) * yc)
    mix = _dot(mixp.astype(BF16), wout_ref[...])
    x1 = x_ref[...] + m[:, 2 * d:3 * d] * mix
    x1_ref[...] = x1
    h2 = _rms(x1) * n2_ref[...] * (1.0 + m[:, 4 * d:5 * d]) + m[:, 3 * d:4 * d]
    h2_ref[...] = h2
    pq = _dot(h2.astype(BF16), wpq_ref[...])
    for hp in range(2 * PEER_HEADS):
        pq_ref[hp] = pq[:, hp * PEER_HALF:(hp + 1) * PEER_HALF].astype(BF16)


def _out_proj(x2d, att, four, pool, g, mod, mod_row, seq_len, lw):
    t, d = x2d.shape
    tm = min(TOK_TILE, seq_len)
    tiles_per_seq = seq_len // tm

    def const(shape):
        return pl.BlockSpec(shape, lambda i: (0,) * len(shape))

    def row(w):
        return pl.BlockSpec((tm, w), lambda i: (i, 0))

    nhp = 2 * PEER_HEADS
    return pl.pallas_call(
        functools.partial(_out_kernel, d),
        out_shape=[jax.ShapeDtypeStruct((t, d), F32), jax.ShapeDtypeStruct((t, d), F32),
                   jax.ShapeDtypeStruct((nhp, t, PEER_HALF), BF16)],
        grid=(t // tm,),
        in_specs=[row(d), row(ATT_WIDTH), row(FOURIER_WIDTH), row(POOL_WIDTH), row(N_BRANCH * d),
                  pl.BlockSpec((1, 1, mod.shape[-1]), lambda i: (mod_row(i, tiles_per_seq), 0, 0)),
                  const(lw["woa"].shape), const(lw["wob"].shape), const(lw["woc"].shape),
                  const(lw["wout"].shape), const((1, d)), const(lw["wpq"].shape)],
        out_specs=[row(d), row(d), pl.BlockSpec((nhp, tm, PEER_HALF), lambda i: (0, i, 0))],
        compiler_params=_params(("arbitrary",)),
        name="out_proj",
    )(x2d, att, four, pool, g, mod, lw["woa"], lw["wob"], lw["woc"], lw["wout"], lw["n2"],
      lw["wpq"])


def _select_round(s, iota, n):
    mx = jnp.max(s, axis=0, keepdims=True)
    idx = jnp.min(jnp.where(s == mx, iota, n), axis=0, keepdims=True)
    hit = iota == idx
    return mx, idx, hit, jnp.where(hit, -jnp.inf, s)


def _route_kernel(pq_ref, keys_ref, eidx_ref, gate_ref, sv_a, si_a, sv_b, si_b, cand_ref, cidx_ref,
                  ts_ref):
    tm = pq_ref.shape[1]
    iota_k = lax.broadcasted_iota(I32, (N_KEYS, tm), 0).astype(F32)
    iota_c = lax.broadcasted_iota(I32, (CAND_ROWS, tm), 0).astype(F32)
    cand_ref[N_CAND:CAND_ROWS, :] = jnp.full((CAND_ROWS - N_CAND, tm), -jnp.inf, F32)
    cidx_ref[N_CAND:CAND_ROWS, :] = jnp.zeros((CAND_ROWS - N_CAND, tm), F32)

    def sub_key_topk(hd, sv_ref, si_ref):
        s = [_dot_nt(keys_ref[2 * hd + p], pq_ref[2 * hd + p]) for p in range(2)]
        for r in range(PEER_TOPK):
            for p in range(2):
                mx, idx, _, s[p] = _select_round(s[p], iota_k, float(N_KEYS))
                sv_ref[p, r:r + 1, :] = mx
                si_ref[p, r:r + 1, :] = idx

    def pair_topk(hd, sv_ref, si_ref):
        off = 0
        for i, cnt in STAIR:
            cand_ref[off:off + cnt, :] = sv_ref[0, i:i + 1, :] + sv_ref[1, 0:cnt, :]
            cidx_ref[off:off + cnt, :] = si_ref[0, i:i + 1, :] * N_KEYS + si_ref[1, 0:cnt, :]
            off += cnt
        cidx = cidx_ref[...]
        c = cand_ref[...]
        base = pl.multiple_of(hd * PEER_TOPK, PEER_TOPK)
        for r in range(PEER_TOPK):
            mx, _, hit, c = _select_round(c, iota_c, float(CAND_ROWS))
            ts_ref[r:r + 1, :] = mx
            expert = jnp.sum(jnp.where(hit, cidx, 0.0), axis=0, keepdims=True)
            eidx_ref[pl.ds(base + r, 1), :] = expert.astype(I32)
        ts = ts_ref[...]
        ex = jnp.exp(ts - ts[0:1, :])
        gate_ref[pl.ds(base, PEER_TOPK), :] = ex / jnp.sum(ex, axis=0, keepdims=True)

    sub_key_topk(0, sv_a, si_a)

    def two_heads(j, carry):
        hd = 2 * j
        sub_key_topk(hd + 1, sv_b, si_b)
        pair_topk(hd, sv_a, si_a)
        sub_key_topk(hd + 2, sv_a, si_a)
        pair_topk(hd + 1, sv_b, si_b)
        return carry

    lax.fori_loop(0, PEER_HEADS // 2 - 1, two_heads, 0)
    sub_key_topk(PEER_HEADS - 1, sv_b, si_b)
    pair_topk(PEER_HEADS - 2, sv_a, si_a)
    pair_topk(PEER_HEADS - 1, sv_b, si_b)


def _route(pq, keys):
    nhp, t, _ = pq.shape
    tm = ROUTE_TILE
    lists = pltpu.VMEM((2, PEER_TOPK, tm), F32)
    return pl.pallas_call(
        _route_kernel,
        out_shape=[jax.ShapeDtypeStruct((PEER_SLOTS, t), I32),
                   jax.ShapeDtypeStruct((PEER_SLOTS, t), F32)],
        grid=(t // tm,),
        in_specs=[pl.BlockSpec((nhp, tm, PEER_HALF), lambda i: (0, i, 0)),
                  pl.BlockSpec((nhp, N_KEYS, PEER_HALF), lambda i: (0, 0, 0))],
        out_specs=[pl.BlockSpec((PEER_SLOTS, tm), lambda i: (0, i)),
                   pl.BlockSpec((PEER_SLOTS, tm), lambda i: (0, i))],
        scratch_shapes=[lists, lists, lists, lists,
                        pltpu.VMEM((CAND_ROWS, tm), F32),
                        pltpu.VMEM((CAND_ROWS, tm), F32),
                        pltpu.VMEM((PEER_TOPK, tm), F32)],
        compiler_params=_params(("arbitrary",)),
        name="peer_route",
    )(pq, keys)


def _fold_rows(a, b, keep_a, shift):
    return jnp.where(keep_a, a + pltpu.roll(a, SUBLANES - shift, 0), b + pltpu.roll(b, shift, 0))


def _peer_kernel(final, idx_ref, idxn_ref, gate_ref, h_ref, x_ref, g2_ref, fg_ref, tab_ref,
                 o_ref, buf_a, buf_b, act_ref, sem):
    i = pl.program_id(0)
    n = pl.num_programs(0)
    grp_tok = PEER_TOK // PEER_WAIT_GROUPS
    grp_rows = grp_tok * PEER_SLOTS
    sub = lax.broadcasted_iota(I32, (SUBLANES, LANES), 0)
    keep = {sh: (sub & sh) == 0 for sh in (4, 2, 1)}
    g2 = g2_ref[0, 0]

    def row_copy(ids, t_src, t_dst, k, buf, s):
        return pltpu.make_async_copy(tab_ref.at[ids[t_src, k]], buf.at[t_dst * PEER_SLOTS + k],
                                     sem.at[s, t_dst // grp_tok])

    def issue_token(ids, t_src, t_dst, buf, s):
        for k in range(PEER_SLOTS):
            row_copy(ids, t_src, t_dst, k, buf, s).start(priority=k % 2)

    def wait_group(buf, s, q):
        pltpu.make_async_copy(tab_ref.at[pl.ds(0, grp_rows)],
                              buf.at[pl.ds(q * grp_rows, grp_rows)], sem.at[s, q]).wait()

    def token(buf, tl, tb):
        r0 = tl * PEER_SLOTS
        hv = h_ref[tb]
        groups = []
        for g in range(PEER_SLOTS // SUBLANES):
            p = [buf[r0 + g * SUBLANES + j].astype(F32)[0:SUBLANES] * hv for j in range(SUBLANES)]
            for sh in (4, 2, 1):
                half = len(p) // 2
                p = [_fold_rows(p[j], p[j + half], keep[sh], sh) for j in range(half)]
            groups.append(p[0])
        part = jnp.concatenate(groups, axis=0)
        a = jnp.sum(part, axis=1, keepdims=True)
        act = gate_ref[0][:, tb:tb + 1] * jax.nn.gelu(a)
        act_ref[...] = jnp.broadcast_to(act, (PEER_SLOTS, LANES))
        accs = [jnp.zeros((SUBLANES, LANES), F32) for _ in range(4)]
        for k in range(PEER_SLOTS):
            up = buf[r0 + k].astype(F32)[SUBLANES:2 * SUBLANES]
            accs[k % 4] = accs[k % 4] + act_ref[k:k + 1, :] * up
        xr = x_ref[tb] + g2 * ((accs[0] + accs[1]) + (accs[2] + accs[3]))
        if final:
            ms = jnp.sum(jnp.sum(xr * xr, axis=1, keepdims=True), axis=0, keepdims=True)
            xr = xr * lax.rsqrt(ms / (SUBLANES * LANES) + EPS) * fg_ref[...]
        o_ref[tb] = xr

    @pl.when(i == 0)
    def _():
        def body(t, carry):
            issue_token(idx_ref, t, t, buf_a, 0)
            return carry
        lax.fori_loop(0, PEER_TOK, body, 0)

    for t in range(PEER_TOK):
        if t % grp_tok == 0:
            wait_group(buf_a, 0, t // grp_tok)
        issue_token(idx_ref, PEER_TOK + t, t, buf_b, 1)
        token(buf_a, t, t)
    for t in range(PEER_TOK):
        if t % grp_tok == 0:
            wait_group(buf_b, 1, t // grp_tok)
        issue_token(idxn_ref, t, t, buf_a, 0)
        token(buf_b, t, PEER_TOK + t)

    @pl.when(i == n - 1)
    def _():
        for q in range(PEER_WAIT_GROUPS):
            wait_group(buf_a, 0, q)


def _peer(eidx, gate3, h3, x3, g2, mod_row, seq_len, fg, table, final):
    t = h3.shape[0]
    tb = 2 * PEER_TOK
    nb = t // tb
    steps_per_seq = seq_len // tb
    rows = PEER_TOK * PEER_SLOTS
    tile = (SUBLANES, LANES)
    pair = (2 * SUBLANES, LANES)
    return pl.pallas_call(
        functools.partial(_peer_kernel, final),
        out_shape=jax.ShapeDtypeStruct((t,) + tile, F32),
        grid=(nb,),
        in_specs=[pl.BlockSpec((tb, PEER_SLOTS), lambda i: (i, 0), memory_space=pltpu.SMEM),
                  pl.BlockSpec((tb, PEER_SLOTS), lambda i: (jnp.minimum(i + 1, nb - 1), 0),
                               memory_space=pltpu.SMEM),
                  pl.BlockSpec((1, PEER_SLOTS, tb), lambda i: (i, 0, 0)),
                  pl.BlockSpec((tb,) + tile, lambda i: (i, 0, 0)),
                  pl.BlockSpec((tb,) + tile, lambda i: (i, 0, 0)),
                  pl.BlockSpec((1, 1) + tile, lambda i: (mod_row(i, steps_per_seq), 0, 0, 0)),
                  pl.BlockSpec(tile, lambda i: (0, 0)),
                  pl.BlockSpec(memory_space=pl.ANY)],
        out_specs=pl.BlockSpec((tb,) + tile, lambda i: (i, 0, 0)),
        scratch_shapes=[pltpu.VMEM((rows,) + pair, BF16), pltpu.VMEM((rows,) + pair, BF16),
                        pltpu.VMEM((PEER_SLOTS, LANES), F32),
                        pltpu.SemaphoreType.DMA((2, PEER_WAIT_GROUPS))],
        compiler_params=pltpu.CompilerParams(dimension_semantics=("arbitrary",),
                                             vmem_limit_bytes=PEER_VMEM_LIMIT),
        name="peer_experts",
    )(eidx, eidx, gate3, h3, x3, g2, fg, table)


def _peer_sc_body(n_tok, d, eidx_hbm, gate_hbm, h_hbm, tab_hbm, y_hbm, idx_v, idx_n, gate_v, gate_n,
                  h_v, h_n, rows_a, rows_b, out_v, sem_a, sem_b, sem_n):
    wid = lax.axis_index("s") * SC_CORES + lax.axis_index("c")
    per_worker = n_tok // SC_WORKERS
    tok0 = wid * per_worker
    nj = d // SC_LANES
    lane = lax.iota(I32, SC_LANES)
    zero = jnp.zeros((SC_LANES,), F32)
    hi_mask = jnp.full((SC_LANES,), 0xFFFF0000, jnp.uint32)

    def gather(idx_ref, c, rows, sem):
        row0 = pl.multiple_of(c * SC_CHUNK, SC_CHUNK)
        return pltpu.make_async_copy(tab_hbm.at[idx_ref.at[pl.ds(row0, SC_CHUNK)]], rows, sem)

    def copy_words(src, dst, n):
        def step(j, carry):
            off = pl.multiple_of(j * SC_LANES, SC_LANES)
            dst[pl.ds(off, SC_LANES)] = src[pl.ds(off, SC_LANES)]
            return carry
        lax.fori_loop(0, n // SC_LANES, step, 0)

    def compute(c, rows):
        row0 = pl.multiple_of(c * SC_CHUNK, SC_CHUNK)

        def down_step(j, accs):
            off = pl.multiple_of(j * SC_LANES, SC_LANES)
            hj = h_v[pl.ds(off, SC_LANES)]
            out = []
            for r in range(SC_CHUNK):
                w = rows[r, pl.ds(off, SC_LANES)]
                dn = lax.bitcast_convert_type(w & hi_mask, F32)
                out.append(accs[r] + dn * hj)
            return tuple(out)

        accs = lax.fori_loop(0, nj, down_step, tuple(zero for _ in range(SC_CHUNK)))
        acts = []
        for g in range(SC_CHUNK // SC_LANES):
            a = zero
            for r in range(SC_LANES):
                a = jnp.where(lane == r, jnp.sum(accs[g * SC_LANES + r]), a)
            gt = gate_v[pl.ds(row0 + g * SC_LANES, SC_LANES)]
            u = GELU_C * (a + 0.044715 * (a * a * a))
            th = 1.0 - 2.0 / (jnp.exp(2.0 * u) + 1.0)
            act = gt * (0.5 * a * (1.0 + th))
            for r in range(SC_LANES):
                acts.append(jnp.sum(jnp.where(lane == r, act, 0.0)))

        def up_step(j, carry):
            off = pl.multiple_of(j * SC_LANES, SC_LANES)
            o = out_v[pl.ds(off, SC_LANES)]
            for r in range(SC_CHUNK):
                w = rows[r, pl.ds(off, SC_LANES)]
                up = lax.bitcast_convert_type(w << 16, F32)
                o = o + acts[r] * up
            out_v[pl.ds(off, SC_LANES)] = o
            return carry

        lax.fori_loop(0, nj, up_step, 0)

    def token(ti, carry):
        t = tok0 + ti
        tn = jnp.minimum(t + 1, tok0 + per_worker - 1)

        def clear(j, carry2):
            out_v[pl.ds(pl.multiple_of(j * SC_LANES, SC_LANES), SC_LANES)] = zero
            return carry2

        lax.fori_loop(0, nj, clear, 0)

        def pair(pp, carry2):
            gather(idx_v, 2 * pp + 1, rows_b, sem_b).start()
            gather(idx_v, 2 * pp, rows_a, sem_a).wait()
            compute(2 * pp, rows_a)

            @pl.when(pp == 0)
            def _():
                gather(idx_v, 2, rows_a, sem_a).start()

            @pl.when(pp == 1)
            def _():
                pltpu.sync_copy(eidx_hbm.at[tn], idx_n)
                gather(idx_n, 0, rows_a, sem_a).start()
                pltpu.make_async_copy(gate_hbm.at[tn], gate_n, sem_n).start()
                pltpu.make_async_copy(h_hbm.at[tn], h_n, sem_n).start()

            gather(idx_v, 2 * pp + 1, rows_b, sem_b).wait()
            compute(2 * pp + 1, rows_b)
            return carry2

        lax.fori_loop(0, PEER_SLOTS // (2 * SC_CHUNK), pair, 0)
        pltpu.sync_copy(out_v, y_hbm.at[t])
        pltpu.make_async_copy(gate_hbm.at[tn], gate_n, sem_n).wait()
        pltpu.make_async_copy(h_hbm.at[tn], h_n, sem_n).wait()
        copy_words(idx_n, idx_v, PEER_SLOTS)
        copy_words(gate_n, gate_v, PEER_SLOTS)
        copy_words(h_n, h_v, d)
        return carry

    pltpu.sync_copy(eidx_hbm.at[tok0], idx_v)
    pltpu.sync_copy(gate_hbm.at[tok0], gate_v)
    pltpu.sync_copy(h_hbm.at[tok0], h_v)
    gather(idx_v, 0, rows_a, sem_a).start()
    lax.fori_loop(0, per_worker, token, 0)
    gather(idx_v, 0, rows_a, sem_a).wait()


def _peer_sc(eidx, gate, h2, sc_table):
    n_tok, d = h2.shape
    mesh = plsc.VectorSubcoreMesh(core_axis_name="c", subcore_axis_name="s",
                                  num_cores=SC_CORES, num_subcores=SC_SUBCORES)
    return pl.kernel(
        functools.partial(_peer_sc_body, n_tok, d),
        out_type=jax.ShapeDtypeStruct((n_tok, d), F32),
        mesh=mesh,
        scratch_types=[pltpu.VMEM((PEER_SLOTS,), I32), pltpu.VMEM((PEER_SLOTS,), I32),
                       pltpu.VMEM((PEER_SLOTS,), F32), pltpu.VMEM((PEER_SLOTS,), F32),
                       pltpu.VMEM((d,), F32), pltpu.VMEM((d,), F32),
                       pltpu.VMEM((SC_CHUNK, d), jnp.uint32), pltpu.VMEM((SC_CHUNK, d), jnp.uint32),
                       pltpu.VMEM((d,), F32), pltpu.SemaphoreType.DMA, pltpu.SemaphoreType.DMA,
                       pltpu.SemaphoreType.DMA],
        compiler_params=pltpu.CompilerParams(needs_layout_passes=False),
        name="peer_experts_sc",
    )(eidx, gate, h2, sc_table)


def _finish_kernel(final, d, y_ref, x_ref, m_ref, fg_ref, o_ref):
    xr = x_ref[...] + m_ref[0][:, 5 * d:6 * d] * y_ref[...]
    if final:
        xr = _rms(xr) * fg_ref[...]
    o_ref[...] = xr


def _peer_finish(y, x1, mod, mod_row, seq_len, tok0, fg, final):
    t, d = x1.shape
    tm = min(TOK_TILE, seq_len)
    tiles_per_seq = seq_len // tm
    tile0 = tok0 // tm
    return pl.pallas_call(
        functools.partial(_finish_kernel, final, d),
        out_shape=jax.ShapeDtypeStruct((t, d), F32),
        grid=(t // tm,),
        in_specs=[pl.BlockSpec((tm, d), lambda i: (i, 0)),
                  pl.BlockSpec((tm, d), lambda i: (i, 0)),
                  pl.BlockSpec((1, 1, mod.shape[-1]),
                               lambda i: (mod_row(i + tile0, tiles_per_seq), 0, 0)),
                  pl.BlockSpec((1, d), lambda i: (0, 0))],
        out_specs=pl.BlockSpec((tm, d), lambda i: (i, 0)),
        compiler_params=_params(("arbitrary",)),
        name="peer_finish",
    )(y, x1, mod, fg.reshape(1, d))


def _rope_perm(w):
    q = QK_ROPE // 4
    a1, a2, b1, b2 = (w[..., j * q:(j + 1) * q] for j in range(4))
    return jnp.concatenate([-a2, a1, -b2, b1], axis=-1)


def _prep_layer(l, d, w_in, b_gate, q_norm_g, w_uq, kv_norm_g, w_ukv, w_oa, w_ob, w_grp,
                pool_scale, w_oc, w_out, w_pq, norm1_g, norm2_g):
    wi = w_in[l]
    s0 = Q_LORA
    s1 = s0 + KV_LORA
    s2 = s1 + QK_ROPE
    s3 = s2 + FOURIER_WIDTH
    s4 = s3 + POOL_WIDTH
    w_kr = wi[:, s1:s2]
    zl = jnp.zeros((d, QK_NOPE), F32)
    zr = jnp.zeros((d, HEAD_PAD - QK_NOPE - QK_ROPE), F32)
    wall = jnp.concatenate([wi[:, 0:s1], wi[:, s2:], zl, w_kr, zr, zl, _rope_perm(w_kr), zr],
                           axis=1).astype(BF16)
    wq = w_uq[l].reshape(Q_LORA, N_HEADS, QK_NOPE + QK_ROPE)
    qpad = jnp.zeros((Q_LORA, N_HEADS, HEAD_PAD - QK_NOPE - QK_ROPE), F32)
    wq_full = jnp.concatenate([wq, qpad], axis=-1).reshape(Q_LORA, N_HEADS * HEAD_PAD)
    wq_perm = jnp.concatenate([jnp.zeros((Q_LORA, N_HEADS, QK_NOPE), F32),
                               _rope_perm(wq[..., QK_NOPE:]), qpad],
                              axis=-1).reshape(Q_LORA, N_HEADS * HEAD_PAD)
    wkv = w_ukv[l].reshape(KV_LORA, N_HEADS, QK_NOPE + V_HEAD)
    wk = jnp.concatenate([wkv[..., :QK_NOPE],
                          jnp.zeros((KV_LORA, N_HEADS, HEAD_PAD - QK_NOPE), F32)],
                         axis=-1).reshape(KV_LORA, N_HEADS * HEAD_PAD)
    wv = wkv[..., QK_NOPE:].reshape(KV_LORA, ATT_WIDTH)
    cidx = jnp.arange(FOURIER_WIDTH, dtype=I32)
    ang = (2.0 * math.pi / FOURIER_WIDTH) * ((cidx[:, None] * cidx[None, :]) % FOURIER_WIDTH
                                             ).astype(F32)
    fc = jnp.concatenate([jnp.cos(ang), jnp.sin(ang)], axis=1).astype(BF16)
    ng = len(POOL_WINDOWS)
    wg = jnp.zeros((ng, POOL_GROUP, ng, POOL_GROUP), F32)
    for gi in range(ng):
        wg = wg.at[gi, :, gi, :].set(w_grp[l, gi])
    return {
        "wall": wall, "n1": norm1_g[l][None, :], "n2": norm2_g[l][None, :],
        "qg": q_norm_g[l][None, :], "kvg": kv_norm_g[l][None, :],
        "wq": wq_full.astype(BF16), "wqp": wq_perm.astype(BF16),
        "wk": wk.astype(BF16), "wv": wv.astype(BF16), "fc": fc,
        "bg": b_gate[l][None, :],
        "wg": wg.reshape(POOL_WIDTH, POOL_WIDTH).astype(BF16),
        "ps": pool_scale[l][None, :],
        "woa": w_oa[l].astype(BF16), "wob": w_ob[l].astype(BF16), "woc": w_oc[l].astype(BF16),
        "wout": w_out[l].astype(BF16), "wpq": w_pq[l].astype(BF16),
    }


def _bf16_bits(w):
    return lax.bitcast_convert_type(w.astype(BF16), jnp.uint16).astype(jnp.uint32)


def _rope_tables(seq_len, rope):
    zeros_n = jnp.zeros((seq_len, QK_NOPE), F32)
    zeros_p = jnp.zeros((seq_len, HEAD_PAD - QK_NOPE - QK_ROPE), F32)
    ones_n = jnp.ones((seq_len, QK_NOPE), F32)
    if rope:
        pos = jnp.arange(seq_len, dtype=I32)
        half = QK_ROPE // 2
        inv_freq = ROPE_BASE ** (-jnp.arange(0, half, 2, dtype=F32) / half)
        ang_r = (pos // GRID_W).astype(F32)[:, None] * inv_freq
        ang_c = (pos % GRID_W).astype(F32)[:, None] * inv_freq
        cos = jnp.concatenate([jnp.cos(ang_r)] * 2 + [jnp.cos(ang_c)] * 2, axis=1)
        sin = jnp.concatenate([jnp.sin(ang_r)] * 2 + [jnp.sin(ang_c)] * 2, axis=1)
    else:
        cos = jnp.ones((seq_len, QK_ROPE), F32)
        sin = jnp.zeros((seq_len, QK_ROPE), F32)
    cq = jnp.concatenate([ones_n, cos, zeros_p], axis=1) * ATT_SCALE
    sq = jnp.concatenate([zeros_n, sin, zeros_p], axis=1) * ATT_SCALE
    ck = jnp.concatenate([zeros_n, cos, zeros_p], axis=1)
    sk = jnp.concatenate([zeros_n, sin, zeros_p], axis=1)
    return cq, sq, ck, sk


def _dft(seq_len):
    idx = jnp.arange(seq_len, dtype=I32)
    ang = (2.0 * math.pi / seq_len) * ((idx[:, None] * idx[None, :]) % seq_len).astype(F32)
    return jnp.cos(ang).astype(BF16), jnp.sin(ang).astype(BF16)


def _peer_block(x1, h2, pq, keys, mod, mod_row, seq_len, fg, table, sc_table, n_sc, final):
    t, d = x1.shape
    tile = (SUBLANES, LANES)
    eidx_n, gate_n = _route(pq, keys)
    eidx = eidx_n.T
    n_tc = t - n_sc
    mod_tiles = mod.reshape(MOD_ROWS, 6, SUBLANES, LANES)[:, 5:6]
    gate3 = gate_n[:, :n_tc].reshape(PEER_SLOTS, n_tc // (2 * PEER_TOK), 2 * PEER_TOK)
    out = _peer(eidx[:n_tc], gate3.transpose(1, 0, 2), h2[:n_tc].reshape((n_tc,) + tile),
                x1[:n_tc].reshape((n_tc,) + tile), mod_tiles, mod_row, seq_len,
                fg.reshape(tile), table, final).reshape(n_tc, d)
    if n_sc == 0:
        return out
    y = _peer_sc(eidx[n_tc:], gate_n.T[n_tc:], h2[n_tc:], sc_table)
    out_sc = _peer_finish(y, x1[n_tc:], mod, mod_row, seq_len, n_tc, fg, final)
    return jnp.concatenate([out, out_sc], axis=0)


def kernel(x, c, ctx, c_ctx, w_mod, b_mod, norm1_g, norm2_g, w_in, b_gate, q_norm_g, w_uq,
           kv_norm_g, w_ukv, w_oa, w_ob, w_grp, pool_scale, w_oc, w_out, w_pq, peer_keys,
           peer_down, peer_up, final_g):
    batch, seq_len, d = x.shape
    ctx_len = ctx.shape[1]
    depth = w_mod.shape[0]
    assert d == SUBLANES * LANES and batch + 1 <= MOD_ROWS
    assert seq_len % TOK_TILE == 0 and ctx_len % (2 * PEER_TOK) == 0 and ctx_len % LANES == 0
    tile = (SUBLANES, LANES)

    cvec = jnp.concatenate([c, c_ctx[None, :], jnp.zeros((MOD_ROWS - batch - 1, d), F32)], axis=0)
    mod_all = _modulation(cvec, w_mod, b_mod)

    def x_row(b0):
        return lambda i, per_seq: b0 + i // per_seq

    def c_row(i, per_seq):
        return batch

    tabs_x = _rope_tables(seq_len, True)
    tabs_c = _rope_tables(ctx_len, False)
    dft_x = _dft(seq_len)
    dft_c = _dft(ctx_len)

    n_chains = PEER_CHAINS if batch % PEER_CHAINS == 0 else 1
    bpc = batch // n_chains
    chains = [x[j * bpc:(j + 1) * bpc].reshape(bpc * seq_len, d) for j in range(n_chains)]
    cs = ctx.reshape(batch * ctx_len, d)
    n_sc = PEER_SC_TOKENS if bpc * seq_len >= 2 * PEER_SC_TOKENS else 0
    for l in range(depth):
        last = l == depth - 1
        lw = _prep_layer(l, d, w_in, b_gate, q_norm_g, w_uq, kv_norm_g, w_ukv, w_oa, w_ob, w_grp,
                         pool_scale, w_oc, w_out, w_pq, norm1_g, norm2_g)
        mod = mod_all[l].reshape(MOD_ROWS, 1, 6 * d)
        keys = peer_keys[l].reshape(2 * PEER_HEADS, N_KEYS, PEER_HALF).astype(BF16)
        table = jnp.concatenate([peer_down[l].reshape((-1,) + tile),
                                 peer_up[l].reshape((-1,) + tile)], axis=1).astype(BF16)
        sc_table = (_bf16_bits(peer_down[l]) << 16) | _bf16_bits(peer_up[l])

        if last:
            kc, vc = _in_proj(cs, mod, c_row, ctx_len, lw, tabs_c, False)
        else:
            kc, vc, qc, abc, zpc, gc = _in_proj(cs, mod, c_row, ctx_len, lw, tabs_c, True)

        for j in range(n_chains):
            row = x_row(j * bpc)
            xs = chains[j]
            ctx_rows = slice(j * bpc * ctx_len, (j + 1) * bpc * ctx_len)
            kx, vx, qx, abx, zpx, gx = _in_proj(xs, mod, row, seq_len, lw, tabs_x, True)
            att_x = _attention(qx, kx, vx, bpc, seq_len, (kc[ctx_rows], vc[ctx_rows]))
            four_x = _fourier(abx, bpc, seq_len, dft_x)
            pool_x = _pool(zpx, bpc, seq_len, lw["wg"], lw["ps"])
            x1, h2, pq = _out_proj(xs, att_x, four_x, pool_x, gx, mod, row, seq_len, lw)
            chains[j] = _peer_block(x1, h2, pq, keys, mod, row, seq_len, final_g, table, sc_table,
                                    n_sc, last)

        if not last:
            att_c = _attention(qc, kc, vc, batch, ctx_len, None)
            four_c = _fourier(abc, batch, ctx_len, dft_c)
            pool_c = _pool(zpc, batch, ctx_len, lw["wg"], lw["ps"])
            c1, hc2, pqc = _out_proj(cs, att_c, four_c, pool_c, gc, mod, c_row, ctx_len, lw)
            cs = _peer_block(c1, hc2, pqc, keys, mod, c_row, ctx_len, final_g, table, None, 0,
                             False)
    xs = jnp.concatenate(chains, axis=0)
    return xs.reshape(batch, seq_len, d)
```

```python
import functools
import math

import jax
import jax.numpy as jnp
from jax import lax
from jax.experimental import pallas as pl
from jax.experimental.pallas import tpu as pltpu
from jax.experimental.pallas import tpu_sc as plsc

F32 = jnp.float32
BF16 = jnp.bfloat16
I32 = jnp.int32

GRID_W = 64
N_HEADS = 8
Q_LORA = 256
KV_LORA = 128
QK_NOPE = 64
QK_ROPE = 32
V_HEAD = 64
ATT_WIDTH = N_HEADS * V_HEAD
ATT_SCALE = (QK_NOPE + QK_ROPE) ** -0.5
ROPE_BASE = 10000.0
FOURIER_WIDTH = 256
POOL_WINDOWS = (2, 4, 8, 16)
POOL_GROUP = 64
POOL_WIDTH = POOL_GROUP * len(POOL_WINDOWS)
N_BRANCH = 3
PEER_HEADS = 8
N_KEYS = 128
PEER_QDIM = 256
PEER_HALF = PEER_QDIM // 2
PEER_TOPK = 16
PEER_SLOTS = PEER_HEADS * PEER_TOPK
STAIR = tuple((i, PEER_TOPK // (i + 1)) for i in range(PEER_TOPK))
N_CAND = sum(cnt for _, cnt in STAIR)
CAND_ROWS = -(-N_CAND // 8) * 8
EPS = 1e-6

LANES = 128
SUBLANES = 8
HEAD_PAD = 128
POOL_PAD = 8
MOD_ROWS = 16
VMEM_LIMIT = 48 * 1024 * 1024

TOK_TILE = 256
ATT_Q_TILE = 256
FOUR_TILE = 512
ROUTE_TILE = 128
PEER_TOK = 16
PEER_WAIT_GROUPS = 4
PEER_VMEM_LIMIT = 56 * 1024 * 1024

SC_CORES = 2
SC_SUBCORES = 16
SC_WORKERS = SC_CORES * SC_SUBCORES
SC_LANES = 16
SC_CHUNK = 32
PEER_CHAINS = 2
PEER_SC_TOKENS = 6656
GELU_C = math.sqrt(2.0 / math.pi)

C_CQ = 0
C_CKV = C_CQ + Q_LORA
C_ZF = C_CKV + KV_LORA
C_ZP = C_ZF + FOURIER_WIDTH
C_ZG = C_ZP + POOL_WIDTH
C_KR = None


def _params(sem=None):
    return pltpu.CompilerParams(dimension_semantics=sem, vmem_limit_bytes=VMEM_LIMIT)


def _rms(x):
    return x * lax.rsqrt(jnp.mean(x * x, axis=-1, keepdims=True) + EPS)


def _dot(a, b):
    return jnp.dot(a, b, preferred_element_type=F32)


def _dot_nt(a, b):
    return lax.dot_general(a, b, (((1,), (1,)), ((), ())), preferred_element_type=F32)


def _mod_kernel(c_ref, w_ref, b_ref, o_ref):
    c = c_ref[...]
    s = c * jax.nn.sigmoid(c)
    o_ref[0] = jnp.dot(s, w_ref[0], preferred_element_type=F32,
                       precision=lax.Precision.HIGHEST) + b_ref[0]


def _modulation(cvec, w_mod, b_mod):
    depth, d, n = w_mod.shape
    tn = 1536
    return pl.pallas_call(
        _mod_kernel,
        out_shape=jax.ShapeDtypeStruct((depth, MOD_ROWS, n), F32),
        grid=(depth, n // tn),
        in_specs=[pl.BlockSpec((MOD_ROWS, d), lambda l, j: (0, 0)),
                  pl.BlockSpec((1, d, tn), lambda l, j: (l, 0, j)),
                  pl.BlockSpec((1, 1, tn), lambda l, j: (l, 0, j))],
        out_specs=pl.BlockSpec((1, MOD_ROWS, tn), lambda l, j: (l, 0, j)),
        compiler_params=_params(("arbitrary", "arbitrary")),
        name="modulation",
    )(cvec, w_mod, b_mod.reshape(depth, 1, n))


def _in_kernel(d, full, x_ref, m_ref, n1_ref, wall_ref, qg_ref, wq_ref, wqp_ref, kvg_ref,
               wk_ref, wv_ref, fc_ref, bg_ref, cq_ref, sq_ref, ck_ref, sk_ref, *outs):
    x = x_ref[...]
    m = m_ref[0]
    h = _rms(x) * n1_ref[...] * (1.0 + m[:, d:2 * d]) + m[:, 0:d]
    hb = h.astype(BF16)
    c_kr = C_ZG + N_BRANCH * d
    if full:
        z = _dot(hb, wall_ref[...])
        k_ref, v_ref, q_ref, ab_ref, zp_ref, g_ref = outs
    else:
        z = None
        k_ref, v_ref = outs
    def col(lo, hi):
        if full:
            return z[:, lo:hi]
        return _dot(hb, wall_ref[:, lo:hi])

    ckv = col(C_CKV, C_ZF)
    ckvn = (_rms(ckv) * kvg_ref[...]).astype(BF16)
    kf = _dot(ckvn, wk_ref[...])
    kr = col(c_kr, c_kr + LANES) * ck_ref[...] + col(c_kr + LANES, c_kr + 2 * LANES) * sk_ref[...]
    for hh in range(N_HEADS):
        sl = slice(hh * HEAD_PAD, (hh + 1) * HEAD_PAD)
        k_ref[:, sl] = (kf[:, sl] + kr).astype(BF16)
    v_ref[...] = _dot(ckvn, wv_ref[...]).astype(BF16)
    if not full:
        return
    cq = z[:, C_CQ:C_CKV]
    cqn = (_rms(cq) * qg_ref[...]).astype(BF16)
    qf = _dot(cqn, wq_ref[...])
    qr = _dot(cqn, wqp_ref[...])
    cosq = cq_ref[...]
    sinq = sq_ref[...]
    for hh in range(N_HEADS):
        sl = slice(hh * HEAD_PAD, (hh + 1) * HEAD_PAD)
        q_ref[:, sl] = (qf[:, sl] * cosq + qr[:, sl] * sinq).astype(BF16)
    ab_ref[...] = _dot(z[:, C_ZF:C_ZP].astype(BF16), fc_ref[...]).astype(BF16)
    zp_ref[...] = z[:, C_ZP:C_ZG]
    g_ref[...] = jax.nn.sigmoid(z[:, C_ZG:c_kr] + bg_ref[...]).astype(BF16)


def _in_proj(x2d, mod, mod_row, seq_len, lw, tabs, full):
    t, d = x2d.shape
    tm = min(TOK_TILE, seq_len)
    tiles_per_seq = seq_len // tm
    wall = lw["wall"]
    nw = wall.shape[1]
    cq, sq, ck, sk = tabs

    def const(shape):
        return pl.BlockSpec(shape, lambda i: (0,) * len(shape))

    def pos(i):
        return (i % tiles_per_seq, 0)

    in_specs = [
        pl.BlockSpec((tm, d), lambda i: (i, 0)),
        pl.BlockSpec((1, 1, mod.shape[-1]), lambda i: (mod_row(i, tiles_per_seq), 0, 0)),
        const((1, d)), const((d, nw)), const((1, Q_LORA)),
        const(lw["wq"].shape), const(lw["wqp"].shape), const((1, KV_LORA)),
        const(lw["wk"].shape), const(lw["wv"].shape), const(lw["fc"].shape),
        const((1, N_BRANCH * d)),
        pl.BlockSpec((tm, LANES), pos), pl.BlockSpec((tm, LANES), pos),
        pl.BlockSpec((tm, LANES), pos), pl.BlockSpec((tm, LANES), pos),
    ]
    kw = N_HEADS * HEAD_PAD
    out_shape = [jax.ShapeDtypeStruct((t, kw), BF16), jax.ShapeDtypeStruct((t, ATT_WIDTH), BF16)]
    out_specs = [pl.BlockSpec((tm, kw), lambda i: (i, 0)),
                 pl.BlockSpec((tm, ATT_WIDTH), lambda i: (i, 0))]
    if full:
        out_shape += [jax.ShapeDtypeStruct((t, kw), BF16),
                      jax.ShapeDtypeStruct((t, 2 * FOURIER_WIDTH), BF16),
                      jax.ShapeDtypeStruct((t, POOL_WIDTH), F32),
                      jax.ShapeDtypeStruct((t, N_BRANCH * d), BF16)]
        out_specs += [pl.BlockSpec((tm, kw), lambda i: (i, 0)),
                      pl.BlockSpec((tm, 2 * FOURIER_WIDTH), lambda i: (i, 0)),
                      pl.BlockSpec((tm, POOL_WIDTH), lambda i: (i, 0)),
                      pl.BlockSpec((tm, N_BRANCH * d), lambda i: (i, 0))]
    return pl.pallas_call(
        functools.partial(_in_kernel, d, full),
        out_shape=out_shape,
        grid=(t // tm,),
        in_specs=in_specs,
        out_specs=out_specs,
        compiler_params=_params(("arbitrary",)),
        name="in_proj" if full else "ctx_kv_proj",
    )(x2d, mod, lw["n1"], wall, lw["qg"], lw["wq"], lw["wqp"], lw["kvg"], lw["wk"], lw["wv"],
      lw["fc"], lw["bg"], cq, sq, ck, sk)


def _attn_kernel(has_ctx, q_ref, k_ref, v_ref, *rest):
    if has_ctx:
        kc_ref, vc_ref, o_ref = rest
    else:
        (o_ref,) = rest
    outs = []
    for hh in range(2):
        sl = slice(hh * HEAD_PAD, (hh + 1) * HEAD_PAD)
        q = q_ref[:, sl]
        s = _dot_nt(q, k_ref[:, sl])
        mx = jnp.max(s, axis=-1, keepdims=True)
        if has_ctx:
            sc = _dot_nt(q, kc_ref[:, sl])
            mx = jnp.maximum(mx, jnp.max(sc, axis=-1, keepdims=True))
        p = jnp.exp(s - mx)
        den = jnp.sum(p, axis=-1, keepdims=True)
        o = _dot(p.astype(BF16), v_ref[...])
        if has_ctx:
            pc = jnp.exp(sc - mx)
            den = den + jnp.sum(pc, axis=-1, keepdims=True)
            o = o + _dot(pc.astype(BF16), vc_ref[...])
        outs.append(o / den)
    lane = lax.broadcasted_iota(I32, outs[0].shape, 1)
    o_ref[...] = jnp.where(lane < V_HEAD, outs[0], outs[1]).astype(BF16)


def _attention(q, k, v, batch, seq_len, ctx_kv):
    t = q.shape[0]
    tq = min(ATT_Q_TILE, seq_len)
    nq = seq_len // tq
    pair_w = 2 * HEAD_PAD
    in_specs = [pl.BlockSpec((tq, pair_w), lambda b, j, i: (b * nq + i, j)),
                pl.BlockSpec((seq_len, pair_w), lambda b, j, i: (b, j)),
                pl.BlockSpec((seq_len, 2 * V_HEAD), lambda b, j, i: (b, j))]
    args = [q, k, v]
    if ctx_kv is not None:
        kc, vc = ctx_kv
        lc = kc.shape[0] // batch
        in_specs += [pl.BlockSpec((lc, pair_w), lambda b, j, i: (b, j)),
                     pl.BlockSpec((lc, 2 * V_HEAD), lambda b, j, i: (b, j))]
        args += [kc, vc]
    return pl.pallas_call(
        functools.partial(_attn_kernel, ctx_kv is not None),
        out_shape=jax.ShapeDtypeStruct((t, ATT_WIDTH), BF16),
        grid=(batch, N_HEADS // 2, nq),
        in_specs=in_specs,
        out_specs=pl.BlockSpec((tq, 2 * V_HEAD), lambda b, j, i: (b * nq + i, j)),
        compiler_params=_params(("arbitrary", "arbitrary", "arbitrary")),
        name="attention" if ctx_kv is not None else "ctx_attention",
    )(*args)


def _fourier_kernel(norm, c_ref, s_ref, ab_ref, o_ref):
    a = ab_ref[:, 0:FOURIER_WIDTH]
    b = ab_ref[:, FOURIER_WIDTH:2 * FOURIER_WIDTH]
    o = _dot(c_ref[...], a) - _dot(s_ref[...], b)
    o_ref[...] = (o * norm).astype(BF16)


def _fourier(ab, batch, seq_len, dft):
    t = ab.shape[0]
    tm = min(FOUR_TILE, seq_len)
    nt = seq_len // tm
    cl, sl = dft
    norm = 1.0 / math.sqrt(seq_len * FOURIER_WIDTH)
    return pl.pallas_call(
        functools.partial(_fourier_kernel, norm),
        out_shape=jax.ShapeDtypeStruct((t, FOURIER_WIDTH), BF16),
        grid=(nt, batch),
        in_specs=[pl.BlockSpec((tm, seq_len), lambda i, b: (i, 0)),
                  pl.BlockSpec((tm, seq_len), lambda i, b: (i, 0)),
                  pl.BlockSpec((seq_len, 2 * FOURIER_WIDTH), lambda i, b: (b, 0))],
        out_specs=pl.BlockSpec((tm, FOURIER_WIDTH), lambda i, b: (b * nt + i, 0)),
        compiler_params=_params(("arbitrary", "arbitrary")),
        name="fourier",
    )(cl, sl, ab)


def _pool_kernel(seq_len, z_ref, wg_ref, ps_ref, o_ref, pad_ref, s_ref):
    n = seq_len
    p = n + 2 * POOL_PAD
    z = z_ref[...]
    zeros = jnp.zeros((POOL_PAD, POOL_WIDTH), F32)
    pad_ref[0:POOL_PAD, :] = zeros
    pad_ref[POOL_PAD + n:p, :] = zeros
    pad_ref[POOL_PAD:POOL_PAD + n, :] = z
    s_ref[0:p - 1, :] = pad_ref[0:p - 1, :] + pad_ref[1:p, :]
    w2 = s_ref[POOL_PAD - 1:POOL_PAD - 1 + n, :]
    pad_ref[0:p - 3, :] = s_ref[0:p - 3, :] + s_ref[2:p - 1, :]
    w4 = pad_ref[POOL_PAD - 2:POOL_PAD - 2 + n, :]
    s_ref[0:p - 7, :] = pad_ref[0:p - 7, :] + pad_ref[4:p - 3, :]
    w8 = s_ref[POOL_PAD - 4:POOL_PAD - 4 + n, :]
    pad_ref[0:p - 15, :] = s_ref[0:p - 15, :] + s_ref[8:p - 7, :]
    w16 = pad_ref[0:n, :]
    pos = lax.broadcasted_iota(I32, (n, POOL_WIDTH), 0)
    grp = lax.broadcasted_iota(I32, (n, POOL_WIDTH), 1) // POOL_GROUP
    win = jnp.where(grp == 0, w2, jnp.where(grp == 1, w4, jnp.where(grp == 2, w8, w16)))
    half = jnp.where(grp == 0, 1, jnp.where(grp == 1, 2, jnp.where(grp == 2, 4, 8)))
    lo = jnp.maximum(pos - half, 0)
    hi = jnp.minimum(pos + half, n)
    cnt = (hi - lo).astype(F32)
    pooled = win / cnt - z
    y = _dot(pooled.astype(BF16), wg_ref[...])
    o_ref[...] = (y * ps_ref[...]).astype(BF16)


def _pool(zp, batch, seq_len, wg_bd, pool_scale):
    t = zp.shape[0]
    return pl.pallas_call(
        functools.partial(_pool_kernel, seq_len),
        out_shape=jax.ShapeDtypeStruct((t, POOL_WIDTH), BF16),
        grid=(batch,),
        in_specs=[pl.BlockSpec((seq_len, POOL_WIDTH), lambda b: (b, 0)),
                  pl.BlockSpec((POOL_WIDTH, POOL_WIDTH), lambda b: (0, 0)),
                  pl.BlockSpec((1, POOL_WIDTH), lambda b: (0, 0))],
        out_specs=pl.BlockSpec((seq_len, POOL_WIDTH), lambda b: (b, 0)),
        scratch_shapes=[pltpu.VMEM((seq_len + 2 * POOL_PAD, POOL_WIDTH), F32),
                        pltpu.VMEM((seq_len + 2 * POOL_PAD, POOL_WIDTH), F32)],
        compiler_params=_params(("arbitrary",)),
        name="pool",
    )(zp, wg_bd, pool_scale)


def _out_kernel(d, x_ref, att_ref, four_ref, pool_ref, g_ref, m_ref, woa_ref, wob_ref, woc_ref,
                wout_ref, n2_ref, wpq_ref, x1_ref, h2_ref, pq_ref):
    m = m_ref[0]
    ya = _dot(att_ref[...], woa_ref[...])
    yb = _dot(four_ref[...], wob_ref[...])
    yc = _dot(pool_ref[...], woc_ref[...])
    mixp = (g_ref[:, 0:d].astype(F32) * ya + g_ref[:, d:2 * d].astype(F32) * yb
            + g_ref[:, 2 * d:3 * d].astype(F32) * yc)
    mix = _dot(mixp.astype(BF16), wout_ref[...])
    x1 = x_ref[...] + m[:, 2 * d:3 * d] * mix
    x1_ref[...] = x1
    h2 = _rms(x1) * n2_ref[...] * (1.0 + m[:, 4 * d:5 * d]) + m[:, 3 * d:4 * d]
    h2_ref[...] = h2
    pq = _dot(h2.astype(BF16), wpq_ref[...])
    for hp in range(2 * PEER_HEADS):
        pq_ref[hp] = pq[:, hp * PEER_HALF:(hp + 1) * PEER_HALF].astype(BF16)


def _out_proj(x2d, att, four, pool, g, mod, mod_row, seq_len, lw):
    t, d = x2d.shape
    tm = min(TOK_TILE, seq_len)
    tiles_per_seq = seq_len // tm

    def const(shape):
        return pl.BlockSpec(shape, lambda i: (0,) * len(shape))

    def row(w):
        return pl.BlockSpec((tm, w), lambda i: (i, 0))

    nhp = 2 * PEER_HEADS
    return pl.pallas_call(
        functools.partial(_out_kernel, d),
        out_shape=[jax.ShapeDtypeStruct((t, d), F32), jax.ShapeDtypeStruct((t, d), F32),
                   jax.ShapeDtypeStruct((nhp, t, PEER_HALF), BF16)],
        grid=(t // tm,),
        in_specs=[row(d), row(ATT_WIDTH), row(FOURIER_WIDTH), row(POOL_WIDTH), row(N_BRANCH * d),
                  pl.BlockSpec((1, 1, mod.shape[-1]), lambda i: (mod_row(i, tiles_per_seq), 0, 0)),
                  const(lw["woa"].shape), const(lw["wob"].shape), const(lw["woc"].shape),
                  const(lw["wout"].shape), const((1, d)), const(lw["wpq"].shape)],
        out_specs=[row(d), row(d), pl.BlockSpec((nhp, tm, PEER_HALF), lambda i: (0, i, 0))],
        compiler_params=_params(("arbitrary",)),
        name="out_proj",
    )(x2d, att, four, pool, g, mod, lw["woa"], lw["wob"], lw["woc"], lw["wout"], lw["n2"],
      lw["wpq"])


def _select_round(s, iota, n):
    mx = jnp.max(s, axis=0, keepdims=True)
    idx = jnp.min(jnp.where(s == mx, iota, n), axis=0, keepdims=True)
    hit = iota == idx
    return mx, idx, hit, jnp.where(hit, -jnp.inf, s)


def _route_kernel(pq_ref, keys_ref, eidx_ref, gate_ref, sv_a, si_a, sv_b, si_b, cand_ref, cidx_ref,
                  ts_ref):
    tm = pq_ref.shape[1]
    iota_k = lax.broadcasted_iota(I32, (N_KEYS, tm), 0).astype(F32)
    iota_c = lax.broadcasted_iota(I32, (CAND_ROWS, tm), 0).astype(F32)
    cand_ref[N_CAND:CAND_ROWS, :] = jnp.full((CAND_ROWS - N_CAND, tm), -jnp.inf, F32)
    cidx_ref[N_CAND:CAND_ROWS, :] = jnp.zeros((CAND_ROWS - N_CAND, tm), F32)

    def sub_key_topk(hd, sv_ref, si_ref):
        s = [_dot_nt(keys_ref[2 * hd + p], pq_ref[2 * hd + p]) for p in range(2)]
        for r in range(PEER_TOPK):
            for p in range(2):
                mx, idx, _, s[p] = _select_round(s[p], iota_k, float(N_KEYS))
                sv_ref[p, r:r + 1, :] = mx
                si_ref[p, r:r + 1, :] = idx

    def pair_topk(hd, sv_ref, si_ref):
        off = 0
        for i, cnt in STAIR:
            cand_ref[off:off + cnt, :] = sv_ref[0, i:i + 1, :] + sv_ref[1, 0:cnt, :]
            cidx_ref[off:off + cnt, :] = si_ref[0, i:i + 1, :] * N_KEYS + si_ref[1, 0:cnt, :]
            off += cnt
        cidx = cidx_ref[...]
        c = cand_ref[...]
        base = pl.multiple_of(hd * PEER_TOPK, PEER_TOPK)
        for r in range(PEER_TOPK):
            mx, _, hit, c = _select_round(c, iota_c, float(CAND_ROWS))
            ts_ref[r:r + 1, :] = mx
            expert = jnp.sum(jnp.where(hit, cidx, 0.0), axis=0, keepdims=True)
            eidx_ref[pl.ds(base + r, 1), :] = expert.astype(I32)
        ts = ts_ref[...]
        ex = jnp.exp(ts - ts[0:1, :])
        gate_ref[pl.ds(base, PEER_TOPK), :] = ex / jnp.sum(ex, axis=0, keepdims=True)

    sub_key_topk(0, sv_a, si_a)

    def two_heads(j, carry):
        hd = 2 * j
        sub_key_topk(hd + 1, sv_b, si_b)
        pair_topk(hd, sv_a, si_a)
        sub_key_topk(hd + 2, sv_a, si_a)
        pair_topk(hd + 1, sv_b, si_b)
        return carry

    lax.fori_loop(0, PEER_HEADS // 2 - 1, two_heads, 0)
    sub_key_topk(PEER_HEADS - 1, sv_b, si_b)
    pair_topk(PEER_HEADS - 2, sv_a, si_a)
    pair_topk(PEER_HEADS - 1, sv_b, si_b)


def _route(pq, keys):
    nhp, t, _ = pq.shape
    tm = ROUTE_TILE
    lists = pltpu.VMEM((2, PEER_TOPK, tm), F32)
    return pl.pallas_call(
        _route_kernel,
        out_shape=[jax.ShapeDtypeStruct((PEER_SLOTS, t), I32),
                   jax.ShapeDtypeStruct((PEER_SLOTS, t), F32)],
        grid=(t // tm,),
        in_specs=[pl.BlockSpec((nhp, tm, PEER_HALF), lambda i: (0, i, 0)),
                  pl.BlockSpec((nhp, N_KEYS, PEER_HALF), lambda i: (0, 0, 0))],
        out_specs=[pl.BlockSpec((PEER_SLOTS, tm), lambda i: (0, i)),
                   pl.BlockSpec((PEER_SLOTS, tm), lambda i: (0, i))],
        scratch_shapes=[lists, lists, lists, lists,
                        pltpu.VMEM((CAND_ROWS, tm), F32),
                        pltpu.VMEM((CAND_ROWS, tm), F32),
                        pltpu.VMEM((PEER_TOPK, tm), F32)],
        compiler_params=_params(("arbitrary",)),
        name="peer_route",
    )(pq, keys)


def _fold_rows(a, b, keep_a, shift):
    return jnp.where(keep_a, a + pltpu.roll(a, SUBLANES - shift, 0), b + pltpu.roll(b, shift, 0))


def _peer_kernel(final, idx_ref, idxn_ref, gate_ref, h_ref, x_ref, g2_ref, fg_ref, tab_ref,
                 o_ref, buf_a, buf_b, act_ref, sem):
    i = pl.program_id(0)
    n = pl.num_programs(0)
    grp_tok = PEER_TOK // PEER_WAIT_GROUPS
    grp_rows = grp_tok * PEER_SLOTS
    sub = lax.broadcasted_iota(I32, (SUBLANES, LANES), 0)
    keep = {sh: (sub & sh) == 0 for sh in (4, 2, 1)}
    g2 = g2_ref[0, 0]

    def row_copy(ids, t_src, t_dst, k, buf, s):
        return pltpu.make_async_copy(tab_ref.at[ids[t_src, k]], buf.at[t_dst * PEER_SLOTS + k],
                                     sem.at[s, t_dst // grp_tok])

    def issue_token(ids, t_src, t_dst, buf, s):
        for k in range(PEER_SLOTS):
            row_copy(ids, t_src, t_dst, k, buf, s).start(priority=k % 2)

    def wait_group(buf, s, q):
        pltpu.make_async_copy(tab_ref.at[pl.ds(0, grp_rows)],
                              buf.at[pl.ds(q * grp_rows, grp_rows)], sem.at[s, q]).wait()

    def token(buf, tl, tb):
        r0 = tl * PEER_SLOTS
        hv = h_ref[tb]
        groups = []
        for g in range(PEER_SLOTS // SUBLANES):
            p = [buf[r0 + g * SUBLANES + j].astype(F32)[0:SUBLANES] * hv for j in range(SUBLANES)]
            for sh in (4, 2, 1):
                half = len(p) // 2
                p = [_fold_rows(p[j], p[j + half], keep[sh], sh) for j in range(half)]
            groups.append(p[0])
        part = jnp.concatenate(groups, axis=0)
        a = jnp.sum(part, axis=1, keepdims=True)
        act = gate_ref[0][:, tb:tb + 1] * jax.nn.gelu(a)
        act_ref[...] = jnp.broadcast_to(act, (PEER_SLOTS, LANES))
        accs = [jnp.zeros((SUBLANES, LANES), F32) for _ in range(4)]
        for k in range(PEER_SLOTS):
            up = buf[r0 + k].astype(F32)[SUBLANES:2 * SUBLANES]
            accs[k % 4] = accs[k % 4] + act_ref[k:k + 1, :] * up
        xr = x_ref[tb] + g2 * ((accs[0] + accs[1]) + (accs[2] + accs[3]))
        if final:
            ms = jnp.sum(jnp.sum(xr * xr, axis=1, keepdims=True), axis=0, keepdims=True)
            xr = xr * lax.rsqrt(ms / (SUBLANES * LANES) + EPS) * fg_ref[...]
        o_ref[tb] = xr

    @pl.when(i == 0)
    def _():
        def body(t, carry):
            issue_token(idx_ref, t, t, buf_a, 0)
            return carry
        lax.fori_loop(0, PEER_TOK, body, 0)

    for t in range(PEER_TOK):
        if t % grp_tok == 0:
            wait_group(buf_a, 0, t // grp_tok)
        issue_token(idx_ref, PEER_TOK + t, t, buf_b, 1)
        token(buf_a, t, t)
    for t in range(PEER_TOK):
        if t % grp_tok == 0:
            wait_group(buf_b, 1, t // grp_tok)
        issue_token(idxn_ref, t, t, buf_a, 0)
        token(buf_b, t, PEER_TOK + t)

    @pl.when(i == n - 1)
    def _():
        for q in range(PEER_WAIT_GROUPS):
            wait_group(buf_a, 0, q)


def _peer(eidx, gate3, h3, x3, g2, mod_row, seq_len, fg, table, final):
    t = h3.shape[0]
    tb = 2 * PEER_TOK
    nb = t // tb
    steps_per_seq = seq_len // tb
    rows = PEER_TOK * PEER_SLOTS
    tile = (SUBLANES, LANES)
    pair = (2 * SUBLANES, LANES)
    return pl.pallas_call(
        functools.partial(_peer_kernel, final),
        out_shape=jax.ShapeDtypeStruct((t,) + tile, F32),
        grid=(nb,),
        in_specs=[pl.BlockSpec((tb, PEER_SLOTS), lambda i: (i, 0), memory_space=pltpu.SMEM),
                  pl.BlockSpec((tb, PEER_SLOTS), lambda i: (jnp.minimum(i + 1, nb - 1), 0),
                               memory_space=pltpu.SMEM),
                  pl.BlockSpec((1, PEER_SLOTS, tb), lambda i: (i, 0, 0)),
                  pl.BlockSpec((tb,) + tile, lambda i: (i, 0, 0)),
                  pl.BlockSpec((tb,) + tile, lambda i: (i, 0, 0)),
                  pl.BlockSpec((1, 1) + tile, lambda i: (mod_row(i, steps_per_seq), 0, 0, 0)),
                  pl.BlockSpec(tile, lambda i: (0, 0)),
                  pl.BlockSpec(memory_space=pl.ANY)],
        out_specs=pl.BlockSpec((tb,) + tile, lambda i: (i, 0, 0)),
        scratch_shapes=[pltpu.VMEM((rows,) + pair, BF16), pltpu.VMEM((rows,) + pair, BF16),
                        pltpu.VMEM((PEER_SLOTS, LANES), F32),
                        pltpu.SemaphoreType.DMA((2, PEER_WAIT_GROUPS))],
        compiler_params=pltpu.CompilerParams(dimension_semantics=("arbitrary",),
                                             vmem_limit_bytes=PEER_VMEM_LIMIT),
        name="peer_experts",
    )(eidx, eidx, gate3, h3, x3, g2, fg, table)


def _peer_sc_body(n_tok, d, eidx_hbm, gate_hbm, h_hbm, tab_hbm, y_hbm, idx_v, idx_n, gate_v, gate_n,
                  h_v, h_n, rows_a, rows_b, out_v, sem_a, sem_b, sem_n):
    wid = lax.axis_index("s") * SC_CORES + lax.axis_index("c")
    per_worker = n_tok // SC_WORKERS
    tok0 = wid * per_worker
    nj = d // SC_LANES
    lane = lax.iota(I32, SC_LANES)
    zero = jnp.zeros((SC_LANES,), F32)
    hi_mask = jnp.full((SC_LANES,), 0xFFFF0000, jnp.uint32)

    def gather(idx_ref, c, rows, sem):
        row0 = pl.multiple_of(c * SC_CHUNK, SC_CHUNK)
        return pltpu.make_async_copy(tab_hbm.at[idx_ref.at[pl.ds(row0, SC_CHUNK)]], rows, sem)

    def copy_words(src, dst, n):
        def step(j, carry):
            off = pl.multiple_of(j * SC_LANES, SC_LANES)
            dst[pl.ds(off, SC_LANES)] = src[pl.ds(off, SC_LANES)]
            return carry
        lax.fori_loop(0, n // SC_LANES, step, 0)

    def compute(c, rows):
        row0 = pl.multiple_of(c * SC_CHUNK, SC_CHUNK)

        def down_step(j, accs):
            off = pl.multiple_of(j * SC_LANES, SC_LANES)
            hj = h_v[pl.ds(off, SC_LANES)]
            out = []
            for r in range(SC_CHUNK):
                w = rows[r, pl.ds(off, SC_LANES)]
                dn = lax.bitcast_convert_type(w & hi_mask, F32)
                out.append(accs[r] + dn * hj)
            return tuple(out)

        accs = lax.fori_loop(0, nj, down_step, tuple(zero for _ in range(SC_CHUNK)))
        acts = []
        for g in range(SC_CHUNK // SC_LANES):
            a = zero
            for r in range(SC_LANES):
                a = jnp.where(lane == r, jnp.sum(accs[g * SC_LANES + r]), a)
            gt = gate_v[pl.ds(row0 + g * SC_LANES, SC_LANES)]
            u = GELU_C * (a + 0.044715 * (a * a * a))
            th = 1.0 - 2.0 / (jnp.exp(2.0 * u) + 1.0)
            act = gt * (0.5 * a * (1.0 + th))
            for r in range(SC_LANES):
                acts.append(jnp.sum(jnp.where(lane == r, act, 0.0)))

        def up_step(j, carry):
            off = pl.multiple_of(j * SC_LANES, SC_LANES)
            o = out_v[pl.ds(off, SC_LANES)]
            for r in range(SC_CHUNK):
                w = rows[r, pl.ds(off, SC_LANES)]
                up = lax.bitcast_convert_type(w << 16, F32)
                o = o + acts[r] * up
            out_v[pl.ds(off, SC_LANES)] = o
            return carry

        lax.fori_loop(0, nj, up_step, 0)

    def token(ti, carry):
        t = tok0 + ti
        tn = jnp.minimum(t + 1, tok0 + per_worker - 1)

        def clear(j, carry2):
            out_v[pl.ds(pl.multiple_of(j * SC_LANES, SC_LANES), SC_LANES)] = zero
            return carry2

        lax.fori_loop(0, nj, clear, 0)

        def pair(pp, carry2):
            gather(idx_v, 2 * pp + 1, rows_b, sem_b).start()
            gather(idx_v, 2 * pp, rows_a, sem_a).wait()
            compute(2 * pp, rows_a)

            @pl.when(pp == 0)
            def _():
                gather(idx_v, 2, rows_a, sem_a).start()

            @pl.when(pp == 1)
            def _():
                pltpu.sync_copy(eidx_hbm.at[tn], idx_n)
                gather(idx_n, 0, rows_a, sem_a).start()
                pltpu.make_async_copy(gate_hbm.at[tn], gate_n, sem_n).start()
                pltpu.make_async_copy(h_hbm.at[tn], h_n, sem_n).start()

            gather(idx_v, 2 * pp + 1, rows_b, sem_b).wait()
            compute(2 * pp + 1, rows_b)
            return carry2

        lax.fori_loop(0, PEER_SLOTS // (2 * SC_CHUNK), pair, 0)
        pltpu.sync_copy(out_v, y_hbm.at[t])
        pltpu.make_async_copy(gate_hbm.at[tn], gate_n, sem_n).wait()
        pltpu.make_async_copy(h_hbm.at[tn], h_n, sem_n).wait()
        copy_words(idx_n, idx_v, PEER_SLOTS)
        copy_words(gate_n, gate_v, PEER_SLOTS)
        copy_words(h_n, h_v, d)
        return carry

    pltpu.sync_copy(eidx_hbm.at[tok0], idx_v)
    pltpu.sync_copy(gate_hbm.at[tok0], gate_v)
    pltpu.sync_copy(h_hbm.at[tok0], h_v)
    gather(idx_v, 0, rows_a, sem_a).start()
    lax.fori_loop(0, per_worker, token, 0)
    gather(idx_v, 0, rows_a, sem_a).wait()


def _peer_sc(eidx, gate, h2, sc_table):
    n_tok, d = h2.shape
    mesh = plsc.VectorSubcoreMesh(core_axis_name="c", subcore_axis_name="s",
                                  num_cores=SC_CORES, num_subcores=SC_SUBCORES)
    return pl.kernel(
        functools.partial(_peer_sc_body, n_tok, d),
        out_type=jax.ShapeDtypeStruct((n_tok, d), F32),
        mesh=mesh,
        scratch_types=[pltpu.VMEM((PEER_SLOTS,), I32), pltpu.VMEM((PEER_SLOTS,), I32),
                       pltpu.VMEM((PEER_SLOTS,), F32), pltpu.VMEM((PEER_SLOTS,), F32),
                       pltpu.VMEM((d,), F32), pltpu.VMEM((d,), F32),
                       pltpu.VMEM((SC_CHUNK, d), jnp.uint32), pltpu.VMEM((SC_CHUNK, d), jnp.uint32),
                       pltpu.VMEM((d,), F32), pltpu.SemaphoreType.DMA, pltpu.SemaphoreType.DMA,
                       pltpu.SemaphoreType.DMA],
        compiler_params=pltpu.CompilerParams(needs_layout_passes=False),
        name="peer_experts_sc",
    )(eidx, gate, h2, sc_table)


def _finish_kernel(final, d, y_ref, x_ref, m_ref, fg_ref, o_ref):
    xr = x_ref[...] + m_ref[0][:, 5 * d:6 * d] * y_ref[...]
    if final:
        xr = _rms(xr) * fg_ref[...]
    o_ref[...] = xr


def _peer_finish(y, x1, mod, mod_row, seq_len, tok0, fg, final):
    t, d = x1.shape
    tm = min(TOK_TILE, seq_len)
    tiles_per_seq = seq_len // tm
    tile0 = tok0 // tm
    return pl.pallas_call(
        functools.partial(_finish_kernel, final, d),
        out_shape=jax.ShapeDtypeStruct((t, d), F32),
        grid=(t // tm,),
        in_specs=[pl.BlockSpec((tm, d), lambda i: (i, 0)),
                  pl.BlockSpec((tm, d), lambda i: (i, 0)),
                  pl.BlockSpec((1, 1, mod.shape[-1]),
                               lambda i: (mod_row(i + tile0, tiles_per_seq), 0, 0)),
                  pl.BlockSpec((1, d), lambda i: (0, 0))],
        out_specs=pl.BlockSpec((tm, d), lambda i: (i, 0)),
        compiler_params=_params(("arbitrary",)),
        name="peer_finish",
    )(y, x1, mod, fg.reshape(1, d))


def _rope_perm(w):
    q = QK_ROPE // 4
    a1, a2, b1, b2 = (w[..., j * q:(j + 1) * q] for j in range(4))
    return jnp.concatenate([-a2, a1, -b2, b1], axis=-1)


def _prep_layer(l, d, w_in, b_gate, q_norm_g, w_uq, kv_norm_g, w_ukv, w_oa, w_ob, w_grp,
                pool_scale, w_oc, w_out, w_pq, norm1_g, norm2_g):
    wi = w_in[l]
    s0 = Q_LORA
    s1 = s0 + KV_LORA
    s2 = s1 + QK_ROPE
    s3 = s2 + FOURIER_WIDTH
    s4 = s3 + POOL_WIDTH
    w_kr = wi[:, s1:s2]
    zl = jnp.zeros((d, QK_NOPE), F32)
    zr = jnp.zeros((d, HEAD_PAD - QK_NOPE - QK_ROPE), F32)
    wall = jnp.concatenate([wi[:, 0:s1], wi[:, s2:], zl, w_kr, zr, zl, _rope_perm(w_kr), zr],
                           axis=1).astype(BF16)
    wq = w_uq[l].reshape(Q_LORA, N_HEADS, QK_NOPE + QK_ROPE)
    qpad = jnp.zeros((Q_LORA, N_HEADS, HEAD_PAD - QK_NOPE - QK_ROPE), F32)
    wq_full = jnp.concatenate([wq, qpad], axis=-1).reshape(Q_LORA, N_HEADS * HEAD_PAD)
    wq_perm = jnp.concatenate([jnp.zeros((Q_LORA, N_HEADS, QK_NOPE), F32),
                               _rope_perm(wq[..., QK_NOPE:]), qpad],
                              axis=-1).reshape(Q_LORA, N_HEADS * HEAD_PAD)
    wkv = w_ukv[l].reshape(KV_LORA, N_HEADS, QK_NOPE + V_HEAD)
    wk = jnp.concatenate([wkv[..., :QK_NOPE],
                          jnp.zeros((KV_LORA, N_HEADS, HEAD_PAD - QK_NOPE), F32)],
                         axis=-1).reshape(KV_LORA, N_HEADS * HEAD_PAD)
    wv = wkv[..., QK_NOPE:].reshape(KV_LORA, ATT_WIDTH)
    cidx = jnp.arange(FOURIER_WIDTH, dtype=I32)
    ang = (2.0 * math.pi / FOURIER_WIDTH) * ((cidx[:, None] * cidx[None, :]) % FOURIER_WIDTH
                                             ).astype(F32)
    fc = jnp.concatenate([jnp.cos(ang), jnp.sin(ang)], axis=1).astype(BF16)
    ng = len(POOL_WINDOWS)
    wg = jnp.zeros((ng, POOL_GROUP, ng, POOL_GROUP), F32)
    for gi in range(ng):
        wg = wg.at[gi, :, gi, :].set(w_grp[l, gi])
    return {
        "wall": wall, "n1": norm1_g[l][None, :], "n2": norm2_g[l][None, :],
        "qg": q_norm_g[l][None, :], "kvg": kv_norm_g[l][None, :],
        "wq": wq_full.astype(BF16), "wqp": wq_perm.astype(BF16),
        "wk": wk.astype(BF16), "wv": wv.astype(BF16), "fc": fc,
        "bg": b_gate[l][None, :],
        "wg": wg.reshape(POOL_WIDTH, POOL_WIDTH).astype(BF16),
        "ps": pool_scale[l][None, :],
        "woa": w_oa[l].astype(BF16), "wob": w_ob[l].astype(BF16), "woc": w_oc[l].astype(BF16),
        "wout": w_out[l].astype(BF16), "wpq": w_pq[l].astype(BF16),
    }


def _bf16_bits(w):
    return lax.bitcast_convert_type(w.astype(BF16), jnp.uint16).astype(jnp.uint32)


def _rope_tables(seq_len, rope):
    zeros_n = jnp.zeros((seq_len, QK_NOPE), F32)
    zeros_p = jnp.zeros((seq_len, HEAD_PAD - QK_NOPE - QK_ROPE), F32)
    ones_n = jnp.ones((seq_len, QK_NOPE), F32)
    if rope:
        pos = jnp.arange(seq_len, dtype=I32)
        half = QK_ROPE // 2
        inv_freq = ROPE_BASE ** (-jnp.arange(0, half, 2, dtype=F32) / half)
        ang_r = (pos // GRID_W).astype(F32)[:, None] * inv_freq
        ang_c = (pos % GRID_W).astype(F32)[:, None] * inv_freq
        cos = jnp.concatenate([jnp.cos(ang_r)] * 2 + [jnp.cos(ang_c)] * 2, axis=1)
        sin = jnp.concatenate([jnp.sin(ang_r)] * 2 + [jnp.sin(ang_c)] * 2, axis=1)
    else:
        cos = jnp.ones((seq_len, QK_ROPE), F32)
        sin = jnp.zeros((seq_len, QK_ROPE), F32)
    cq = jnp.concatenate([ones_n, cos, zeros_p], axis=1) * ATT_SCALE
    sq = jnp.concatenate([zeros_n, sin, zeros_p], axis=1) * ATT_SCALE
    ck = jnp.concatenate([zeros_n, cos, zeros_p], axis=1)
    sk = jnp.concatenate([zeros_n, sin, zeros_p], axis=1)
    return cq, sq, ck, sk


def _dft(seq_len):
    idx = jnp.arange(seq_len, dtype=I32)
    ang = (2.0 * math.pi / seq_len) * ((idx[:, None] * idx[None, :]) % seq_len).astype(F32)
    return jnp.cos(ang).astype(BF16), jnp.sin(ang).astype(BF16)


def _peer_block(x1, h2, pq, keys, mod, mod_row, seq_len, fg, table, sc_table, n_sc, final):
    t, d = x1.shape
    tile = (SUBLANES, LANES)
    eidx_n, gate_n = _route(pq, keys)
    eidx = eidx_n.T
    n_tc = t - n_sc
    mod_tiles = mod.reshape(MOD_ROWS, 6, SUBLANES, LANES)[:, 5:6]
    gate3 = gate_n[:, :n_tc].reshape(PEER_SLOTS, n_tc // (2 * PEER_TOK), 2 * PEER_TOK)
    out = _peer(eidx[:n_tc], gate3.transpose(1, 0, 2), h2[:n_tc].reshape((n_tc,) + tile),
                x1[:n_tc].reshape((n_tc,) + tile), mod_tiles, mod_row, seq_len,
                fg.reshape(tile), table, final).reshape(n_tc, d)
    if n_sc == 0:
        return out
    y = _peer_sc(eidx[n_tc:], gate_n.T[n_tc:], h2[n_tc:], sc_table)
    out_sc = _peer_finish(y, x1[n_tc:], mod, mod_row, seq_len, n_tc, fg, final)
    return jnp.concatenate([out, out_sc], axis=0)


def kernel(x, c, ctx, c_ctx, w_mod, b_mod, norm1_g, norm2_g, w_in, b_gate, q_norm_g, w_uq,
           kv_norm_g, w_ukv, w_oa, w_ob, w_grp, pool_scale, w_oc, w_out, w_pq, peer_keys,
           peer_down, peer_up, final_g):
    batch, seq_len, d = x.shape
    ctx_len = ctx.shape[1]
    depth = w_mod.shape[0]
    assert d == SUBLANES * LANES and batch + 1 <= MOD_ROWS
    assert seq_len % TOK_TILE == 0 and ctx_len % (2 * PEER_TOK) == 0 and ctx_len % LANES == 0
    tile = (SUBLANES, LANES)

    cvec = jnp.concatenate([c, c_ctx[None, :], jnp.zeros((MOD_ROWS - batch - 1, d), F32)], axis=0)
    mod_all = _modulation(cvec, w_mod, b_mod)

    def x_row(b0):
        return lambda i, per_seq: b0 + i // per_seq

    def c_row(i, per_seq):
        return batch

    tabs_x = _rope_tables(seq_len, True)
    tabs_c = _rope_tables(ctx_len, False)
    dft_x = _dft(seq_len)
    dft_c = _dft(ctx_len)

    n_chains = PEER_CHAINS if batch % PEER_CHAINS == 0 else 1
    bpc = batch // n_chains
    n_sc = PEER_SC_TOKENS if bpc * seq_len >= 2 * PEER_SC_TOKENS else 0

    tables = [jnp.concatenate([peer_down[l].reshape((-1,) + tile),
                               peer_up[l].reshape((-1,) + tile)], axis=1).astype(BF16)
              for l in range(depth)]
    sc_tables = [(_bf16_bits(peer_down[l]) << 16) | _bf16_bits(peer_up[l]) for l in range(depth)]
    x, ctx, tables, sc_tables = lax.optimization_barrier((x, ctx, tables, sc_tables))

    chains = [x[j * bpc:(j + 1) * bpc].reshape(bpc * seq_len, d) for j in range(n_chains)]
    cs = ctx.reshape(batch * ctx_len, d)
    for l in range(depth):
        last = l == depth - 1
        lw = _prep_layer(l, d, w_in, b_gate, q_norm_g, w_uq, kv_norm_g, w_ukv, w_oa, w_ob, w_grp,
                         pool_scale, w_oc, w_out, w_pq, norm1_g, norm2_g)
        mod = mod_all[l].reshape(MOD_ROWS, 1, 6 * d)
        keys = peer_keys[l].reshape(2 * PEER_HEADS, N_KEYS, PEER_HALF).astype(BF16)
        table = tables[l]
        sc_table = sc_tables[l]

        if last:
            kc, vc = _in_proj(cs, mod, c_row, ctx_len, lw, tabs_c, False)
        else:
            kc, vc, qc, abc, zpc, gc = _in_proj(cs, mod, c_row, ctx_len, lw, tabs_c, True)

        for j in range(n_chains):
            row = x_row(j * bpc)
            xs = chains[j]
            ctx_rows = slice(j * bpc * ctx_len, (j + 1) * bpc * ctx_len)
            kx, vx, qx, abx, zpx, gx = _in_proj(xs, mod, row, seq_len, lw, tabs_x, True)
            att_x = _attention(qx, kx, vx, bpc, seq_len, (kc[ctx_rows], vc[ctx_rows]))
            four_x = _fourier(abx, bpc, seq_len, dft_x)
            pool_x = _pool(zpx, bpc, seq_len, lw["wg"], lw["ps"])
            x1, h2, pq = _out_proj(xs, att_x, four_x, pool_x, gx, mod, row, seq_len, lw)
            chains[j] = _peer_block(x1, h2, pq, keys, mod, row, seq_len, final_g, table, sc_table,
                                    n_sc, last)

        if not last:
            att_c = _attention(qc, kc, vc, batch, ctx_len, None)
            four_c = _fourier(abc, batch, ctx_len, dft_c)
            pool_c = _pool(zpc, batch, ctx_len, lw["wg"], lw["ps"])
            c1, hc2, pqc = _out_proj(cs, att_c, four_c, pool_c, gc, mod, c_row, ctx_len, lw)
            cs = _peer_block(c1, hc2, pqc, keys, mod, c_row, ctx_len, final_g, table, None, 0,
                             False)
    xs = jnp.concatenate(chains, axis=0)
    return xs.reshape(batch, seq_len, d)
```

```python
import functools
import math

import jax
import jax.numpy as jnp
from jax import lax
from jax.experimental import pallas as pl
from jax.experimental.pallas import tpu as pltpu
from jax.experimental.pallas import tpu_sc as plsc

F32 = jnp.float32
BF16 = jnp.bfloat16
I32 = jnp.int32

GRID_W = 64
N_HEADS = 8
Q_LORA = 256
KV_LORA = 128
QK_NOPE = 64
QK_ROPE = 32
V_HEAD = 64
ATT_WIDTH = N_HEADS * V_HEAD
ATT_SCALE = (QK_NOPE + QK_ROPE) ** -0.5
ROPE_BASE = 10000.0
FOURIER_WIDTH = 256
POOL_WINDOWS = (2, 4, 8, 16)
POOL_GROUP = 64
POOL_WIDTH = POOL_GROUP * len(POOL_WINDOWS)
N_BRANCH = 3
PEER_HEADS = 8
N_KEYS = 128
PEER_QDIM = 256
PEER_HALF = PEER_QDIM // 2
PEER_TOPK = 16
PEER_SLOTS = PEER_HEADS * PEER_TOPK
STAIR = tuple((i, PEER_TOPK // (i + 1)) for i in range(PEER_TOPK))
N_CAND = sum(cnt for _, cnt in STAIR)
CAND_ROWS = -(-N_CAND // 8) * 8
EPS = 1e-6

LANES = 128
SUBLANES = 8
HEAD_PAD = 128
POOL_PAD = 8
MOD_ROWS = 16
VMEM_LIMIT = 48 * 1024 * 1024

TOK_TILE = 256
ATT_Q_TILE = 256
FOUR_TILE = 512
ROUTE_TILE = 128
PEER_TOK = 16
PEER_WAIT_GROUPS = 4
PEER_VMEM_LIMIT = 56 * 1024 * 1024

SC_CORES = 2
SC_SUBCORES = 16
SC_WORKERS = SC_CORES * SC_SUBCORES
SC_LANES = 16
SC_CHUNK = 32
PEER_CHAINS = 2
PEER_SC_TOKENS = 5888
PEER_SC_TOKENS_FIRST = 7936
PEER_SC_TOKENS_MIDDLE = 7680
GELU_C = math.sqrt(2.0 / math.pi)

C_CQ = 0
C_CKV = C_CQ + Q_LORA
C_ZF = C_CKV + KV_LORA
C_ZP = C_ZF + FOURIER_WIDTH
C_ZG = C_ZP + POOL_WIDTH
C_KR = None


def _params(sem=None):
    return pltpu.CompilerParams(dimension_semantics=sem, vmem_limit_bytes=VMEM_LIMIT)


def _rms(x):
    return x * lax.rsqrt(jnp.mean(x * x, axis=-1, keepdims=True) + EPS)


def _dot(a, b):
    return jnp.dot(a, b, preferred_element_type=F32)


def _dot_nt(a, b):
    return lax.dot_general(a, b, (((1,), (1,)), ((), ())), preferred_element_type=F32)


def _mod_kernel(c_ref, w_ref, b_ref, o_ref):
    c = c_ref[...]
    s = c * jax.nn.sigmoid(c)
    o_ref[0] = jnp.dot(s, w_ref[0], preferred_element_type=F32,
                       precision=lax.Precision.HIGHEST) + b_ref[0]


def _modulation(cvec, w_mod, b_mod):
    depth, d, n = w_mod.shape
    tn = 1536
    return pl.pallas_call(
        _mod_kernel,
        out_shape=jax.ShapeDtypeStruct((depth, MOD_ROWS, n), F32),
        grid=(depth, n // tn),
        in_specs=[pl.BlockSpec((MOD_ROWS, d), lambda l, j: (0, 0)),
                  pl.BlockSpec((1, d, tn), lambda l, j: (l, 0, j)),
                  pl.BlockSpec((1, 1, tn), lambda l, j: (l, 0, j))],
        out_specs=pl.BlockSpec((1, MOD_ROWS, tn), lambda l, j: (l, 0, j)),
        compiler_params=_params(("arbitrary", "arbitrary")),
        name="modulation",
    )(cvec, w_mod, b_mod.reshape(depth, 1, n))


def _in_kernel(d, full, x_ref, m_ref, n1_ref, wall_ref, qg_ref, wq_ref, wqp_ref, kvg_ref,
               wk_ref, wv_ref, fc_ref, bg_ref, cq_ref, sq_ref, ck_ref, sk_ref, *outs):
    x = x_ref[...]
    m = m_ref[0]
    h = _rms(x) * n1_ref[...] * (1.0 + m[:, d:2 * d]) + m[:, 0:d]
    hb = h.astype(BF16)
    c_kr = C_ZG + N_BRANCH * d
    if full:
        z = _dot(hb, wall_ref[...])
        k_ref, v_ref, q_ref, ab_ref, zp_ref, g_ref = outs
    else:
        z = None
        k_ref, v_ref = outs
    def col(lo, hi):
        if full:
            return z[:, lo:hi]
        return _dot(hb, wall_ref[:, lo:hi])

    ckv = col(C_CKV, C_ZF)
    ckvn = (_rms(ckv) * kvg_ref[...]).astype(BF16)
    kf = _dot(ckvn, wk_ref[...])
    kr = col(c_kr, c_kr + LANES) * ck_ref[...] + col(c_kr + LANES, c_kr + 2 * LANES) * sk_ref[...]
    for hh in range(N_HEADS):
        sl = slice(hh * HEAD_PAD, (hh + 1) * HEAD_PAD)
        k_ref[:, sl] = (kf[:, sl] + kr).astype(BF16)
    v_ref[...] = _dot(ckvn, wv_ref[...]).astype(BF16)
    if not full:
        return
    cq = z[:, C_CQ:C_CKV]
    cqn = (_rms(cq) * qg_ref[...]).astype(BF16)
    qf = _dot(cqn, wq_ref[...])
    qr = _dot(cqn, wqp_ref[...])
    cosq = cq_ref[...]
    sinq = sq_ref[...]
    for hh in range(N_HEADS):
        sl = slice(hh * HEAD_PAD, (hh + 1) * HEAD_PAD)
        q_ref[:, sl] = (qf[:, sl] * cosq + qr[:, sl] * sinq).astype(BF16)
    ab_ref[...] = _dot(z[:, C_ZF:C_ZP].astype(BF16), fc_ref[...]).astype(BF16)
    zp_ref[...] = z[:, C_ZP:C_ZG]
    g_ref[...] = jax.nn.sigmoid(z[:, C_ZG:c_kr] + bg_ref[...]).astype(BF16)


def _in_proj(x2d, mod, mod_row, seq_len, lw, tabs, full):
    t, d = x2d.shape
    tm = min(TOK_TILE, seq_len)
    tiles_per_seq = seq_len // tm
    wall = lw["wall"]
    nw = wall.shape[1]
    cq, sq, ck, sk = tabs

    def const(shape):
        return pl.BlockSpec(shape, lambda i: (0,) * len(shape))

    def pos(i):
        return (i % tiles_per_seq, 0)

    in_specs = [
        pl.BlockSpec((tm, d), lambda i: (i, 0)),
        pl.BlockSpec((1, 1, mod.shape[-1]), lambda i: (mod_row(i, tiles_per_seq), 0, 0)),
        const((1, d)), const((d, nw)), const((1, Q_LORA)),
        const(lw["wq"].shape), const(lw["wqp"].shape), const((1, KV_LORA)),
        const(lw["wk"].shape), const(lw["wv"].shape), const(lw["fc"].shape),
        const((1, N_BRANCH * d)),
        pl.BlockSpec((tm, LANES), pos), pl.BlockSpec((tm, LANES), pos),
        pl.BlockSpec((tm, LANES), pos), pl.BlockSpec((tm, LANES), pos),
    ]
    kw = N_HEADS * HEAD_PAD
    out_shape = [jax.ShapeDtypeStruct((t, kw), BF16), jax.ShapeDtypeStruct((t, ATT_WIDTH), BF16)]
    out_specs = [pl.BlockSpec((tm, kw), lambda i: (i, 0)),
                 pl.BlockSpec((tm, ATT_WIDTH), lambda i: (i, 0))]
    if full:
        out_shape += [jax.ShapeDtypeStruct((t, kw), BF16),
                      jax.ShapeDtypeStruct((t, 2 * FOURIER_WIDTH), BF16),
                      jax.ShapeDtypeStruct((t, POOL_WIDTH), F32),
                      jax.ShapeDtypeStruct((t, N_BRANCH * d), BF16)]
        out_specs += [pl.BlockSpec((tm, kw), lambda i: (i, 0)),
                      pl.BlockSpec((tm, 2 * FOURIER_WIDTH), lambda i: (i, 0)),
                      pl.BlockSpec((tm, POOL_WIDTH), lambda i: (i, 0)),
                      pl.BlockSpec((tm, N_BRANCH * d), lambda i: (i, 0))]
    return pl.pallas_call(
        functools.partial(_in_kernel, d, full),
        out_shape=out_shape,
        grid=(t // tm,),
        in_specs=in_specs,
        out_specs=out_specs,
        compiler_params=_params(("arbitrary",)),
        name="in_proj" if full else "ctx_kv_proj",
    )(x2d, mod, lw["n1"], wall, lw["qg"], lw["wq"], lw["wqp"], lw["kvg"], lw["wk"], lw["wv"],
      lw["fc"], lw["bg"], cq, sq, ck, sk)


def _attn_kernel(has_ctx, q_ref, k_ref, v_ref, *rest):
    if has_ctx:
        kc_ref, vc_ref, o_ref = rest
    else:
        (o_ref,) = rest
    outs = []
    for hh in range(2):
        sl = slice(hh * HEAD_PAD, (hh + 1) * HEAD_PAD)
        q = q_ref[:, sl]
        s = _dot_nt(q, k_ref[:, sl])
        mx = jnp.max(s, axis=-1, keepdims=True)
        if has_ctx:
            sc = _dot_nt(q, kc_ref[:, sl])
            mx = jnp.maximum(mx, jnp.max(sc, axis=-1, keepdims=True))
        p = jnp.exp(s - mx)
        den = jnp.sum(p, axis=-1, keepdims=True)
        o = _dot(p.astype(BF16), v_ref[...])
        if has_ctx:
            pc = jnp.exp(sc - mx)
            den = den + jnp.sum(pc, axis=-1, keepdims=True)
            o = o + _dot(pc.astype(BF16), vc_ref[...])
        outs.append(o / den)
    lane = lax.broadcasted_iota(I32, outs[0].shape, 1)
    o_ref[...] = jnp.where(lane < V_HEAD, outs[0], outs[1]).astype(BF16)


def _attention(q, k, v, batch, seq_len, ctx_kv):
    t = q.shape[0]
    tq = min(ATT_Q_TILE, seq_len)
    nq = seq_len // tq
    pair_w = 2 * HEAD_PAD
    in_specs = [pl.BlockSpec((tq, pair_w), lambda b, j, i: (b * nq + i, j)),
                pl.BlockSpec((seq_len, pair_w), lambda b, j, i: (b, j)),
                pl.BlockSpec((seq_len, 2 * V_HEAD), lambda b, j, i: (b, j))]
    args = [q, k, v]
    if ctx_kv is not None:
        kc, vc = ctx_kv
        lc = kc.shape[0] // batch
        in_specs += [pl.BlockSpec((lc, pair_w), lambda b, j, i: (b, j)),
                     pl.BlockSpec((lc, 2 * V_HEAD), lambda b, j, i: (b, j))]
        args += [kc, vc]
    return pl.pallas_call(
        functools.partial(_attn_kernel, ctx_kv is not None),
        out_shape=jax.ShapeDtypeStruct((t, ATT_WIDTH), BF16),
        grid=(batch, N_HEADS // 2, nq),
        in_specs=in_specs,
        out_specs=pl.BlockSpec((tq, 2 * V_HEAD), lambda b, j, i: (b * nq + i, j)),
        compiler_params=_params(("arbitrary", "arbitrary", "arbitrary")),
        name="attention" if ctx_kv is not None else "ctx_attention",
    )(*args)


def _fourier_kernel(norm, c_ref, s_ref, ab_ref, o_ref):
    a = ab_ref[:, 0:FOURIER_WIDTH]
    b = ab_ref[:, FOURIER_WIDTH:2 * FOURIER_WIDTH]
    o = _dot(c_ref[...], a) - _dot(s_ref[...], b)
    o_ref[...] = (o * norm).astype(BF16)


def _fourier(ab, batch, seq_len, dft):
    t = ab.shape[0]
    tm = min(FOUR_TILE, seq_len)
    nt = seq_len // tm
    cl, sl = dft
    norm = 1.0 / math.sqrt(seq_len * FOURIER_WIDTH)
    return pl.pallas_call(
        functools.partial(_fourier_kernel, norm),
        out_shape=jax.ShapeDtypeStruct((t, FOURIER_WIDTH), BF16),
        grid=(nt, batch),
        in_specs=[pl.BlockSpec((tm, seq_len), lambda i, b: (i, 0)),
                  pl.BlockSpec((tm, seq_len), lambda i, b: (i, 0)),
                  pl.BlockSpec((seq_len, 2 * FOURIER_WIDTH), lambda i, b: (b, 0))],
        out_specs=pl.BlockSpec((tm, FOURIER_WIDTH), lambda i, b: (b * nt + i, 0)),
        compiler_params=_params(("arbitrary", "arbitrary")),
        name="fourier",
    )(cl, sl, ab)


def _pool_kernel(seq_len, z_ref, wg_ref, ps_ref, o_ref, pad_ref, s_ref):
    n = seq_len
    p = n + 2 * POOL_PAD
    z = z_ref[...]
    zeros = jnp.zeros((POOL_PAD, POOL_WIDTH), F32)
    pad_ref[0:POOL_PAD, :] = zeros
    pad_ref[POOL_PAD + n:p, :] = zeros
    pad_ref[POOL_PAD:POOL_PAD + n, :] = z
    s_ref[0:p - 1, :] = pad_ref[0:p - 1, :] + pad_ref[1:p, :]
    w2 = s_ref[POOL_PAD - 1:POOL_PAD - 1 + n, :]
    pad_ref[0:p - 3, :] = s_ref[0:p - 3, :] + s_ref[2:p - 1, :]
    w4 = pad_ref[POOL_PAD - 2:POOL_PAD - 2 + n, :]
    s_ref[0:p - 7, :] = pad_ref[0:p - 7, :] + pad_ref[4:p - 3, :]
    w8 = s_ref[POOL_PAD - 4:POOL_PAD - 4 + n, :]
    pad_ref[0:p - 15, :] = s_ref[0:p - 15, :] + s_ref[8:p - 7, :]
    w16 = pad_ref[0:n, :]
    pos = lax.broadcasted_iota(I32, (n, POOL_WIDTH), 0)
    grp = lax.broadcasted_iota(I32, (n, POOL_WIDTH), 1) // POOL_GROUP
    win = jnp.where(grp == 0, w2, jnp.where(grp == 1, w4, jnp.where(grp == 2, w8, w16)))
    half = jnp.where(grp == 0, 1, jnp.where(grp == 1, 2, jnp.where(grp == 2, 4, 8)))
    lo = jnp.maximum(pos - half, 0)
    hi = jnp.minimum(pos + half, n)
    cnt = (hi - lo).astype(F32)
    pooled = win / cnt - z
    y = _dot(pooled.astype(BF16), wg_ref[...])
    o_ref[...] = (y * ps_ref[...]).astype(BF16)


def _pool(zp, batch, seq_len, wg_bd, pool_scale):
    t = zp.shape[0]
    return pl.pallas_call(
        functools.partial(_pool_kernel, seq_len),
        out_shape=jax.ShapeDtypeStruct((t, POOL_WIDTH), BF16),
        grid=(batch,),
        in_specs=[pl.BlockSpec((seq_len, POOL_WIDTH), lambda b: (b, 0)),
                  pl.BlockSpec((POOL_WIDTH, POOL_WIDTH), lambda b: (0, 0)),
                  pl.BlockSpec((1, POOL_WIDTH), lambda b: (0, 0))],
        out_specs=pl.BlockSpec((seq_len, POOL_WIDTH), lambda b: (b, 0)),
        scratch_shapes=[pltpu.VMEM((seq_len + 2 * POOL_PAD, POOL_WIDTH), F32),
                        pltpu.VMEM((seq_len + 2 * POOL_PAD, POOL_WIDTH), F32)],
        compiler_params=_params(("arbitrary",)),
        name="pool",
    )(zp, wg_bd, pool_scale)


def _out_kernel(d, x_ref, att_ref, four_ref, pool_ref, g_ref, m_ref, woa_ref, wob_ref, woc_ref,
                wout_ref, n2_ref, wpq_ref, x1_ref, h2_ref, pq_ref):
    m = m_ref[0]
    ya = _dot(att_ref[...], woa_ref[...])
    yb = _dot(four_ref[...], wob_ref[...])
    yc = _dot(pool_ref[...], woc_ref[...])
    mixp = (g_ref[:, 0:d].astype(F32) * ya + g_ref[:, d:2 * d].astype(F32) * yb
            + g_ref[:, 2 * d:3 * d].astype(F32) * yc)
    mix = _dot(mixp.astype(BF16), wout_ref[...])
    x1 = x_ref[...] + m[:, 2 * d:3 * d] * mix
    x1_ref[...] = x1
    h2 = _rms(x1) * n2_ref[...] * (1.0 + m[:, 4 * d:5 * d]) + m[:, 3 * d:4 * d]
    h2_ref[...] = h2
    pq = _dot(h2.astype(BF16), wpq_ref[...])
    for hp in range(2 * PEER_HEADS):
        pq_ref[hp] = pq[:, hp * PEER_HALF:(hp + 1) * PEER_HALF].astype(BF16)


def _out_proj(x2d, att, four, pool, g, mod, mod_row, seq_len, lw):
    t, d = x2d.shape
    tm = min(TOK_TILE, seq_len)
    tiles_per_seq = seq_len // tm

    def const(shape):
        return pl.BlockSpec(shape, lambda i: (0,) * len(shape))

    def row(w):
        return pl.BlockSpec((tm, w), lambda i: (i, 0))

    nhp = 2 * PEER_HEADS
    return pl.pallas_call(
        functools.partial(_out_kernel, d),
        out_shape=[jax.ShapeDtypeStruct((t, d), F32), jax.ShapeDtypeStruct((t, d), F32),
                   jax.ShapeDtypeStruct((nhp, t, PEER_HALF), BF16)],
        grid=(t // tm,),
        in_specs=[row(d), row(ATT_WIDTH), row(FOURIER_WIDTH), row(POOL_WIDTH), row(N_BRANCH * d),
                  pl.BlockSpec((1, 1, mod.shape[-1]), lambda i: (mod_row(i, tiles_per_seq), 0, 0)),
                  const(lw["woa"].shape), const(lw["wob"].shape), const(lw["woc"].shape),
                  const(lw["wout"].shape), const((1, d)), const(lw["wpq"].shape)],
        out_specs=[row(d), row(d), pl.BlockSpec((nhp, tm, PEER_HALF), lambda i: (0, i, 0))],
        compiler_params=_params(("arbitrary",)),
        name="out_proj",
    )(x2d, att, four, pool, g, mod, lw["woa"], lw["wob"], lw["woc"], lw["wout"], lw["n2"],
      lw["wpq"])


def _select_round(s, iota, n):
    mx = jnp.max(s, axis=0, keepdims=True)
    idx = jnp.min(jnp.where(s == mx, iota, n), axis=0, keepdims=True)
    hit = iota == idx
    return mx, idx, hit, jnp.where(hit, -jnp.inf, s)


def _route_kernel(pq_ref, keys_ref, eidx_ref, gate_ref, sv_a, si_a, sv_b, si_b, cand_ref, cidx_ref,
                  ts_ref, eidx_s, gate_s):
    tm = pq_ref.shape[1]
    iota_k = lax.broadcasted_iota(I32, (N_KEYS, tm), 0).astype(F32)
    iota_c = lax.broadcasted_iota(I32, (CAND_ROWS, tm), 0).astype(F32)
    cand_ref[N_CAND:CAND_ROWS, :] = jnp.full((CAND_ROWS - N_CAND, tm), -jnp.inf, F32)
    cidx_ref[N_CAND:CAND_ROWS, :] = jnp.zeros((CAND_ROWS - N_CAND, tm), F32)

    def sub_key_topk(hd, sv_ref, si_ref):
        s = [_dot_nt(keys_ref[2 * hd + p], pq_ref[2 * hd + p]) for p in range(2)]
        for r in range(PEER_TOPK):
            for p in range(2):
                mx, idx, _, s[p] = _select_round(s[p], iota_k, float(N_KEYS))
                sv_ref[p, r:r + 1, :] = mx
                si_ref[p, r:r + 1, :] = idx

    def pair_topk(hd, sv_ref, si_ref):
        off = 0
        for i, cnt in STAIR:
            cand_ref[off:off + cnt, :] = sv_ref[0, i:i + 1, :] + sv_ref[1, 0:cnt, :]
            cidx_ref[off:off + cnt, :] = si_ref[0, i:i + 1, :] * N_KEYS + si_ref[1, 0:cnt, :]
            off += cnt
        cidx = cidx_ref[...]
        c = cand_ref[...]
        base = pl.multiple_of(hd * PEER_TOPK, PEER_TOPK)
        for r in range(PEER_TOPK):
            mx, _, hit, c = _select_round(c, iota_c, float(CAND_ROWS))
            ts_ref[r:r + 1, :] = mx
            expert = jnp.sum(jnp.where(hit, cidx, 0.0), axis=0, keepdims=True)
            eidx_s[pl.ds(base + r, 1), :] = expert.astype(I32)
        ts = ts_ref[...]
        ex = jnp.exp(ts - ts[0:1, :])
        gate_s[pl.ds(base, PEER_TOPK), :] = ex / jnp.sum(ex, axis=0, keepdims=True)

    sub_key_topk(0, sv_a, si_a)

    def two_heads(j, carry):
        hd = 2 * j
        sub_key_topk(hd + 1, sv_b, si_b)
        pair_topk(hd, sv_a, si_a)
        sub_key_topk(hd + 2, sv_a, si_a)
        pair_topk(hd + 1, sv_b, si_b)
        return carry

    lax.fori_loop(0, PEER_HEADS // 2 - 1, two_heads, 0)
    sub_key_topk(PEER_HEADS - 1, sv_b, si_b)
    pair_topk(PEER_HEADS - 2, sv_a, si_a)
    pair_topk(PEER_HEADS - 1, sv_b, si_b)
    eidx_ref[...] = eidx_s[...].T
    gate_ref[...] = gate_s[...].T


def _route(pq, keys):
    nhp, t, _ = pq.shape
    tm = ROUTE_TILE
    lists = pltpu.VMEM((2, PEER_TOPK, tm), F32)
    return pl.pallas_call(
        _route_kernel,
        out_shape=[jax.ShapeDtypeStruct((t, PEER_SLOTS), I32),
                   jax.ShapeDtypeStruct((t, PEER_SLOTS), F32)],
        grid=(t // tm,),
        in_specs=[pl.BlockSpec((nhp, tm, PEER_HALF), lambda i: (0, i, 0)),
                  pl.BlockSpec((nhp, N_KEYS, PEER_HALF), lambda i: (0, 0, 0))],
        out_specs=[pl.BlockSpec((tm, PEER_SLOTS), lambda i: (i, 0)),
                   pl.BlockSpec((tm, PEER_SLOTS), lambda i: (i, 0))],
        scratch_shapes=[lists, lists, lists, lists,
                        pltpu.VMEM((CAND_ROWS, tm), F32),
                        pltpu.VMEM((CAND_ROWS, tm), F32),
                        pltpu.VMEM((PEER_TOPK, tm), F32),
                        pltpu.VMEM((PEER_SLOTS, tm), I32), pltpu.VMEM((PEER_SLOTS, tm), F32)],
        compiler_params=_params(("arbitrary",)),
        name="peer_route",
    )(pq, keys)


def _fold_rows(a, b, keep_a, shift):
    return jnp.where(keep_a, a + pltpu.roll(a, SUBLANES - shift, 0), b + pltpu.roll(b, shift, 0))


def _peer_kernel(final, idx_ref, idxn_ref, gate_ref, h_ref, x_ref, g2_ref, fg_ref, tab_ref,
                 o_ref, buf_a, buf_b, act_ref, sem):
    i = pl.program_id(0)
    n = pl.num_programs(0)
    grp_tok = PEER_TOK // PEER_WAIT_GROUPS
    grp_rows = grp_tok * PEER_SLOTS
    sub = lax.broadcasted_iota(I32, (SUBLANES, LANES), 0)
    keep = {sh: (sub & sh) == 0 for sh in (4, 2, 1)}
    g2 = g2_ref[0, 0]
    gate_t = gate_ref[...].T

    def row_copy(ids, t_src, t_dst, k, buf, s):
        return pltpu.make_async_copy(tab_ref.at[ids[t_src, k]], buf.at[t_dst * PEER_SLOTS + k],
                                     sem.at[s, t_dst // grp_tok])

    def issue_token(ids, t_src, t_dst, buf, s):
        for k in range(PEER_SLOTS):
            row_copy(ids, t_src, t_dst, k, buf, s).start(priority=k % 2)

    def wait_group(buf, s, q):
        pltpu.make_async_copy(tab_ref.at[pl.ds(0, grp_rows)],
                              buf.at[pl.ds(q * grp_rows, grp_rows)], sem.at[s, q]).wait()

    def token(buf, tl, tb):
        r0 = tl * PEER_SLOTS
        hv = h_ref[tb]
        groups = []
        for g in range(PEER_SLOTS // SUBLANES):
            p = [buf[r0 + g * SUBLANES + j].astype(F32)[0:SUBLANES] * hv for j in range(SUBLANES)]
            for sh in (4, 2, 1):
                half = len(p) // 2
                p = [_fold_rows(p[j], p[j + half], keep[sh], sh) for j in range(half)]
            groups.append(p[0])
        part = jnp.concatenate(groups, axis=0)
        a = jnp.sum(part, axis=1, keepdims=True)
        act = gate_t[:, tb:tb + 1] * jax.nn.gelu(a)
        act_ref[...] = jnp.broadcast_to(act, (PEER_SLOTS, LANES))
        accs = [jnp.zeros((SUBLANES, LANES), F32) for _ in range(4)]
        for k in range(PEER_SLOTS):
            up = buf[r0 + k].astype(F32)[SUBLANES:2 * SUBLANES]
            accs[k % 4] = accs[k % 4] + act_ref[k:k + 1, :] * up
        xr = x_ref[tb] + g2 * ((accs[0] + accs[1]) + (accs[2] + accs[3]))
        if final:
            ms = jnp.sum(jnp.sum(xr * xr, axis=1, keepdims=True), axis=0, keepdims=True)
            xr = xr * lax.rsqrt(ms / (SUBLANES * LANES) + EPS) * fg_ref[...]
        o_ref[tb] = xr

    @pl.when(i == 0)
    def _():
        def body(t, carry):
            issue_token(idx_ref, t, t, buf_a, 0)
            return carry
        lax.fori_loop(0, PEER_TOK, body, 0)

    for t in range(PEER_TOK):
        if t % grp_tok == 0:
            wait_group(buf_a, 0, t // grp_tok)
        issue_token(idx_ref, PEER_TOK + t, t, buf_b, 1)
        token(buf_a, t, t)
    for t in range(PEER_TOK):
        if t % grp_tok == 0:
            wait_group(buf_b, 1, t // grp_tok)
        issue_token(idxn_ref, t, t, buf_a, 0)
        token(buf_b, t, PEER_TOK + t)

    @pl.when(i == n - 1)
    def _():
        for q in range(PEER_WAIT_GROUPS):
            wait_group(buf_a, 0, q)


def _peer(n_tok, eidx, gate, h3, x3, g2, mod_row, seq_len, fg, table, final):
    t = n_tok
    tb = 2 * PEER_TOK
    nb = t // tb
    steps_per_seq = seq_len // tb
    rows = PEER_TOK * PEER_SLOTS
    tile = (SUBLANES, LANES)
    pair = (2 * SUBLANES, LANES)
    return pl.pallas_call(
        functools.partial(_peer_kernel, final),
        out_shape=jax.ShapeDtypeStruct((t,) + tile, F32),
        grid=(nb,),
        in_specs=[pl.BlockSpec((tb, PEER_SLOTS), lambda i: (i, 0), memory_space=pltpu.SMEM),
                  pl.BlockSpec((tb, PEER_SLOTS), lambda i: (jnp.minimum(i + 1, nb - 1), 0),
                               memory_space=pltpu.SMEM),
                  pl.BlockSpec((tb, PEER_SLOTS), lambda i: (i, 0)),
                  pl.BlockSpec((tb,) + tile, lambda i: (i, 0, 0)),
                  pl.BlockSpec((tb,) + tile, lambda i: (i, 0, 0)),
                  pl.BlockSpec((1, 1) + tile, lambda i: (mod_row(i, steps_per_seq), 0, 0, 0)),
                  pl.BlockSpec(tile, lambda i: (0, 0)),
                  pl.BlockSpec(memory_space=pl.ANY)],
        out_specs=pl.BlockSpec((tb,) + tile, lambda i: (i, 0, 0)),
        scratch_shapes=[pltpu.VMEM((rows,) + pair, BF16), pltpu.VMEM((rows,) + pair, BF16),
                        pltpu.VMEM((PEER_SLOTS, LANES), F32),
                        pltpu.SemaphoreType.DMA((2, PEER_WAIT_GROUPS))],
        compiler_params=pltpu.CompilerParams(dimension_semantics=("arbitrary",),
                                             vmem_limit_bytes=PEER_VMEM_LIMIT),
        name="peer_experts",
    )(eidx, eidx, gate, h3, x3, g2, fg, table)


def _peer_sc_body(first, n_tok, d, eidx_hbm, gate_hbm, h_hbm, tab_hbm, y_hbm, idx_v, idx_n, gate_v, gate_n,
                  h_v, h_n, rows_a, rows_b, out_v, sem_a, sem_b, sem_n):
    wid = lax.axis_index("s") * SC_CORES + lax.axis_index("c")
    per_worker = n_tok // SC_WORKERS
    tok0 = first + wid * per_worker
    nj = d // SC_LANES
    lane = lax.iota(I32, SC_LANES)
    zero = jnp.zeros((SC_LANES,), F32)
    hi_mask = jnp.full((SC_LANES,), 0xFFFF0000, jnp.uint32)

    def gather(idx_ref, c, rows, sem):
        row0 = pl.multiple_of(c * SC_CHUNK, SC_CHUNK)
        return pltpu.make_async_copy(tab_hbm.at[idx_ref.at[pl.ds(row0, SC_CHUNK)]], rows, sem)

    def copy_words(src, dst, n):
        def step(j, carry):
            off = pl.multiple_of(j * SC_LANES, SC_LANES)
            dst[pl.ds(off, SC_LANES)] = src[pl.ds(off, SC_LANES)]
            return carry
        lax.fori_loop(0, n // SC_LANES, step, 0)

    def compute(c, rows):
        row0 = pl.multiple_of(c * SC_CHUNK, SC_CHUNK)

        def down_step(j, accs):
            off = pl.multiple_of(j * SC_LANES, SC_LANES)
            hj = h_v[pl.ds(off, SC_LANES)]
            out = []
            for r in range(SC_CHUNK):
                w = rows[r, pl.ds(off, SC_LANES)]
                dn = lax.bitcast_convert_type(w & hi_mask, F32)
                out.append(accs[r] + dn * hj)
            return tuple(out)

        accs = lax.fori_loop(0, nj, down_step, tuple(zero for _ in range(SC_CHUNK)))
        acts = []
        for g in range(SC_CHUNK // SC_LANES):
            a = zero
            for r in range(SC_LANES):
                a = jnp.where(lane == r, jnp.sum(accs[g * SC_LANES + r]), a)
            gt = gate_v[pl.ds(row0 + g * SC_LANES, SC_LANES)]
            u = GELU_C * (a + 0.044715 * (a * a * a))
            th = 1.0 - 2.0 / (jnp.exp(2.0 * u) + 1.0)
            act = gt * (0.5 * a * (1.0 + th))
            for r in range(SC_LANES):
                acts.append(jnp.sum(jnp.where(lane == r, act, 0.0)))

        def up_step(j, carry):
            off = pl.multiple_of(j * SC_LANES, SC_LANES)
            o = out_v[pl.ds(off, SC_LANES)]
            for r in range(SC_CHUNK):
                w = rows[r, pl.ds(off, SC_LANES)]
                up = lax.bitcast_convert_type(w << 16, F32)
                o = o + acts[r] * up
            out_v[pl.ds(off, SC_LANES)] = o
            return carry

        lax.fori_loop(0, nj, up_step, 0)

    def token(ti, carry):
        t = tok0 + ti
        tn = jnp.minimum(t + 1, tok0 + per_worker - 1)

        def clear(j, carry2):
            out_v[pl.ds(pl.multiple_of(j * SC_LANES, SC_LANES), SC_LANES)] = zero
            return carry2

        lax.fori_loop(0, nj, clear, 0)

        def pair(pp, carry2):
            gather(idx_v, 2 * pp + 1, rows_b, sem_b).start()
            gather(idx_v, 2 * pp, rows_a, sem_a).wait()
            compute(2 * pp, rows_a)

            @pl.when(pp == 0)
            def _():
                gather(idx_v, 2, rows_a, sem_a).start()

            @pl.when(pp == 1)
            def _():
                pltpu.sync_copy(eidx_hbm.at[tn], idx_n)
                gather(idx_n, 0, rows_a, sem_a).start()
                pltpu.make_async_copy(gate_hbm.at[tn], gate_n, sem_n).start()
                pltpu.make_async_copy(h_hbm.at[tn], h_n, sem_n).start()

            gather(idx_v, 2 * pp + 1, rows_b, sem_b).wait()
            compute(2 * pp + 1, rows_b)
            return carry2

        lax.fori_loop(0, PEER_SLOTS // (2 * SC_CHUNK), pair, 0)
        pltpu.sync_copy(out_v, y_hbm.at[t - first])
        pltpu.make_async_copy(gate_hbm.at[tn], gate_n, sem_n).wait()
        pltpu.make_async_copy(h_hbm.at[tn], h_n, sem_n).wait()
        copy_words(idx_n, idx_v, PEER_SLOTS)
        copy_words(gate_n, gate_v, PEER_SLOTS)
        copy_words(h_n, h_v, d)
        return carry

    pltpu.sync_copy(eidx_hbm.at[tok0], idx_v)
    pltpu.sync_copy(gate_hbm.at[tok0], gate_v)
    pltpu.sync_copy(h_hbm.at[tok0], h_v)
    gather(idx_v, 0, rows_a, sem_a).start()
    lax.fori_loop(0, per_worker, token, 0)
    gather(idx_v, 0, rows_a, sem_a).wait()


def _peer_sc(first, n_tok, eidx, gate, h2, sc_table):
    d = h2.shape[1]
    mesh = plsc.VectorSubcoreMesh(core_axis_name="c", subcore_axis_name="s",
                                  num_cores=SC_CORES, num_subcores=SC_SUBCORES)
    return pl.kernel(
        functools.partial(_peer_sc_body, first, n_tok, d),
        out_type=jax.ShapeDtypeStruct((n_tok, d), F32),
        mesh=mesh,
        scratch_types=[pltpu.VMEM((PEER_SLOTS,), I32), pltpu.VMEM((PEER_SLOTS,), I32),
                       pltpu.VMEM((PEER_SLOTS,), F32), pltpu.VMEM((PEER_SLOTS,), F32),
                       pltpu.VMEM((d,), F32), pltpu.VMEM((d,), F32),
                       pltpu.VMEM((SC_CHUNK, d), jnp.uint32), pltpu.VMEM((SC_CHUNK, d), jnp.uint32),
                       pltpu.VMEM((d,), F32), pltpu.SemaphoreType.DMA, pltpu.SemaphoreType.DMA,
                       pltpu.SemaphoreType.DMA],
        compiler_params=pltpu.CompilerParams(needs_layout_passes=False),
        cost_estimate=pl.CostEstimate(
            flops=4 * n_tok * PEER_SLOTS * d, transcendentals=n_tok * PEER_SLOTS,
            bytes_accessed=n_tok * (PEER_SLOTS * d * 4 + 2 * d * 4 + 2 * PEER_SLOTS * 4)),
        name="peer_experts_sc",
    )(eidx, gate, h2, sc_table)


def _finish_kernel(final, d, y_ref, x_ref, m_ref, fg_ref, o_ref):
    xr = x_ref[...] + m_ref[0][:, 5 * d:6 * d] * y_ref[...]
    if final:
        xr = _rms(xr) * fg_ref[...]
    o_ref[...] = xr


def _peer_finish(y, x1, mod, mod_row, seq_len, tok0, fg, final):
    t, d = y.shape
    tm = min(TOK_TILE, seq_len)
    tiles_per_seq = seq_len // tm
    tile0 = tok0 // tm
    return pl.pallas_call(
        functools.partial(_finish_kernel, final, d),
        out_shape=jax.ShapeDtypeStruct((t, d), F32),
        grid=(t // tm,),
        in_specs=[pl.BlockSpec((tm, d), lambda i: (i, 0)),
                  pl.BlockSpec((tm, d), lambda i: (i + tile0, 0)),
                  pl.BlockSpec((1, 1, mod.shape[-1]),
                               lambda i: (mod_row(i + tile0, tiles_per_seq), 0, 0)),
                  pl.BlockSpec((1, d), lambda i: (0, 0))],
        out_specs=pl.BlockSpec((tm, d), lambda i: (i, 0)),
        compiler_params=_params(("arbitrary",)),
        name="peer_finish",
    )(y, x1, mod, fg.reshape(1, d))


def _rope_perm(w):
    q = QK_ROPE // 4
    a1, a2, b1, b2 = (w[..., j * q:(j + 1) * q] for j in range(4))
    return jnp.concatenate([-a2, a1, -b2, b1], axis=-1)


def _prep_layer(l, d, w_in, b_gate, q_norm_g, w_uq, kv_norm_g, w_ukv, w_oa, w_ob, w_grp,
                pool_scale, w_oc, w_out, w_pq, norm1_g, norm2_g):
    wi = w_in[l]
    s0 = Q_LORA
    s1 = s0 + KV_LORA
    s2 = s1 + QK_ROPE
    s3 = s2 + FOURIER_WIDTH
    s4 = s3 + POOL_WIDTH
    w_kr = wi[:, s1:s2]
    zl = jnp.zeros((d, QK_NOPE), F32)
    zr = jnp.zeros((d, HEAD_PAD - QK_NOPE - QK_ROPE), F32)
    wall = jnp.concatenate([wi[:, 0:s1], wi[:, s2:], zl, w_kr, zr, zl, _rope_perm(w_kr), zr],
                           axis=1).astype(BF16)
    wq = w_uq[l].reshape(Q_LORA, N_HEADS, QK_NOPE + QK_ROPE)
    qpad = jnp.zeros((Q_LORA, N_HEADS, HEAD_PAD - QK_NOPE - QK_ROPE), F32)
    wq_full = jnp.concatenate([wq, qpad], axis=-1).reshape(Q_LORA, N_HEADS * HEAD_PAD)
    wq_perm = jnp.concatenate([jnp.zeros((Q_LORA, N_HEADS, QK_NOPE), F32),
                               _rope_perm(wq[..., QK_NOPE:]), qpad],
                              axis=-1).reshape(Q_LORA, N_HEADS * HEAD_PAD)
    wkv = w_ukv[l].reshape(KV_LORA, N_HEADS, QK_NOPE + V_HEAD)
    wk = jnp.concatenate([wkv[..., :QK_NOPE],
                          jnp.zeros((KV_LORA, N_HEADS, HEAD_PAD - QK_NOPE), F32)],
                         axis=-1).reshape(KV_LORA, N_HEADS * HEAD_PAD)
    wv = wkv[..., QK_NOPE:].reshape(KV_LORA, ATT_WIDTH)
    cidx = jnp.arange(FOURIER_WIDTH, dtype=I32)
    ang = (2.0 * math.pi / FOURIER_WIDTH) * ((cidx[:, None] * cidx[None, :]) % FOURIER_WIDTH
                                             ).astype(F32)
    fc = jnp.concatenate([jnp.cos(ang), jnp.sin(ang)], axis=1).astype(BF16)
    ng = len(POOL_WINDOWS)
    wg = jnp.zeros((ng, POOL_GROUP, ng, POOL_GROUP), F32)
    for gi in range(ng):
        wg = wg.at[gi, :, gi, :].set(w_grp[l, gi])
    return {
        "wall": wall, "n1": norm1_g[l][None, :], "n2": norm2_g[l][None, :],
        "qg": q_norm_g[l][None, :], "kvg": kv_norm_g[l][None, :],
        "wq": wq_full.astype(BF16), "wqp": wq_perm.astype(BF16),
        "wk": wk.astype(BF16), "wv": wv.astype(BF16), "fc": fc,
        "bg": b_gate[l][None, :],
        "wg": wg.reshape(POOL_WIDTH, POOL_WIDTH).astype(BF16),
        "ps": pool_scale[l][None, :],
        "woa": w_oa[l].astype(BF16), "wob": w_ob[l].astype(BF16), "woc": w_oc[l].astype(BF16),
        "wout": w_out[l].astype(BF16), "wpq": w_pq[l].astype(BF16),
    }


def _bf16_bits(w):
    return lax.bitcast_convert_type(w.astype(BF16), jnp.uint16).astype(jnp.uint32)


def _rope_tables(seq_len, rope):
    zeros_n = jnp.zeros((seq_len, QK_NOPE), F32)
    zeros_p = jnp.zeros((seq_len, HEAD_PAD - QK_NOPE - QK_ROPE), F32)
    ones_n = jnp.ones((seq_len, QK_NOPE), F32)
    if rope:
        pos = jnp.arange(seq_len, dtype=I32)
        half = QK_ROPE // 2
        inv_freq = ROPE_BASE ** (-jnp.arange(0, half, 2, dtype=F32) / half)
        ang_r = (pos // GRID_W).astype(F32)[:, None] * inv_freq
        ang_c = (pos % GRID_W).astype(F32)[:, None] * inv_freq
        cos = jnp.concatenate([jnp.cos(ang_r)] * 2 + [jnp.cos(ang_c)] * 2, axis=1)
        sin = jnp.concatenate([jnp.sin(ang_r)] * 2 + [jnp.sin(ang_c)] * 2, axis=1)
    else:
        cos = jnp.ones((seq_len, QK_ROPE), F32)
        sin = jnp.zeros((seq_len, QK_ROPE), F32)
    cq = jnp.concatenate([ones_n, cos, zeros_p], axis=1) * ATT_SCALE
    sq = jnp.concatenate([zeros_n, sin, zeros_p], axis=1) * ATT_SCALE
    ck = jnp.concatenate([zeros_n, cos, zeros_p], axis=1)
    sk = jnp.concatenate([zeros_n, sin, zeros_p], axis=1)
    return cq, sq, ck, sk


def _dft(seq_len):
    idx = jnp.arange(seq_len, dtype=I32)
    ang = (2.0 * math.pi / seq_len) * ((idx[:, None] * idx[None, :]) % seq_len).astype(F32)
    return jnp.cos(ang).astype(BF16), jnp.sin(ang).astype(BF16)


def _peer_block(x1, h2, pq, keys, mod, mod_row, seq_len, fg, table, sc_table, n_sc, final):
    t, d = x1.shape
    tile = (SUBLANES, LANES)
    eidx, gate = _route(pq, keys)
    n_tc = t - n_sc
    mod_tiles = mod.reshape(MOD_ROWS, 6, SUBLANES, LANES)[:, 5:6]
    out = _peer(n_tc, eidx, gate, h2.reshape((t,) + tile), x1.reshape((t,) + tile), mod_tiles,
                mod_row, seq_len, fg.reshape(tile), table, final).reshape(n_tc, d)
    if n_sc == 0:
        return out, ()
    y = _peer_sc(n_tc, n_sc, eidx, gate, h2, sc_table)
    out_sc = _peer_finish(y, x1, mod, mod_row, seq_len, n_tc, fg, final)
    return jnp.concatenate([out, out_sc], axis=0), (out, eidx, gate, h2)


def kernel(x, c, ctx, c_ctx, w_mod, b_mod, norm1_g, norm2_g, w_in, b_gate, q_norm_g, w_uq,
           kv_norm_g, w_ukv, w_oa, w_ob, w_grp, pool_scale, w_oc, w_out, w_pq, peer_keys,
           peer_down, peer_up, final_g):
    batch, seq_len, d = x.shape
    ctx_len = ctx.shape[1]
    depth = w_mod.shape[0]
    assert d == SUBLANES * LANES and batch + 1 <= MOD_ROWS
    assert seq_len % TOK_TILE == 0 and ctx_len % (2 * PEER_TOK) == 0 and ctx_len % LANES == 0
    tile = (SUBLANES, LANES)

    cvec = jnp.concatenate([c, c_ctx[None, :], jnp.zeros((MOD_ROWS - batch - 1, d), F32)], axis=0)
    mod_all = _modulation(cvec, w_mod, b_mod)

    def x_row(b0):
        return lambda i, per_seq: b0 + i // per_seq

    def c_row(i, per_seq):
        return batch

    tabs_x = _rope_tables(seq_len, True)
    tabs_c = _rope_tables(ctx_len, False)
    dft_x = _dft(seq_len)
    dft_c = _dft(ctx_len)

    n_chains = PEER_CHAINS if batch % PEER_CHAINS == 0 else 1
    bpc = batch // n_chains
    use_sc = bpc * seq_len >= 2 * max(PEER_SC_TOKENS, PEER_SC_TOKENS_FIRST, PEER_SC_TOKENS_MIDDLE)

    def sc_tokens(l, j):
        if not use_sc:
            return 0
        if l == depth - 1 and j == n_chains - 1:
            return PEER_SC_TOKENS
        return PEER_SC_TOKENS_FIRST if (l, j) == (0, 0) else PEER_SC_TOKENS_MIDDLE

    tables = [jnp.concatenate([peer_down[l].reshape((-1,) + tile),
                               peer_up[l].reshape((-1,) + tile)], axis=1).astype(BF16)
              for l in range(depth)]
    sc_tables = [(_bf16_bits(peer_down[l]) << 16) | _bf16_bits(peer_up[l]) for l in range(depth)]
    x, ctx, tables, sc_tables = lax.optimization_barrier((x, ctx, tables, sc_tables))

    chains = [x[j * bpc:(j + 1) * bpc].reshape(bpc * seq_len, d) for j in range(n_chains)]
    cs = ctx.reshape(batch * ctx_len, d)
    for l in range(depth):
        last = l == depth - 1
        lw = _prep_layer(l, d, w_in, b_gate, q_norm_g, w_uq, kv_norm_g, w_ukv, w_oa, w_ob, w_grp,
                         pool_scale, w_oc, w_out, w_pq, norm1_g, norm2_g)
        mod = mod_all[l].reshape(MOD_ROWS, 1, 6 * d)
        keys = peer_keys[l].reshape(2 * PEER_HEADS, N_KEYS, PEER_HALF).astype(BF16)
        table = tables[l]
        sc_table = sc_tables[l]

        if last:
            kc, vc = _in_proj(cs, mod, c_row, ctx_len, lw, tabs_c, False)
        else:
            kc, vc, qc, abc, zpc, gc = _in_proj(cs, mod, c_row, ctx_len, lw, tabs_c, True)

        sc_args = ()
        for j in range(n_chains):
            row = x_row(j * bpc)
            sc_args, xs = lax.optimization_barrier((sc_args, chains[j]))
            ctx_rows = slice(j * bpc * ctx_len, (j + 1) * bpc * ctx_len)
            kx, vx, qx, abx, zpx, gx = _in_proj(xs, mod, row, seq_len, lw, tabs_x, True)
            att_x = _attention(qx, kx, vx, bpc, seq_len, (kc[ctx_rows], vc[ctx_rows]))
            four_x = _fourier(abx, bpc, seq_len, dft_x)
            pool_x = _pool(zpx, bpc, seq_len, lw["wg"], lw["ps"])
            x1, h2, pq = _out_proj(xs, att_x, four_x, pool_x, gx, mod, row, seq_len, lw)
            chains[j], sc_args = _peer_block(x1, h2, pq, keys, mod, row, seq_len, final_g, table,
                                             sc_table, sc_tokens(l, j), last)

        if not last:
            att_c = _attention(qc, kc, vc, batch, ctx_len, None)
            four_c = _fourier(abc, batch, ctx_len, dft_c)
            pool_c = _pool(zpc, batch, ctx_len, lw["wg"], lw["ps"])
            c1, hc2, pqc = _out_proj(cs, att_c, four_c, pool_c, gc, mod, c_row, ctx_len, lw)
            cs, _ = _peer_block(c1, hc2, pqc, keys, mod, c_row, ctx_len, final_g, table, None, 0,
                                False)
    xs = jnp.concatenate(chains, axis=0)
    return xs.reshape(batch, seq_len, d)
```

```python
import functools
import math

import jax
import jax.numpy as jnp
from jax import lax
from jax.experimental import pallas as pl
from jax.experimental.pallas import tpu as pltpu
from jax.experimental.pallas import tpu_sc as plsc

F32 = jnp.float32
BF16 = jnp.bfloat16
I32 = jnp.int32

GRID_W = 64
N_HEADS = 8
Q_LORA = 256
KV_LORA = 128
QK_NOPE = 64
QK_ROPE = 32
V_HEAD = 64
ATT_WIDTH = N_HEADS * V_HEAD
ATT_SCALE = (QK_NOPE + QK_ROPE) ** -0.5
ROPE_BASE = 10000.0
FOURIER_WIDTH = 256
POOL_WINDOWS = (2, 4, 8, 16)
POOL_GROUP = 64
POOL_WIDTH = POOL_GROUP * len(POOL_WINDOWS)
N_BRANCH = 3
PEER_HEADS = 8
N_KEYS = 128
PEER_QDIM = 256
PEER_HALF = PEER_QDIM // 2
PEER_TOPK = 16
PEER_SLOTS = PEER_HEADS * PEER_TOPK
STAIR = tuple((i, PEER_TOPK // (i + 1)) for i in range(PEER_TOPK))
N_CAND = sum(cnt for _, cnt in STAIR)
CAND_ROWS = -(-N_CAND // 8) * 8
EPS = 1e-6

LANES = 128
SUBLANES = 8
HEAD_PAD = 128
POOL_PAD = 8
MOD_ROWS = 16
VMEM_LIMIT = 48 * 1024 * 1024

TOK_TILE = 256
ATT_Q_TILE = 256
FOUR_TILE = 512
ROUTE_TILE = 128
PEER_TOK = 16
PEER_WAIT_GROUPS = 4
PEER_VMEM_LIMIT = 56 * 1024 * 1024

SC_CORES = 2
SC_SUBCORES = 16
SC_WORKERS = SC_CORES * SC_SUBCORES
SC_LANES = 16
SC_CHUNK = 32
PEER_CHAINS = 2
PEER_SC_TOKENS = 5888
PEER_SC_TOKENS_FIRST = 8704
PEER_SC_TOKENS_MIDDLE = 8704
GELU_C = math.sqrt(2.0 / math.pi)

C_CQ = 0
C_CKV = C_CQ + Q_LORA
C_ZF = C_CKV + KV_LORA
C_ZP = C_ZF + FOURIER_WIDTH
C_ZG = C_ZP + POOL_WIDTH
C_KR = None


def _params(sem=None):
    return pltpu.CompilerParams(dimension_semantics=sem, vmem_limit_bytes=VMEM_LIMIT)


def _rms(x):
    return x * lax.rsqrt(jnp.mean(x * x, axis=-1, keepdims=True) + EPS)


def _dot(a, b):
    return jnp.dot(a, b, preferred_element_type=F32)


def _dot_nt(a, b):
    return lax.dot_general(a, b, (((1,), (1,)), ((), ())), preferred_element_type=F32)


def _mod_kernel(c_ref, w_ref, b_ref, o_ref):
    c = c_ref[...]
    s = c * jax.nn.sigmoid(c)
    o_ref[0] = jnp.dot(s, w_ref[0], preferred_element_type=F32,
                       precision=lax.Precision.HIGHEST) + b_ref[0]


def _modulation(cvec, w_mod, b_mod):
    depth, d, n = w_mod.shape
    tn = 1536
    return pl.pallas_call(
        _mod_kernel,
        out_shape=jax.ShapeDtypeStruct((depth, MOD_ROWS, n), F32),
        grid=(depth, n // tn),
        in_specs=[pl.BlockSpec((MOD_ROWS, d), lambda l, j: (0, 0)),
                  pl.BlockSpec((1, d, tn), lambda l, j: (l, 0, j)),
                  pl.BlockSpec((1, 1, tn), lambda l, j: (l, 0, j))],
        out_specs=pl.BlockSpec((1, MOD_ROWS, tn), lambda l, j: (l, 0, j)),
        compiler_params=_params(("arbitrary", "arbitrary")),
        name="modulation",
    )(cvec, w_mod, b_mod.reshape(depth, 1, n))


def _in_kernel(d, full, x_ref, m_ref, n1_ref, wall_ref, qg_ref, wq_ref, wqp_ref, kvg_ref,
               wk_ref, wv_ref, fc_ref, bg_ref, cq_ref, sq_ref, ck_ref, sk_ref, *outs):
    x = x_ref[...]
    m = m_ref[0]
    h = _rms(x) * n1_ref[...] * (1.0 + m[:, d:2 * d]) + m[:, 0:d]
    hb = h.astype(BF16)
    c_kr = C_ZG + N_BRANCH * d
    if full:
        z = _dot(hb, wall_ref[...])
        k_ref, v_ref, q_ref, ab_ref, zp_ref, g_ref = outs
    else:
        z = None
        k_ref, v_ref = outs
    def col(lo, hi):
        if full:
            return z[:, lo:hi]
        return _dot(hb, wall_ref[:, lo:hi])

    ckv = col(C_CKV, C_ZF)
    ckvn = (_rms(ckv) * kvg_ref[...]).astype(BF16)
    kf = _dot(ckvn, wk_ref[...])
    kr = col(c_kr, c_kr + LANES) * ck_ref[...] + col(c_kr + LANES, c_kr + 2 * LANES) * sk_ref[...]
    for hh in range(N_HEADS):
        sl = slice(hh * HEAD_PAD, (hh + 1) * HEAD_PAD)
        k_ref[:, sl] = (kf[:, sl] + kr).astype(BF16)
    v_ref[...] = _dot(ckvn, wv_ref[...]).astype(BF16)
    if not full:
        return
    cq = z[:, C_CQ:C_CKV]
    cqn = (_rms(cq) * qg_ref[...]).astype(BF16)
    qf = _dot(cqn, wq_ref[...])
    qr = _dot(cqn, wqp_ref[...])
    cosq = cq_ref[...]
    sinq = sq_ref[...]
    for hh in range(N_HEADS):
        sl = slice(hh * HEAD_PAD, (hh + 1) * HEAD_PAD)
        q_ref[:, sl] = (qf[:, sl] * cosq + qr[:, sl] * sinq).astype(BF16)
    ab_ref[...] = _dot(z[:, C_ZF:C_ZP].astype(BF16), fc_ref[...]).astype(BF16)
    zp_ref[...] = z[:, C_ZP:C_ZG]
    g_ref[...] = jax.nn.sigmoid(z[:, C_ZG:c_kr] + bg_ref[...]).astype(BF16)


def _in_proj(x2d, mod, mod_row, seq_len, lw, tabs, full):
    t, d = x2d.shape
    tm = min(TOK_TILE, seq_len)
    tiles_per_seq = seq_len // tm
    wall = lw["wall"]
    nw = wall.shape[1]
    cq, sq, ck, sk = tabs

    def const(shape):
        return pl.BlockSpec(shape, lambda i: (0,) * len(shape))

    def pos(i):
        return (i % tiles_per_seq, 0)

    in_specs = [
        pl.BlockSpec((tm, d), lambda i: (i, 0)),
        pl.BlockSpec((1, 1, mod.shape[-1]), lambda i: (mod_row(i, tiles_per_seq), 0, 0)),
        const((1, d)), const((d, nw)), const((1, Q_LORA)),
        const(lw["wq"].shape), const(lw["wqp"].shape), const((1, KV_LORA)),
        const(lw["wk"].shape), const(lw["wv"].shape), const(lw["fc"].shape),
        const((1, N_BRANCH * d)),
        pl.BlockSpec((tm, LANES), pos), pl.BlockSpec((tm, LANES), pos),
        pl.BlockSpec((tm, LANES), pos), pl.BlockSpec((tm, LANES), pos),
    ]
    kw = N_HEADS * HEAD_PAD
    out_shape = [jax.ShapeDtypeStruct((t, kw), BF16), jax.ShapeDtypeStruct((t, ATT_WIDTH), BF16)]
    out_specs = [pl.BlockSpec((tm, kw), lambda i: (i, 0)),
                 pl.BlockSpec((tm, ATT_WIDTH), lambda i: (i, 0))]
    if full:
        out_shape += [jax.ShapeDtypeStruct((t, kw), BF16),
                      jax.ShapeDtypeStruct((t, 2 * FOURIER_WIDTH), BF16),
                      jax.ShapeDtypeStruct((t, POOL_WIDTH), F32),
                      jax.ShapeDtypeStruct((t, N_BRANCH * d), BF16)]
        out_specs += [pl.BlockSpec((tm, kw), lambda i: (i, 0)),
                      pl.BlockSpec((tm, 2 * FOURIER_WIDTH), lambda i: (i, 0)),
                      pl.BlockSpec((tm, POOL_WIDTH), lambda i: (i, 0)),
                      pl.BlockSpec((tm, N_BRANCH * d), lambda i: (i, 0))]
    return pl.pallas_call(
        functools.partial(_in_kernel, d, full),
        out_shape=out_shape,
        grid=(t // tm,),
        in_specs=in_specs,
        out_specs=out_specs,
        compiler_params=_params(("arbitrary",)),
        name="in_proj" if full else "ctx_kv_proj",
    )(x2d, mod, lw["n1"], wall, lw["qg"], lw["wq"], lw["wqp"], lw["kvg"], lw["wk"], lw["wv"],
      lw["fc"], lw["bg"], cq, sq, ck, sk)


def _attn_kernel(has_ctx, q_ref, k_ref, v_ref, *rest):
    if has_ctx:
        kc_ref, vc_ref, o_ref = rest
    else:
        (o_ref,) = rest
    outs = []
    for hh in range(2):
        sl = slice(hh * HEAD_PAD, (hh + 1) * HEAD_PAD)
        q = q_ref[:, sl]
        s = _dot_nt(q, k_ref[:, sl])
        mx = jnp.max(s, axis=-1, keepdims=True)
        if has_ctx:
            sc = _dot_nt(q, kc_ref[:, sl])
            mx = jnp.maximum(mx, jnp.max(sc, axis=-1, keepdims=True))
        p = jnp.exp(s - mx)
        den = jnp.sum(p, axis=-1, keepdims=True)
        o = _dot(p.astype(BF16), v_ref[...])
        if has_ctx:
            pc = jnp.exp(sc - mx)
            den = den + jnp.sum(pc, axis=-1, keepdims=True)
            o = o + _dot(pc.astype(BF16), vc_ref[...])
        outs.append(o / den)
    lane = lax.broadcasted_iota(I32, outs[0].shape, 1)
    o_ref[...] = jnp.where(lane < V_HEAD, outs[0], outs[1]).astype(BF16)


def _attention(q, k, v, batch, seq_len, ctx_kv):
    t = q.shape[0]
    tq = min(ATT_Q_TILE, seq_len)
    nq = seq_len // tq
    pair_w = 2 * HEAD_PAD
    in_specs = [pl.BlockSpec((tq, pair_w), lambda b, j, i: (b * nq + i, j)),
                pl.BlockSpec((seq_len, pair_w), lambda b, j, i: (b, j)),
                pl.BlockSpec((seq_len, 2 * V_HEAD), lambda b, j, i: (b, j))]
    args = [q, k, v]
    if ctx_kv is not None:
        kc, vc = ctx_kv
        lc = kc.shape[0] // batch
        in_specs += [pl.BlockSpec((lc, pair_w), lambda b, j, i: (b, j)),
                     pl.BlockSpec((lc, 2 * V_HEAD), lambda b, j, i: (b, j))]
        args += [kc, vc]
    return pl.pallas_call(
        functools.partial(_attn_kernel, ctx_kv is not None),
        out_shape=jax.ShapeDtypeStruct((t, ATT_WIDTH), BF16),
        grid=(batch, N_HEADS // 2, nq),
        in_specs=in_specs,
        out_specs=pl.BlockSpec((tq, 2 * V_HEAD), lambda b, j, i: (b * nq + i, j)),
        compiler_params=_params(("arbitrary", "arbitrary", "arbitrary")),
        name="attention" if ctx_kv is not None else "ctx_attention",
    )(*args)


def _fourier_kernel(norm, c_ref, s_ref, ab_ref, o_ref):
    a = ab_ref[:, 0:FOURIER_WIDTH]
    b = ab_ref[:, FOURIER_WIDTH:2 * FOURIER_WIDTH]
    o = _dot(c_ref[...], a) - _dot(s_ref[...], b)
    o_ref[...] = (o * norm).astype(BF16)


def _fourier(ab, batch, seq_len, dft):
    t = ab.shape[0]
    tm = min(FOUR_TILE, seq_len)
    nt = seq_len // tm
    cl, sl = dft
    norm = 1.0 / math.sqrt(seq_len * FOURIER_WIDTH)
    return pl.pallas_call(
        functools.partial(_fourier_kernel, norm),
        out_shape=jax.ShapeDtypeStruct((t, FOURIER_WIDTH), BF16),
        grid=(nt, batch),
        in_specs=[pl.BlockSpec((tm, seq_len), lambda i, b: (i, 0)),
                  pl.BlockSpec((tm, seq_len), lambda i, b: (i, 0)),
                  pl.BlockSpec((seq_len, 2 * FOURIER_WIDTH), lambda i, b: (b, 0))],
        out_specs=pl.BlockSpec((tm, FOURIER_WIDTH), lambda i, b: (b * nt + i, 0)),
        compiler_params=_params(("arbitrary", "arbitrary")),
        name="fourier",
    )(cl, sl, ab)


def _pool_kernel(seq_len, z_ref, wg_ref, ps_ref, o_ref, pad_ref, s_ref):
    n = seq_len
    p = n + 2 * POOL_PAD
    z = z_ref[...]
    zeros = jnp.zeros((POOL_PAD, POOL_WIDTH), F32)
    pad_ref[0:POOL_PAD, :] = zeros
    pad_ref[POOL_PAD + n:p, :] = zeros
    pad_ref[POOL_PAD:POOL_PAD + n, :] = z
    s_ref[0:p - 1, :] = pad_ref[0:p - 1, :] + pad_ref[1:p, :]
    w2 = s_ref[POOL_PAD - 1:POOL_PAD - 1 + n, :]
    pad_ref[0:p - 3, :] = s_ref[0:p - 3, :] + s_ref[2:p - 1, :]
    w4 = pad_ref[POOL_PAD - 2:POOL_PAD - 2 + n, :]
    s_ref[0:p - 7, :] = pad_ref[0:p - 7, :] + pad_ref[4:p - 3, :]
    w8 = s_ref[POOL_PAD - 4:POOL_PAD - 4 + n, :]
    pad_ref[0:p - 15, :] = s_ref[0:p - 15, :] + s_ref[8:p - 7, :]
    w16 = pad_ref[0:n, :]
    pos = lax.broadcasted_iota(I32, (n, POOL_WIDTH), 0)
    grp = lax.broadcasted_iota(I32, (n, POOL_WIDTH), 1) // POOL_GROUP
    win = jnp.where(grp == 0, w2, jnp.where(grp == 1, w4, jnp.where(grp == 2, w8, w16)))
    half = jnp.where(grp == 0, 1, jnp.where(grp == 1, 2, jnp.where(grp == 2, 4, 8)))
    lo = jnp.maximum(pos - half, 0)
    hi = jnp.minimum(pos + half, n)
    cnt = (hi - lo).astype(F32)
    pooled = win / cnt - z
    y = _dot(pooled.astype(BF16), wg_ref[...])
    o_ref[...] = (y * ps_ref[...]).astype(BF16)


def _pool(zp, batch, seq_len, wg_bd, pool_scale):
    t = zp.shape[0]
    return pl.pallas_call(
        functools.partial(_pool_kernel, seq_len),
        out_shape=jax.ShapeDtypeStruct((t, POOL_WIDTH), BF16),
        grid=(batch,),
        in_specs=[pl.BlockSpec((seq_len, POOL_WIDTH), lambda b: (b, 0)),
                  pl.BlockSpec((POOL_WIDTH, POOL_WIDTH), lambda b: (0, 0)),
                  pl.BlockSpec((1, POOL_WIDTH), lambda b: (0, 0))],
        out_specs=pl.BlockSpec((seq_len, POOL_WIDTH), lambda b: (b, 0)),
        scratch_shapes=[pltpu.VMEM((seq_len + 2 * POOL_PAD, POOL_WIDTH), F32),
                        pltpu.VMEM((seq_len + 2 * POOL_PAD, POOL_WIDTH), F32)],
        compiler_params=_params(("arbitrary",)),
        name="pool",
    )(zp, wg_bd, pool_scale)


def _out_kernel(d, x_ref, att_ref, four_ref, pool_ref, g_ref, m_ref, woa_ref, wob_ref, woc_ref,
                wout_ref, n2_ref, wpq_ref, x1_ref, h2_ref, pq_ref):
    m = m_ref[0]
    ya = _dot(att_ref[...], woa_ref[...])
    yb = _dot(four_ref[...], wob_ref[...])
    yc = _dot(pool_ref[...], woc_ref[...])
    mixp = (g_ref[:, 0:d].astype(F32) * ya + g_ref[:, d:2 * d].astype(F32) * yb
            + g_ref[:, 2 * d:3 * d].astype(F32) * yc)
    mix = _dot(mixp.astype(BF16), wout_ref[...])
    x1 = x_ref[...] + m[:, 2 * d:3 * d] * mix
    x1_ref[...] = x1
    h2 = _rms(x1) * n2_ref[...] * (1.0 + m[:, 4 * d:5 * d]) + m[:, 3 * d:4 * d]
    h2_ref[...] = h2
    pq = _dot(h2.astype(BF16), wpq_ref[...])
    for hp in range(2 * PEER_HEADS):
        pq_ref[hp] = pq[:, hp * PEER_HALF:(hp + 1) * PEER_HALF].astype(BF16)


def _out_proj(x2d, att, four, pool, g, mod, mod_row, seq_len, lw):
    t, d = x2d.shape
    tm = min(TOK_TILE, seq_len)
    tiles_per_seq = seq_len // tm

    def const(shape):
        return pl.BlockSpec(shape, lambda i: (0,) * len(shape))

    def row(w):
        return pl.BlockSpec((tm, w), lambda i: (i, 0))

    nhp = 2 * PEER_HEADS
    return pl.pallas_call(
        functools.partial(_out_kernel, d),
        out_shape=[jax.ShapeDtypeStruct((t, d), F32), jax.ShapeDtypeStruct((t, d), F32),
                   jax.ShapeDtypeStruct((nhp, t, PEER_HALF), BF16)],
        grid=(t // tm,),
        in_specs=[row(d), row(ATT_WIDTH), row(FOURIER_WIDTH), row(POOL_WIDTH), row(N_BRANCH * d),
                  pl.BlockSpec((1, 1, mod.shape[-1]), lambda i: (mod_row(i, tiles_per_seq), 0, 0)),
                  const(lw["woa"].shape), const(lw["wob"].shape), const(lw["woc"].shape),
                  const(lw["wout"].shape), const((1, d)), const(lw["wpq"].shape)],
        out_specs=[row(d), row(d), pl.BlockSpec((nhp, tm, PEER_HALF), lambda i: (0, i, 0))],
        compiler_params=_params(("arbitrary",)),
        name="out_proj",
    )(x2d, att, four, pool, g, mod, lw["woa"], lw["wob"], lw["woc"], lw["wout"], lw["n2"],
      lw["wpq"])


def _select_round(s, iota, n):
    mx = jnp.max(s, axis=0, keepdims=True)
    idx = jnp.min(jnp.where(s == mx, iota, n), axis=0, keepdims=True)
    hit = iota == idx
    return mx, idx, hit, jnp.where(hit, -jnp.inf, s)


def _route_kernel(pq_ref, keys_ref, eidx_ref, gate_ref, sv_a, si_a, sv_b, si_b, cand_ref, cidx_ref,
                  ts_ref, eidx_s, gate_s):
    tm = pq_ref.shape[1]
    iota_k = lax.broadcasted_iota(I32, (N_KEYS, tm), 0).astype(F32)
    iota_c = lax.broadcasted_iota(I32, (CAND_ROWS, tm), 0).astype(F32)
    cand_ref[N_CAND:CAND_ROWS, :] = jnp.full((CAND_ROWS - N_CAND, tm), -jnp.inf, F32)
    cidx_ref[N_CAND:CAND_ROWS, :] = jnp.zeros((CAND_ROWS - N_CAND, tm), F32)

    def sub_key_topk(hd, sv_ref, si_ref):
        s = [_dot_nt(keys_ref[2 * hd + p], pq_ref[2 * hd + p]) for p in range(2)]
        for r in range(PEER_TOPK):
            for p in range(2):
                mx, idx, _, s[p] = _select_round(s[p], iota_k, float(N_KEYS))
                sv_ref[p, r:r + 1, :] = mx
                si_ref[p, r:r + 1, :] = idx

    def pair_topk(hd, sv_ref, si_ref):
        off = 0
        for i, cnt in STAIR:
            cand_ref[off:off + cnt, :] = sv_ref[0, i:i + 1, :] + sv_ref[1, 0:cnt, :]
            cidx_ref[off:off + cnt, :] = si_ref[0, i:i + 1, :] * N_KEYS + si_ref[1, 0:cnt, :]
            off += cnt
        cidx = cidx_ref[...]
        c = cand_ref[...]
        base = pl.multiple_of(hd * PEER_TOPK, PEER_TOPK)
        for r in range(PEER_TOPK):
            mx, _, hit, c = _select_round(c, iota_c, float(CAND_ROWS))
            ts_ref[r:r + 1, :] = mx
            expert = jnp.sum(jnp.where(hit, cidx, 0.0), axis=0, keepdims=True)
            eidx_s[pl.ds(base + r, 1), :] = expert.astype(I32)
        ts = ts_ref[...]
        ex = jnp.exp(ts - ts[0:1, :])
        gate_s[pl.ds(base, PEER_TOPK), :] = ex / jnp.sum(ex, axis=0, keepdims=True)

    sub_key_topk(0, sv_a, si_a)

    def two_heads(j, carry):
        hd = 2 * j
        sub_key_topk(hd + 1, sv_b, si_b)
        pair_topk(hd, sv_a, si_a)
        sub_key_topk(hd + 2, sv_a, si_a)
        pair_topk(hd + 1, sv_b, si_b)
        return carry

    lax.fori_loop(0, PEER_HEADS // 2 - 1, two_heads, 0)
    sub_key_topk(PEER_HEADS - 1, sv_b, si_b)
    pair_topk(PEER_HEADS - 2, sv_a, si_a)
    pair_topk(PEER_HEADS - 1, sv_b, si_b)
    eidx_ref[...] = eidx_s[...].T
    gate_ref[...] = gate_s[...].T


def _route(pq, keys):
    nhp, t, _ = pq.shape
    tm = ROUTE_TILE
    lists = pltpu.VMEM((2, PEER_TOPK, tm), F32)
    return pl.pallas_call(
        _route_kernel,
        out_shape=[jax.ShapeDtypeStruct((t, PEER_SLOTS), I32),
                   jax.ShapeDtypeStruct((t, PEER_SLOTS), F32)],
        grid=(t // tm,),
        in_specs=[pl.BlockSpec((nhp, tm, PEER_HALF), lambda i: (0, i, 0)),
                  pl.BlockSpec((nhp, N_KEYS, PEER_HALF), lambda i: (0, 0, 0))],
        out_specs=[pl.BlockSpec((tm, PEER_SLOTS), lambda i: (i, 0)),
                   pl.BlockSpec((tm, PEER_SLOTS), lambda i: (i, 0))],
        scratch_shapes=[lists, lists, lists, lists,
                        pltpu.VMEM((CAND_ROWS, tm), F32),
                        pltpu.VMEM((CAND_ROWS, tm), F32),
                        pltpu.VMEM((PEER_TOPK, tm), F32),
                        pltpu.VMEM((PEER_SLOTS, tm), I32), pltpu.VMEM((PEER_SLOTS, tm), F32)],
        compiler_params=_params(("arbitrary",)),
        name="peer_route",
    )(pq, keys)


def _fold_rows(a, b, keep_a, shift):
    return jnp.where(keep_a, a + pltpu.roll(a, SUBLANES - shift, 0), b + pltpu.roll(b, shift, 0))


def _peer_kernel(final, idx_ref, idxn_ref, gate_ref, h_ref, x_ref, g2_ref, fg_ref, tab_ref,
                 o_ref, buf_a, buf_b, act_ref, sem):
    i = pl.program_id(0)
    n = pl.num_programs(0)
    grp_tok = PEER_TOK // PEER_WAIT_GROUPS
    grp_rows = grp_tok * PEER_SLOTS
    sub = lax.broadcasted_iota(I32, (SUBLANES, LANES), 0)
    keep = {sh: (sub & sh) == 0 for sh in (4, 2, 1)}
    g2 = g2_ref[0, 0]
    gate_t = gate_ref[...].T

    def row_copy(ids, t_src, t_dst, k, buf, s):
        return pltpu.make_async_copy(tab_ref.at[ids[t_src, k]], buf.at[t_dst * PEER_SLOTS + k],
                                     sem.at[s, t_dst // grp_tok])

    def issue_token(ids, t_src, t_dst, buf, s):
        for k in range(PEER_SLOTS):
            row_copy(ids, t_src, t_dst, k, buf, s).start(priority=k % 2)

    def wait_group(buf, s, q):
        pltpu.make_async_copy(tab_ref.at[pl.ds(0, grp_rows)],
                              buf.at[pl.ds(q * grp_rows, grp_rows)], sem.at[s, q]).wait()

    def token(buf, tl, tb):
        r0 = tl * PEER_SLOTS
        hv = h_ref[tb]
        groups = []
        for g in range(PEER_SLOTS // SUBLANES):
            p = [buf[r0 + g * SUBLANES + j].astype(F32)[0:SUBLANES] * hv for j in range(SUBLANES)]
            for sh in (4, 2, 1):
                half = len(p) // 2
                p = [_fold_rows(p[j], p[j + half], keep[sh], sh) for j in range(half)]
            groups.append(p[0])
        part = jnp.concatenate(groups, axis=0)
        a = jnp.sum(part, axis=1, keepdims=True)
        act = gate_t[:, tb:tb + 1] * jax.nn.gelu(a)
        act_ref[...] = jnp.broadcast_to(act, (PEER_SLOTS, LANES))
        accs = [jnp.zeros((SUBLANES, LANES), F32) for _ in range(4)]
        for k in range(PEER_SLOTS):
            up = buf[r0 + k].astype(F32)[SUBLANES:2 * SUBLANES]
            accs[k % 4] = accs[k % 4] + act_ref[k:k + 1, :] * up
        xr = x_ref[tb] + g2 * ((accs[0] + accs[1]) + (accs[2] + accs[3]))
        if final:
            ms = jnp.sum(jnp.sum(xr * xr, axis=1, keepdims=True), axis=0, keepdims=True)
            xr = xr * lax.rsqrt(ms / (SUBLANES * LANES) + EPS) * fg_ref[...]
        o_ref[tb] = xr

    @pl.when(i == 0)
    def _():
        def body(t, carry):
            issue_token(idx_ref, t, t, buf_a, 0)
            return carry
        lax.fori_loop(0, PEER_TOK, body, 0)

    for t in range(PEER_TOK):
        if t % grp_tok == 0:
            wait_group(buf_a, 0, t // grp_tok)
        issue_token(idx_ref, PEER_TOK + t, t, buf_b, 1)
        token(buf_a, t, t)
    for t in range(PEER_TOK):
        if t % grp_tok == 0:
            wait_group(buf_b, 1, t // grp_tok)
        issue_token(idxn_ref, t, t, buf_a, 0)
        token(buf_b, t, PEER_TOK + t)

    @pl.when(i == n - 1)
    def _():
        for q in range(PEER_WAIT_GROUPS):
            wait_group(buf_a, 0, q)


def _peer(n_tok, eidx, gate, h3, x3, g2, mod_row, seq_len, fg, table, final):
    t = n_tok
    tb = 2 * PEER_TOK
    nb = t // tb
    steps_per_seq = seq_len // tb
    rows = PEER_TOK * PEER_SLOTS
    tile = (SUBLANES, LANES)
    pair = (2 * SUBLANES, LANES)
    return pl.pallas_call(
        functools.partial(_peer_kernel, final),
        out_shape=jax.ShapeDtypeStruct((t,) + tile, F32),
        grid=(nb,),
        in_specs=[pl.BlockSpec((tb, PEER_SLOTS), lambda i: (i, 0), memory_space=pltpu.SMEM),
                  pl.BlockSpec((tb, PEER_SLOTS), lambda i: (jnp.minimum(i + 1, nb - 1), 0),
                               memory_space=pltpu.SMEM),
                  pl.BlockSpec((tb, PEER_SLOTS), lambda i: (i, 0)),
                  pl.BlockSpec((tb,) + tile, lambda i: (i, 0, 0)),
                  pl.BlockSpec((tb,) + tile, lambda i: (i, 0, 0)),
                  pl.BlockSpec((1, 1) + tile, lambda i: (mod_row(i, steps_per_seq), 0, 0, 0)),
                  pl.BlockSpec(tile, lambda i: (0, 0)),
                  pl.BlockSpec(memory_space=pl.ANY)],
        out_specs=pl.BlockSpec((tb,) + tile, lambda i: (i, 0, 0)),
        scratch_shapes=[pltpu.VMEM((rows,) + pair, BF16), pltpu.VMEM((rows,) + pair, BF16),
                        pltpu.VMEM((PEER_SLOTS, LANES), F32),
                        pltpu.SemaphoreType.DMA((2, PEER_WAIT_GROUPS))],
        compiler_params=pltpu.CompilerParams(dimension_semantics=("arbitrary",),
                                             vmem_limit_bytes=PEER_VMEM_LIMIT),
        name="peer_experts",
    )(eidx, eidx, gate, h3, x3, g2, fg, table)


def _peer_sc_body(first, n_tok, d, eidx_hbm, gate_hbm, h_hbm, tab_hbm, y_hbm, idx_v, idx_n, gate_v, gate_n,
                  h_v, h_n, rows_a, rows_b, out_v, sem_a, sem_b, sem_n):
    wid = lax.axis_index("s") * SC_CORES + lax.axis_index("c")
    per_worker = n_tok // SC_WORKERS
    tok0 = first + wid * per_worker
    nj = d // SC_LANES
    lane = lax.iota(I32, SC_LANES)
    zero = jnp.zeros((SC_LANES,), F32)
    hi_mask = jnp.full((SC_LANES,), 0xFFFF0000, jnp.uint32)

    def gather(idx_ref, c, rows, sem):
        row0 = pl.multiple_of(c * SC_CHUNK, SC_CHUNK)
        return pltpu.make_async_copy(tab_hbm.at[idx_ref.at[pl.ds(row0, SC_CHUNK)]], rows, sem)

    def copy_words(src, dst, n):
        def step(j, carry):
            off = pl.multiple_of(j * SC_LANES, SC_LANES)
            dst[pl.ds(off, SC_LANES)] = src[pl.ds(off, SC_LANES)]
            return carry
        lax.fori_loop(0, n // SC_LANES, step, 0)

    def compute(c, rows):
        row0 = pl.multiple_of(c * SC_CHUNK, SC_CHUNK)

        def down_step(j, accs):
            off = pl.multiple_of(j * SC_LANES, SC_LANES)
            hj = h_v[pl.ds(off, SC_LANES)]
            out = []
            for r in range(SC_CHUNK):
                w = rows[r, pl.ds(off, SC_LANES)]
                dn = lax.bitcast_convert_type(w & hi_mask, F32)
                out.append(accs[r] + dn * hj)
            return tuple(out)

        accs = lax.fori_loop(0, nj, down_step, tuple(zero for _ in range(SC_CHUNK)))
        acts = []
        for g in range(SC_CHUNK // SC_LANES):
            a = zero
            for r in range(SC_LANES):
                a = jnp.where(lane == r, jnp.sum(accs[g * SC_LANES + r]), a)
            gt = gate_v[pl.ds(row0 + g * SC_LANES, SC_LANES)]
            u = GELU_C * (a + 0.044715 * (a * a * a))
            th = 1.0 - 2.0 / (jnp.exp(2.0 * u) + 1.0)
            act = gt * (0.5 * a * (1.0 + th))
            for r in range(SC_LANES):
                acts.append(jnp.sum(jnp.where(lane == r, act, 0.0)))

        def up_step(j, carry):
            off = pl.multiple_of(j * SC_LANES, SC_LANES)
            o = out_v[pl.ds(off, SC_LANES)]
            for r in range(SC_CHUNK):
                w = rows[r, pl.ds(off, SC_LANES)]
                up = lax.bitcast_convert_type(w << 16, F32)
                o = o + acts[r] * up
            out_v[pl.ds(off, SC_LANES)] = o
            return carry

        lax.fori_loop(0, nj, up_step, 0)

    def token(ti, carry):
        t = tok0 + ti
        tn = jnp.minimum(t + 1, tok0 + per_worker - 1)

        def clear(j, carry2):
            out_v[pl.ds(pl.multiple_of(j * SC_LANES, SC_LANES), SC_LANES)] = zero
            return carry2

        lax.fori_loop(0, nj, clear, 0)

        def pair(pp, carry2):
            gather(idx_v, 2 * pp + 1, rows_b, sem_b).start()
            gather(idx_v, 2 * pp, rows_a, sem_a).wait()
            compute(2 * pp, rows_a)

            @pl.when(pp == 0)
            def _():
                gather(idx_v, 2, rows_a, sem_a).start()

            @pl.when(pp == 1)
            def _():
                pltpu.sync_copy(eidx_hbm.at[tn], idx_n)
                gather(idx_n, 0, rows_a, sem_a).start()
                pltpu.make_async_copy(gate_hbm.at[tn], gate_n, sem_n).start()
                pltpu.make_async_copy(h_hbm.at[tn], h_n, sem_n).start()

            gather(idx_v, 2 * pp + 1, rows_b, sem_b).wait()
            compute(2 * pp + 1, rows_b)
            return carry2

        lax.fori_loop(0, PEER_SLOTS // (2 * SC_CHUNK), pair, 0)
        pltpu.sync_copy(out_v, y_hbm.at[t - first])
        pltpu.make_async_copy(gate_hbm.at[tn], gate_n, sem_n).wait()
        pltpu.make_async_copy(h_hbm.at[tn], h_n, sem_n).wait()
        copy_words(idx_n, idx_v, PEER_SLOTS)
        copy_words(gate_n, gate_v, PEER_SLOTS)
        copy_words(h_n, h_v, d)
        return carry

    pltpu.sync_copy(eidx_hbm.at[tok0], idx_v)
    pltpu.sync_copy(gate_hbm.at[tok0], gate_v)
    pltpu.sync_copy(h_hbm.at[tok0], h_v)
    gather(idx_v, 0, rows_a, sem_a).start()
    lax.fori_loop(0, per_worker, token, 0)
    gather(idx_v, 0, rows_a, sem_a).wait()


def _peer_sc(first, n_tok, eidx, gate, h2, sc_table):
    d = h2.shape[1]
    mesh = plsc.VectorSubcoreMesh(core_axis_name="c", subcore_axis_name="s",
                                  num_cores=SC_CORES, num_subcores=SC_SUBCORES)
    return pl.kernel(
        functools.partial(_peer_sc_body, first, n_tok, d),
        out_type=jax.ShapeDtypeStruct((n_tok, d), F32),
        mesh=mesh,
        scratch_types=[pltpu.VMEM((PEER_SLOTS,), I32), pltpu.VMEM((PEER_SLOTS,), I32),
                       pltpu.VMEM((PEER_SLOTS,), F32), pltpu.VMEM((PEER_SLOTS,), F32),
                       pltpu.VMEM((d,), F32), pltpu.VMEM((d,), F32),
                       pltpu.VMEM((SC_CHUNK, d), jnp.uint32), pltpu.VMEM((SC_CHUNK, d), jnp.uint32),
                       pltpu.VMEM((d,), F32), pltpu.SemaphoreType.DMA, pltpu.SemaphoreType.DMA,
                       pltpu.SemaphoreType.DMA],
        compiler_params=pltpu.CompilerParams(needs_layout_passes=False),
        cost_estimate=pl.CostEstimate(
            flops=4 * n_tok * PEER_SLOTS * d, transcendentals=n_tok * PEER_SLOTS,
            bytes_accessed=n_tok * (PEER_SLOTS * d * 4 + 2 * d * 4 + 2 * PEER_SLOTS * 4)),
        name="peer_experts_sc",
    )(eidx, gate, h2, sc_table)


def _finish_kernel(final, d, y_ref, x_ref, m_ref, fg_ref, o_ref):
    xr = x_ref[...] + m_ref[0][:, 5 * d:6 * d] * y_ref[...]
    if final:
        xr = _rms(xr) * fg_ref[...]
    o_ref[...] = xr


def _peer_finish(y, x1, mod, mod_row, seq_len, tok0, fg, final):
    t, d = y.shape
    tm = min(TOK_TILE, seq_len)
    tiles_per_seq = seq_len // tm
    tile0 = tok0 // tm
    return pl.pallas_call(
        functools.partial(_finish_kernel, final, d),
        out_shape=jax.ShapeDtypeStruct((t, d), F32),
        grid=(t // tm,),
        in_specs=[pl.BlockSpec((tm, d), lambda i: (i, 0)),
                  pl.BlockSpec((tm, d), lambda i: (i + tile0, 0)),
                  pl.BlockSpec((1, 1, mod.shape[-1]),
                               lambda i: (mod_row(i + tile0, tiles_per_seq), 0, 0)),
                  pl.BlockSpec((1, d), lambda i: (0, 0))],
        out_specs=pl.BlockSpec((tm, d), lambda i: (i, 0)),
        compiler_params=_params(("arbitrary",)),
        name="peer_finish",
    )(y, x1, mod, fg.reshape(1, d))


def _rope_perm(w):
    q = QK_ROPE // 4
    a1, a2, b1, b2 = (w[..., j * q:(j + 1) * q] for j in range(4))
    return jnp.concatenate([-a2, a1, -b2, b1], axis=-1)


def _prep_layer(l, d, w_in, b_gate, q_norm_g, w_uq, kv_norm_g, w_ukv, w_oa, w_ob, w_grp,
                pool_scale, w_oc, w_out, w_pq, norm1_g, norm2_g):
    wi = w_in[l]
    s0 = Q_LORA
    s1 = s0 + KV_LORA
    s2 = s1 + QK_ROPE
    s3 = s2 + FOURIER_WIDTH
    s4 = s3 + POOL_WIDTH
    w_kr = wi[:, s1:s2]
    zl = jnp.zeros((d, QK_NOPE), F32)
    zr = jnp.zeros((d, HEAD_PAD - QK_NOPE - QK_ROPE), F32)
    wall = jnp.concatenate([wi[:, 0:s1], wi[:, s2:], zl, w_kr, zr, zl, _rope_perm(w_kr), zr],
                           axis=1).astype(BF16)
    wq = w_uq[l].reshape(Q_LORA, N_HEADS, QK_NOPE + QK_ROPE)
    qpad = jnp.zeros((Q_LORA, N_HEADS, HEAD_PAD - QK_NOPE - QK_ROPE), F32)
    wq_full = jnp.concatenate([wq, qpad], axis=-1).reshape(Q_LORA, N_HEADS * HEAD_PAD)
    wq_perm = jnp.concatenate([jnp.zeros((Q_LORA, N_HEADS, QK_NOPE), F32),
                               _rope_perm(wq[..., QK_NOPE:]), qpad],
                              axis=-1).reshape(Q_LORA, N_HEADS * HEAD_PAD)
    wkv = w_ukv[l].reshape(KV_LORA, N_HEADS, QK_NOPE + V_HEAD)
    wk = jnp.concatenate([wkv[..., :QK_NOPE],
                          jnp.zeros((KV_LORA, N_HEADS, HEAD_PAD - QK_NOPE), F32)],
                         axis=-1).reshape(KV_LORA, N_HEADS * HEAD_PAD)
    wv = wkv[..., QK_NOPE:].reshape(KV_LORA, ATT_WIDTH)
    cidx = jnp.arange(FOURIER_WIDTH, dtype=I32)
    ang = (2.0 * math.pi / FOURIER_WIDTH) * ((cidx[:, None] * cidx[None, :]) % FOURIER_WIDTH
                                             ).astype(F32)
    fc = jnp.concatenate([jnp.cos(ang), jnp.sin(ang)], axis=1).astype(BF16)
    ng = len(POOL_WINDOWS)
    wg = jnp.zeros((ng, POOL_GROUP, ng, POOL_GROUP), F32)
    for gi in range(ng):
        wg = wg.at[gi, :, gi, :].set(w_grp[l, gi])
    return {
        "wall": wall, "n1": norm1_g[l][None, :], "n2": norm2_g[l][None, :],
        "qg": q_norm_g[l][None, :], "kvg": kv_norm_g[l][None, :],
        "wq": wq_full.astype(BF16), "wqp": wq_perm.astype(BF16),
        "wk": wk.astype(BF16), "wv": wv.astype(BF16), "fc": fc,
        "bg": b_gate[l][None, :],
        "wg": wg.reshape(POOL_WIDTH, POOL_WIDTH).astype(BF16),
        "ps": pool_scale[l][None, :],
        "woa": w_oa[l].astype(BF16), "wob": w_ob[l].astype(BF16), "woc": w_oc[l].astype(BF16),
        "wout": w_out[l].astype(BF16), "wpq": w_pq[l].astype(BF16),
    }


def _bf16_bits(w):
    return lax.bitcast_convert_type(w.astype(BF16), jnp.uint16).astype(jnp.uint32)


def _rope_tables(seq_len, rope):
    zeros_n = jnp.zeros((seq_len, QK_NOPE), F32)
    zeros_p = jnp.zeros((seq_len, HEAD_PAD - QK_NOPE - QK_ROPE), F32)
    ones_n = jnp.ones((seq_len, QK_NOPE), F32)
    if rope:
        pos = jnp.arange(seq_len, dtype=I32)
        half = QK_ROPE // 2
        inv_freq = ROPE_BASE ** (-jnp.arange(0, half, 2, dtype=F32) / half)
        ang_r = (pos // GRID_W).astype(F32)[:, None] * inv_freq
        ang_c = (pos % GRID_W).astype(F32)[:, None] * inv_freq
        cos = jnp.concatenate([jnp.cos(ang_r)] * 2 + [jnp.cos(ang_c)] * 2, axis=1)
        sin = jnp.concatenate([jnp.sin(ang_r)] * 2 + [jnp.sin(ang_c)] * 2, axis=1)
    else:
        cos = jnp.ones((seq_len, QK_ROPE), F32)
        sin = jnp.zeros((seq_len, QK_ROPE), F32)
    cq = jnp.concatenate([ones_n, cos, zeros_p], axis=1) * ATT_SCALE
    sq = jnp.concatenate([zeros_n, sin, zeros_p], axis=1) * ATT_SCALE
    ck = jnp.concatenate([zeros_n, cos, zeros_p], axis=1)
    sk = jnp.concatenate([zeros_n, sin, zeros_p], axis=1)
    return cq, sq, ck, sk


def _dft(seq_len):
    idx = jnp.arange(seq_len, dtype=I32)
    ang = (2.0 * math.pi / seq_len) * ((idx[:, None] * idx[None, :]) % seq_len).astype(F32)
    return jnp.cos(ang).astype(BF16), jnp.sin(ang).astype(BF16)


def _peer_block(x1, h2, pq, keys, mod, mod_row, seq_len, fg, table, sc_table, n_sc, final):
    t, d = x1.shape
    tile = (SUBLANES, LANES)
    eidx, gate = _route(pq, keys)
    n_tc = t - n_sc
    mod_tiles = mod.reshape(MOD_ROWS, 6, SUBLANES, LANES)[:, 5:6]
    out = _peer(n_tc, eidx, gate, h2.reshape((t,) + tile), x1.reshape((t,) + tile), mod_tiles,
                mod_row, seq_len, fg.reshape(tile), table, final).reshape(n_tc, d)
    if n_sc == 0:
        return out, ()
    y = _peer_sc(n_tc, n_sc, eidx, gate, h2, sc_table)
    out_sc = _peer_finish(y, x1, mod, mod_row, seq_len, n_tc, fg, final)
    return jnp.concatenate([out, out_sc], axis=0), (out, eidx, gate, h2)


def kernel(x, c, ctx, c_ctx, w_mod, b_mod, norm1_g, norm2_g, w_in, b_gate, q_norm_g, w_uq,
           kv_norm_g, w_ukv, w_oa, w_ob, w_grp, pool_scale, w_oc, w_out, w_pq, peer_keys,
           peer_down, peer_up, final_g):
    batch, seq_len, d = x.shape
    ctx_len = ctx.shape[1]
    depth = w_mod.shape[0]
    assert d == SUBLANES * LANES and batch + 1 <= MOD_ROWS
    assert seq_len % TOK_TILE == 0 and ctx_len % (2 * PEER_TOK) == 0 and ctx_len % LANES == 0
    tile = (SUBLANES, LANES)

    cvec = jnp.concatenate([c, c_ctx[None, :], jnp.zeros((MOD_ROWS - batch - 1, d), F32)], axis=0)
    mod_all = _modulation(cvec, w_mod, b_mod)

    def x_row(b0):
        return lambda i, per_seq: b0 + i // per_seq

    def c_row(i, per_seq):
        return batch

    tabs_x = _rope_tables(seq_len, True)
    tabs_c = _rope_tables(ctx_len, False)
    dft_x = _dft(seq_len)
    dft_c = _dft(ctx_len)

    n_chains = PEER_CHAINS if batch % PEER_CHAINS == 0 else 1
    bpc = batch // n_chains
    use_sc = bpc * seq_len >= (max(PEER_SC_TOKENS, PEER_SC_TOKENS_FIRST, PEER_SC_TOKENS_MIDDLE)
                               + TOK_TILE)

    def sc_tokens(l, j):
        if not use_sc:
            return 0
        if l == depth - 1 and j == n_chains - 1:
            return PEER_SC_TOKENS
        return PEER_SC_TOKENS_FIRST if (l, j) == (0, 0) else PEER_SC_TOKENS_MIDDLE

    tables = [jnp.concatenate([peer_down[l].reshape((-1,) + tile),
                               peer_up[l].reshape((-1,) + tile)], axis=1).astype(BF16)
              for l in range(depth)]
    sc_tables = [(_bf16_bits(peer_down[l]) << 16) | _bf16_bits(peer_up[l]) for l in range(depth)]
    x, ctx, tables, sc_tables = lax.optimization_barrier((x, ctx, tables, sc_tables))

    chains = [x[j * bpc:(j + 1) * bpc].reshape(bpc * seq_len, d) for j in range(n_chains)]
    cs = ctx.reshape(batch * ctx_len, d)
    sc_args = ()
    for l in range(depth):
        last = l == depth - 1
        lw = _prep_layer(l, d, w_in, b_gate, q_norm_g, w_uq, kv_norm_g, w_ukv, w_oa, w_ob, w_grp,
                         pool_scale, w_oc, w_out, w_pq, norm1_g, norm2_g)
        mod = mod_all[l].reshape(MOD_ROWS, 1, 6 * d)
        keys = peer_keys[l].reshape(2 * PEER_HEADS, N_KEYS, PEER_HALF).astype(BF16)
        table = tables[l]
        sc_table = sc_tables[l]

        if last:
            kc, vc = _in_proj(cs, mod, c_row, ctx_len, lw, tabs_c, False)
        else:
            kc, vc, qc, abc, zpc, gc = _in_proj(cs, mod, c_row, ctx_len, lw, tabs_c, True)

        for j in range(n_chains):
            row = x_row(j * bpc)
            sc_args, xs = lax.optimization_barrier((sc_args, chains[j]))
            ctx_rows = slice(j * bpc * ctx_len, (j + 1) * bpc * ctx_len)
            kx, vx, qx, abx, zpx, gx = _in_proj(xs, mod, row, seq_len, lw, tabs_x, True)
            att_x = _attention(qx, kx, vx, bpc, seq_len, (kc[ctx_rows], vc[ctx_rows]))
            four_x = _fourier(abx, bpc, seq_len, dft_x)
            pool_x = _pool(zpx, bpc, seq_len, lw["wg"], lw["ps"])
            x1, h2, pq = _out_proj(xs, att_x, four_x, pool_x, gx, mod, row, seq_len, lw)
            chains[j], sc_args = _peer_block(x1, h2, pq, keys, mod, row, seq_len, final_g, table,
                                             sc_table, sc_tokens(l, j), last)

        if not last:
            att_c = _attention(qc, kc, vc, batch, ctx_len, None)
            four_c = _fourier(abc, batch, ctx_len, dft_c)
            pool_c = _pool(zpc, batch, ctx_len, lw["wg"], lw["ps"])
            c1, hc2, pqc = _out_proj(cs, att_c, four_c, pool_c, gc, mod, c_row, ctx_len, lw)
            cs, _ = _peer_block(c1, hc2, pqc, keys, mod, c_row, ctx_len, final_g, table, None, 0,
                                False)
    xs = jnp.concatenate(chains, axis=0)
    return xs.reshape(batch, seq_len, d)
```

```python
import functools
import math

import jax
import jax.numpy as jnp
from jax import lax
from jax.experimental import pallas as pl
from jax.experimental.pallas import tpu as pltpu
from jax.experimental.pallas import tpu_sc as plsc

F32 = jnp.float32
BF16 = jnp.bfloat16
I32 = jnp.int32

GRID_W = 64
N_HEADS = 8
Q_LORA = 256
KV_LORA = 128
QK_NOPE = 64
QK_ROPE = 32
V_HEAD = 64
ATT_WIDTH = N_HEADS * V_HEAD
ATT_SCALE = (QK_NOPE + QK_ROPE) ** -0.5
ROPE_BASE = 10000.0
FOURIER_WIDTH = 256
POOL_WINDOWS = (2, 4, 8, 16)
POOL_GROUP = 64
POOL_WIDTH = POOL_GROUP * len(POOL_WINDOWS)
N_BRANCH = 3
PEER_HEADS = 8
N_KEYS = 128
PEER_QDIM = 256
PEER_HALF = PEER_QDIM // 2
PEER_TOPK = 16
PEER_SLOTS = PEER_HEADS * PEER_TOPK
STAIR = tuple((i, PEER_TOPK // (i + 1)) for i in range(PEER_TOPK))
N_CAND = sum(cnt for _, cnt in STAIR)
CAND_ROWS = -(-N_CAND // 8) * 8
EPS = 1e-6

LANES = 128
SUBLANES = 8
HEAD_PAD = 128
POOL_PAD = 8
MOD_ROWS = 16
VMEM_LIMIT = 48 * 1024 * 1024

TOK_TILE = 256
ATT_Q_TILE = 256
FOUR_TILE = 512
DFT_SPLIT = 64
ROUTE_TILE = 128
PEER_TOK = 16
PEER_WAIT_GROUPS = 4
PEER_VMEM_LIMIT = 56 * 1024 * 1024

SC_CORES = 2
SC_SUBCORES = 16
SC_WORKERS = SC_CORES * SC_SUBCORES
SC_LANES = 16
SC_CHUNK = 32
PEER_CHAINS = 2
PEER_SC_TOKENS = 5888
PEER_SC_TOKENS_FIRST = 8960
PEER_SC_TOKENS_MIDDLE = 8960
GELU_C = math.sqrt(2.0 / math.pi)

C_CQ = 0
C_CKV = C_CQ + Q_LORA
C_ZF = C_CKV + KV_LORA
C_ZP = C_ZF + FOURIER_WIDTH
C_ZG = C_ZP + POOL_WIDTH
C_KR = None


def _params(sem=None):
    return pltpu.CompilerParams(dimension_semantics=sem, vmem_limit_bytes=VMEM_LIMIT)


def _rms(x):
    return x * lax.rsqrt(jnp.mean(x * x, axis=-1, keepdims=True) + EPS)


def _dot(a, b):
    return jnp.dot(a, b, preferred_element_type=F32)


def _dot_nt(a, b):
    return lax.dot_general(a, b, (((1,), (1,)), ((), ())), preferred_element_type=F32)


def _mod_kernel(c_ref, w_ref, b_ref, o_ref):
    c = c_ref[...]
    s = c * jax.nn.sigmoid(c)
    o_ref[0] = jnp.dot(s, w_ref[0], preferred_element_type=F32,
                       precision=lax.Precision.HIGHEST) + b_ref[0]


def _modulation(cvec, w_mod, b_mod):
    depth, d, n = w_mod.shape
    tn = 1536
    return pl.pallas_call(
        _mod_kernel,
        out_shape=jax.ShapeDtypeStruct((depth, MOD_ROWS, n), F32),
        grid=(depth, n // tn),
        in_specs=[pl.BlockSpec((MOD_ROWS, d), lambda l, j: (0, 0)),
                  pl.BlockSpec((1, d, tn), lambda l, j: (l, 0, j)),
                  pl.BlockSpec((1, 1, tn), lambda l, j: (l, 0, j))],
        out_specs=pl.BlockSpec((1, MOD_ROWS, tn), lambda l, j: (l, 0, j)),
        compiler_params=_params(("arbitrary", "arbitrary")),
        name="modulation",
    )(cvec, w_mod, b_mod.reshape(depth, 1, n))


def _in_kernel(d, full, x_ref, m_ref, n1_ref, wall_ref, qg_ref, wq_ref, wqp_ref, kvg_ref,
               wk_ref, wv_ref, fc_ref, bg_ref, cq_ref, sq_ref, ck_ref, sk_ref, *outs):
    x = x_ref[...]
    m = m_ref[0]
    h = _rms(x) * n1_ref[...] * (1.0 + m[:, d:2 * d]) + m[:, 0:d]
    hb = h.astype(BF16)
    c_kr = C_ZG + N_BRANCH * d
    if full:
        z = _dot(hb, wall_ref[...])
        k_ref, v_ref, q_ref, ab_ref, zp_ref, g_ref = outs
    else:
        z = None
        k_ref, v_ref = outs
    def col(lo, hi):
        if full:
            return z[:, lo:hi]
        return _dot(hb, wall_ref[:, lo:hi])

    ckv = col(C_CKV, C_ZF)
    ckvn = (_rms(ckv) * kvg_ref[...]).astype(BF16)
    kf = _dot(ckvn, wk_ref[...])
    kr = col(c_kr, c_kr + LANES) * ck_ref[...] + col(c_kr + LANES, c_kr + 2 * LANES) * sk_ref[...]
    for hh in range(N_HEADS):
        sl = slice(hh * HEAD_PAD, (hh + 1) * HEAD_PAD)
        k_ref[:, sl] = (kf[:, sl] + kr).astype(BF16)
    v_ref[...] = _dot(ckvn, wv_ref[...]).astype(BF16)
    if not full:
        return
    cq = z[:, C_CQ:C_CKV]
    cqn = (_rms(cq) * qg_ref[...]).astype(BF16)
    qf = _dot(cqn, wq_ref[...])
    qr = _dot(cqn, wqp_ref[...])
    cosq = cq_ref[...]
    sinq = sq_ref[...]
    for hh in range(N_HEADS):
        sl = slice(hh * HEAD_PAD, (hh + 1) * HEAD_PAD)
        q_ref[:, sl] = (qf[:, sl] * cosq + qr[:, sl] * sinq).astype(BF16)
    ab_ref[...] = _dot(z[:, C_ZF:C_ZP].astype(BF16), fc_ref[...]).astype(BF16)
    zp_ref[...] = z[:, C_ZP:C_ZG]
    g_ref[...] = jax.nn.sigmoid(z[:, C_ZG:c_kr] + bg_ref[...]).astype(BF16)


def _in_proj(x2d, mod, mod_row, seq_len, lw, tabs, full):
    t, d = x2d.shape
    tm = min(TOK_TILE, seq_len)
    tiles_per_seq = seq_len // tm
    wall = lw["wall"]
    nw = wall.shape[1]
    cq, sq, ck, sk = tabs

    def const(shape):
        return pl.BlockSpec(shape, lambda i: (0,) * len(shape))

    def pos(i):
        return (i % tiles_per_seq, 0)

    in_specs = [
        pl.BlockSpec((tm, d), lambda i: (i, 0)),
        pl.BlockSpec((1, 1, mod.shape[-1]), lambda i: (mod_row(i, tiles_per_seq), 0, 0)),
        const((1, d)), const((d, nw)), const((1, Q_LORA)),
        const(lw["wq"].shape), const(lw["wqp"].shape), const((1, KV_LORA)),
        const(lw["wk"].shape), const(lw["wv"].shape), const(lw["fc"].shape),
        const((1, N_BRANCH * d)),
        pl.BlockSpec((tm, LANES), pos), pl.BlockSpec((tm, LANES), pos),
        pl.BlockSpec((tm, LANES), pos), pl.BlockSpec((tm, LANES), pos),
    ]
    kw = N_HEADS * HEAD_PAD
    out_shape = [jax.ShapeDtypeStruct((t, kw), BF16), jax.ShapeDtypeStruct((t, ATT_WIDTH), BF16)]
    out_specs = [pl.BlockSpec((tm, kw), lambda i: (i, 0)),
                 pl.BlockSpec((tm, ATT_WIDTH), lambda i: (i, 0))]
    if full:
        out_shape += [jax.ShapeDtypeStruct((t, kw), BF16),
                      jax.ShapeDtypeStruct((t, 2 * FOURIER_WIDTH), BF16),
                      jax.ShapeDtypeStruct((t, POOL_WIDTH), F32),
                      jax.ShapeDtypeStruct((t, N_BRANCH * d), BF16)]
        out_specs += [pl.BlockSpec((tm, kw), lambda i: (i, 0)),
                      pl.BlockSpec((tm, 2 * FOURIER_WIDTH), lambda i: (i, 0)),
                      pl.BlockSpec((tm, POOL_WIDTH), lambda i: (i, 0)),
                      pl.BlockSpec((tm, N_BRANCH * d), lambda i: (i, 0))]
    return pl.pallas_call(
        functools.partial(_in_kernel, d, full),
        out_shape=out_shape,
        grid=(t // tm,),
        in_specs=in_specs,
        out_specs=out_specs,
        compiler_params=_params(("arbitrary",)),
        name="in_proj" if full else "ctx_kv_proj",
    )(x2d, mod, lw["n1"], wall, lw["qg"], lw["wq"], lw["wqp"], lw["kvg"], lw["wk"], lw["wv"],
      lw["fc"], lw["bg"], cq, sq, ck, sk)


def _attn_kernel(has_ctx, q_ref, k_ref, v_ref, *rest):
    if has_ctx:
        kc_ref, vc_ref, o_ref = rest
    else:
        (o_ref,) = rest
    outs = []
    for hh in range(2):
        sl = slice(hh * HEAD_PAD, (hh + 1) * HEAD_PAD)
        q = q_ref[:, sl]
        s = _dot_nt(q, k_ref[:, sl])
        mx = jnp.max(s, axis=-1, keepdims=True)
        if has_ctx:
            sc = _dot_nt(q, kc_ref[:, sl])
            mx = jnp.maximum(mx, jnp.max(sc, axis=-1, keepdims=True))
        p = jnp.exp(s - mx)
        den = jnp.sum(p, axis=-1, keepdims=True)
        o = _dot(p.astype(BF16), v_ref[...])
        if has_ctx:
            pc = jnp.exp(sc - mx)
            den = den + jnp.sum(pc, axis=-1, keepdims=True)
            o = o + _dot(pc.astype(BF16), vc_ref[...])
        outs.append(o / den)
    lane = lax.broadcasted_iota(I32, outs[0].shape, 1)
    o_ref[...] = jnp.where(lane < V_HEAD, outs[0], outs[1]).astype(BF16)


def _attention(q, k, v, batch, seq_len, ctx_kv):
    t = q.shape[0]
    tq = min(ATT_Q_TILE, seq_len)
    nq = seq_len // tq
    pair_w = 2 * HEAD_PAD
    in_specs = [pl.BlockSpec((tq, pair_w), lambda b, j, i: (b * nq + i, j)),
                pl.BlockSpec((seq_len, pair_w), lambda b, j, i: (b, j)),
                pl.BlockSpec((seq_len, 2 * V_HEAD), lambda b, j, i: (b, j))]
    args = [q, k, v]
    if ctx_kv is not None:
        kc, vc = ctx_kv
        lc = kc.shape[0] // batch
        in_specs += [pl.BlockSpec((lc, pair_w), lambda b, j, i: (b, j)),
                     pl.BlockSpec((lc, 2 * V_HEAD), lambda b, j, i: (b, j))]
        args += [kc, vc]
    return pl.pallas_call(
        functools.partial(_attn_kernel, ctx_kv is not None),
        out_shape=jax.ShapeDtypeStruct((t, ATT_WIDTH), BF16),
        grid=(batch, N_HEADS // 2, nq),
        in_specs=in_specs,
        out_specs=pl.BlockSpec((tq, 2 * V_HEAD), lambda b, j, i: (b * nq + i, j)),
        compiler_params=_params(("arbitrary", "arbitrary", "arbitrary")),
        name="attention" if ctx_kv is not None else "ctx_attention",
    )(*args)


def _fourier_kernel(norm, c_ref, s_ref, ab_ref, o_ref):
    a = ab_ref[:, 0:FOURIER_WIDTH]
    b = ab_ref[:, FOURIER_WIDTH:2 * FOURIER_WIDTH]
    o = _dot(c_ref[...], a) - _dot(s_ref[...], b)
    o_ref[...] = (o * norm).astype(BF16)


def _fourier(ab, batch, seq_len, dft):
    t = ab.shape[0]
    tm = min(FOUR_TILE, seq_len)
    nt = seq_len // tm
    cl, sl = dft
    norm = 1.0 / math.sqrt(seq_len * FOURIER_WIDTH)
    return pl.pallas_call(
        functools.partial(_fourier_kernel, norm),
        out_shape=jax.ShapeDtypeStruct((t, FOURIER_WIDTH), BF16),
        grid=(nt, batch),
        in_specs=[pl.BlockSpec((tm, seq_len), lambda i, b: (i, 0)),
                  pl.BlockSpec((tm, seq_len), lambda i, b: (i, 0)),
                  pl.BlockSpec((seq_len, 2 * FOURIER_WIDTH), lambda i, b: (b, 0))],
        out_specs=pl.BlockSpec((tm, FOURIER_WIDTH), lambda i, b: (b * nt + i, 0)),
        compiler_params=_params(("arbitrary", "arbitrary")),
        name="fourier",
    )(cl, sl, ab)


def _pool_kernel(seq_len, z_ref, wg_ref, ps_ref, o_ref, pad_ref, s_ref):
    n = seq_len
    p = n + 2 * POOL_PAD
    z = z_ref[...]
    zeros = jnp.zeros((POOL_PAD, POOL_WIDTH), F32)
    pad_ref[0:POOL_PAD, :] = zeros
    pad_ref[POOL_PAD + n:p, :] = zeros
    pad_ref[POOL_PAD:POOL_PAD + n, :] = z
    s_ref[0:p - 1, :] = pad_ref[0:p - 1, :] + pad_ref[1:p, :]
    w2 = s_ref[POOL_PAD - 1:POOL_PAD - 1 + n, :]
    pad_ref[0:p - 3, :] = s_ref[0:p - 3, :] + s_ref[2:p - 1, :]
    w4 = pad_ref[POOL_PAD - 2:POOL_PAD - 2 + n, :]
    s_ref[0:p - 7, :] = pad_ref[0:p - 7, :] + pad_ref[4:p - 3, :]
    w8 = s_ref[POOL_PAD - 4:POOL_PAD - 4 + n, :]
    pad_ref[0:p - 15, :] = s_ref[0:p - 15, :] + s_ref[8:p - 7, :]
    w16 = pad_ref[0:n, :]
    pos = lax.broadcasted_iota(I32, (n, POOL_WIDTH), 0)
    grp = lax.broadcasted_iota(I32, (n, POOL_WIDTH), 1) // POOL_GROUP
    win = jnp.where(grp == 0, w2, jnp.where(grp == 1, w4, jnp.where(grp == 2, w8, w16)))
    half = jnp.where(grp == 0, 1, jnp.where(grp == 1, 2, jnp.where(grp == 2, 4, 8)))
    lo = jnp.maximum(pos - half, 0)
    hi = jnp.minimum(pos + half, n)
    cnt = (hi - lo).astype(F32)
    pooled = win / cnt - z
    y = _dot(pooled.astype(BF16), wg_ref[...])
    o_ref[...] = (y * ps_ref[...]).astype(BF16)


def _pool(zp, batch, seq_len, wg_bd, pool_scale):
    t = zp.shape[0]
    return pl.pallas_call(
        functools.partial(_pool_kernel, seq_len),
        out_shape=jax.ShapeDtypeStruct((t, POOL_WIDTH), BF16),
        grid=(batch,),
        in_specs=[pl.BlockSpec((seq_len, POOL_WIDTH), lambda b: (b, 0)),
                  pl.BlockSpec((POOL_WIDTH, POOL_WIDTH), lambda b: (0, 0)),
                  pl.BlockSpec((1, POOL_WIDTH), lambda b: (0, 0))],
        out_specs=pl.BlockSpec((seq_len, POOL_WIDTH), lambda b: (b, 0)),
        scratch_shapes=[pltpu.VMEM((seq_len + 2 * POOL_PAD, POOL_WIDTH), F32),
                        pltpu.VMEM((seq_len + 2 * POOL_PAD, POOL_WIDTH), F32)],
        compiler_params=_params(("arbitrary",)),
        name="pool",
    )(zp, wg_bd, pool_scale)


def _out_kernel(d, x_ref, att_ref, four_ref, pool_ref, g_ref, m_ref, woa_ref, wob_ref, woc_ref,
                wout_ref, n2_ref, wpq_ref, x1_ref, h2_ref, pq_ref):
    m = m_ref[0]
    ya = _dot(att_ref[...], woa_ref[...])
    yb = _dot(four_ref[...], wob_ref[...])
    yc = _dot(pool_ref[...], woc_ref[...])
    mixp = (g_ref[:, 0:d].astype(F32) * ya + g_ref[:, d:2 * d].astype(F32) * yb
            + g_ref[:, 2 * d:3 * d].astype(F32) * yc)
    mix = _dot(mixp.astype(BF16), wout_ref[...])
    x1 = x_ref[...] + m[:, 2 * d:3 * d] * mix
    x1_ref[...] = x1
    h2 = _rms(x1) * n2_ref[...] * (1.0 + m[:, 4 * d:5 * d]) + m[:, 3 * d:4 * d]
    h2_ref[...] = h2
    pq = _dot(h2.astype(BF16), wpq_ref[...])
    for hp in range(2 * PEER_HEADS):
        pq_ref[hp] = pq[:, hp * PEER_HALF:(hp + 1) * PEER_HALF].astype(BF16)


def _out_proj(x2d, att, four, pool, g, mod, mod_row, seq_len, lw):
    t, d = x2d.shape
    tm = min(TOK_TILE, seq_len)
    tiles_per_seq = seq_len // tm

    def const(shape):
        return pl.BlockSpec(shape, lambda i: (0,) * len(shape))

    def row(w):
        return pl.BlockSpec((tm, w), lambda i: (i, 0))

    nhp = 2 * PEER_HEADS
    return pl.pallas_call(
        functools.partial(_out_kernel, d),
        out_shape=[jax.ShapeDtypeStruct((t, d), F32), jax.ShapeDtypeStruct((t, d), F32),
                   jax.ShapeDtypeStruct((nhp, t, PEER_HALF), BF16)],
        grid=(t // tm,),
        in_specs=[row(d), row(ATT_WIDTH), row(FOURIER_WIDTH), row(POOL_WIDTH), row(N_BRANCH * d),
                  pl.BlockSpec((1, 1, mod.shape[-1]), lambda i: (mod_row(i, tiles_per_seq), 0, 0)),
                  const(lw["woa"].shape), const(lw["wob"].shape), const(lw["woc"].shape),
                  const(lw["wout"].shape), const((1, d)), const(lw["wpq"].shape)],
        out_specs=[row(d), row(d), pl.BlockSpec((nhp, tm, PEER_HALF), lambda i: (0, i, 0))],
        compiler_params=_params(("arbitrary",)),
        name="out_proj",
    )(x2d, att, four, pool, g, mod, lw["woa"], lw["wob"], lw["woc"], lw["wout"], lw["n2"],
      lw["wpq"])


def _select_round(s, iota, n):
    mx = jnp.max(s, axis=0, keepdims=True)
    idx = jnp.min(jnp.where(s == mx, iota, n), axis=0, keepdims=True)
    hit = iota == idx
    return mx, idx, hit, jnp.where(hit, -jnp.inf, s)


def _route_kernel(pq_ref, keys_ref, eidx_ref, gate_ref, sv_a, si_a, sv_b, si_b, cand_ref, cidx_ref,
                  ts_ref, eidx_s, gate_s):
    tm = pq_ref.shape[1]
    iota_k = lax.broadcasted_iota(I32, (N_KEYS, tm), 0).astype(F32)
    iota_c = lax.broadcasted_iota(I32, (CAND_ROWS, tm), 0).astype(F32)
    cand_ref[N_CAND:CAND_ROWS, :] = jnp.full((CAND_ROWS - N_CAND, tm), -jnp.inf, F32)
    cidx_ref[N_CAND:CAND_ROWS, :] = jnp.zeros((CAND_ROWS - N_CAND, tm), F32)

    def sub_key_topk(hd, sv_ref, si_ref):
        s = [_dot_nt(keys_ref[2 * hd + p], pq_ref[2 * hd + p]) for p in range(2)]
        for r in range(PEER_TOPK):
            for p in range(2):
                mx, idx, _, s[p] = _select_round(s[p], iota_k, float(N_KEYS))
                sv_ref[p, r:r + 1, :] = mx
                si_ref[p, r:r + 1, :] = idx

    def pair_topk(hd, sv_ref, si_ref):
        off = 0
        for i, cnt in STAIR:
            cand_ref[off:off + cnt, :] = sv_ref[0, i:i + 1, :] + sv_ref[1, 0:cnt, :]
            cidx_ref[off:off + cnt, :] = si_ref[0, i:i + 1, :] * N_KEYS + si_ref[1, 0:cnt, :]
            off += cnt
        cidx = cidx_ref[...]
        c = cand_ref[...]
        base = pl.multiple_of(hd * PEER_TOPK, PEER_TOPK)
        for r in range(PEER_TOPK):
            mx, _, hit, c = _select_round(c, iota_c, float(CAND_ROWS))
            ts_ref[r:r + 1, :] = mx
            expert = jnp.sum(jnp.where(hit, cidx, 0.0), axis=0, keepdims=True)
            eidx_s[pl.ds(base + r, 1), :] = expert.astype(I32)
        ts = ts_ref[...]
        ex = jnp.exp(ts - ts[0:1, :])
        gate_s[pl.ds(base, PEER_TOPK), :] = ex / jnp.sum(ex, axis=0, keepdims=True)

    sub_key_topk(0, sv_a, si_a)

    def two_heads(j, carry):
        hd = 2 * j
        sub_key_topk(hd + 1, sv_b, si_b)
        pair_topk(hd, sv_a, si_a)
        sub_key_topk(hd + 2, sv_a, si_a)
        pair_topk(hd + 1, sv_b, si_b)
        return carry

    lax.fori_loop(0, PEER_HEADS // 2 - 1, two_heads, 0)
    sub_key_topk(PEER_HEADS - 1, sv_b, si_b)
    pair_topk(PEER_HEADS - 2, sv_a, si_a)
    pair_topk(PEER_HEADS - 1, sv_b, si_b)
    eidx_ref[...] = eidx_s[...].T
    gate_ref[...] = gate_s[...].T


def _route(pq, keys):
    nhp, t, _ = pq.shape
    tm = ROUTE_TILE
    lists = pltpu.VMEM((2, PEER_TOPK, tm), F32)
    return pl.pallas_call(
        _route_kernel,
        out_shape=[jax.ShapeDtypeStruct((t, PEER_SLOTS), I32),
                   jax.ShapeDtypeStruct((t, PEER_SLOTS), F32)],
        grid=(t // tm,),
        in_specs=[pl.BlockSpec((nhp, tm, PEER_HALF), lambda i: (0, i, 0)),
                  pl.BlockSpec((nhp, N_KEYS, PEER_HALF), lambda i: (0, 0, 0))],
        out_specs=[pl.BlockSpec((tm, PEER_SLOTS), lambda i: (i, 0)),
                   pl.BlockSpec((tm, PEER_SLOTS), lambda i: (i, 0))],
        scratch_shapes=[lists, lists, lists, lists,
                        pltpu.VMEM((CAND_ROWS, tm), F32),
                        pltpu.VMEM((CAND_ROWS, tm), F32),
                        pltpu.VMEM((PEER_TOPK, tm), F32),
                        pltpu.VMEM((PEER_SLOTS, tm), I32), pltpu.VMEM((PEER_SLOTS, tm), F32)],
        compiler_params=_params(("arbitrary",)),
        name="peer_route",
    )(pq, keys)


def _fold_rows(a, b, keep_a, shift):
    return jnp.where(keep_a, a + pltpu.roll(a, SUBLANES - shift, 0), b + pltpu.roll(b, shift, 0))


def _peer_kernel(final, idx_ref, idxn_ref, gate_ref, h_ref, x_ref, g2_ref, fg_ref, tab_ref,
                 o_ref, buf_a, buf_b, act_ref, sem):
    i = pl.program_id(0)
    n = pl.num_programs(0)
    grp_tok = PEER_TOK // PEER_WAIT_GROUPS
    grp_rows = grp_tok * PEER_SLOTS
    sub = lax.broadcasted_iota(I32, (SUBLANES, LANES), 0)
    keep = {sh: (sub & sh) == 0 for sh in (4, 2, 1)}
    g2 = g2_ref[0, 0]
    gate_t = gate_ref[...].T

    def row_copy(ids, t_src, t_dst, k, buf, s):
        return pltpu.make_async_copy(tab_ref.at[ids[t_src, k]], buf.at[t_dst * PEER_SLOTS + k],
                                     sem.at[s, t_dst // grp_tok])

    def issue_token(ids, t_src, t_dst, buf, s):
        for k in range(PEER_SLOTS):
            row_copy(ids, t_src, t_dst, k, buf, s).start(priority=k % 2)

    def wait_group(buf, s, q):
        pltpu.make_async_copy(tab_ref.at[pl.ds(0, grp_rows)],
                              buf.at[pl.ds(q * grp_rows, grp_rows)], sem.at[s, q]).wait()

    def token(buf, tl, tb):
        r0 = tl * PEER_SLOTS
        hv = h_ref[tb]
        groups = []
        for g in range(PEER_SLOTS // SUBLANES):
            p = [buf[r0 + g * SUBLANES + j].astype(F32)[0:SUBLANES] * hv for j in range(SUBLANES)]
            for sh in (4, 2, 1):
                half = len(p) // 2
                p = [_fold_rows(p[j], p[j + half], keep[sh], sh) for j in range(half)]
            groups.append(p[0])
        part = jnp.concatenate(groups, axis=0)
        a = jnp.sum(part, axis=1, keepdims=True)
        act = gate_t[:, tb:tb + 1] * jax.nn.gelu(a)
        act_ref[...] = jnp.broadcast_to(act, (PEER_SLOTS, LANES))
        accs = [jnp.zeros((SUBLANES, LANES), F32) for _ in range(4)]
        for k in range(PEER_SLOTS):
            up = buf[r0 + k].astype(F32)[SUBLANES:2 * SUBLANES]
            accs[k % 4] = accs[k % 4] + act_ref[k:k + 1, :] * up
        xr = x_ref[tb] + g2 * ((accs[0] + accs[1]) + (accs[2] + accs[3]))
        if final:
            ms = jnp.sum(jnp.sum(xr * xr, axis=1, keepdims=True), axis=0, keepdims=True)
            xr = xr * lax.rsqrt(ms / (SUBLANES * LANES) + EPS) * fg_ref[...]
        o_ref[tb] = xr

    @pl.when(i == 0)
    def _():
        def body(t, carry):
            issue_token(idx_ref, t, t, buf_a, 0)
            return carry
        lax.fori_loop(0, PEER_TOK, body, 0)

    for t in range(PEER_TOK):
        if t % grp_tok == 0:
            wait_group(buf_a, 0, t // grp_tok)
        issue_token(idx_ref, PEER_TOK + t, t, buf_b, 1)
        token(buf_a, t, t)
    for t in range(PEER_TOK):
        if t % grp_tok == 0:
            wait_group(buf_b, 1, t // grp_tok)
        issue_token(idxn_ref, t, t, buf_a, 0)
        token(buf_b, t, PEER_TOK + t)

    @pl.when(i == n - 1)
    def _():
        for q in range(PEER_WAIT_GROUPS):
            wait_group(buf_a, 0, q)


def _peer(n_tok, eidx, gate, h3, x3, g2, mod_row, seq_len, fg, table, final):
    t = n_tok
    tb = 2 * PEER_TOK
    nb = t // tb
    steps_per_seq = seq_len // tb
    rows = PEER_TOK * PEER_SLOTS
    tile = (SUBLANES, LANES)
    pair = (2 * SUBLANES, LANES)
    return pl.pallas_call(
        functools.partial(_peer_kernel, final),
        out_shape=jax.ShapeDtypeStruct((t,) + tile, F32),
        grid=(nb,),
        in_specs=[pl.BlockSpec((tb, PEER_SLOTS), lambda i: (i, 0), memory_space=pltpu.SMEM),
                  pl.BlockSpec((tb, PEER_SLOTS), lambda i: (jnp.minimum(i + 1, nb - 1), 0),
                               memory_space=pltpu.SMEM),
                  pl.BlockSpec((tb, PEER_SLOTS), lambda i: (i, 0)),
                  pl.BlockSpec((tb,) + tile, lambda i: (i, 0, 0)),
                  pl.BlockSpec((tb,) + tile, lambda i: (i, 0, 0)),
                  pl.BlockSpec((1, 1) + tile, lambda i: (mod_row(i, steps_per_seq), 0, 0, 0)),
                  pl.BlockSpec(tile, lambda i: (0, 0)),
                  pl.BlockSpec(memory_space=pl.ANY)],
        out_specs=pl.BlockSpec((tb,) + tile, lambda i: (i, 0, 0)),
        scratch_shapes=[pltpu.VMEM((rows,) + pair, BF16), pltpu.VMEM((rows,) + pair, BF16),
                        pltpu.VMEM((PEER_SLOTS, LANES), F32),
                        pltpu.SemaphoreType.DMA((2, PEER_WAIT_GROUPS))],
        compiler_params=pltpu.CompilerParams(dimension_semantics=("arbitrary",),
                                             vmem_limit_bytes=PEER_VMEM_LIMIT),
        name="peer_experts",
    )(eidx, eidx, gate, h3, x3, g2, fg, table)


def _peer_sc_body(first, n_tok, d, eidx_hbm, gate_hbm, h_hbm, tab_hbm, y_hbm, idx_v, idx_n, gate_v, gate_n,
                  h_v, h_n, rows_a, rows_b, out_v, sem_a, sem_b, sem_n):
    wid = lax.axis_index("s") * SC_CORES + lax.axis_index("c")
    per_worker = n_tok // SC_WORKERS
    tok0 = first + wid * per_worker
    nj = d // SC_LANES
    lane = lax.iota(I32, SC_LANES)
    zero = jnp.zeros((SC_LANES,), F32)
    hi_mask = jnp.full((SC_LANES,), 0xFFFF0000, jnp.uint32)

    def gather(idx_ref, c, rows, sem):
        row0 = pl.multiple_of(c * SC_CHUNK, SC_CHUNK)
        return pltpu.make_async_copy(tab_hbm.at[idx_ref.at[pl.ds(row0, SC_CHUNK)]], rows, sem)

    def copy_words(src, dst, n):
        def step(j, carry):
            off = pl.multiple_of(j * SC_LANES, SC_LANES)
            dst[pl.ds(off, SC_LANES)] = src[pl.ds(off, SC_LANES)]
            return carry
        lax.fori_loop(0, n // SC_LANES, step, 0)

    def compute(c, rows):
        row0 = pl.multiple_of(c * SC_CHUNK, SC_CHUNK)

        def down_step(j, accs):
            off = pl.multiple_of(j * SC_LANES, SC_LANES)
            hj = h_v[pl.ds(off, SC_LANES)]
            out = []
            for r in range(SC_CHUNK):
                w = rows[r, pl.ds(off, SC_LANES)]
                dn = lax.bitcast_convert_type(w & hi_mask, F32)
                out.append(accs[r] + dn * hj)
            return tuple(out)

        accs = lax.fori_loop(0, nj, down_step, tuple(zero for _ in range(SC_CHUNK)))
        acts = []
        for g in range(SC_CHUNK // SC_LANES):
            a = zero
            for r in range(SC_LANES):
                a = jnp.where(lane == r, jnp.sum(accs[g * SC_LANES + r]), a)
            gt = gate_v[pl.ds(row0 + g * SC_LANES, SC_LANES)]
            u = GELU_C * (a + 0.044715 * (a * a * a))
            th = 1.0 - 2.0 / (jnp.exp(2.0 * u) + 1.0)
            act = gt * (0.5 * a * (1.0 + th))
            for r in range(SC_LANES):
                acts.append(jnp.sum(jnp.where(lane == r, act, 0.0)))

        def up_step(j, carry):
            off = pl.multiple_of(j * SC_LANES, SC_LANES)
            o = out_v[pl.ds(off, SC_LANES)]
            for r in range(SC_CHUNK):
                w = rows[r, pl.ds(off, SC_LANES)]
                up = lax.bitcast_convert_type(w << 16, F32)
                o = o + acts[r] * up
            out_v[pl.ds(off, SC_LANES)] = o
            return carry

        lax.fori_loop(0, nj, up_step, 0)

    def token(ti, carry):
        t = tok0 + ti
        tn = jnp.minimum(t + 1, tok0 + per_worker - 1)

        def clear(j, carry2):
            out_v[pl.ds(pl.multiple_of(j * SC_LANES, SC_LANES), SC_LANES)] = zero
            return carry2

        lax.fori_loop(0, nj, clear, 0)

        def pair(pp, carry2):
            gather(idx_v, 2 * pp + 1, rows_b, sem_b).start()
            gather(idx_v, 2 * pp, rows_a, sem_a).wait()
            compute(2 * pp, rows_a)

            @pl.when(pp == 0)
            def _():
                gather(idx_v, 2, rows_a, sem_a).start()

            @pl.when(pp == 1)
            def _():
                pltpu.sync_copy(eidx_hbm.at[tn], idx_n)
                gather(idx_n, 0, rows_a, sem_a).start()
                pltpu.make_async_copy(gate_hbm.at[tn], gate_n, sem_n).start()
                pltpu.make_async_copy(h_hbm.at[tn], h_n, sem_n).start()

            gather(idx_v, 2 * pp + 1, rows_b, sem_b).wait()
            compute(2 * pp + 1, rows_b)
            return carry2

        lax.fori_loop(0, PEER_SLOTS // (2 * SC_CHUNK), pair, 0)
        pltpu.sync_copy(out_v, y_hbm.at[t - first])
        pltpu.make_async_copy(gate_hbm.at[tn], gate_n, sem_n).wait()
        pltpu.make_async_copy(h_hbm.at[tn], h_n, sem_n).wait()
        copy_words(idx_n, idx_v, PEER_SLOTS)
        copy_words(gate_n, gate_v, PEER_SLOTS)
        copy_words(h_n, h_v, d)
        return carry

    pltpu.sync_copy(eidx_hbm.at[tok0], idx_v)
    pltpu.sync_copy(gate_hbm.at[tok0], gate_v)
    pltpu.sync_copy(h_hbm.at[tok0], h_v)
    gather(idx_v, 0, rows_a, sem_a).start()
    lax.fori_loop(0, per_worker, token, 0)
    gather(idx_v, 0, rows_a, sem_a).wait()


def _peer_sc(first, n_tok, eidx, gate, h2, sc_table):
    d = h2.shape[1]
    mesh = plsc.VectorSubcoreMesh(core_axis_name="c", subcore_axis_name="s",
                                  num_cores=SC_CORES, num_subcores=SC_SUBCORES)
    return pl.kernel(
        functools.partial(_peer_sc_body, first, n_tok, d),
        out_type=jax.ShapeDtypeStruct((n_tok, d), F32),
        mesh=mesh,
        scratch_types=[pltpu.VMEM((PEER_SLOTS,), I32), pltpu.VMEM((PEER_SLOTS,), I32),
                       pltpu.VMEM((PEER_SLOTS,), F32), pltpu.VMEM((PEER_SLOTS,), F32),
                       pltpu.VMEM((d,), F32), pltpu.VMEM((d,), F32),
                       pltpu.VMEM((SC_CHUNK, d), jnp.uint32), pltpu.VMEM((SC_CHUNK, d), jnp.uint32),
                       pltpu.VMEM((d,), F32), pltpu.SemaphoreType.DMA, pltpu.SemaphoreType.DMA,
                       pltpu.SemaphoreType.DMA],
        compiler_params=pltpu.CompilerParams(needs_layout_passes=False),
        cost_estimate=pl.CostEstimate(
            flops=4 * n_tok * PEER_SLOTS * d, transcendentals=n_tok * PEER_SLOTS,
            bytes_accessed=n_tok * (PEER_SLOTS * d * 4 + 2 * d * 4 + 2 * PEER_SLOTS * 4)),
        name="peer_experts_sc",
    )(eidx, gate, h2, sc_table)


def _finish_kernel(final, d, y_ref, x_ref, m_ref, fg_ref, o_ref):
    xr = x_ref[...] + m_ref[0][:, 5 * d:6 * d] * y_ref[...]
    if final:
        xr = _rms(xr) * fg_ref[...]
    o_ref[...] = xr


def _peer_finish(y, x1, mod, mod_row, seq_len, tok0, fg, final):
    t, d = y.shape
    tm = min(TOK_TILE, seq_len)
    tiles_per_seq = seq_len // tm
    tile0 = tok0 // tm
    return pl.pallas_call(
        functools.partial(_finish_kernel, final, d),
        out_shape=jax.ShapeDtypeStruct((t, d), F32),
        grid=(t // tm,),
        in_specs=[pl.BlockSpec((tm, d), lambda i: (i, 0)),
                  pl.BlockSpec((tm, d), lambda i: (i + tile0, 0)),
                  pl.BlockSpec((1, 1, mod.shape[-1]),
                               lambda i: (mod_row(i + tile0, tiles_per_seq), 0, 0)),
                  pl.BlockSpec((1, d), lambda i: (0, 0))],
        out_specs=pl.BlockSpec((tm, d), lambda i: (i, 0)),
        compiler_params=_params(("arbitrary",)),
        name="peer_finish",
    )(y, x1, mod, fg.reshape(1, d))


def _rope_perm(w):
    q = QK_ROPE // 4
    a1, a2, b1, b2 = (w[..., j * q:(j + 1) * q] for j in range(4))
    return jnp.concatenate([-a2, a1, -b2, b1], axis=-1)


def _prep_layer(l, d, w_in, b_gate, q_norm_g, w_uq, kv_norm_g, w_ukv, w_oa, w_ob, w_grp,
                pool_scale, w_oc, w_out, w_pq, norm1_g, norm2_g):
    wi = w_in[l]
    s0 = Q_LORA
    s1 = s0 + KV_LORA
    s2 = s1 + QK_ROPE
    s3 = s2 + FOURIER_WIDTH
    s4 = s3 + POOL_WIDTH
    w_kr = wi[:, s1:s2]
    zl = jnp.zeros((d, QK_NOPE), F32)
    zr = jnp.zeros((d, HEAD_PAD - QK_NOPE - QK_ROPE), F32)
    wall = jnp.concatenate([wi[:, 0:s1], wi[:, s2:], zl, w_kr, zr, zl, _rope_perm(w_kr), zr],
                           axis=1).astype(BF16)
    wq = w_uq[l].reshape(Q_LORA, N_HEADS, QK_NOPE + QK_ROPE)
    qpad = jnp.zeros((Q_LORA, N_HEADS, HEAD_PAD - QK_NOPE - QK_ROPE), F32)
    wq_full = jnp.concatenate([wq, qpad], axis=-1).reshape(Q_LORA, N_HEADS * HEAD_PAD)
    wq_perm = jnp.concatenate([jnp.zeros((Q_LORA, N_HEADS, QK_NOPE), F32),
                               _rope_perm(wq[..., QK_NOPE:]), qpad],
                              axis=-1).reshape(Q_LORA, N_HEADS * HEAD_PAD)
    wkv = w_ukv[l].reshape(KV_LORA, N_HEADS, QK_NOPE + V_HEAD)
    wk = jnp.concatenate([wkv[..., :QK_NOPE],
                          jnp.zeros((KV_LORA, N_HEADS, HEAD_PAD - QK_NOPE), F32)],
                         axis=-1).reshape(KV_LORA, N_HEADS * HEAD_PAD)
    wv = wkv[..., QK_NOPE:].reshape(KV_LORA, ATT_WIDTH)
    cidx = jnp.arange(FOURIER_WIDTH, dtype=I32)
    ang = (2.0 * math.pi / FOURIER_WIDTH) * ((cidx[:, None] * cidx[None, :]) % FOURIER_WIDTH
                                             ).astype(F32)
    fc = jnp.concatenate([jnp.cos(ang), jnp.sin(ang)], axis=1).astype(BF16)
    ng = len(POOL_WINDOWS)
    wg = jnp.zeros((ng, POOL_GROUP, ng, POOL_GROUP), F32)
    for gi in range(ng):
        wg = wg.at[gi, :, gi, :].set(w_grp[l, gi])
    return {
        "wall": wall, "n1": norm1_g[l][None, :], "n2": norm2_g[l][None, :],
        "qg": q_norm_g[l][None, :], "kvg": kv_norm_g[l][None, :],
        "wq": wq_full.astype(BF16), "wqp": wq_perm.astype(BF16),
        "wk": wk.astype(BF16), "wv": wv.astype(BF16), "fc": fc,
        "bg": b_gate[l][None, :],
        "wg": wg.reshape(POOL_WIDTH, POOL_WIDTH).astype(BF16),
        "ps": pool_scale[l][None, :],
        "woa": w_oa[l].astype(BF16), "wob": w_ob[l].astype(BF16), "woc": w_oc[l].astype(BF16),
        "wout": w_out[l].astype(BF16), "wpq": w_pq[l].astype(BF16),
    }


def _bf16_bits(w):
    return lax.bitcast_convert_type(w.astype(BF16), jnp.uint16).astype(jnp.uint32)


def _rope_tables(seq_len, rope):
    zeros_n = jnp.zeros((seq_len, QK_NOPE), F32)
    zeros_p = jnp.zeros((seq_len, HEAD_PAD - QK_NOPE - QK_ROPE), F32)
    ones_n = jnp.ones((seq_len, QK_NOPE), F32)
    if rope:
        pos = jnp.arange(seq_len, dtype=I32)
        half = QK_ROPE // 2
        inv_freq = ROPE_BASE ** (-jnp.arange(0, half, 2, dtype=F32) / half)
        ang_r = (pos // GRID_W).astype(F32)[:, None] * inv_freq
        ang_c = (pos % GRID_W).astype(F32)[:, None] * inv_freq
        cos = jnp.concatenate([jnp.cos(ang_r)] * 2 + [jnp.cos(ang_c)] * 2, axis=1)
        sin = jnp.concatenate([jnp.sin(ang_r)] * 2 + [jnp.sin(ang_c)] * 2, axis=1)
    else:
        cos = jnp.ones((seq_len, QK_ROPE), F32)
        sin = jnp.zeros((seq_len, QK_ROPE), F32)
    cq = jnp.concatenate([ones_n, cos, zeros_p], axis=1) * ATT_SCALE
    sq = jnp.concatenate([zeros_n, sin, zeros_p], axis=1) * ATT_SCALE
    ck = jnp.concatenate([zeros_n, cos, zeros_p], axis=1)
    sk = jnp.concatenate([zeros_n, sin, zeros_p], axis=1)
    return cq, sq, ck, sk


def _dft(seq_len):
    n = seq_len
    a = DFT_SPLIT if n % DFT_SPLIT == 0 else 1
    b = n // a
    k = jnp.arange(n, dtype=I32)[None, :]
    ang_a = (2.0 * math.pi / a) * ((jnp.arange(a, dtype=I32)[:, None] * k) % a).astype(F32)
    ang_b = (2.0 * math.pi / n) * ((jnp.arange(b, dtype=I32)[:, None] * k) % n).astype(F32)
    ca, sa = jnp.cos(ang_a)[:, None, :], jnp.sin(ang_a)[:, None, :]
    cb, sb = jnp.cos(ang_b)[None, :, :], jnp.sin(ang_b)[None, :, :]
    cos = (ca * cb - sa * sb).reshape(n, n)
    sin = (sa * cb + ca * sb).reshape(n, n)
    return cos.astype(BF16), sin.astype(BF16)


def _peer_block(x1, h2, pq, keys, mod, mod_row, seq_len, fg, table, sc_table, n_sc, final):
    t, d = x1.shape
    tile = (SUBLANES, LANES)
    eidx, gate = _route(pq, keys)
    n_tc = t - n_sc
    mod_tiles = mod.reshape(MOD_ROWS, 6, SUBLANES, LANES)[:, 5:6]
    out = _peer(n_tc, eidx, gate, h2.reshape((t,) + tile), x1.reshape((t,) + tile), mod_tiles,
                mod_row, seq_len, fg.reshape(tile), table, final).reshape(n_tc, d)
    if n_sc == 0:
        return out, ()
    y = _peer_sc(n_tc, n_sc, eidx, gate, h2, sc_table)
    out_sc = _peer_finish(y, x1, mod, mod_row, seq_len, n_tc, fg, final)
    return jnp.concatenate([out, out_sc], axis=0), (out, eidx, gate, h2)


def kernel(x, c, ctx, c_ctx, w_mod, b_mod, norm1_g, norm2_g, w_in, b_gate, q_norm_g, w_uq,
           kv_norm_g, w_ukv, w_oa, w_ob, w_grp, pool_scale, w_oc, w_out, w_pq, peer_keys,
           peer_down, peer_up, final_g):
    batch, seq_len, d = x.shape
    ctx_len = ctx.shape[1]
    depth = w_mod.shape[0]
    assert d == SUBLANES * LANES and batch + 1 <= MOD_ROWS
    assert seq_len % TOK_TILE == 0 and ctx_len % (2 * PEER_TOK) == 0 and ctx_len % LANES == 0
    tile = (SUBLANES, LANES)

    cvec = jnp.concatenate([c, c_ctx[None, :], jnp.zeros((MOD_ROWS - batch - 1, d), F32)], axis=0)
    mod_all = _modulation(cvec, w_mod, b_mod)

    def x_row(b0):
        return lambda i, per_seq: b0 + i // per_seq

    def c_row(i, per_seq):
        return batch

    tabs_x = _rope_tables(seq_len, True)
    tabs_c = _rope_tables(ctx_len, False)
    dft_x = _dft(seq_len)
    dft_c = _dft(ctx_len)

    n_chains = PEER_CHAINS if batch % PEER_CHAINS == 0 else 1
    bpc = batch // n_chains
    use_sc = bpc * seq_len >= (max(PEER_SC_TOKENS, PEER_SC_TOKENS_FIRST, PEER_SC_TOKENS_MIDDLE)
                               + TOK_TILE)

    def sc_tokens(l, j):
        if not use_sc:
            return 0
        if l == depth - 1 and j == n_chains - 1:
            return PEER_SC_TOKENS
        return PEER_SC_TOKENS_FIRST if (l, j) == (0, 0) else PEER_SC_TOKENS_MIDDLE

    tables = [jnp.concatenate([peer_down[l].reshape((-1,) + tile),
                               peer_up[l].reshape((-1,) + tile)], axis=1).astype(BF16)
              for l in range(depth)]
    sc_tables = [(_bf16_bits(peer_down[l]) << 16) | _bf16_bits(peer_up[l]) for l in range(depth)]
    x, ctx, tables, sc_tables = lax.optimization_barrier((x, ctx, tables, sc_tables))

    chains = [x[j * bpc:(j + 1) * bpc].reshape(bpc * seq_len, d) for j in range(n_chains)]
    cs = ctx.reshape(batch * ctx_len, d)
    sc_args = ()
    for l in range(depth):
        last = l == depth - 1
        lw = _prep_layer(l, d, w_in, b_gate, q_norm_g, w_uq, kv_norm_g, w_ukv, w_oa, w_ob, w_grp,
                         pool_scale, w_oc, w_out, w_pq, norm1_g, norm2_g)
        mod = mod_all[l].reshape(MOD_ROWS, 1, 6 * d)
        keys = peer_keys[l].reshape(2 * PEER_HEADS, N_KEYS, PEER_HALF).astype(BF16)
        table = tables[l]
        sc_table = sc_tables[l]

        if last:
            kc, vc = _in_proj(cs, mod, c_row, ctx_len, lw, tabs_c, False)
        else:
            kc, vc, qc, abc, zpc, gc = _in_proj(cs, mod, c_row, ctx_len, lw, tabs_c, True)

        for j in range(n_chains):
            row = x_row(j * bpc)
            sc_args, xs = lax.optimization_barrier((sc_args, chains[j]))
            ctx_rows = slice(j * bpc * ctx_len, (j + 1) * bpc * ctx_len)
            kx, vx, qx, abx, zpx, gx = _in_proj(xs, mod, row, seq_len, lw, tabs_x, True)
            att_x = _attention(qx, kx, vx, bpc, seq_len, (kc[ctx_rows], vc[ctx_rows]))
            four_x = _fourier(abx, bpc, seq_len, dft_x)
            pool_x = _pool(zpx, bpc, seq_len, lw["wg"], lw["ps"])
            x1, h2, pq = _out_proj(xs, att_x, four_x, pool_x, gx, mod, row, seq_len, lw)
            chains[j], sc_args = _peer_block(x1, h2, pq, keys, mod, row, seq_len, final_g, table,
                                             sc_table, sc_tokens(l, j), last)

        if not last:
            att_c = _attention(qc, kc, vc, batch, ctx_len, None)
            four_c = _fourier(abc, batch, ctx_len, dft_c)
            pool_c = _pool(zpc, batch, ctx_len, lw["wg"], lw["ps"])
            c1, hc2, pqc = _out_proj(cs, att_c, four_c, pool_c, gc, mod, c_row, ctx_len, lw)
            cs, _ = _peer_block(c1, hc2, pqc, keys, mod, c_row, ctx_len, final_g, table, None, 0,
                                False)
    xs = jnp.concatenate(chains, axis=0)
    return xs.reshape(batch, seq_len, d)
```

```python
import functools
import math

import jax
import jax.numpy as jnp
from jax import lax
from jax.experimental import pallas as pl
from jax.experimental.pallas import tpu as pltpu
from jax.experimental.pallas import tpu_sc as plsc

F32 = jnp.float32
BF16 = jnp.bfloat16
I32 = jnp.int32

GRID_W = 64
N_HEADS = 8
Q_LORA = 256
KV_LORA = 128
QK_NOPE = 64
QK_ROPE = 32
V_HEAD = 64
ATT_WIDTH = N_HEADS * V_HEAD
ATT_SCALE = (QK_NOPE + QK_ROPE) ** -0.5
ROPE_BASE = 10000.0
FOURIER_WIDTH = 256
POOL_WINDOWS = (2, 4, 8, 16)
POOL_GROUP = 64
POOL_WIDTH = POOL_GROUP * len(POOL_WINDOWS)
N_BRANCH = 3
PEER_HEADS = 8
N_KEYS = 128
PEER_QDIM = 256
PEER_HALF = PEER_QDIM // 2
PEER_TOPK = 16
PEER_SLOTS = PEER_HEADS * PEER_TOPK
STAIR = tuple((i, PEER_TOPK // (i + 1)) for i in range(PEER_TOPK))
N_CAND = sum(cnt for _, cnt in STAIR)
CAND_ROWS = -(-N_CAND // 8) * 8
EPS = 1e-6

LANES = 128
SUBLANES = 8
HEAD_PAD = 128
POOL_PAD = 8
MOD_ROWS = 16
VMEM_LIMIT = 48 * 1024 * 1024

TOK_TILE = 256
ATT_Q_TILE = 256
FOUR_TILE = 512
DFT_SPLIT = 64
ROUTE_TILE = 128
PEER_TOK = 16
PEER_WAIT_GROUPS = 4
PEER_VMEM_LIMIT = 56 * 1024 * 1024

SC_CORES = 2
SC_SUBCORES = 16
SC_WORKERS = SC_CORES * SC_SUBCORES
SC_LANES = 16
SC_CHUNK = 32
PEER_CHAINS = 2
PEER_SC_TOKENS = 5888
PEER_SC_TOKENS_FIRST = 8960
PEER_SC_TOKENS_MIDDLE = 9216
PEER_SC_TOKENS_LAYER_END = 9984
GELU_C = math.sqrt(2.0 / math.pi)

C_CQ = 0
C_CKV = C_CQ + Q_LORA
C_ZF = C_CKV + KV_LORA
C_ZP = C_ZF + FOURIER_WIDTH
C_ZG = C_ZP + POOL_WIDTH
C_KR = None


def _params(sem=None):
    return pltpu.CompilerParams(dimension_semantics=sem, vmem_limit_bytes=VMEM_LIMIT)


def _rms(x):
    return x * lax.rsqrt(jnp.mean(x * x, axis=-1, keepdims=True) + EPS)


def _dot(a, b):
    return jnp.dot(a, b, preferred_element_type=F32)


def _dot_nt(a, b):
    return lax.dot_general(a, b, (((1,), (1,)), ((), ())), preferred_element_type=F32)


def _mod_kernel(c_ref, w_ref, b_ref, o_ref):
    c = c_ref[...]
    s = c * jax.nn.sigmoid(c)
    o_ref[0] = jnp.dot(s, w_ref[0], preferred_element_type=F32,
                       precision=lax.Precision.HIGHEST) + b_ref[0]


def _modulation(cvec, w_mod, b_mod):
    depth, d, n = w_mod.shape
    tn = 1536
    return pl.pallas_call(
        _mod_kernel,
        out_shape=jax.ShapeDtypeStruct((depth, MOD_ROWS, n), F32),
        grid=(depth, n // tn),
        in_specs=[pl.BlockSpec((MOD_ROWS, d), lambda l, j: (0, 0)),
                  pl.BlockSpec((1, d, tn), lambda l, j: (l, 0, j)),
                  pl.BlockSpec((1, 1, tn), lambda l, j: (l, 0, j))],
        out_specs=pl.BlockSpec((1, MOD_ROWS, tn), lambda l, j: (l, 0, j)),
        compiler_params=_params(("arbitrary", "arbitrary")),
        name="modulation",
    )(cvec, w_mod, b_mod.reshape(depth, 1, n))


def _in_kernel(d, full, x_ref, m_ref, n1_ref, wall_ref, qg_ref, wq_ref, wqp_ref, kvg_ref,
               wk_ref, wv_ref, fc_ref, bg_ref, cq_ref, sq_ref, ck_ref, sk_ref, *outs):
    x = x_ref[...]
    m = m_ref[0]
    h = _rms(x) * n1_ref[...] * (1.0 + m[:, d:2 * d]) + m[:, 0:d]
    hb = h.astype(BF16)
    c_kr = C_ZG + N_BRANCH * d
    if full:
        z = _dot(hb, wall_ref[...])
        k_ref, v_ref, q_ref, ab_ref, zp_ref, g_ref = outs
    else:
        z = None
        k_ref, v_ref = outs
    def col(lo, hi):
        if full:
            return z[:, lo:hi]
        return _dot(hb, wall_ref[:, lo:hi])

    ckv = col(C_CKV, C_ZF)
    ckvn = (_rms(ckv) * kvg_ref[...]).astype(BF16)
    kf = _dot(ckvn, wk_ref[...])
    kr = col(c_kr, c_kr + LANES) * ck_ref[...] + col(c_kr + LANES, c_kr + 2 * LANES) * sk_ref[...]
    for hh in range(N_HEADS):
        sl = slice(hh * HEAD_PAD, (hh + 1) * HEAD_PAD)
        k_ref[:, sl] = (kf[:, sl] + kr).astype(BF16)
    v_ref[...] = _dot(ckvn, wv_ref[...]).astype(BF16)
    if not full:
        return
    cq = z[:, C_CQ:C_CKV]
    cqn = (_rms(cq) * qg_ref[...]).astype(BF16)
    qf = _dot(cqn, wq_ref[...])
    qr = _dot(cqn, wqp_ref[...])
    cosq = cq_ref[...]
    sinq = sq_ref[...]
    for hh in range(N_HEADS):
        sl = slice(hh * HEAD_PAD, (hh + 1) * HEAD_PAD)
        q_ref[:, sl] = (qf[:, sl] * cosq + qr[:, sl] * sinq).astype(BF16)
    ab_ref[...] = _dot(z[:, C_ZF:C_ZP].astype(BF16), fc_ref[...]).astype(BF16)
    zp_ref[...] = z[:, C_ZP:C_ZG]
    g_ref[...] = jax.nn.sigmoid(z[:, C_ZG:c_kr] + bg_ref[...]).astype(BF16)


def _in_proj(x2d, mod, mod_row, seq_len, lw, tabs, full):
    t, d = x2d.shape
    tm = min(TOK_TILE, seq_len)
    tiles_per_seq = seq_len // tm
    wall = lw["wall"]
    nw = wall.shape[1]
    cq, sq, ck, sk = tabs

    def const(shape):
        return pl.BlockSpec(shape, lambda i: (0,) * len(shape))

    def pos(i):
        return (i % tiles_per_seq, 0)

    in_specs = [
        pl.BlockSpec((tm, d), lambda i: (i, 0)),
        pl.BlockSpec((1, 1, mod.shape[-1]), lambda i: (mod_row(i, tiles_per_seq), 0, 0)),
        const((1, d)), const((d, nw)), const((1, Q_LORA)),
        const(lw["wq"].shape), const(lw["wqp"].shape), const((1, KV_LORA)),
        const(lw["wk"].shape), const(lw["wv"].shape), const(lw["fc"].shape),
        const((1, N_BRANCH * d)),
        pl.BlockSpec((tm, LANES), pos), pl.BlockSpec((tm, LANES), pos),
        pl.BlockSpec((tm, LANES), pos), pl.BlockSpec((tm, LANES), pos),
    ]
    kw = N_HEADS * HEAD_PAD
    out_shape = [jax.ShapeDtypeStruct((t, kw), BF16), jax.ShapeDtypeStruct((t, ATT_WIDTH), BF16)]
    out_specs = [pl.BlockSpec((tm, kw), lambda i: (i, 0)),
                 pl.BlockSpec((tm, ATT_WIDTH), lambda i: (i, 0))]
    if full:
        out_shape += [jax.ShapeDtypeStruct((t, kw), BF16),
                      jax.ShapeDtypeStruct((t, 2 * FOURIER_WIDTH), BF16),
                      jax.ShapeDtypeStruct((t, POOL_WIDTH), F32),
                      jax.ShapeDtypeStruct((t, N_BRANCH * d), BF16)]
        out_specs += [pl.BlockSpec((tm, kw), lambda i: (i, 0)),
                      pl.BlockSpec((tm, 2 * FOURIER_WIDTH), lambda i: (i, 0)),
                      pl.BlockSpec((tm, POOL_WIDTH), lambda i: (i, 0)),
                      pl.BlockSpec((tm, N_BRANCH * d), lambda i: (i, 0))]
    return pl.pallas_call(
        functools.partial(_in_kernel, d, full),
        out_shape=out_shape,
        grid=(t // tm,),
        in_specs=in_specs,
        out_specs=out_specs,
        compiler_params=_params(("arbitrary",)),
        name="in_proj" if full else "ctx_kv_proj",
    )(x2d, mod, lw["n1"], wall, lw["qg"], lw["wq"], lw["wqp"], lw["kvg"], lw["wk"], lw["wv"],
      lw["fc"], lw["bg"], cq, sq, ck, sk)


def _attn_kernel(has_ctx, q_ref, k_ref, v_ref, *rest):
    if has_ctx:
        kc_ref, vc_ref, o_ref = rest
    else:
        (o_ref,) = rest
    outs = []
    for hh in range(2):
        sl = slice(hh * HEAD_PAD, (hh + 1) * HEAD_PAD)
        q = q_ref[:, sl]
        s = _dot_nt(q, k_ref[:, sl])
        mx = jnp.max(s, axis=-1, keepdims=True)
        if has_ctx:
            sc = _dot_nt(q, kc_ref[:, sl])
            mx = jnp.maximum(mx, jnp.max(sc, axis=-1, keepdims=True))
        p = jnp.exp(s - mx)
        den = jnp.sum(p, axis=-1, keepdims=True)
        o = _dot(p.astype(BF16), v_ref[...])
        if has_ctx:
            pc = jnp.exp(sc - mx)
            den = den + jnp.sum(pc, axis=-1, keepdims=True)
            o = o + _dot(pc.astype(BF16), vc_ref[...])
        outs.append(o / den)
    lane = lax.broadcasted_iota(I32, outs[0].shape, 1)
    o_ref[...] = jnp.where(lane < V_HEAD, outs[0], outs[1]).astype(BF16)


def _attention(q, k, v, batch, seq_len, ctx_kv):
    t = q.shape[0]
    tq = min(ATT_Q_TILE, seq_len)
    nq = seq_len // tq
    pair_w = 2 * HEAD_PAD
    in_specs = [pl.BlockSpec((tq, pair_w), lambda b, j, i: (b * nq + i, j)),
                pl.BlockSpec((seq_len, pair_w), lambda b, j, i: (b, j)),
                pl.BlockSpec((seq_len, 2 * V_HEAD), lambda b, j, i: (b, j))]
    args = [q, k, v]
    if ctx_kv is not None:
        kc, vc = ctx_kv
        lc = kc.shape[0] // batch
        in_specs += [pl.BlockSpec((lc, pair_w), lambda b, j, i: (b, j)),
                     pl.BlockSpec((lc, 2 * V_HEAD), lambda b, j, i: (b, j))]
        args += [kc, vc]
    return pl.pallas_call(
        functools.partial(_attn_kernel, ctx_kv is not None),
        out_shape=jax.ShapeDtypeStruct((t, ATT_WIDTH), BF16),
        grid=(batch, N_HEADS // 2, nq),
        in_specs=in_specs,
        out_specs=pl.BlockSpec((tq, 2 * V_HEAD), lambda b, j, i: (b * nq + i, j)),
        compiler_params=_params(("arbitrary", "arbitrary", "arbitrary")),
        name="attention" if ctx_kv is not None else "ctx_attention",
    )(*args)


def _fourier_kernel(norm, c_ref, s_ref, ab_ref, o_ref):
    a = ab_ref[:, 0:FOURIER_WIDTH]
    b = ab_ref[:, FOURIER_WIDTH:2 * FOURIER_WIDTH]
    o = _dot(c_ref[...], a) - _dot(s_ref[...], b)
    o_ref[...] = (o * norm).astype(BF16)


def _fourier(ab, batch, seq_len, dft):
    t = ab.shape[0]
    tm = min(FOUR_TILE, seq_len)
    nt = seq_len // tm
    cl, sl = dft
    norm = 1.0 / math.sqrt(seq_len * FOURIER_WIDTH)
    return pl.pallas_call(
        functools.partial(_fourier_kernel, norm),
        out_shape=jax.ShapeDtypeStruct((t, FOURIER_WIDTH), BF16),
        grid=(nt, batch),
        in_specs=[pl.BlockSpec((tm, seq_len), lambda i, b: (i, 0)),
                  pl.BlockSpec((tm, seq_len), lambda i, b: (i, 0)),
                  pl.BlockSpec((seq_len, 2 * FOURIER_WIDTH), lambda i, b: (b, 0))],
        out_specs=pl.BlockSpec((tm, FOURIER_WIDTH), lambda i, b: (b * nt + i, 0)),
        compiler_params=_params(("arbitrary", "arbitrary")),
        name="fourier",
    )(cl, sl, ab)


def _pool_kernel(seq_len, z_ref, wg_ref, ps_ref, o_ref, pad_ref, s_ref):
    n = seq_len
    p = n + 2 * POOL_PAD
    z = z_ref[...]
    zeros = jnp.zeros((POOL_PAD, POOL_WIDTH), F32)
    pad_ref[0:POOL_PAD, :] = zeros
    pad_ref[POOL_PAD + n:p, :] = zeros
    pad_ref[POOL_PAD:POOL_PAD + n, :] = z
    s_ref[0:p - 1, :] = pad_ref[0:p - 1, :] + pad_ref[1:p, :]
    w2 = s_ref[POOL_PAD - 1:POOL_PAD - 1 + n, :]
    pad_ref[0:p - 3, :] = s_ref[0:p - 3, :] + s_ref[2:p - 1, :]
    w4 = pad_ref[POOL_PAD - 2:POOL_PAD - 2 + n, :]
    s_ref[0:p - 7, :] = pad_ref[0:p - 7, :] + pad_ref[4:p - 3, :]
    w8 = s_ref[POOL_PAD - 4:POOL_PAD - 4 + n, :]
    pad_ref[0:p - 15, :] = s_ref[0:p - 15, :] + s_ref[8:p - 7, :]
    w16 = pad_ref[0:n, :]
    pos = lax.broadcasted_iota(I32, (n, POOL_WIDTH), 0)
    grp = lax.broadcasted_iota(I32, (n, POOL_WIDTH), 1) // POOL_GROUP
    win = jnp.where(grp == 0, w2, jnp.where(grp == 1, w4, jnp.where(grp == 2, w8, w16)))
    half = jnp.where(grp == 0, 1, jnp.where(grp == 1, 2, jnp.where(grp == 2, 4, 8)))
    lo = jnp.maximum(pos - half, 0)
    hi = jnp.minimum(pos + half, n)
    cnt = (hi - lo).astype(F32)
    pooled = win / cnt - z
    y = _dot(pooled.astype(BF16), wg_ref[...])
    o_ref[...] = (y * ps_ref[...]).astype(BF16)


def _pool(zp, batch, seq_len, wg_bd, pool_scale):
    t = zp.shape[0]
    return pl.pallas_call(
        functools.partial(_pool_kernel, seq_len),
        out_shape=jax.ShapeDtypeStruct((t, POOL_WIDTH), BF16),
        grid=(batch,),
        in_specs=[pl.BlockSpec((seq_len, POOL_WIDTH), lambda b: (b, 0)),
                  pl.BlockSpec((POOL_WIDTH, POOL_WIDTH), lambda b: (0, 0)),
                  pl.BlockSpec((1, POOL_WIDTH), lambda b: (0, 0))],
        out_specs=pl.BlockSpec((seq_len, POOL_WIDTH), lambda b: (b, 0)),
        scratch_shapes=[pltpu.VMEM((seq_len + 2 * POOL_PAD, POOL_WIDTH), F32),
                        pltpu.VMEM((seq_len + 2 * POOL_PAD, POOL_WIDTH), F32)],
        compiler_params=_params(("arbitrary",)),
        name="pool",
    )(zp, wg_bd, pool_scale)


def _out_kernel(d, x_ref, att_ref, four_ref, pool_ref, g_ref, m_ref, woa_ref, wob_ref, woc_ref,
                wout_ref, n2_ref, wpq_ref, x1_ref, h2_ref, pq_ref):
    m = m_ref[0]
    ya = _dot(att_ref[...], woa_ref[...])
    yb = _dot(four_ref[...], wob_ref[...])
    yc = _dot(pool_ref[...], woc_ref[...])
    mixp = (g_ref[:, 0:d].astype(F32) * ya + g_ref[:, d:2 * d].astype(F32) * yb
            + g_ref[:, 2 * d:3 * d].astype(F32) * yc)
    mix = _dot(mixp.astype(BF16), wout_ref[...])
    x1 = x_ref[...] + m[:, 2 * d:3 * d] * mix
    x1_ref[...] = x1
    h2 = _rms(x1) * n2_ref[...] * (1.0 + m[:, 4 * d:5 * d]) + m[:, 3 * d:4 * d]
    h2_ref[...] = h2
    pq = _dot(h2.astype(BF16), wpq_ref[...])
    for hp in range(2 * PEER_HEADS):
        pq_ref[hp] = pq[:, hp * PEER_HALF:(hp + 1) * PEER_HALF].astype(BF16)


def _out_proj(x2d, att, four, pool, g, mod, mod_row, seq_len, lw):
    t, d = x2d.shape
    tm = min(TOK_TILE, seq_len)
    tiles_per_seq = seq_len // tm

    def const(shape):
        return pl.BlockSpec(shape, lambda i: (0,) * len(shape))

    def row(w):
        return pl.BlockSpec((tm, w), lambda i: (i, 0))

    nhp = 2 * PEER_HEADS
    return pl.pallas_call(
        functools.partial(_out_kernel, d),
        out_shape=[jax.ShapeDtypeStruct((t, d), F32), jax.ShapeDtypeStruct((t, d), F32),
                   jax.ShapeDtypeStruct((nhp, t, PEER_HALF), BF16)],
        grid=(t // tm,),
        in_specs=[row(d), row(ATT_WIDTH), row(FOURIER_WIDTH), row(POOL_WIDTH), row(N_BRANCH * d),
                  pl.BlockSpec((1, 1, mod.shape[-1]), lambda i: (mod_row(i, tiles_per_seq), 0, 0)),
                  const(lw["woa"].shape), const(lw["wob"].shape), const(lw["woc"].shape),
                  const(lw["wout"].shape), const((1, d)), const(lw["wpq"].shape)],
        out_specs=[row(d), row(d), pl.BlockSpec((nhp, tm, PEER_HALF), lambda i: (0, i, 0))],
        compiler_params=_params(("arbitrary",)),
        name="out_proj",
    )(x2d, att, four, pool, g, mod, lw["woa"], lw["wob"], lw["woc"], lw["wout"], lw["n2"],
      lw["wpq"])


def _select_round(s, iota, n):
    mx = jnp.max(s, axis=0, keepdims=True)
    idx = jnp.min(jnp.where(s == mx, iota, n), axis=0, keepdims=True)
    hit = iota == idx
    return mx, idx, hit, jnp.where(hit, -jnp.inf, s)


def _route_kernel(pq_ref, keys_ref, eidx_ref, gate_ref, sv_a, si_a, sv_b, si_b, cand_ref, cidx_ref,
                  ts_ref, eidx_s, gate_s):
    tm = pq_ref.shape[1]
    iota_k = lax.broadcasted_iota(I32, (N_KEYS, tm), 0).astype(F32)
    iota_c = lax.broadcasted_iota(I32, (CAND_ROWS, tm), 0).astype(F32)
    cand_ref[N_CAND:CAND_ROWS, :] = jnp.full((CAND_ROWS - N_CAND, tm), -jnp.inf, F32)
    cidx_ref[N_CAND:CAND_ROWS, :] = jnp.zeros((CAND_ROWS - N_CAND, tm), F32)

    def sub_key_topk(hd, sv_ref, si_ref):
        s = [_dot_nt(keys_ref[2 * hd + p], pq_ref[2 * hd + p]) for p in range(2)]
        for r in range(PEER_TOPK):
            for p in range(2):
                mx, idx, _, s[p] = _select_round(s[p], iota_k, float(N_KEYS))
                sv_ref[p, r:r + 1, :] = mx
                si_ref[p, r:r + 1, :] = idx

    def pair_topk(hd, sv_ref, si_ref):
        off = 0
        for i, cnt in STAIR:
            cand_ref[off:off + cnt, :] = sv_ref[0, i:i + 1, :] + sv_ref[1, 0:cnt, :]
            cidx_ref[off:off + cnt, :] = si_ref[0, i:i + 1, :] * N_KEYS + si_ref[1, 0:cnt, :]
            off += cnt
        cidx = cidx_ref[...]
        c = cand_ref[...]
        base = pl.multiple_of(hd * PEER_TOPK, PEER_TOPK)
        for r in range(PEER_TOPK):
            mx, _, hit, c = _select_round(c, iota_c, float(CAND_ROWS))
            ts_ref[r:r + 1, :] = mx
            expert = jnp.sum(jnp.where(hit, cidx, 0.0), axis=0, keepdims=True)
            eidx_s[pl.ds(base + r, 1), :] = expert.astype(I32)
        ts = ts_ref[...]
        ex = jnp.exp(ts - ts[0:1, :])
        gate_s[pl.ds(base, PEER_TOPK), :] = ex / jnp.sum(ex, axis=0, keepdims=True)

    sub_key_topk(0, sv_a, si_a)

    def two_heads(j, carry):
        hd = 2 * j
        sub_key_topk(hd + 1, sv_b, si_b)
        pair_topk(hd, sv_a, si_a)
        sub_key_topk(hd + 2, sv_a, si_a)
        pair_topk(hd + 1, sv_b, si_b)
        return carry

    lax.fori_loop(0, PEER_HEADS // 2 - 1, two_heads, 0)
    sub_key_topk(PEER_HEADS - 1, sv_b, si_b)
    pair_topk(PEER_HEADS - 2, sv_a, si_a)
    pair_topk(PEER_HEADS - 1, sv_b, si_b)
    eidx_ref[...] = eidx_s[...].T
    gate_ref[...] = gate_s[...].T


def _route(pq, keys):
    nhp, t, _ = pq.shape
    tm = ROUTE_TILE
    lists = pltpu.VMEM((2, PEER_TOPK, tm), F32)
    return pl.pallas_call(
        _route_kernel,
        out_shape=[jax.ShapeDtypeStruct((t, PEER_SLOTS), I32),
                   jax.ShapeDtypeStruct((t, PEER_SLOTS), F32)],
        grid=(t // tm,),
        in_specs=[pl.BlockSpec((nhp, tm, PEER_HALF), lambda i: (0, i, 0)),
                  pl.BlockSpec((nhp, N_KEYS, PEER_HALF), lambda i: (0, 0, 0))],
        out_specs=[pl.BlockSpec((tm, PEER_SLOTS), lambda i: (i, 0)),
                   pl.BlockSpec((tm, PEER_SLOTS), lambda i: (i, 0))],
        scratch_shapes=[lists, lists, lists, lists,
                        pltpu.VMEM((CAND_ROWS, tm), F32),
                        pltpu.VMEM((CAND_ROWS, tm), F32),
                        pltpu.VMEM((PEER_TOPK, tm), F32),
                        pltpu.VMEM((PEER_SLOTS, tm), I32), pltpu.VMEM((PEER_SLOTS, tm), F32)],
        compiler_params=_params(("arbitrary",)),
        name="peer_route",
    )(pq, keys)


def _fold_rows(a, b, keep_a, shift):
    return jnp.where(keep_a, a + pltpu.roll(a, SUBLANES - shift, 0), b + pltpu.roll(b, shift, 0))


def _peer_kernel(final, idx_ref, idxn_ref, gate_ref, h_ref, x_ref, g2_ref, fg_ref, tab_ref,
                 o_ref, buf_a, buf_b, act_ref, sem):
    i = pl.program_id(0)
    n = pl.num_programs(0)
    grp_tok = PEER_TOK // PEER_WAIT_GROUPS
    grp_rows = grp_tok * PEER_SLOTS
    sub = lax.broadcasted_iota(I32, (SUBLANES, LANES), 0)
    keep = {sh: (sub & sh) == 0 for sh in (4, 2, 1)}
    g2 = g2_ref[0, 0]
    gate_t = gate_ref[...].T

    def row_copy(ids, t_src, t_dst, k, buf, s):
        return pltpu.make_async_copy(tab_ref.at[ids[t_src, k]], buf.at[t_dst * PEER_SLOTS + k],
                                     sem.at[s, t_dst // grp_tok])

    def issue_token(ids, t_src, t_dst, buf, s):
        for k in range(PEER_SLOTS):
            row_copy(ids, t_src, t_dst, k, buf, s).start(priority=k % 2)

    def wait_group(buf, s, q):
        pltpu.make_async_copy(tab_ref.at[pl.ds(0, grp_rows)],
                              buf.at[pl.ds(q * grp_rows, grp_rows)], sem.at[s, q]).wait()

    def token(buf, tl, tb):
        r0 = tl * PEER_SLOTS
        hv = h_ref[tb]
        groups = []
        for g in range(PEER_SLOTS // SUBLANES):
            p = [buf[r0 + g * SUBLANES + j].astype(F32)[0:SUBLANES] * hv for j in range(SUBLANES)]
            for sh in (4, 2, 1):
                half = len(p) // 2
                p = [_fold_rows(p[j], p[j + half], keep[sh], sh) for j in range(half)]
            groups.append(p[0])
        part = jnp.concatenate(groups, axis=0)
        a = jnp.sum(part, axis=1, keepdims=True)
        act = gate_t[:, tb:tb + 1] * jax.nn.gelu(a)
        act_ref[...] = jnp.broadcast_to(act, (PEER_SLOTS, LANES))
        accs = [jnp.zeros((SUBLANES, LANES), F32) for _ in range(4)]
        for k in range(PEER_SLOTS):
            up = buf[r0 + k].astype(F32)[SUBLANES:2 * SUBLANES]
            accs[k % 4] = accs[k % 4] + act_ref[k:k + 1, :] * up
        xr = x_ref[tb] + g2 * ((accs[0] + accs[1]) + (accs[2] + accs[3]))
        if final:
            ms = jnp.sum(jnp.sum(xr * xr, axis=1, keepdims=True), axis=0, keepdims=True)
            xr = xr * lax.rsqrt(ms / (SUBLANES * LANES) + EPS) * fg_ref[...]
        o_ref[tb] = xr

    @pl.when(i == 0)
    def _():
        def body(t, carry):
            issue_token(idx_ref, t, t, buf_a, 0)
            return carry
        lax.fori_loop(0, PEER_TOK, body, 0)

    for t in range(PEER_TOK):
        if t % grp_tok == 0:
            wait_group(buf_a, 0, t // grp_tok)
        issue_token(idx_ref, PEER_TOK + t, t, buf_b, 1)
        token(buf_a, t, t)
    for t in range(PEER_TOK):
        if t % grp_tok == 0:
            wait_group(buf_b, 1, t // grp_tok)
        issue_token(idxn_ref, t, t, buf_a, 0)
        token(buf_b, t, PEER_TOK + t)

    @pl.when(i == n - 1)
    def _():
        for q in range(PEER_WAIT_GROUPS):
            wait_group(buf_a, 0, q)


def _peer(n_tok, eidx, gate, h3, x3, g2, mod_row, seq_len, fg, table, final):
    t = n_tok
    tb = 2 * PEER_TOK
    nb = t // tb
    steps_per_seq = seq_len // tb
    rows = PEER_TOK * PEER_SLOTS
    tile = (SUBLANES, LANES)
    pair = (2 * SUBLANES, LANES)
    return pl.pallas_call(
        functools.partial(_peer_kernel, final),
        out_shape=jax.ShapeDtypeStruct((t,) + tile, F32),
        grid=(nb,),
        in_specs=[pl.BlockSpec((tb, PEER_SLOTS), lambda i: (i, 0), memory_space=pltpu.SMEM),
                  pl.BlockSpec((tb, PEER_SLOTS), lambda i: (jnp.minimum(i + 1, nb - 1), 0),
                               memory_space=pltpu.SMEM),
                  pl.BlockSpec((tb, PEER_SLOTS), lambda i: (i, 0)),
                  pl.BlockSpec((tb,) + tile, lambda i: (i, 0, 0)),
                  pl.BlockSpec((tb,) + tile, lambda i: (i, 0, 0)),
                  pl.BlockSpec((1, 1) + tile, lambda i: (mod_row(i, steps_per_seq), 0, 0, 0)),
                  pl.BlockSpec(tile, lambda i: (0, 0)),
                  pl.BlockSpec(memory_space=pl.ANY)],
        out_specs=pl.BlockSpec((tb,) + tile, lambda i: (i, 0, 0)),
        scratch_shapes=[pltpu.VMEM((rows,) + pair, BF16), pltpu.VMEM((rows,) + pair, BF16),
                        pltpu.VMEM((PEER_SLOTS, LANES), F32),
                        pltpu.SemaphoreType.DMA((2, PEER_WAIT_GROUPS))],
        compiler_params=pltpu.CompilerParams(dimension_semantics=("arbitrary",),
                                             vmem_limit_bytes=PEER_VMEM_LIMIT),
        name="peer_experts",
    )(eidx, eidx, gate, h3, x3, g2, fg, table)


def _peer_sc_body(first, n_tok, d, eidx_hbm, gate_hbm, h_hbm, tab_hbm, y_hbm, idx_v, idx_n, gate_v, gate_n,
                  h_v, h_n, rows_a, rows_b, out_v, sem_a, sem_b, sem_n):
    wid = lax.axis_index("s") * SC_CORES + lax.axis_index("c")
    per_worker = n_tok // SC_WORKERS
    tok0 = first + wid * per_worker
    nj = d // SC_LANES
    lane = lax.iota(I32, SC_LANES)
    zero = jnp.zeros((SC_LANES,), F32)
    hi_mask = jnp.full((SC_LANES,), 0xFFFF0000, jnp.uint32)

    def gather(idx_ref, c, rows, sem):
        row0 = pl.multiple_of(c * SC_CHUNK, SC_CHUNK)
        return pltpu.make_async_copy(tab_hbm.at[idx_ref.at[pl.ds(row0, SC_CHUNK)]], rows, sem)

    def copy_words(src, dst, n):
        def step(j, carry):
            off = pl.multiple_of(j * SC_LANES, SC_LANES)
            dst[pl.ds(off, SC_LANES)] = src[pl.ds(off, SC_LANES)]
            return carry
        lax.fori_loop(0, n // SC_LANES, step, 0)

    def compute(c, rows):
        row0 = pl.multiple_of(c * SC_CHUNK, SC_CHUNK)

        def down_step(j, accs):
            off = pl.multiple_of(j * SC_LANES, SC_LANES)
            hj = h_v[pl.ds(off, SC_LANES)]
            out = []
            for r in range(SC_CHUNK):
                w = rows[r, pl.ds(off, SC_LANES)]
                dn = lax.bitcast_convert_type(w & hi_mask, F32)
                out.append(accs[r] + dn * hj)
            return tuple(out)

        accs = lax.fori_loop(0, nj, down_step, tuple(zero for _ in range(SC_CHUNK)))
        acts = []
        for g in range(SC_CHUNK // SC_LANES):
            a = zero
            for r in range(SC_LANES):
                a = jnp.where(lane == r, jnp.sum(accs[g * SC_LANES + r]), a)
            gt = gate_v[pl.ds(row0 + g * SC_LANES, SC_LANES)]
            u = GELU_C * (a + 0.044715 * (a * a * a))
            th = 1.0 - 2.0 / (jnp.exp(2.0 * u) + 1.0)
            act = gt * (0.5 * a * (1.0 + th))
            for r in range(SC_LANES):
                acts.append(jnp.sum(jnp.where(lane == r, act, 0.0)))

        def up_step(j, carry):
            off = pl.multiple_of(j * SC_LANES, SC_LANES)
            o = out_v[pl.ds(off, SC_LANES)]
            for r in range(SC_CHUNK):
                w = rows[r, pl.ds(off, SC_LANES)]
                up = lax.bitcast_convert_type(w << 16, F32)
                o = o + acts[r] * up
            out_v[pl.ds(off, SC_LANES)] = o
            return carry

        lax.fori_loop(0, nj, up_step, 0)

    def token(ti, carry):
        t = tok0 + ti
        tn = jnp.minimum(t + 1, tok0 + per_worker - 1)

        def clear(j, carry2):
            out_v[pl.ds(pl.multiple_of(j * SC_LANES, SC_LANES), SC_LANES)] = zero
            return carry2

        lax.fori_loop(0, nj, clear, 0)

        def pair(pp, carry2):
            gather(idx_v, 2 * pp + 1, rows_b, sem_b).start()
            gather(idx_v, 2 * pp, rows_a, sem_a).wait()
            compute(2 * pp, rows_a)

            @pl.when(pp == 0)
            def _():
                gather(idx_v, 2, rows_a, sem_a).start()

            @pl.when(pp == 1)
            def _():
                pltpu.sync_copy(eidx_hbm.at[tn], idx_n)
                gather(idx_n, 0, rows_a, sem_a).start()
                pltpu.make_async_copy(gate_hbm.at[tn], gate_n, sem_n).start()
                pltpu.make_async_copy(h_hbm.at[tn], h_n, sem_n).start()

            gather(idx_v, 2 * pp + 1, rows_b, sem_b).wait()
            compute(2 * pp + 1, rows_b)
            return carry2

        lax.fori_loop(0, PEER_SLOTS // (2 * SC_CHUNK), pair, 0)
        pltpu.sync_copy(out_v, y_hbm.at[t - first])
        pltpu.make_async_copy(gate_hbm.at[tn], gate_n, sem_n).wait()
        pltpu.make_async_copy(h_hbm.at[tn], h_n, sem_n).wait()
        copy_words(idx_n, idx_v, PEER_SLOTS)
        copy_words(gate_n, gate_v, PEER_SLOTS)
        copy_words(h_n, h_v, d)
        return carry

    pltpu.sync_copy(eidx_hbm.at[tok0], idx_v)
    pltpu.sync_copy(gate_hbm.at[tok0], gate_v)
    pltpu.sync_copy(h_hbm.at[tok0], h_v)
    gather(idx_v, 0, rows_a, sem_a).start()
    lax.fori_loop(0, per_worker, token, 0)
    gather(idx_v, 0, rows_a, sem_a).wait()


def _peer_sc(first, n_tok, eidx, gate, h2, sc_table):
    d = h2.shape[1]
    mesh = plsc.VectorSubcoreMesh(core_axis_name="c", subcore_axis_name="s",
                                  num_cores=SC_CORES, num_subcores=SC_SUBCORES)
    return pl.kernel(
        functools.partial(_peer_sc_body, first, n_tok, d),
        out_type=jax.ShapeDtypeStruct((n_tok, d), F32),
        mesh=mesh,
        scratch_types=[pltpu.VMEM((PEER_SLOTS,), I32), pltpu.VMEM((PEER_SLOTS,), I32),
                       pltpu.VMEM((PEER_SLOTS,), F32), pltpu.VMEM((PEER_SLOTS,), F32),
                       pltpu.VMEM((d,), F32), pltpu.VMEM((d,), F32),
                       pltpu.VMEM((SC_CHUNK, d), jnp.uint32), pltpu.VMEM((SC_CHUNK, d), jnp.uint32),
                       pltpu.VMEM((d,), F32), pltpu.SemaphoreType.DMA, pltpu.SemaphoreType.DMA,
                       pltpu.SemaphoreType.DMA],
        compiler_params=pltpu.CompilerParams(needs_layout_passes=False),
        cost_estimate=pl.CostEstimate(
            flops=4 * n_tok * PEER_SLOTS * d, transcendentals=n_tok * PEER_SLOTS,
            bytes_accessed=n_tok * (PEER_SLOTS * d * 4 + 2 * d * 4 + 2 * PEER_SLOTS * 4)),
        name="peer_experts_sc",
    )(eidx, gate, h2, sc_table)


def _finish_kernel(final, d, y_ref, x_ref, m_ref, fg_ref, o_ref):
    xr = x_ref[...] + m_ref[0][:, 5 * d:6 * d] * y_ref[...]
    if final:
        xr = _rms(xr) * fg_ref[...]
    o_ref[...] = xr


def _peer_finish(y, x1, mod, mod_row, seq_len, tok0, fg, final):
    t, d = y.shape
    tm = min(TOK_TILE, seq_len)
    tiles_per_seq = seq_len // tm
    tile0 = tok0 // tm
    return pl.pallas_call(
        functools.partial(_finish_kernel, final, d),
        out_shape=jax.ShapeDtypeStruct((t, d), F32),
        grid=(t // tm,),
        in_specs=[pl.BlockSpec((tm, d), lambda i: (i, 0)),
                  pl.BlockSpec((tm, d), lambda i: (i + tile0, 0)),
                  pl.BlockSpec((1, 1, mod.shape[-1]),
                               lambda i: (mod_row(i + tile0, tiles_per_seq), 0, 0)),
                  pl.BlockSpec((1, d), lambda i: (0, 0))],
        out_specs=pl.BlockSpec((tm, d), lambda i: (i, 0)),
        compiler_params=_params(("arbitrary",)),
        name="peer_finish",
    )(y, x1, mod, fg.reshape(1, d))


def _rope_perm(w):
    q = QK_ROPE // 4
    a1, a2, b1, b2 = (w[..., j * q:(j + 1) * q] for j in range(4))
    return jnp.concatenate([-a2, a1, -b2, b1], axis=-1)


def _prep_layer(l, d, w_in, b_gate, q_norm_g, w_uq, kv_norm_g, w_ukv, w_oa, w_ob, w_grp,
                pool_scale, w_oc, w_out, w_pq, norm1_g, norm2_g):
    wi = w_in[l]
    s0 = Q_LORA
    s1 = s0 + KV_LORA
    s2 = s1 + QK_ROPE
    s3 = s2 + FOURIER_WIDTH
    s4 = s3 + POOL_WIDTH
    w_kr = wi[:, s1:s2]
    zl = jnp.zeros((d, QK_NOPE), F32)
    zr = jnp.zeros((d, HEAD_PAD - QK_NOPE - QK_ROPE), F32)
    wall = jnp.concatenate([wi[:, 0:s1], wi[:, s2:], zl, w_kr, zr, zl, _rope_perm(w_kr), zr],
                           axis=1).astype(BF16)
    wq = w_uq[l].reshape(Q_LORA, N_HEADS, QK_NOPE + QK_ROPE)
    qpad = jnp.zeros((Q_LORA, N_HEADS, HEAD_PAD - QK_NOPE - QK_ROPE), F32)
    wq_full = jnp.concatenate([wq, qpad], axis=-1).reshape(Q_LORA, N_HEADS * HEAD_PAD)
    wq_perm = jnp.concatenate([jnp.zeros((Q_LORA, N_HEADS, QK_NOPE), F32),
                               _rope_perm(wq[..., QK_NOPE:]), qpad],
                              axis=-1).reshape(Q_LORA, N_HEADS * HEAD_PAD)
    wkv = w_ukv[l].reshape(KV_LORA, N_HEADS, QK_NOPE + V_HEAD)
    wk = jnp.concatenate([wkv[..., :QK_NOPE],
                          jnp.zeros((KV_LORA, N_HEADS, HEAD_PAD - QK_NOPE), F32)],
                         axis=-1).reshape(KV_LORA, N_HEADS * HEAD_PAD)
    wv = wkv[..., QK_NOPE:].reshape(KV_LORA, ATT_WIDTH)
    cidx = jnp.arange(FOURIER_WIDTH, dtype=I32)
    ang = (2.0 * math.pi / FOURIER_WIDTH) * ((cidx[:, None] * cidx[None, :]) % FOURIER_WIDTH
                                             ).astype(F32)
    fc = jnp.concatenate([jnp.cos(ang), jnp.sin(ang)], axis=1).astype(BF16)
    ng = len(POOL_WINDOWS)
    wg = jnp.zeros((ng, POOL_GROUP, ng, POOL_GROUP), F32)
    for gi in range(ng):
        wg = wg.at[gi, :, gi, :].set(w_grp[l, gi])
    return {
        "wall": wall, "n1": norm1_g[l][None, :], "n2": norm2_g[l][None, :],
        "qg": q_norm_g[l][None, :], "kvg": kv_norm_g[l][None, :],
        "wq": wq_full.astype(BF16), "wqp": wq_perm.astype(BF16),
        "wk": wk.astype(BF16), "wv": wv.astype(BF16), "fc": fc,
        "bg": b_gate[l][None, :],
        "wg": wg.reshape(POOL_WIDTH, POOL_WIDTH).astype(BF16),
        "ps": pool_scale[l][None, :],
        "woa": w_oa[l].astype(BF16), "wob": w_ob[l].astype(BF16), "woc": w_oc[l].astype(BF16),
        "wout": w_out[l].astype(BF16), "wpq": w_pq[l].astype(BF16),
    }


def _bf16_bits(w):
    return lax.bitcast_convert_type(w.astype(BF16), jnp.uint16).astype(jnp.uint32)


def _rope_tables(seq_len, rope):
    zeros_n = jnp.zeros((seq_len, QK_NOPE), F32)
    zeros_p = jnp.zeros((seq_len, HEAD_PAD - QK_NOPE - QK_ROPE), F32)
    ones_n = jnp.ones((seq_len, QK_NOPE), F32)
    if rope:
        pos = jnp.arange(seq_len, dtype=I32)
        half = QK_ROPE // 2
        inv_freq = ROPE_BASE ** (-jnp.arange(0, half, 2, dtype=F32) / half)
        ang_r = (pos // GRID_W).astype(F32)[:, None] * inv_freq
        ang_c = (pos % GRID_W).astype(F32)[:, None] * inv_freq
        cos = jnp.concatenate([jnp.cos(ang_r)] * 2 + [jnp.cos(ang_c)] * 2, axis=1)
        sin = jnp.concatenate([jnp.sin(ang_r)] * 2 + [jnp.sin(ang_c)] * 2, axis=1)
    else:
        cos = jnp.ones((seq_len, QK_ROPE), F32)
        sin = jnp.zeros((seq_len, QK_ROPE), F32)
    cq = jnp.concatenate([ones_n, cos, zeros_p], axis=1) * ATT_SCALE
    sq = jnp.concatenate([zeros_n, sin, zeros_p], axis=1) * ATT_SCALE
    ck = jnp.concatenate([zeros_n, cos, zeros_p], axis=1)
    sk = jnp.concatenate([zeros_n, sin, zeros_p], axis=1)
    return cq, sq, ck, sk


def _dft(seq_len):
    n = seq_len
    a = DFT_SPLIT if n % DFT_SPLIT == 0 else 1
    b = n // a
    k = jnp.arange(n, dtype=I32)[None, :]
    ang_a = (2.0 * math.pi / a) * ((jnp.arange(a, dtype=I32)[:, None] * k) % a).astype(F32)
    ang_b = (2.0 * math.pi / n) * ((jnp.arange(b, dtype=I32)[:, None] * k) % n).astype(F32)
    ca, sa = jnp.cos(ang_a)[:, None, :], jnp.sin(ang_a)[:, None, :]
    cb, sb = jnp.cos(ang_b)[None, :, :], jnp.sin(ang_b)[None, :, :]
    cos = (ca * cb - sa * sb).reshape(n, n)
    sin = (sa * cb + ca * sb).reshape(n, n)
    return cos.astype(BF16), sin.astype(BF16)


def _peer_block(x1, h2, pq, keys, mod, mod_row, seq_len, fg, table, sc_table, n_sc, final):
    t, d = x1.shape
    tile = (SUBLANES, LANES)
    eidx, gate = _route(pq, keys)
    n_tc = t - n_sc
    mod_tiles = mod.reshape(MOD_ROWS, 6, SUBLANES, LANES)[:, 5:6]
    out = _peer(n_tc, eidx, gate, h2.reshape((t,) + tile), x1.reshape((t,) + tile), mod_tiles,
                mod_row, seq_len, fg.reshape(tile), table, final).reshape(n_tc, d)
    if n_sc == 0:
        return out, ()
    y = _peer_sc(n_tc, n_sc, eidx, gate, h2, sc_table)
    out_sc = _peer_finish(y, x1, mod, mod_row, seq_len, n_tc, fg, final)
    return jnp.concatenate([out, out_sc], axis=0), (out, eidx, gate, h2)


def kernel(x, c, ctx, c_ctx, w_mod, b_mod, norm1_g, norm2_g, w_in, b_gate, q_norm_g, w_uq,
           kv_norm_g, w_ukv, w_oa, w_ob, w_grp, pool_scale, w_oc, w_out, w_pq, peer_keys,
           peer_down, peer_up, final_g):
    batch, seq_len, d = x.shape
    ctx_len = ctx.shape[1]
    depth = w_mod.shape[0]
    assert d == SUBLANES * LANES and batch + 1 <= MOD_ROWS
    assert seq_len % TOK_TILE == 0 and ctx_len % (2 * PEER_TOK) == 0 and ctx_len % LANES == 0
    tile = (SUBLANES, LANES)

    cvec = jnp.concatenate([c, c_ctx[None, :], jnp.zeros((MOD_ROWS - batch - 1, d), F32)], axis=0)
    mod_all = _modulation(cvec, w_mod, b_mod)

    def x_row(b0):
        return lambda i, per_seq: b0 + i // per_seq

    def c_row(i, per_seq):
        return batch

    tabs_x = _rope_tables(seq_len, True)
    tabs_c = _rope_tables(ctx_len, False)
    dft_x = _dft(seq_len)
    dft_c = _dft(ctx_len)

    n_chains = PEER_CHAINS if batch % PEER_CHAINS == 0 else 1
    bpc = batch // n_chains
    use_sc = bpc * seq_len >= TOK_TILE + max(PEER_SC_TOKENS, PEER_SC_TOKENS_FIRST,
                                             PEER_SC_TOKENS_MIDDLE, PEER_SC_TOKENS_LAYER_END)

    def sc_tokens(l, j):
        if not use_sc:
            return 0
        if j == n_chains - 1:
            return PEER_SC_TOKENS if l == depth - 1 else PEER_SC_TOKENS_LAYER_END
        return PEER_SC_TOKENS_FIRST if (l, j) == (0, 0) else PEER_SC_TOKENS_MIDDLE

    tables = [jnp.concatenate([peer_down[l].reshape((-1,) + tile),
                               peer_up[l].reshape((-1,) + tile)], axis=1).astype(BF16)
              for l in range(depth)]
    sc_tables = [(_bf16_bits(peer_down[l]) << 16) | _bf16_bits(peer_up[l]) for l in range(depth)]
    x, ctx, tables, sc_tables = lax.optimization_barrier((x, ctx, tables, sc_tables))

    chains = [x[j * bpc:(j + 1) * bpc].reshape(bpc * seq_len, d) for j in range(n_chains)]
    cs = ctx.reshape(batch * ctx_len, d)
    sc_args = ()
    for l in range(depth):
        last = l == depth - 1
        lw = _prep_layer(l, d, w_in, b_gate, q_norm_g, w_uq, kv_norm_g, w_ukv, w_oa, w_ob, w_grp,
                         pool_scale, w_oc, w_out, w_pq, norm1_g, norm2_g)
        mod = mod_all[l].reshape(MOD_ROWS, 1, 6 * d)
        keys = peer_keys[l].reshape(2 * PEER_HEADS, N_KEYS, PEER_HALF).astype(BF16)
        table = tables[l]
        sc_table = sc_tables[l]

        if last:
            kc, vc = _in_proj(cs, mod, c_row, ctx_len, lw, tabs_c, False)
        else:
            kc, vc, qc, abc, zpc, gc = _in_proj(cs, mod, c_row, ctx_len, lw, tabs_c, True)

        for j in range(n_chains):
            row = x_row(j * bpc)
            sc_args, xs = lax.optimization_barrier((sc_args, chains[j]))
            ctx_rows = slice(j * bpc * ctx_len, (j + 1) * bpc * ctx_len)
            kx, vx, qx, abx, zpx, gx = _in_proj(xs, mod, row, seq_len, lw, tabs_x, True)
            att_x = _attention(qx, kx, vx, bpc, seq_len, (kc[ctx_rows], vc[ctx_rows]))
            four_x = _fourier(abx, bpc, seq_len, dft_x)
            pool_x = _pool(zpx, bpc, seq_len, lw["wg"], lw["ps"])
            x1, h2, pq = _out_proj(xs, att_x, four_x, pool_x, gx, mod, row, seq_len, lw)
            chains[j], sc_args = _peer_block(x1, h2, pq, keys, mod, row, seq_len, final_g, table,
                                             sc_table, sc_tokens(l, j), last)

        if not last:
            att_c = _attention(qc, kc, vc, batch, ctx_len, None)
            four_c = _fourier(abc, batch, ctx_len, dft_c)
            pool_c = _pool(zpc, batch, ctx_len, lw["wg"], lw["ps"])
            c1, hc2, pqc = _out_proj(cs, att_c, four_c, pool_c, gc, mod, c_row, ctx_len, lw)
            cs, _ = _peer_block(c1, hc2, pqc, keys, mod, c_row, ctx_len, final_g, table, None, 0,
                                False)
    xs = jnp.concatenate(chains, axis=0)
    return xs.reshape(batch, seq_len, d)
```

```python
import functools
import math

import jax
import jax.numpy as jnp
from jax import lax
from jax.experimental import pallas as pl
from jax.experimental.pallas import tpu as pltpu
from jax.experimental.pallas import tpu_sc as plsc

F32 = jnp.float32
BF16 = jnp.bfloat16
I32 = jnp.int32

GRID_W = 64
N_HEADS = 8
Q_LORA = 256
KV_LORA = 128
QK_NOPE = 64
QK_ROPE = 32
V_HEAD = 64
ATT_WIDTH = N_HEADS * V_HEAD
ATT_SCALE = (QK_NOPE + QK_ROPE) ** -0.5
ROPE_BASE = 10000.0
FOURIER_WIDTH = 256
POOL_WINDOWS = (2, 4, 8, 16)
POOL_GROUP = 64
POOL_WIDTH = POOL_GROUP * len(POOL_WINDOWS)
N_BRANCH = 3
PEER_HEADS = 8
N_KEYS = 128
PEER_QDIM = 256
PEER_HALF = PEER_QDIM // 2
PEER_TOPK = 16
PEER_SLOTS = PEER_HEADS * PEER_TOPK
EPS = 1e-6

LANES = 128
SUBLANES = 8
HEAD_PAD = 128
POOL_PAD = 8
MOD_ROWS = 16
VMEM_LIMIT = 48 * 1024 * 1024

STAIR = tuple((i, PEER_TOPK // (i + 1)) for i in range(PEER_TOPK))
N_CAND = sum(cnt for _, cnt in STAIR)
CAND_ROWS = -(-N_CAND // SUBLANES) * SUBLANES

TOK_TILE = 256
ATT_Q_TILE = 256
FOUR_TILE = 512
DFT_SPLIT = 64
ROUTE_TILE = 128
PEER_TOK = 16
PEER_WAIT_GROUPS = 4
PEER_VMEM_LIMIT = 56 * 1024 * 1024

SC_CORES = 2
SC_SUBCORES = 16
SC_WORKERS = SC_CORES * SC_SUBCORES
SC_LANES = 16
SC_CHUNK = 32
PEER_CHAINS = 2
PEER_SC_TOKENS = 5888
PEER_SC_TOKENS_FIRST = 8960
PEER_SC_TOKENS_MIDDLE = 8960
PEER_SC_TOKENS_LAYER_END = 9984
GELU_C = math.sqrt(2.0 / math.pi)

C_CQ = 0
C_CKV = C_CQ + Q_LORA
C_ZF = C_CKV + KV_LORA
C_ZP = C_ZF + FOURIER_WIDTH
C_ZG = C_ZP + POOL_WIDTH


def _params(sem=None):
    return pltpu.CompilerParams(dimension_semantics=sem, vmem_limit_bytes=VMEM_LIMIT)


def _rms(x):
    return x * lax.rsqrt(jnp.mean(x * x, axis=-1, keepdims=True) + EPS)


def _dot(a, b):
    return jnp.dot(a, b, preferred_element_type=F32)


def _dot_nt(a, b):
    return lax.dot_general(a, b, (((1,), (1,)), ((), ())), preferred_element_type=F32)


def _mod_kernel(c_ref, w_ref, b_ref, o_ref):
    c = c_ref[...]
    s = c * jax.nn.sigmoid(c)
    o_ref[0] = jnp.dot(s, w_ref[0], preferred_element_type=F32,
                       precision=lax.Precision.HIGHEST) + b_ref[0]


def _modulation(cvec, w_mod, b_mod):
    depth, d, n = w_mod.shape
    tn = 1536
    return pl.pallas_call(
        _mod_kernel,
        out_shape=jax.ShapeDtypeStruct((depth, MOD_ROWS, n), F32),
        grid=(depth, n // tn),
        in_specs=[pl.BlockSpec((MOD_ROWS, d), lambda l, j: (0, 0)),
                  pl.BlockSpec((1, d, tn), lambda l, j: (l, 0, j)),
                  pl.BlockSpec((1, 1, tn), lambda l, j: (l, 0, j))],
        out_specs=pl.BlockSpec((1, MOD_ROWS, tn), lambda l, j: (l, 0, j)),
        compiler_params=_params(("arbitrary", "arbitrary")),
        name="modulation",
    )(cvec, w_mod, b_mod.reshape(depth, 1, n))


def _in_kernel(d, full, x_ref, m_ref, n1_ref, wall_ref, qg_ref, wq_ref, wqp_ref, kvg_ref,
               wk_ref, wv_ref, fc_ref, bg_ref, cq_ref, sq_ref, ck_ref, sk_ref, *outs):
    x = x_ref[...]
    m = m_ref[0]
    h = _rms(x) * n1_ref[...] * (1.0 + m[:, d:2 * d]) + m[:, 0:d]
    hb = h.astype(BF16)
    c_kr = C_ZG + N_BRANCH * d
    if full:
        z = _dot(hb, wall_ref[...])
        k_ref, v_ref, q_ref, ab_ref, zp_ref, g_ref = outs
    else:
        z = None
        k_ref, v_ref = outs
    def col(lo, hi):
        if full:
            return z[:, lo:hi]
        return _dot(hb, wall_ref[:, lo:hi])

    ckv = col(C_CKV, C_ZF)
    ckvn = (_rms(ckv) * kvg_ref[...]).astype(BF16)
    kf = _dot(ckvn, wk_ref[...])
    kr = col(c_kr, c_kr + LANES) * ck_ref[...] + col(c_kr + LANES, c_kr + 2 * LANES) * sk_ref[...]
    for hh in range(N_HEADS):
        sl = slice(hh * HEAD_PAD, (hh + 1) * HEAD_PAD)
        k_ref[:, sl] = (kf[:, sl] + kr).astype(BF16)
    v_ref[...] = _dot(ckvn, wv_ref[...]).astype(BF16)
    if not full:
        return
    cq = z[:, C_CQ:C_CKV]
    cqn = (_rms(cq) * qg_ref[...]).astype(BF16)
    qf = _dot(cqn, wq_ref[...])
    qr = _dot(cqn, wqp_ref[...])
    cosq = cq_ref[...]
    sinq = sq_ref[...]
    for hh in range(N_HEADS):
        sl = slice(hh * HEAD_PAD, (hh + 1) * HEAD_PAD)
        q_ref[:, sl] = (qf[:, sl] * cosq + qr[:, sl] * sinq).astype(BF16)
    ab_ref[...] = _dot(z[:, C_ZF:C_ZP].astype(BF16), fc_ref[...]).astype(BF16)
    zp_ref[...] = z[:, C_ZP:C_ZG]
    g_ref[...] = jax.nn.sigmoid(z[:, C_ZG:c_kr] + bg_ref[...]).astype(BF16)


def _in_proj(x2d, mod, mod_row, seq_len, lw, tabs, full):
    t, d = x2d.shape
    tm = min(TOK_TILE, seq_len)
    tiles_per_seq = seq_len // tm
    wall = lw["wall"]
    nw = wall.shape[1]
    cq, sq, ck, sk = tabs

    def const(shape):
        return pl.BlockSpec(shape, lambda i: (0,) * len(shape))

    def pos(i):
        return (i % tiles_per_seq, 0)

    in_specs = [
        pl.BlockSpec((tm, d), lambda i: (i, 0)),
        pl.BlockSpec((1, 1, mod.shape[-1]), lambda i: (mod_row(i, tiles_per_seq), 0, 0)),
        const((1, d)), const((d, nw)), const((1, Q_LORA)),
        const(lw["wq"].shape), const(lw["wqp"].shape), const((1, KV_LORA)),
        const(lw["wk"].shape), const(lw["wv"].shape), const(lw["fc"].shape),
        const((1, N_BRANCH * d)),
        pl.BlockSpec((tm, LANES), pos), pl.BlockSpec((tm, LANES), pos),
        pl.BlockSpec((tm, LANES), pos), pl.BlockSpec((tm, LANES), pos),
    ]
    kw = N_HEADS * HEAD_PAD
    out_shape = [jax.ShapeDtypeStruct((t, kw), BF16), jax.ShapeDtypeStruct((t, ATT_WIDTH), BF16)]
    out_specs = [pl.BlockSpec((tm, kw), lambda i: (i, 0)),
                 pl.BlockSpec((tm, ATT_WIDTH), lambda i: (i, 0))]
    if full:
        out_shape += [jax.ShapeDtypeStruct((t, kw), BF16),
                      jax.ShapeDtypeStruct((t, 2 * FOURIER_WIDTH), BF16),
                      jax.ShapeDtypeStruct((t, POOL_WIDTH), F32),
                      jax.ShapeDtypeStruct((t, N_BRANCH * d), BF16)]
        out_specs += [pl.BlockSpec((tm, kw), lambda i: (i, 0)),
                      pl.BlockSpec((tm, 2 * FOURIER_WIDTH), lambda i: (i, 0)),
                      pl.BlockSpec((tm, POOL_WIDTH), lambda i: (i, 0)),
                      pl.BlockSpec((tm, N_BRANCH * d), lambda i: (i, 0))]
    return pl.pallas_call(
        functools.partial(_in_kernel, d, full),
        out_shape=out_shape,
        grid=(t // tm,),
        in_specs=in_specs,
        out_specs=out_specs,
        compiler_params=_params(("arbitrary",)),
        name="in_proj" if full else "ctx_kv_proj",
    )(x2d, mod, lw["n1"], wall, lw["qg"], lw["wq"], lw["wqp"], lw["kvg"], lw["wk"], lw["wv"],
      lw["fc"], lw["bg"], cq, sq, ck, sk)


def _attn_kernel(has_ctx, q_ref, k_ref, v_ref, *rest):
    if has_ctx:
        kc_ref, vc_ref, o_ref = rest
    else:
        (o_ref,) = rest
    outs = []
    for hh in range(2):
        sl = slice(hh * HEAD_PAD, (hh + 1) * HEAD_PAD)
        q = q_ref[:, sl]
        s = _dot_nt(q, k_ref[:, sl])
        mx = jnp.max(s, axis=-1, keepdims=True)
        if has_ctx:
            sc = _dot_nt(q, kc_ref[:, sl])
            mx = jnp.maximum(mx, jnp.max(sc, axis=-1, keepdims=True))
        p = jnp.exp(s - mx)
        den = jnp.sum(p, axis=-1, keepdims=True)
        o = _dot(p.astype(BF16), v_ref[...])
        if has_ctx:
            pc = jnp.exp(sc - mx)
            den = den + jnp.sum(pc, axis=-1, keepdims=True)
            o = o + _dot(pc.astype(BF16), vc_ref[...])
        outs.append(o / den)
    lane = lax.broadcasted_iota(I32, outs[0].shape, 1)
    o_ref[...] = jnp.where(lane < V_HEAD, outs[0], outs[1]).astype(BF16)


def _attention(q, k, v, batch, seq_len, ctx_kv):
    t = q.shape[0]
    tq = min(ATT_Q_TILE, seq_len)
    nq = seq_len // tq
    pair_w = 2 * HEAD_PAD
    in_specs = [pl.BlockSpec((tq, pair_w), lambda b, j, i: (b * nq + i, j)),
                pl.BlockSpec((seq_len, pair_w), lambda b, j, i: (b, j)),
                pl.BlockSpec((seq_len, 2 * V_HEAD), lambda b, j, i: (b, j))]
    args = [q, k, v]
    if ctx_kv is not None:
        kc, vc = ctx_kv
        lc = kc.shape[0] // batch
        in_specs += [pl.BlockSpec((lc, pair_w), lambda b, j, i: (b, j)),
                     pl.BlockSpec((lc, 2 * V_HEAD), lambda b, j, i: (b, j))]
        args += [kc, vc]
    return pl.pallas_call(
        functools.partial(_attn_kernel, ctx_kv is not None),
        out_shape=jax.ShapeDtypeStruct((t, ATT_WIDTH), BF16),
        grid=(batch, N_HEADS // 2, nq),
        in_specs=in_specs,
        out_specs=pl.BlockSpec((tq, 2 * V_HEAD), lambda b, j, i: (b * nq + i, j)),
        compiler_params=_params(("arbitrary", "arbitrary", "arbitrary")),
        name="attention" if ctx_kv is not None else "ctx_attention",
    )(*args)


def _fourier_kernel(norm, c_ref, s_ref, ab_ref, o_ref):
    a = ab_ref[:, 0:FOURIER_WIDTH]
    b = ab_ref[:, FOURIER_WIDTH:2 * FOURIER_WIDTH]
    o = _dot(c_ref[...], a) - _dot(s_ref[...], b)
    o_ref[...] = (o * norm).astype(BF16)


def _fourier(ab, batch, seq_len, dft):
    t = ab.shape[0]
    tm = min(FOUR_TILE, seq_len)
    nt = seq_len // tm
    cl, sl = dft
    norm = 1.0 / math.sqrt(seq_len * FOURIER_WIDTH)
    return pl.pallas_call(
        functools.partial(_fourier_kernel, norm),
        out_shape=jax.ShapeDtypeStruct((t, FOURIER_WIDTH), BF16),
        grid=(nt, batch),
        in_specs=[pl.BlockSpec((tm, seq_len), lambda i, b: (i, 0)),
                  pl.BlockSpec((tm, seq_len), lambda i, b: (i, 0)),
                  pl.BlockSpec((seq_len, 2 * FOURIER_WIDTH), lambda i, b: (b, 0))],
        out_specs=pl.BlockSpec((tm, FOURIER_WIDTH), lambda i, b: (b * nt + i, 0)),
        compiler_params=_params(("arbitrary", "arbitrary")),
        name="fourier",
    )(cl, sl, ab)


def _pool_kernel(seq_len, z_ref, wg_ref, ps_ref, o_ref, pad_ref, s_ref):
    n = seq_len
    p = n + 2 * POOL_PAD
    z = z_ref[...]
    zeros = jnp.zeros((POOL_PAD, POOL_WIDTH), F32)
    pad_ref[0:POOL_PAD, :] = zeros
    pad_ref[POOL_PAD + n:p, :] = zeros
    pad_ref[POOL_PAD:POOL_PAD + n, :] = z
    s_ref[0:p - 1, :] = pad_ref[0:p - 1, :] + pad_ref[1:p, :]
    w2 = s_ref[POOL_PAD - 1:POOL_PAD - 1 + n, :]
    pad_ref[0:p - 3, :] = s_ref[0:p - 3, :] + s_ref[2:p - 1, :]
    w4 = pad_ref[POOL_PAD - 2:POOL_PAD - 2 + n, :]
    s_ref[0:p - 7, :] = pad_ref[0:p - 7, :] + pad_ref[4:p - 3, :]
    w8 = s_ref[POOL_PAD - 4:POOL_PAD - 4 + n, :]
    pad_ref[0:p - 15, :] = s_ref[0:p - 15, :] + s_ref[8:p - 7, :]
    w16 = pad_ref[0:n, :]
    pos = lax.broadcasted_iota(I32, (n, POOL_WIDTH), 0)
    grp = lax.broadcasted_iota(I32, (n, POOL_WIDTH), 1) // POOL_GROUP
    win = jnp.where(grp == 0, w2, jnp.where(grp == 1, w4, jnp.where(grp == 2, w8, w16)))
    half = jnp.where(grp == 0, 1, jnp.where(grp == 1, 2, jnp.where(grp == 2, 4, 8)))
    lo = jnp.maximum(pos - half, 0)
    hi = jnp.minimum(pos + half, n)
    cnt = (hi - lo).astype(F32)
    pooled = win / cnt - z
    y = _dot(pooled.astype(BF16), wg_ref[...])
    o_ref[...] = (y * ps_ref[...]).astype(BF16)


def _pool(zp, batch, seq_len, wg_bd, pool_scale):
    t = zp.shape[0]
    return pl.pallas_call(
        functools.partial(_pool_kernel, seq_len),
        out_shape=jax.ShapeDtypeStruct((t, POOL_WIDTH), BF16),
        grid=(batch,),
        in_specs=[pl.BlockSpec((seq_len, POOL_WIDTH), lambda b: (b, 0)),
                  pl.BlockSpec((POOL_WIDTH, POOL_WIDTH), lambda b: (0, 0)),
                  pl.BlockSpec((1, POOL_WIDTH), lambda b: (0, 0))],
        out_specs=pl.BlockSpec((seq_len, POOL_WIDTH), lambda b: (b, 0)),
        scratch_shapes=[pltpu.VMEM((seq_len + 2 * POOL_PAD, POOL_WIDTH), F32),
                        pltpu.VMEM((seq_len + 2 * POOL_PAD, POOL_WIDTH), F32)],
        compiler_params=_params(("arbitrary",)),
        name="pool",
    )(zp, wg_bd, pool_scale)


def _out_kernel(d, x_ref, att_ref, four_ref, pool_ref, g_ref, m_ref, woa_ref, wob_ref, woc_ref,
                wout_ref, n2_ref, wpq_ref, x1_ref, h2_ref, pq_ref):
    m = m_ref[0]
    ya = _dot(att_ref[...], woa_ref[...])
    yb = _dot(four_ref[...], wob_ref[...])
    yc = _dot(pool_ref[...], woc_ref[...])
    mixp = (g_ref[:, 0:d].astype(F32) * ya + g_ref[:, d:2 * d].astype(F32) * yb
            + g_ref[:, 2 * d:3 * d].astype(F32) * yc)
    mix = _dot(mixp.astype(BF16), wout_ref[...])
    x1 = x_ref[...] + m[:, 2 * d:3 * d] * mix
    x1_ref[...] = x1
    h2 = _rms(x1) * n2_ref[...] * (1.0 + m[:, 4 * d:5 * d]) + m[:, 3 * d:4 * d]
    h2_ref[...] = h2
    pq = _dot(h2.astype(BF16), wpq_ref[...])
    for hp in range(2 * PEER_HEADS):
        pq_ref[hp] = pq[:, hp * PEER_HALF:(hp + 1) * PEER_HALF].astype(BF16)


def _out_proj(x2d, att, four, pool, g, mod, mod_row, seq_len, lw):
    t, d = x2d.shape
    tm = min(TOK_TILE, seq_len)
    tiles_per_seq = seq_len // tm

    def const(shape):
        return pl.BlockSpec(shape, lambda i: (0,) * len(shape))

    def row(w):
        return pl.BlockSpec((tm, w), lambda i: (i, 0))

    nhp = 2 * PEER_HEADS
    return pl.pallas_call(
        functools.partial(_out_kernel, d),
        out_shape=[jax.ShapeDtypeStruct((t, d), F32), jax.ShapeDtypeStruct((t, d), F32),
                   jax.ShapeDtypeStruct((nhp, t, PEER_HALF), BF16)],
        grid=(t // tm,),
        in_specs=[row(d), row(ATT_WIDTH), row(FOURIER_WIDTH), row(POOL_WIDTH), row(N_BRANCH * d),
                  pl.BlockSpec((1, 1, mod.shape[-1]), lambda i: (mod_row(i, tiles_per_seq), 0, 0)),
                  const(lw["woa"].shape), const(lw["wob"].shape), const(lw["woc"].shape),
                  const(lw["wout"].shape), const((1, d)), const(lw["wpq"].shape)],
        out_specs=[row(d), row(d), pl.BlockSpec((nhp, tm, PEER_HALF), lambda i: (0, i, 0))],
        compiler_params=_params(("arbitrary",)),
        name="out_proj",
    )(x2d, att, four, pool, g, mod, lw["woa"], lw["wob"], lw["woc"], lw["wout"], lw["n2"],
      lw["wpq"])


def _select_round(s, iota, n):
    mx = jnp.max(s, axis=0, keepdims=True)
    idx = jnp.min(jnp.where(s == mx, iota, n), axis=0, keepdims=True)
    hit = iota == idx
    return mx, idx, hit, jnp.where(hit, -jnp.inf, s)


def _route_kernel(pq_ref, keys_ref, eidx_ref, gate_ref, sv_a, si_a, sv_b, si_b, cand_ref, cidx_ref,
                  ts_ref, eidx_s, gate_s):
    tm = pq_ref.shape[1]
    iota_k = lax.broadcasted_iota(I32, (N_KEYS, tm), 0).astype(F32)
    iota_c = lax.broadcasted_iota(I32, (CAND_ROWS, tm), 0).astype(F32)
    cand_ref[N_CAND:CAND_ROWS, :] = jnp.full((CAND_ROWS - N_CAND, tm), -jnp.inf, F32)
    cidx_ref[N_CAND:CAND_ROWS, :] = jnp.zeros((CAND_ROWS - N_CAND, tm), F32)

    def sub_key_topk(hd, sv_ref, si_ref):
        s = [_dot_nt(keys_ref[2 * hd + p], pq_ref[2 * hd + p]) for p in range(2)]
        for r in range(PEER_TOPK):
            for p in range(2):
                mx, idx, _, s[p] = _select_round(s[p], iota_k, float(N_KEYS))
                sv_ref[p, r:r + 1, :] = mx
                si_ref[p, r:r + 1, :] = idx

    def pair_topk(hd, sv_ref, si_ref):
        off = 0
        for i, cnt in STAIR:
            cand_ref[off:off + cnt, :] = sv_ref[0, i:i + 1, :] + sv_ref[1, 0:cnt, :]
            cidx_ref[off:off + cnt, :] = si_ref[0, i:i + 1, :] * N_KEYS + si_ref[1, 0:cnt, :]
            off += cnt
        cidx = cidx_ref[...]
        c = cand_ref[...]
        base = pl.multiple_of(hd * PEER_TOPK, PEER_TOPK)
        for r in range(PEER_TOPK):
            mx, _, hit, c = _select_round(c, iota_c, float(CAND_ROWS))
            ts_ref[r:r + 1, :] = mx
            expert = jnp.sum(jnp.where(hit, cidx, 0.0), axis=0, keepdims=True)
            eidx_s[pl.ds(base + r, 1), :] = expert.astype(I32)
        ts = ts_ref[...]
        ex = jnp.exp(ts - ts[0:1, :])
        gate_s[pl.ds(base, PEER_TOPK), :] = ex / jnp.sum(ex, axis=0, keepdims=True)

    sub_key_topk(0, sv_a, si_a)

    def two_heads(j, carry):
        hd = 2 * j
        sub_key_topk(hd + 1, sv_b, si_b)
        pair_topk(hd, sv_a, si_a)
        sub_key_topk(hd + 2, sv_a, si_a)
        pair_topk(hd + 1, sv_b, si_b)
        return carry

    lax.fori_loop(0, PEER_HEADS // 2 - 1, two_heads, 0)
    sub_key_topk(PEER_HEADS - 1, sv_b, si_b)
    pair_topk(PEER_HEADS - 2, sv_a, si_a)
    pair_topk(PEER_HEADS - 1, sv_b, si_b)
    eidx_ref[...] = eidx_s[...].T
    gate_ref[...] = gate_s[...].T


def _route(pq, keys):
    nhp, t, _ = pq.shape
    tm = ROUTE_TILE
    lists = pltpu.VMEM((2, PEER_TOPK, tm), F32)
    return pl.pallas_call(
        _route_kernel,
        out_shape=[jax.ShapeDtypeStruct((t, PEER_SLOTS), I32),
                   jax.ShapeDtypeStruct((t, PEER_SLOTS), F32)],
        grid=(t // tm,),
        in_specs=[pl.BlockSpec((nhp, tm, PEER_HALF), lambda i: (0, i, 0)),
                  pl.BlockSpec((nhp, N_KEYS, PEER_HALF), lambda i: (0, 0, 0))],
        out_specs=[pl.BlockSpec((tm, PEER_SLOTS), lambda i: (i, 0)),
                   pl.BlockSpec((tm, PEER_SLOTS), lambda i: (i, 0))],
        scratch_shapes=[lists, lists, lists, lists,
                        pltpu.VMEM((CAND_ROWS, tm), F32),
                        pltpu.VMEM((CAND_ROWS, tm), F32),
                        pltpu.VMEM((PEER_TOPK, tm), F32),
                        pltpu.VMEM((PEER_SLOTS, tm), I32), pltpu.VMEM((PEER_SLOTS, tm), F32)],
        compiler_params=_params(("arbitrary",)),
        name="peer_route",
    )(pq, keys)


def _fold_rows(a, b, keep_a, shift):
    return jnp.where(keep_a, a + pltpu.roll(a, SUBLANES - shift, 0), b + pltpu.roll(b, shift, 0))


def _peer_kernel(final, idx_ref, idxn_ref, gate_ref, h_ref, x_ref, g2_ref, fg_ref, tab_ref,
                 o_ref, buf_a, buf_b, act_ref, sem):
    i = pl.program_id(0)
    n = pl.num_programs(0)
    grp_tok = PEER_TOK // PEER_WAIT_GROUPS
    grp_rows = grp_tok * PEER_SLOTS
    sub = lax.broadcasted_iota(I32, (SUBLANES, LANES), 0)
    keep = {sh: (sub & sh) == 0 for sh in (4, 2, 1)}
    g2 = g2_ref[0, 0]
    gate_t = gate_ref[...].T

    def row_copy(ids, t_src, t_dst, k, buf, s):
        return pltpu.make_async_copy(tab_ref.at[ids[t_src, k]], buf.at[t_dst * PEER_SLOTS + k],
                                     sem.at[s, t_dst // grp_tok])

    def issue_token(ids, t_src, t_dst, buf, s):
        for k in range(PEER_SLOTS):
            row_copy(ids, t_src, t_dst, k, buf, s).start(priority=k % 2)

    def wait_group(buf, s, q):
        pltpu.make_async_copy(tab_ref.at[pl.ds(0, grp_rows)],
                              buf.at[pl.ds(q * grp_rows, grp_rows)], sem.at[s, q]).wait()

    def token(buf, tl, tb):
        r0 = tl * PEER_SLOTS
        hv = h_ref[tb]
        groups = []
        for g in range(PEER_SLOTS // SUBLANES):
            p = [buf[r0 + g * SUBLANES + j].astype(F32)[0:SUBLANES] * hv for j in range(SUBLANES)]
            for sh in (4, 2, 1):
                half = len(p) // 2
                p = [_fold_rows(p[j], p[j + half], keep[sh], sh) for j in range(half)]
            groups.append(p[0])
        part = jnp.concatenate(groups, axis=0)
        a = jnp.sum(part, axis=1, keepdims=True)
        act = gate_t[:, tb:tb + 1] * jax.nn.gelu(a)
        act_ref[...] = jnp.broadcast_to(act, (PEER_SLOTS, LANES))
        accs = [jnp.zeros((SUBLANES, LANES), F32) for _ in range(4)]
        for k in range(PEER_SLOTS):
            up = buf[r0 + k].astype(F32)[SUBLANES:2 * SUBLANES]
            accs[k % 4] = accs[k % 4] + act_ref[k:k + 1, :] * up
        xr = x_ref[tb] + g2 * ((accs[0] + accs[1]) + (accs[2] + accs[3]))
        if final:
            ms = jnp.sum(jnp.sum(xr * xr, axis=1, keepdims=True), axis=0, keepdims=True)
            xr = xr * lax.rsqrt(ms / (SUBLANES * LANES) + EPS) * fg_ref[...]
        o_ref[tb] = xr

    @pl.when(i == 0)
    def _():
        def body(t, carry):
            issue_token(idx_ref, t, t, buf_a, 0)
            return carry
        lax.fori_loop(0, PEER_TOK, body, 0)

    for t in range(PEER_TOK):
        if t % grp_tok == 0:
            wait_group(buf_a, 0, t // grp_tok)
        issue_token(idx_ref, PEER_TOK + t, t, buf_b, 1)
        token(buf_a, t, t)
    for t in range(PEER_TOK):
        if t % grp_tok == 0:
            wait_group(buf_b, 1, t // grp_tok)
        issue_token(idxn_ref, t, t, buf_a, 0)
        token(buf_b, t, PEER_TOK + t)

    @pl.when(i == n - 1)
    def _():
        for q in range(PEER_WAIT_GROUPS):
            wait_group(buf_a, 0, q)


def _peer(n_tok, eidx, gate, h3, x3, g2, mod_row, seq_len, fg, table, final):
    t = n_tok
    tb = 2 * PEER_TOK
    nb = t // tb
    steps_per_seq = seq_len // tb
    rows = PEER_TOK * PEER_SLOTS
    tile = (SUBLANES, LANES)
    pair = (2 * SUBLANES, LANES)
    return pl.pallas_call(
        functools.partial(_peer_kernel, final),
        out_shape=jax.ShapeDtypeStruct((t,) + tile, F32),
        grid=(nb,),
        in_specs=[pl.BlockSpec((tb, PEER_SLOTS), lambda i: (i, 0), memory_space=pltpu.SMEM),
                  pl.BlockSpec((tb, PEER_SLOTS), lambda i: (jnp.minimum(i + 1, nb - 1), 0),
                               memory_space=pltpu.SMEM),
                  pl.BlockSpec((tb, PEER_SLOTS), lambda i: (i, 0)),
                  pl.BlockSpec((tb,) + tile, lambda i: (i, 0, 0)),
                  pl.BlockSpec((tb,) + tile, lambda i: (i, 0, 0)),
                  pl.BlockSpec((1, 1) + tile, lambda i: (mod_row(i, steps_per_seq), 0, 0, 0)),
                  pl.BlockSpec(tile, lambda i: (0, 0)),
                  pl.BlockSpec(memory_space=pl.ANY)],
        out_specs=pl.BlockSpec((tb,) + tile, lambda i: (i, 0, 0)),
        scratch_shapes=[pltpu.VMEM((rows,) + pair, BF16), pltpu.VMEM((rows,) + pair, BF16),
                        pltpu.VMEM((PEER_SLOTS, LANES), F32),
                        pltpu.SemaphoreType.DMA((2, PEER_WAIT_GROUPS))],
        compiler_params=pltpu.CompilerParams(dimension_semantics=("arbitrary",),
                                             vmem_limit_bytes=PEER_VMEM_LIMIT),
        name="peer_experts",
    )(eidx, eidx, gate, h3, x3, g2, fg, table)


def _peer_sc_body(first, n_tok, d, eidx_hbm, gate_hbm, h_hbm, tab_hbm, y_hbm, idx_v, idx_n, gate_v, gate_n,
                  h_v, h_n, rows_a, rows_b, out_v, sem_a, sem_b, sem_n):
    wid = lax.axis_index("s") * SC_CORES + lax.axis_index("c")
    per_worker = n_tok // SC_WORKERS
    tok0 = first + wid * per_worker
    nj = d // SC_LANES
    lane = lax.iota(I32, SC_LANES)
    zero = jnp.zeros((SC_LANES,), F32)
    hi_mask = jnp.full((SC_LANES,), 0xFFFF0000, jnp.uint32)

    def gather(idx_ref, c, rows, sem):
        row0 = pl.multiple_of(c * SC_CHUNK, SC_CHUNK)
        return pltpu.make_async_copy(tab_hbm.at[idx_ref.at[pl.ds(row0, SC_CHUNK)]], rows, sem)

    def copy_words(src, dst, n):
        def step(j, carry):
            off = pl.multiple_of(j * SC_LANES, SC_LANES)
            dst[pl.ds(off, SC_LANES)] = src[pl.ds(off, SC_LANES)]
            return carry
        lax.fori_loop(0, n // SC_LANES, step, 0)

    def compute(c, rows):
        row0 = pl.multiple_of(c * SC_CHUNK, SC_CHUNK)

        def down_step(j, accs):
            off = pl.multiple_of(j * SC_LANES, SC_LANES)
            hj = h_v[pl.ds(off, SC_LANES)]
            out = []
            for r in range(SC_CHUNK):
                w = rows[r, pl.ds(off, SC_LANES)]
                dn = lax.bitcast_convert_type(w & hi_mask, F32)
                out.append(accs[r] + dn * hj)
            return tuple(out)

        accs = lax.fori_loop(0, nj, down_step, tuple(zero for _ in range(SC_CHUNK)))
        acts = []
        for g in range(SC_CHUNK // SC_LANES):
            a = zero
            for r in range(SC_LANES):
                a = jnp.where(lane == r, jnp.sum(accs[g * SC_LANES + r]), a)
            gt = gate_v[pl.ds(row0 + g * SC_LANES, SC_LANES)]
            u = GELU_C * (a + 0.044715 * (a * a * a))
            th = 1.0 - 2.0 / (jnp.exp(2.0 * u) + 1.0)
            act = gt * (0.5 * a * (1.0 + th))
            for r in range(SC_LANES):
                acts.append(jnp.sum(jnp.where(lane == r, act, 0.0)))

        def up_step(j, carry):
            off = pl.multiple_of(j * SC_LANES, SC_LANES)
            o = out_v[pl.ds(off, SC_LANES)]
            for r in range(SC_CHUNK):
                w = rows[r, pl.ds(off, SC_LANES)]
                up = lax.bitcast_convert_type(w << 16, F32)
                o = o + acts[r] * up
            out_v[pl.ds(off, SC_LANES)] = o
            return carry

        lax.fori_loop(0, nj, up_step, 0)

    def token(ti, carry):
        t = tok0 + ti
        tn = jnp.minimum(t + 1, tok0 + per_worker - 1)

        def clear(j, carry2):
            out_v[pl.ds(pl.multiple_of(j * SC_LANES, SC_LANES), SC_LANES)] = zero
            return carry2

        lax.fori_loop(0, nj, clear, 0)

        def pair(pp, carry2):
            gather(idx_v, 2 * pp + 1, rows_b, sem_b).start()
            gather(idx_v, 2 * pp, rows_a, sem_a).wait()
            compute(2 * pp, rows_a)

            @pl.when(pp == 0)
            def _():
                gather(idx_v, 2, rows_a, sem_a).start()

            @pl.when(pp == 1)
            def _():
                pltpu.sync_copy(eidx_hbm.at[tn], idx_n)
                gather(idx_n, 0, rows_a, sem_a).start()
                pltpu.make_async_copy(gate_hbm.at[tn], gate_n, sem_n).start()
                pltpu.make_async_copy(h_hbm.at[tn], h_n, sem_n).start()

            gather(idx_v, 2 * pp + 1, rows_b, sem_b).wait()
            compute(2 * pp + 1, rows_b)
            return carry2

        lax.fori_loop(0, PEER_SLOTS // (2 * SC_CHUNK), pair, 0)
        pltpu.sync_copy(out_v, y_hbm.at[t - first])
        pltpu.make_async_copy(gate_hbm.at[tn], gate_n, sem_n).wait()
        pltpu.make_async_copy(h_hbm.at[tn], h_n, sem_n).wait()
        copy_words(idx_n, idx_v, PEER_SLOTS)
        copy_words(gate_n, gate_v, PEER_SLOTS)
        copy_words(h_n, h_v, d)
        return carry

    pltpu.sync_copy(eidx_hbm.at[tok0], idx_v)
    pltpu.sync_copy(gate_hbm.at[tok0], gate_v)
    pltpu.sync_copy(h_hbm.at[tok0], h_v)
    gather(idx_v, 0, rows_a, sem_a).start()
    lax.fori_loop(0, per_worker, token, 0)
    gather(idx_v, 0, rows_a, sem_a).wait()


def _peer_sc(first, n_tok, eidx, gate, h2, sc_table):
    d = h2.shape[1]
    mesh = plsc.VectorSubcoreMesh(core_axis_name="c", subcore_axis_name="s",
                                  num_cores=SC_CORES, num_subcores=SC_SUBCORES)
    return pl.kernel(
        functools.partial(_peer_sc_body, first, n_tok, d),
        out_type=jax.ShapeDtypeStruct((n_tok, d), F32),
        mesh=mesh,
        scratch_types=[pltpu.VMEM((PEER_SLOTS,), I32), pltpu.VMEM((PEER_SLOTS,), I32),
                       pltpu.VMEM((PEER_SLOTS,), F32), pltpu.VMEM((PEER_SLOTS,), F32),
                       pltpu.VMEM((d,), F32), pltpu.VMEM((d,), F32),
                       pltpu.VMEM((SC_CHUNK, d), jnp.uint32), pltpu.VMEM((SC_CHUNK, d), jnp.uint32),
                       pltpu.VMEM((d,), F32), pltpu.SemaphoreType.DMA, pltpu.SemaphoreType.DMA,
                       pltpu.SemaphoreType.DMA],
        compiler_params=pltpu.CompilerParams(needs_layout_passes=False),
        cost_estimate=pl.CostEstimate(
            flops=4 * n_tok * PEER_SLOTS * d, transcendentals=n_tok * PEER_SLOTS,
            bytes_accessed=n_tok * (PEER_SLOTS * d * 4 + 2 * d * 4 + 2 * PEER_SLOTS * 4)),
        name="peer_experts_sc",
    )(eidx, gate, h2, sc_table)


def _finish_kernel(final, d, y_ref, x_ref, m_ref, fg_ref, o_ref):
    xr = x_ref[...] + m_ref[0][:, 5 * d:6 * d] * y_ref[...]
    if final:
        xr = _rms(xr) * fg_ref[...]
    o_ref[...] = xr


def _peer_finish(y, x1, mod, mod_row, seq_len, tok0, fg, final):
    t, d = y.shape
    tm = min(TOK_TILE, seq_len)
    tiles_per_seq = seq_len // tm
    tile0 = tok0 // tm
    return pl.pallas_call(
        functools.partial(_finish_kernel, final, d),
        out_shape=jax.ShapeDtypeStruct((t, d), F32),
        grid=(t // tm,),
        in_specs=[pl.BlockSpec((tm, d), lambda i: (i, 0)),
                  pl.BlockSpec((tm, d), lambda i: (i + tile0, 0)),
                  pl.BlockSpec((1, 1, mod.shape[-1]),
                               lambda i: (mod_row(i + tile0, tiles_per_seq), 0, 0)),
                  pl.BlockSpec((1, d), lambda i: (0, 0))],
        out_specs=pl.BlockSpec((tm, d), lambda i: (i, 0)),
        compiler_params=_params(("arbitrary",)),
        name="peer_finish",
    )(y, x1, mod, fg.reshape(1, d))


def _rope_perm(w):
    q = QK_ROPE // 4
    a1, a2, b1, b2 = (w[..., j * q:(j + 1) * q] for j in range(4))
    return jnp.concatenate([-a2, a1, -b2, b1], axis=-1)


def _prep_layer(l, d, w_in, b_gate, q_norm_g, w_uq, kv_norm_g, w_ukv, w_oa, w_ob, w_grp,
                pool_scale, w_oc, w_out, w_pq, norm1_g, norm2_g):
    wi = w_in[l]
    s0 = Q_LORA
    s1 = s0 + KV_LORA
    s2 = s1 + QK_ROPE
    s3 = s2 + FOURIER_WIDTH
    s4 = s3 + POOL_WIDTH
    w_kr = wi[:, s1:s2]
    zl = jnp.zeros((d, QK_NOPE), F32)
    zr = jnp.zeros((d, HEAD_PAD - QK_NOPE - QK_ROPE), F32)
    wall = jnp.concatenate([wi[:, 0:s1], wi[:, s2:], zl, w_kr, zr, zl, _rope_perm(w_kr), zr],
                           axis=1).astype(BF16)
    wq = w_uq[l].reshape(Q_LORA, N_HEADS, QK_NOPE + QK_ROPE)
    qpad = jnp.zeros((Q_LORA, N_HEADS, HEAD_PAD - QK_NOPE - QK_ROPE), F32)
    wq_full = jnp.concatenate([wq, qpad], axis=-1).reshape(Q_LORA, N_HEADS * HEAD_PAD)
    wq_perm = jnp.concatenate([jnp.zeros((Q_LORA, N_HEADS, QK_NOPE), F32),
                               _rope_perm(wq[..., QK_NOPE:]), qpad],
                              axis=-1).reshape(Q_LORA, N_HEADS * HEAD_PAD)
    wkv = w_ukv[l].reshape(KV_LORA, N_HEADS, QK_NOPE + V_HEAD)
    wk = jnp.concatenate([wkv[..., :QK_NOPE],
                          jnp.zeros((KV_LORA, N_HEADS, HEAD_PAD - QK_NOPE), F32)],
                         axis=-1).reshape(KV_LORA, N_HEADS * HEAD_PAD)
    wv = wkv[..., QK_NOPE:].reshape(KV_LORA, ATT_WIDTH)
    cidx = jnp.arange(FOURIER_WIDTH, dtype=I32)
    ang = (2.0 * math.pi / FOURIER_WIDTH) * ((cidx[:, None] * cidx[None, :]) % FOURIER_WIDTH
                                             ).astype(F32)
    fc = jnp.concatenate([jnp.cos(ang), jnp.sin(ang)], axis=1).astype(BF16)
    ng = len(POOL_WINDOWS)
    wg = jnp.zeros((ng, POOL_GROUP, ng, POOL_GROUP), F32)
    for gi in range(ng):
        wg = wg.at[gi, :, gi, :].set(w_grp[l, gi])
    return {
        "wall": wall, "n1": norm1_g[l][None, :], "n2": norm2_g[l][None, :],
        "qg": q_norm_g[l][None, :], "kvg": kv_norm_g[l][None, :],
        "wq": wq_full.astype(BF16), "wqp": wq_perm.astype(BF16),
        "wk": wk.astype(BF16), "wv": wv.astype(BF16), "fc": fc,
        "bg": b_gate[l][None, :],
        "wg": wg.reshape(POOL_WIDTH, POOL_WIDTH).astype(BF16),
        "ps": pool_scale[l][None, :],
        "woa": w_oa[l].astype(BF16), "wob": w_ob[l].astype(BF16), "woc": w_oc[l].astype(BF16),
        "wout": w_out[l].astype(BF16), "wpq": w_pq[l].astype(BF16),
    }


def _bf16_bits(w):
    return lax.bitcast_convert_type(w.astype(BF16), jnp.uint16).astype(jnp.uint32)


def _rope_tables(seq_len, rope):
    zeros_n = jnp.zeros((seq_len, QK_NOPE), F32)
    zeros_p = jnp.zeros((seq_len, HEAD_PAD - QK_NOPE - QK_ROPE), F32)
    ones_n = jnp.ones((seq_len, QK_NOPE), F32)
    if rope:
        pos = jnp.arange(seq_len, dtype=I32)
        half = QK_ROPE // 2
        inv_freq = ROPE_BASE ** (-jnp.arange(0, half, 2, dtype=F32) / half)
        ang_r = (pos // GRID_W).astype(F32)[:, None] * inv_freq
        ang_c = (pos % GRID_W).astype(F32)[:, None] * inv_freq
        cos = jnp.concatenate([jnp.cos(ang_r)] * 2 + [jnp.cos(ang_c)] * 2, axis=1)
        sin = jnp.concatenate([jnp.sin(ang_r)] * 2 + [jnp.sin(ang_c)] * 2, axis=1)
    else:
        cos = jnp.ones((seq_len, QK_ROPE), F32)
        sin = jnp.zeros((seq_len, QK_ROPE), F32)
    cq = jnp.concatenate([ones_n, cos, zeros_p], axis=1) * ATT_SCALE
    sq = jnp.concatenate([zeros_n, sin, zeros_p], axis=1) * ATT_SCALE
    ck = jnp.concatenate([zeros_n, cos, zeros_p], axis=1)
    sk = jnp.concatenate([zeros_n, sin, zeros_p], axis=1)
    return cq, sq, ck, sk


def _dft(seq_len):
    n = seq_len
    a = DFT_SPLIT if n % DFT_SPLIT == 0 else 1
    b = n // a
    k = jnp.arange(n, dtype=I32)[None, :]
    ang_a = (2.0 * math.pi / a) * ((jnp.arange(a, dtype=I32)[:, None] * k) % a).astype(F32)
    ang_b = (2.0 * math.pi / n) * ((jnp.arange(b, dtype=I32)[:, None] * k) % n).astype(F32)
    ca, sa = jnp.cos(ang_a)[:, None, :], jnp.sin(ang_a)[:, None, :]
    cb, sb = jnp.cos(ang_b)[None, :, :], jnp.sin(ang_b)[None, :, :]
    cos = (ca * cb - sa * sb).reshape(n, n)
    sin = (sa * cb + ca * sb).reshape(n, n)
    return cos.astype(BF16), sin.astype(BF16)


def _peer_block(x1, h2, pq, keys, mod, mod_row, seq_len, fg, table, sc_table, n_sc, final):
    t, d = x1.shape
    tile = (SUBLANES, LANES)
    eidx, gate = _route(pq, keys)
    n_tc = t - n_sc
    mod_tiles = mod.reshape(MOD_ROWS, 6, SUBLANES, LANES)[:, 5:6]
    out = _peer(n_tc, eidx, gate, h2.reshape((t,) + tile), x1.reshape((t,) + tile), mod_tiles,
                mod_row, seq_len, fg.reshape(tile), table, final).reshape(n_tc, d)
    if n_sc == 0:
        return out, ()
    y = _peer_sc(n_tc, n_sc, eidx, gate, h2, sc_table)
    out_sc = _peer_finish(y, x1, mod, mod_row, seq_len, n_tc, fg, final)
    return jnp.concatenate([out, out_sc], axis=0), (out, eidx, gate, h2)


def kernel(x, c, ctx, c_ctx, w_mod, b_mod, norm1_g, norm2_g, w_in, b_gate, q_norm_g, w_uq,
           kv_norm_g, w_ukv, w_oa, w_ob, w_grp, pool_scale, w_oc, w_out, w_pq, peer_keys,
           peer_down, peer_up, final_g):
    batch, seq_len, d = x.shape
    ctx_len = ctx.shape[1]
    depth = w_mod.shape[0]
    assert d == SUBLANES * LANES and batch + 1 <= MOD_ROWS
    assert seq_len % TOK_TILE == 0 and ctx_len % (2 * PEER_TOK) == 0 and ctx_len % LANES == 0
    tile = (SUBLANES, LANES)

    cvec = jnp.concatenate([c, c_ctx[None, :], jnp.zeros((MOD_ROWS - batch - 1, d), F32)], axis=0)
    mod_all = _modulation(cvec, w_mod, b_mod)

    def x_row(b0):
        return lambda i, per_seq: b0 + i // per_seq

    def c_row(i, per_seq):
        return batch

    tabs_x = _rope_tables(seq_len, True)
    tabs_c = _rope_tables(ctx_len, False)
    dft_x = _dft(seq_len)
    dft_c = _dft(ctx_len)

    n_chains = PEER_CHAINS if batch % PEER_CHAINS == 0 else 1
    bpc = batch // n_chains
    use_sc = bpc * seq_len >= TOK_TILE + max(PEER_SC_TOKENS, PEER_SC_TOKENS_FIRST,
                                             PEER_SC_TOKENS_MIDDLE, PEER_SC_TOKENS_LAYER_END)

    def sc_tokens(l, j):
        if not use_sc:
            return 0
        if j == n_chains - 1:
            return PEER_SC_TOKENS if l == depth - 1 else PEER_SC_TOKENS_LAYER_END
        return PEER_SC_TOKENS_FIRST if (l, j) == (0, 0) else PEER_SC_TOKENS_MIDDLE

    tables = [jnp.concatenate([peer_down[l].reshape((-1,) + tile),
                               peer_up[l].reshape((-1,) + tile)], axis=1).astype(BF16)
              for l in range(depth)]
    sc_tables = [(_bf16_bits(peer_down[l]) << 16) | _bf16_bits(peer_up[l]) for l in range(depth)]
    x, ctx, tables, sc_tables = lax.optimization_barrier((x, ctx, tables, sc_tables))

    chains = [x[j * bpc:(j + 1) * bpc].reshape(bpc * seq_len, d) for j in range(n_chains)]
    cs = ctx.reshape(batch * ctx_len, d)
    sc_args = ()
    for l in range(depth):
        last = l == depth - 1
        lw = _prep_layer(l, d, w_in, b_gate, q_norm_g, w_uq, kv_norm_g, w_ukv, w_oa, w_ob, w_grp,
                         pool_scale, w_oc, w_out, w_pq, norm1_g, norm2_g)
        mod = mod_all[l].reshape(MOD_ROWS, 1, 6 * d)
        keys = peer_keys[l].reshape(2 * PEER_HEADS, N_KEYS, PEER_HALF).astype(BF16)
        table = tables[l]
        sc_table = sc_tables[l]

        if last:
            kc, vc = _in_proj(cs, mod, c_row, ctx_len, lw, tabs_c, False)
        else:
            kc, vc, qc, abc, zpc, gc = _in_proj(cs, mod, c_row, ctx_len, lw, tabs_c, True)

        for j in range(n_chains):
            row = x_row(j * bpc)
            sc_args, xs = lax.optimization_barrier((sc_args, chains[j]))
            ctx_rows = slice(j * bpc * ctx_len, (j + 1) * bpc * ctx_len)
            kx, vx, qx, abx, zpx, gx = _in_proj(xs, mod, row, seq_len, lw, tabs_x, True)
            att_x = _attention(qx, kx, vx, bpc, seq_len, (kc[ctx_rows], vc[ctx_rows]))
            four_x = _fourier(abx, bpc, seq_len, dft_x)
            pool_x = _pool(zpx, bpc, seq_len, lw["wg"], lw["ps"])
            x1, h2, pq = _out_proj(xs, att_x, four_x, pool_x, gx, mod, row, seq_len, lw)
            chains[j], sc_args = _peer_block(x1, h2, pq, keys, mod, row, seq_len, final_g, table,
                                             sc_table, sc_tokens(l, j), last)

        if not last:
            att_c = _attention(qc, kc, vc, batch, ctx_len, None)
            four_c = _fourier(abc, batch, ctx_len, dft_c)
            pool_c = _pool(zpc, batch, ctx_len, lw["wg"], lw["ps"])
            c1, hc2, pqc = _out_proj(cs, att_c, four_c, pool_c, gc, mod, c_row, ctx_len, lw)
            cs, _ = _peer_block(c1, hc2, pqc, keys, mod, c_row, ctx_len, final_g, table, None, 0,
                                False)
    xs = jnp.concatenate(chains, axis=0)
    return xs.reshape(batch, seq_len, d)
```

```python
import functools
import math

import jax
import jax.numpy as jnp
from jax import lax
from jax.experimental import pallas as pl
from jax.experimental.pallas import tpu as pltpu
from jax.experimental.pallas import tpu_sc as plsc

F32 = jnp.float32
BF16 = jnp.bfloat16
I32 = jnp.int32

GRID_W = 64
N_HEADS = 8
Q_LORA = 256
KV_LORA = 128
QK_NOPE = 64
QK_ROPE = 32
V_HEAD = 64
ATT_WIDTH = N_HEADS * V_HEAD
ATT_SCALE = (QK_NOPE + QK_ROPE) ** -0.5
ROPE_BASE = 10000.0
FOURIER_WIDTH = 256
POOL_WINDOWS = (2, 4, 8, 16)
POOL_GROUP = 64
POOL_WIDTH = POOL_GROUP * len(POOL_WINDOWS)
N_BRANCH = 3
PEER_HEADS = 8
N_KEYS = 128
PEER_QDIM = 256
PEER_HALF = PEER_QDIM // 2
PEER_TOPK = 16
PEER_SLOTS = PEER_HEADS * PEER_TOPK
EPS = 1e-6

LANES = 128
SUBLANES = 8
HEAD_PAD = 128
POOL_PAD = 8
MOD_ROWS = 16
VMEM_LIMIT = 48 * 1024 * 1024

STAIR = tuple((i, PEER_TOPK // (i + 1)) for i in range(PEER_TOPK))
N_CAND = sum(cnt for _, cnt in STAIR)
CAND_ROWS = -(-N_CAND // SUBLANES) * SUBLANES

TOK_TILE = 256
ATT_Q_TILE = 256
FOUR_TILE = 512
DFT_SPLIT = 64
ROUTE_TILE = 256
PEER_TOK = 16
PEER_WAIT_GROUPS = 4
PEER_VMEM_LIMIT = 56 * 1024 * 1024

SC_CORES = 2
SC_SUBCORES = 16
SC_WORKERS = SC_CORES * SC_SUBCORES
SC_LANES = 16
SC_CHUNK = 32
PEER_CHAINS = 2
PEER_SC_TOKENS = 5888
PEER_SC_TOKENS_FIRST = 8704
PEER_SC_TOKENS_MIDDLE = 8704
PEER_SC_TOKENS_LAYER_END = 9728
GELU_C = math.sqrt(2.0 / math.pi)

C_CQ = 0
C_CKV = C_CQ + Q_LORA
C_ZF = C_CKV + KV_LORA
C_ZP = C_ZF + FOURIER_WIDTH
C_ZG = C_ZP + POOL_WIDTH


def _params(sem=None):
    return pltpu.CompilerParams(dimension_semantics=sem, vmem_limit_bytes=VMEM_LIMIT)


def _rms(x):
    return x * lax.rsqrt(jnp.mean(x * x, axis=-1, keepdims=True) + EPS)


def _dot(a, b):
    return jnp.dot(a, b, preferred_element_type=F32)


def _dot_nt(a, b):
    return lax.dot_general(a, b, (((1,), (1,)), ((), ())), preferred_element_type=F32)


def _mod_kernel(c_ref, w_ref, b_ref, o_ref):
    c = c_ref[...]
    s = c * jax.nn.sigmoid(c)
    o_ref[0] = jnp.dot(s, w_ref[0], preferred_element_type=F32,
                       precision=lax.Precision.HIGHEST) + b_ref[0]


def _modulation(cvec, w_mod, b_mod):
    depth, d, n = w_mod.shape
    tn = 1536
    return pl.pallas_call(
        _mod_kernel,
        out_shape=jax.ShapeDtypeStruct((depth, MOD_ROWS, n), F32),
        grid=(depth, n // tn),
        in_specs=[pl.BlockSpec((MOD_ROWS, d), lambda l, j: (0, 0)),
                  pl.BlockSpec((1, d, tn), lambda l, j: (l, 0, j)),
                  pl.BlockSpec((1, 1, tn), lambda l, j: (l, 0, j))],
        out_specs=pl.BlockSpec((1, MOD_ROWS, tn), lambda l, j: (l, 0, j)),
        compiler_params=_params(("arbitrary", "arbitrary")),
        name="modulation",
    )(cvec, w_mod, b_mod.reshape(depth, 1, n))


def _in_kernel(d, full, x_ref, m_ref, n1_ref, wall_ref, qg_ref, wq_ref, wqp_ref, kvg_ref,
               wk_ref, wv_ref, fc_ref, bg_ref, cq_ref, sq_ref, ck_ref, sk_ref, *outs):
    x = x_ref[...]
    m = m_ref[0]
    h = _rms(x) * n1_ref[...] * (1.0 + m[:, d:2 * d]) + m[:, 0:d]
    hb = h.astype(BF16)
    c_kr = C_ZG + N_BRANCH * d
    if full:
        z = _dot(hb, wall_ref[...])
        k_ref, v_ref, q_ref, ab_ref, zp_ref, g_ref = outs
    else:
        z = None
        k_ref, v_ref = outs
    def col(lo, hi):
        if full:
            return z[:, lo:hi]
        return _dot(hb, wall_ref[:, lo:hi])

    ckv = col(C_CKV, C_ZF)
    ckvn = (_rms(ckv) * kvg_ref[...]).astype(BF16)
    kf = _dot(ckvn, wk_ref[...])
    kr = col(c_kr, c_kr + LANES) * ck_ref[...] + col(c_kr + LANES, c_kr + 2 * LANES) * sk_ref[...]
    for hh in range(N_HEADS):
        sl = slice(hh * HEAD_PAD, (hh + 1) * HEAD_PAD)
        k_ref[:, sl] = (kf[:, sl] + kr).astype(BF16)
    v_ref[...] = _dot(ckvn, wv_ref[...]).astype(BF16)
    if not full:
        return
    cq = z[:, C_CQ:C_CKV]
    cqn = (_rms(cq) * qg_ref[...]).astype(BF16)
    qf = _dot(cqn, wq_ref[...])
    qr = _dot(cqn, wqp_ref[...])
    cosq = cq_ref[...]
    sinq = sq_ref[...]
    for hh in range(N_HEADS):
        sl = slice(hh * HEAD_PAD, (hh + 1) * HEAD_PAD)
        q_ref[:, sl] = (qf[:, sl] * cosq + qr[:, sl] * sinq).astype(BF16)
    ab_ref[...] = _dot(z[:, C_ZF:C_ZP].astype(BF16), fc_ref[...]).astype(BF16)
    zp_ref[...] = z[:, C_ZP:C_ZG]
    g_ref[...] = jax.nn.sigmoid(z[:, C_ZG:c_kr] + bg_ref[...]).astype(BF16)


def _in_proj(x2d, mod, mod_row, seq_len, lw, tabs, full):
    t, d = x2d.shape
    tm = min(TOK_TILE, seq_len)
    tiles_per_seq = seq_len // tm
    wall = lw["wall"]
    nw = wall.shape[1]
    cq, sq, ck, sk = tabs

    def const(shape):
        return pl.BlockSpec(shape, lambda i: (0,) * len(shape))

    def pos(i):
        return (i % tiles_per_seq, 0)

    in_specs = [
        pl.BlockSpec((tm, d), lambda i: (i, 0)),
        pl.BlockSpec((1, 1, mod.shape[-1]), lambda i: (mod_row(i, tiles_per_seq), 0, 0)),
        const((1, d)), const((d, nw)), const((1, Q_LORA)),
        const(lw["wq"].shape), const(lw["wqp"].shape), const((1, KV_LORA)),
        const(lw["wk"].shape), const(lw["wv"].shape), const(lw["fc"].shape),
        const((1, N_BRANCH * d)),
        pl.BlockSpec((tm, LANES), pos), pl.BlockSpec((tm, LANES), pos),
        pl.BlockSpec((tm, LANES), pos), pl.BlockSpec((tm, LANES), pos),
    ]
    kw = N_HEADS * HEAD_PAD
    out_shape = [jax.ShapeDtypeStruct((t, kw), BF16), jax.ShapeDtypeStruct((t, ATT_WIDTH), BF16)]
    out_specs = [pl.BlockSpec((tm, kw), lambda i: (i, 0)),
                 pl.BlockSpec((tm, ATT_WIDTH), lambda i: (i, 0))]
    if full:
        out_shape += [jax.ShapeDtypeStruct((t, kw), BF16),
                      jax.ShapeDtypeStruct((t, 2 * FOURIER_WIDTH), BF16),
                      jax.ShapeDtypeStruct((t, POOL_WIDTH), F32),
                      jax.ShapeDtypeStruct((t, N_BRANCH * d), BF16)]
        out_specs += [pl.BlockSpec((tm, kw), lambda i: (i, 0)),
                      pl.BlockSpec((tm, 2 * FOURIER_WIDTH), lambda i: (i, 0)),
                      pl.BlockSpec((tm, POOL_WIDTH), lambda i: (i, 0)),
                      pl.BlockSpec((tm, N_BRANCH * d), lambda i: (i, 0))]
    return pl.pallas_call(
        functools.partial(_in_kernel, d, full),
        out_shape=out_shape,
        grid=(t // tm,),
        in_specs=in_specs,
        out_specs=out_specs,
        compiler_params=_params(("arbitrary",)),
        name="in_proj" if full else "ctx_kv_proj",
    )(x2d, mod, lw["n1"], wall, lw["qg"], lw["wq"], lw["wqp"], lw["kvg"], lw["wk"], lw["wv"],
      lw["fc"], lw["bg"], cq, sq, ck, sk)


def _attn_kernel(has_ctx, q_ref, k_ref, v_ref, *rest):
    if has_ctx:
        kc_ref, vc_ref, o_ref = rest
    else:
        (o_ref,) = rest
    outs = []
    for hh in range(2):
        sl = slice(hh * HEAD_PAD, (hh + 1) * HEAD_PAD)
        q = q_ref[:, sl]
        s = _dot_nt(q, k_ref[:, sl])
        mx = jnp.max(s, axis=-1, keepdims=True)
        if has_ctx:
            sc = _dot_nt(q, kc_ref[:, sl])
            mx = jnp.maximum(mx, jnp.max(sc, axis=-1, keepdims=True))
        p = jnp.exp(s - mx)
        den = jnp.sum(p, axis=-1, keepdims=True)
        o = _dot(p.astype(BF16), v_ref[...])
        if has_ctx:
            pc = jnp.exp(sc - mx)
            den = den + jnp.sum(pc, axis=-1, keepdims=True)
            o = o + _dot(pc.astype(BF16), vc_ref[...])
        outs.append(o / den)
    lane = lax.broadcasted_iota(I32, outs[0].shape, 1)
    o_ref[...] = jnp.where(lane < V_HEAD, outs[0], outs[1]).astype(BF16)


def _attention(q, k, v, batch, seq_len, ctx_kv):
    t = q.shape[0]
    tq = min(ATT_Q_TILE, seq_len)
    nq = seq_len // tq
    pair_w = 2 * HEAD_PAD
    in_specs = [pl.BlockSpec((tq, pair_w), lambda b, j, i: (b * nq + i, j)),
                pl.BlockSpec((seq_len, pair_w), lambda b, j, i: (b, j)),
                pl.BlockSpec((seq_len, 2 * V_HEAD), lambda b, j, i: (b, j))]
    args = [q, k, v]
    if ctx_kv is not None:
        kc, vc = ctx_kv
        lc = kc.shape[0] // batch
        in_specs += [pl.BlockSpec((lc, pair_w), lambda b, j, i: (b, j)),
                     pl.BlockSpec((lc, 2 * V_HEAD), lambda b, j, i: (b, j))]
        args += [kc, vc]
    return pl.pallas_call(
        functools.partial(_attn_kernel, ctx_kv is not None),
        out_shape=jax.ShapeDtypeStruct((t, ATT_WIDTH), BF16),
        grid=(batch, N_HEADS // 2, nq),
        in_specs=in_specs,
        out_specs=pl.BlockSpec((tq, 2 * V_HEAD), lambda b, j, i: (b * nq + i, j)),
        compiler_params=_params(("arbitrary", "arbitrary", "arbitrary")),
        name="attention" if ctx_kv is not None else "ctx_attention",
    )(*args)


def _fourier_kernel(norm, c_ref, s_ref, ab_ref, o_ref):
    a = ab_ref[:, 0:FOURIER_WIDTH]
    b = ab_ref[:, FOURIER_WIDTH:2 * FOURIER_WIDTH]
    o = _dot(c_ref[...], a) - _dot(s_ref[...], b)
    o_ref[...] = (o * norm).astype(BF16)


def _fourier(ab, batch, seq_len, dft):
    t = ab.shape[0]
    tm = min(FOUR_TILE, seq_len)
    nt = seq_len // tm
    cl, sl = dft
    norm = 1.0 / math.sqrt(seq_len * FOURIER_WIDTH)
    return pl.pallas_call(
        functools.partial(_fourier_kernel, norm),
        out_shape=jax.ShapeDtypeStruct((t, FOURIER_WIDTH), BF16),
        grid=(nt, batch),
        in_specs=[pl.BlockSpec((tm, seq_len), lambda i, b: (i, 0)),
                  pl.BlockSpec((tm, seq_len), lambda i, b: (i, 0)),
                  pl.BlockSpec((seq_len, 2 * FOURIER_WIDTH), lambda i, b: (b, 0))],
        out_specs=pl.BlockSpec((tm, FOURIER_WIDTH), lambda i, b: (b * nt + i, 0)),
        compiler_params=_params(("arbitrary", "arbitrary")),
        name="fourier",
    )(cl, sl, ab)


def _pool_kernel(seq_len, z_ref, wg_ref, ps_ref, o_ref, pad_ref, s_ref):
    n = seq_len
    p = n + 2 * POOL_PAD
    z = z_ref[...]
    zeros = jnp.zeros((POOL_PAD, POOL_WIDTH), F32)
    pad_ref[0:POOL_PAD, :] = zeros
    pad_ref[POOL_PAD + n:p, :] = zeros
    pad_ref[POOL_PAD:POOL_PAD + n, :] = z
    s_ref[0:p - 1, :] = pad_ref[0:p - 1, :] + pad_ref[1:p, :]
    w2 = s_ref[POOL_PAD - 1:POOL_PAD - 1 + n, :]
    pad_ref[0:p - 3, :] = s_ref[0:p - 3, :] + s_ref[2:p - 1, :]
    w4 = pad_ref[POOL_PAD - 2:POOL_PAD - 2 + n, :]
    s_ref[0:p - 7, :] = pad_ref[0:p - 7, :] + pad_ref[4:p - 3, :]
    w8 = s_ref[POOL_PAD - 4:POOL_PAD - 4 + n, :]
    pad_ref[0:p - 15, :] = s_ref[0:p - 15, :] + s_ref[8:p - 7, :]
    w16 = pad_ref[0:n, :]
    pos = lax.broadcasted_iota(I32, (n, POOL_WIDTH), 0)
    grp = lax.broadcasted_iota(I32, (n, POOL_WIDTH), 1) // POOL_GROUP
    win = jnp.where(grp == 0, w2, jnp.where(grp == 1, w4, jnp.where(grp == 2, w8, w16)))
    half = jnp.where(grp == 0, 1, jnp.where(grp == 1, 2, jnp.where(grp == 2, 4, 8)))
    lo = jnp.maximum(pos - half, 0)
    hi = jnp.minimum(pos + half, n)
    cnt = (hi - lo).astype(F32)
    pooled = win / cnt - z
    y = _dot(pooled.astype(BF16), wg_ref[...])
    o_ref[...] = (y * ps_ref[...]).astype(BF16)


def _pool(zp, batch, seq_len, wg_bd, pool_scale):
    t = zp.shape[0]
    return pl.pallas_call(
        functools.partial(_pool_kernel, seq_len),
        out_shape=jax.ShapeDtypeStruct((t, POOL_WIDTH), BF16),
        grid=(batch,),
        in_specs=[pl.BlockSpec((seq_len, POOL_WIDTH), lambda b: (b, 0)),
                  pl.BlockSpec((POOL_WIDTH, POOL_WIDTH), lambda b: (0, 0)),
                  pl.BlockSpec((1, POOL_WIDTH), lambda b: (0, 0))],
        out_specs=pl.BlockSpec((seq_len, POOL_WIDTH), lambda b: (b, 0)),
        scratch_shapes=[pltpu.VMEM((seq_len + 2 * POOL_PAD, POOL_WIDTH), F32),
                        pltpu.VMEM((seq_len + 2 * POOL_PAD, POOL_WIDTH), F32)],
        compiler_params=_params(("arbitrary",)),
        name="pool",
    )(zp, wg_bd, pool_scale)


def _out_kernel(d, x_ref, att_ref, four_ref, pool_ref, g_ref, m_ref, woa_ref, wob_ref, woc_ref,
                wout_ref, n2_ref, wpq_ref, x1_ref, h2_ref, pq_ref):
    m = m_ref[0]
    ya = _dot(att_ref[...], woa_ref[...])
    yb = _dot(four_ref[...], wob_ref[...])
    yc = _dot(pool_ref[...], woc_ref[...])
    mixp = (g_ref[:, 0:d].astype(F32) * ya + g_ref[:, d:2 * d].astype(F32) * yb
            + g_ref[:, 2 * d:3 * d].astype(F32) * yc)
    mix = _dot(mixp.astype(BF16), wout_ref[...])
    x1 = x_ref[...] + m[:, 2 * d:3 * d] * mix
    x1_ref[...] = x1
    h2 = _rms(x1) * n2_ref[...] * (1.0 + m[:, 4 * d:5 * d]) + m[:, 3 * d:4 * d]
    h2_ref[...] = h2
    pq = _dot(h2.astype(BF16), wpq_ref[...])
    for hp in range(2 * PEER_HEADS):
        pq_ref[hp] = pq[:, hp * PEER_HALF:(hp + 1) * PEER_HALF].astype(BF16)


def _out_proj(x2d, att, four, pool, g, mod, mod_row, seq_len, lw):
    t, d = x2d.shape
    tm = min(TOK_TILE, seq_len)
    tiles_per_seq = seq_len // tm

    def const(shape):
        return pl.BlockSpec(shape, lambda i: (0,) * len(shape))

    def row(w):
        return pl.BlockSpec((tm, w), lambda i: (i, 0))

    nhp = 2 * PEER_HEADS
    return pl.pallas_call(
        functools.partial(_out_kernel, d),
        out_shape=[jax.ShapeDtypeStruct((t, d), F32), jax.ShapeDtypeStruct((t, d), F32),
                   jax.ShapeDtypeStruct((nhp, t, PEER_HALF), BF16)],
        grid=(t // tm,),
        in_specs=[row(d), row(ATT_WIDTH), row(FOURIER_WIDTH), row(POOL_WIDTH), row(N_BRANCH * d),
                  pl.BlockSpec((1, 1, mod.shape[-1]), lambda i: (mod_row(i, tiles_per_seq), 0, 0)),
                  const(lw["woa"].shape), const(lw["wob"].shape), const(lw["woc"].shape),
                  const(lw["wout"].shape), const((1, d)), const(lw["wpq"].shape)],
        out_specs=[row(d), row(d), pl.BlockSpec((nhp, tm, PEER_HALF), lambda i: (0, i, 0))],
        compiler_params=_params(("arbitrary",)),
        name="out_proj",
    )(x2d, att, four, pool, g, mod, lw["woa"], lw["wob"], lw["woc"], lw["wout"], lw["n2"],
      lw["wpq"])


def _select_round(s, iota, n):
    mx = jnp.max(s, axis=0, keepdims=True)
    idx = jnp.min(jnp.where(s == mx, iota, n), axis=0, keepdims=True)
    hit = iota == idx
    return mx, idx, hit, jnp.where(hit, -jnp.inf, s)


def _route_kernel(pq_ref, keys_ref, eidx_ref, gate_ref, sv_a, si_a, sv_b, si_b, cand_ref, cidx_ref,
                  ts_ref, eidx_s, gate_s):
    tm = pq_ref.shape[1]
    iota_k = lax.broadcasted_iota(I32, (N_KEYS, tm), 0).astype(F32)
    iota_c = lax.broadcasted_iota(I32, (CAND_ROWS, tm), 0).astype(F32)
    cand_ref[N_CAND:CAND_ROWS, :] = jnp.full((CAND_ROWS - N_CAND, tm), -jnp.inf, F32)
    cidx_ref[N_CAND:CAND_ROWS, :] = jnp.zeros((CAND_ROWS - N_CAND, tm), F32)

    def sub_key_topk(hd, sv_ref, si_ref):
        s = [_dot_nt(keys_ref[2 * hd + p], pq_ref[2 * hd + p]) for p in range(2)]
        for r in range(PEER_TOPK):
            for p in range(2):
                mx, idx, _, s[p] = _select_round(s[p], iota_k, float(N_KEYS))
                sv_ref[p, r:r + 1, :] = mx
                si_ref[p, r:r + 1, :] = idx

    def pair_topk(hd, sv_ref, si_ref):
        off = 0
        for i, cnt in STAIR:
            cand_ref[off:off + cnt, :] = sv_ref[0, i:i + 1, :] + sv_ref[1, 0:cnt, :]
            cidx_ref[off:off + cnt, :] = si_ref[0, i:i + 1, :] * N_KEYS + si_ref[1, 0:cnt, :]
            off += cnt
        cidx = cidx_ref[...]
        c = cand_ref[...]
        base = pl.multiple_of(hd * PEER_TOPK, PEER_TOPK)
        for r in range(PEER_TOPK):
            mx, _, hit, c = _select_round(c, iota_c, float(CAND_ROWS))
            ts_ref[r:r + 1, :] = mx
            expert = jnp.sum(jnp.where(hit, cidx, 0.0), axis=0, keepdims=True)
            eidx_s[pl.ds(base + r, 1), :] = expert.astype(I32)
        ts = ts_ref[...]
        ex = jnp.exp(ts - ts[0:1, :])
        gate_s[pl.ds(base, PEER_TOPK), :] = ex / jnp.sum(ex, axis=0, keepdims=True)

    sub_key_topk(0, sv_a, si_a)

    def two_heads(j, carry):
        hd = 2 * j
        sub_key_topk(hd + 1, sv_b, si_b)
        pair_topk(hd, sv_a, si_a)
        sub_key_topk(hd + 2, sv_a, si_a)
        pair_topk(hd + 1, sv_b, si_b)
        return carry

    lax.fori_loop(0, PEER_HEADS // 2 - 1, two_heads, 0)
    sub_key_topk(PEER_HEADS - 1, sv_b, si_b)
    pair_topk(PEER_HEADS - 2, sv_a, si_a)
    pair_topk(PEER_HEADS - 1, sv_b, si_b)
    eidx_ref[...] = eidx_s[...].T
    gate_ref[...] = gate_s[...].T


def _route(pq, keys):
    nhp, t, _ = pq.shape
    tm = ROUTE_TILE if t % ROUTE_TILE == 0 else LANES
    lists = pltpu.VMEM((2, PEER_TOPK, tm), F32)
    return pl.pallas_call(
        _route_kernel,
        out_shape=[jax.ShapeDtypeStruct((t, PEER_SLOTS), I32),
                   jax.ShapeDtypeStruct((t, PEER_SLOTS), F32)],
        grid=(t // tm,),
        in_specs=[pl.BlockSpec((nhp, tm, PEER_HALF), lambda i: (0, i, 0)),
                  pl.BlockSpec((nhp, N_KEYS, PEER_HALF), lambda i: (0, 0, 0))],
        out_specs=[pl.BlockSpec((tm, PEER_SLOTS), lambda i: (i, 0)),
                   pl.BlockSpec((tm, PEER_SLOTS), lambda i: (i, 0))],
        scratch_shapes=[lists, lists, lists, lists,
                        pltpu.VMEM((CAND_ROWS, tm), F32),
                        pltpu.VMEM((CAND_ROWS, tm), F32),
                        pltpu.VMEM((PEER_TOPK, tm), F32),
                        pltpu.VMEM((PEER_SLOTS, tm), I32), pltpu.VMEM((PEER_SLOTS, tm), F32)],
        compiler_params=_params(("arbitrary",)),
        name="peer_route",
    )(pq, keys)


def _fold_rows(a, b, keep_a, shift):
    return jnp.where(keep_a, a + pltpu.roll(a, SUBLANES - shift, 0), b + pltpu.roll(b, shift, 0))


def _peer_kernel(final, idx_ref, idxn_ref, gate_ref, h_ref, x_ref, g2_ref, fg_ref, tab_ref,
                 o_ref, buf_a, buf_b, act_ref, sem):
    i = pl.program_id(0)
    n = pl.num_programs(0)
    grp_tok = PEER_TOK // PEER_WAIT_GROUPS
    grp_rows = grp_tok * PEER_SLOTS
    sub = lax.broadcasted_iota(I32, (SUBLANES, LANES), 0)
    keep = {sh: (sub & sh) == 0 for sh in (4, 2, 1)}
    g2 = g2_ref[0, 0]
    gate_t = gate_ref[...].T

    def row_copy(ids, t_src, t_dst, k, buf, s):
        return pltpu.make_async_copy(tab_ref.at[ids[t_src, k]], buf.at[t_dst * PEER_SLOTS + k],
                                     sem.at[s, t_dst // grp_tok])

    def issue_token(ids, t_src, t_dst, buf, s):
        for k in range(PEER_SLOTS):
            row_copy(ids, t_src, t_dst, k, buf, s).start(priority=k % 2)

    def wait_group(buf, s, q):
        pltpu.make_async_copy(tab_ref.at[pl.ds(0, grp_rows)],
                              buf.at[pl.ds(q * grp_rows, grp_rows)], sem.at[s, q]).wait()

    def token(buf, tl, tb):
        r0 = tl * PEER_SLOTS
        hv = h_ref[tb]
        groups = []
        for g in range(PEER_SLOTS // SUBLANES):
            p = [buf[r0 + g * SUBLANES + j].astype(F32)[0:SUBLANES] * hv for j in range(SUBLANES)]
            for sh in (4, 2, 1):
                half = len(p) // 2
                p = [_fold_rows(p[j], p[j + half], keep[sh], sh) for j in range(half)]
            groups.append(p[0])
        part = jnp.concatenate(groups, axis=0)
        a = jnp.sum(part, axis=1, keepdims=True)
        act = gate_t[:, tb:tb + 1] * jax.nn.gelu(a)
        act_ref[...] = jnp.broadcast_to(act, (PEER_SLOTS, LANES))
        accs = [jnp.zeros((SUBLANES, LANES), F32) for _ in range(4)]
        for k in range(PEER_SLOTS):
            up = buf[r0 + k].astype(F32)[SUBLANES:2 * SUBLANES]
            accs[k % 4] = accs[k % 4] + act_ref[k:k + 1, :] * up
        xr = x_ref[tb] + g2 * ((accs[0] + accs[1]) + (accs[2] + accs[3]))
        if final:
            ms = jnp.sum(jnp.sum(xr * xr, axis=1, keepdims=True), axis=0, keepdims=True)
            xr = xr * lax.rsqrt(ms / (SUBLANES * LANES) + EPS) * fg_ref[...]
        o_ref[tb] = xr

    @pl.when(i == 0)
    def _():
        def body(t, carry):
            issue_token(idx_ref, t, t, buf_a, 0)
            return carry
        lax.fori_loop(0, PEER_TOK, body, 0)

    for t in range(PEER_TOK):
        if t % grp_tok == 0:
            wait_group(buf_a, 0, t // grp_tok)
        issue_token(idx_ref, PEER_TOK + t, t, buf_b, 1)
        token(buf_a, t, t)
    for t in range(PEER_TOK):
        if t % grp_tok == 0:
            wait_group(buf_b, 1, t // grp_tok)
        issue_token(idxn_ref, t, t, buf_a, 0)
        token(buf_b, t, PEER_TOK + t)

    @pl.when(i == n - 1)
    def _():
        for q in range(PEER_WAIT_GROUPS):
            wait_group(buf_a, 0, q)


def _peer(n_tok, eidx, gate, h3, x3, g2, mod_row, seq_len, fg, table, final):
    t = n_tok
    tb = 2 * PEER_TOK
    nb = t // tb
    steps_per_seq = seq_len // tb
    rows = PEER_TOK * PEER_SLOTS
    tile = (SUBLANES, LANES)
    pair = (2 * SUBLANES, LANES)
    return pl.pallas_call(
        functools.partial(_peer_kernel, final),
        out_shape=jax.ShapeDtypeStruct((t,) + tile, F32),
        grid=(nb,),
        in_specs=[pl.BlockSpec((tb, PEER_SLOTS), lambda i: (i, 0), memory_space=pltpu.SMEM),
                  pl.BlockSpec((tb, PEER_SLOTS), lambda i: (jnp.minimum(i + 1, nb - 1), 0),
                               memory_space=pltpu.SMEM),
                  pl.BlockSpec((tb, PEER_SLOTS), lambda i: (i, 0)),
                  pl.BlockSpec((tb,) + tile, lambda i: (i, 0, 0)),
                  pl.BlockSpec((tb,) + tile, lambda i: (i, 0, 0)),
                  pl.BlockSpec((1, 1) + tile, lambda i: (mod_row(i, steps_per_seq), 0, 0, 0)),
                  pl.BlockSpec(tile, lambda i: (0, 0)),
                  pl.BlockSpec(memory_space=pl.ANY)],
        out_specs=pl.BlockSpec((tb,) + tile, lambda i: (i, 0, 0)),
        scratch_shapes=[pltpu.VMEM((rows,) + pair, BF16), pltpu.VMEM((rows,) + pair, BF16),
                        pltpu.VMEM((PEER_SLOTS, LANES), F32),
                        pltpu.SemaphoreType.DMA((2, PEER_WAIT_GROUPS))],
        compiler_params=pltpu.CompilerParams(dimension_semantics=("arbitrary",),
                                             vmem_limit_bytes=PEER_VMEM_LIMIT),
        name="peer_experts",
    )(eidx, eidx, gate, h3, x3, g2, fg, table)


def _peer_sc_body(first, n_tok, d, eidx_hbm, gate_hbm, h_hbm, tab_hbm, y_hbm, idx_v, idx_n, gate_v, gate_n,
                  h_v, h_n, rows_a, rows_b, out_v, sem_a, sem_b, sem_n):
    wid = lax.axis_index("s") * SC_CORES + lax.axis_index("c")
    per_worker = n_tok // SC_WORKERS
    tok0 = first + wid * per_worker
    nj = d // SC_LANES
    lane = lax.iota(I32, SC_LANES)
    zero = jnp.zeros((SC_LANES,), F32)
    hi_mask = jnp.full((SC_LANES,), 0xFFFF0000, jnp.uint32)

    def gather(idx_ref, c, rows, sem):
        row0 = pl.multiple_of(c * SC_CHUNK, SC_CHUNK)
        return pltpu.make_async_copy(tab_hbm.at[idx_ref.at[pl.ds(row0, SC_CHUNK)]], rows, sem)

    def copy_words(src, dst, n):
        def step(j, carry):
            off = pl.multiple_of(j * SC_LANES, SC_LANES)
            dst[pl.ds(off, SC_LANES)] = src[pl.ds(off, SC_LANES)]
            return carry
        lax.fori_loop(0, n // SC_LANES, step, 0)

    def compute(c, rows):
        row0 = pl.multiple_of(c * SC_CHUNK, SC_CHUNK)

        def down_step(j, accs):
            off = pl.multiple_of(j * SC_LANES, SC_LANES)
            hj = h_v[pl.ds(off, SC_LANES)]
            out = []
            for r in range(SC_CHUNK):
                w = rows[r, pl.ds(off, SC_LANES)]
                dn = lax.bitcast_convert_type(w & hi_mask, F32)
                out.append(accs[r] + dn * hj)
            return tuple(out)

        accs = lax.fori_loop(0, nj, down_step, tuple(zero for _ in range(SC_CHUNK)))
        acts = []
        for g in range(SC_CHUNK // SC_LANES):
            a = zero
            for r in range(SC_LANES):
                a = jnp.where(lane == r, jnp.sum(accs[g * SC_LANES + r]), a)
            gt = gate_v[pl.ds(row0 + g * SC_LANES, SC_LANES)]
            u = GELU_C * (a + 0.044715 * (a * a * a))
            th = 1.0 - 2.0 / (jnp.exp(2.0 * u) + 1.0)
            act = gt * (0.5 * a * (1.0 + th))
            for r in range(SC_LANES):
                acts.append(jnp.sum(jnp.where(lane == r, act, 0.0)))

        def up_step(j, carry):
            off = pl.multiple_of(j * SC_LANES, SC_LANES)
            o = out_v[pl.ds(off, SC_LANES)]
            for r in range(SC_CHUNK):
                w = rows[r, pl.ds(off, SC_LANES)]
                up = lax.bitcast_convert_type(w << 16, F32)
                o = o + acts[r] * up
            out_v[pl.ds(off, SC_LANES)] = o
            return carry

        lax.fori_loop(0, nj, up_step, 0)

    def token(ti, carry):
        t = tok0 + ti
        tn = jnp.minimum(t + 1, tok0 + per_worker - 1)

        def clear(j, carry2):
            out_v[pl.ds(pl.multiple_of(j * SC_LANES, SC_LANES), SC_LANES)] = zero
            return carry2

        lax.fori_loop(0, nj, clear, 0)

        def pair(pp, carry2):
            gather(idx_v, 2 * pp + 1, rows_b, sem_b).start()
            gather(idx_v, 2 * pp, rows_a, sem_a).wait()
            compute(2 * pp, rows_a)

            @pl.when(pp == 0)
            def _():
                gather(idx_v, 2, rows_a, sem_a).start()

            @pl.when(pp == 1)
            def _():
                pltpu.sync_copy(eidx_hbm.at[tn], idx_n)
                gather(idx_n, 0, rows_a, sem_a).start()
                pltpu.make_async_copy(gate_hbm.at[tn], gate_n, sem_n).start()
                pltpu.make_async_copy(h_hbm.at[tn], h_n, sem_n).start()

            gather(idx_v, 2 * pp + 1, rows_b, sem_b).wait()
            compute(2 * pp + 1, rows_b)
            return carry2

        lax.fori_loop(0, PEER_SLOTS // (2 * SC_CHUNK), pair, 0)
        pltpu.sync_copy(out_v, y_hbm.at[t - first])
        pltpu.make_async_copy(gate_hbm.at[tn], gate_n, sem_n).wait()
        pltpu.make_async_copy(h_hbm.at[tn], h_n, sem_n).wait()
        copy_words(idx_n, idx_v, PEER_SLOTS)
        copy_words(gate_n, gate_v, PEER_SLOTS)
        copy_words(h_n, h_v, d)
        return carry

    pltpu.sync_copy(eidx_hbm.at[tok0], idx_v)
    pltpu.sync_copy(gate_hbm.at[tok0], gate_v)
    pltpu.sync_copy(h_hbm.at[tok0], h_v)
    gather(idx_v, 0, rows_a, sem_a).start()
    lax.fori_loop(0, per_worker, token, 0)
    gather(idx_v, 0, rows_a, sem_a).wait()


def _peer_sc(first, n_tok, eidx, gate, h2, sc_table):
    d = h2.shape[1]
    mesh = plsc.VectorSubcoreMesh(core_axis_name="c", subcore_axis_name="s",
                                  num_cores=SC_CORES, num_subcores=SC_SUBCORES)
    return pl.kernel(
        functools.partial(_peer_sc_body, first, n_tok, d),
        out_type=jax.ShapeDtypeStruct((n_tok, d), F32),
        mesh=mesh,
        scratch_types=[pltpu.VMEM((PEER_SLOTS,), I32), pltpu.VMEM((PEER_SLOTS,), I32),
                       pltpu.VMEM((PEER_SLOTS,), F32), pltpu.VMEM((PEER_SLOTS,), F32),
                       pltpu.VMEM((d,), F32), pltpu.VMEM((d,), F32),
                       pltpu.VMEM((SC_CHUNK, d), jnp.uint32), pltpu.VMEM((SC_CHUNK, d), jnp.uint32),
                       pltpu.VMEM((d,), F32), pltpu.SemaphoreType.DMA, pltpu.SemaphoreType.DMA,
                       pltpu.SemaphoreType.DMA],
        compiler_params=pltpu.CompilerParams(needs_layout_passes=False),
        cost_estimate=pl.CostEstimate(
            flops=4 * n_tok * PEER_SLOTS * d, transcendentals=n_tok * PEER_SLOTS,
            bytes_accessed=n_tok * (PEER_SLOTS * d * 4 + 2 * d * 4 + 2 * PEER_SLOTS * 4)),
        name="peer_experts_sc",
    )(eidx, gate, h2, sc_table)


def _finish_kernel(final, d, y_ref, x_ref, m_ref, fg_ref, o_ref):
    xr = x_ref[...] + m_ref[0][:, 5 * d:6 * d] * y_ref[...]
    if final:
        xr = _rms(xr) * fg_ref[...]
    o_ref[...] = xr


def _peer_finish(y, x1, mod, mod_row, seq_len, tok0, fg, final):
    t, d = y.shape
    tm = min(TOK_TILE, seq_len)
    tiles_per_seq = seq_len // tm
    tile0 = tok0 // tm
    return pl.pallas_call(
        functools.partial(_finish_kernel, final, d),
        out_shape=jax.ShapeDtypeStruct((t, d), F32),
        grid=(t // tm,),
        in_specs=[pl.BlockSpec((tm, d), lambda i: (i, 0)),
                  pl.BlockSpec((tm, d), lambda i: (i + tile0, 0)),
                  pl.BlockSpec((1, 1, mod.shape[-1]),
                               lambda i: (mod_row(i + tile0, tiles_per_seq), 0, 0)),
                  pl.BlockSpec((1, d), lambda i: (0, 0))],
        out_specs=pl.BlockSpec((tm, d), lambda i: (i, 0)),
        compiler_params=_params(("arbitrary",)),
        name="peer_finish",
    )(y, x1, mod, fg.reshape(1, d))


def _rope_perm(w):
    q = QK_ROPE // 4
    a1, a2, b1, b2 = (w[..., j * q:(j + 1) * q] for j in range(4))
    return jnp.concatenate([-a2, a1, -b2, b1], axis=-1)


def _prep_layer(l, d, w_in, b_gate, q_norm_g, w_uq, kv_norm_g, w_ukv, w_oa, w_ob, w_grp,
                pool_scale, w_oc, w_out, w_pq, norm1_g, norm2_g):
    wi = w_in[l]
    s0 = Q_LORA
    s1 = s0 + KV_LORA
    s2 = s1 + QK_ROPE
    s3 = s2 + FOURIER_WIDTH
    s4 = s3 + POOL_WIDTH
    w_kr = wi[:, s1:s2]
    zl = jnp.zeros((d, QK_NOPE), F32)
    zr = jnp.zeros((d, HEAD_PAD - QK_NOPE - QK_ROPE), F32)
    wall = jnp.concatenate([wi[:, 0:s1], wi[:, s2:], zl, w_kr, zr, zl, _rope_perm(w_kr), zr],
                           axis=1).astype(BF16)
    wq = w_uq[l].reshape(Q_LORA, N_HEADS, QK_NOPE + QK_ROPE)
    qpad = jnp.zeros((Q_LORA, N_HEADS, HEAD_PAD - QK_NOPE - QK_ROPE), F32)
    wq_full = jnp.concatenate([wq, qpad], axis=-1).reshape(Q_LORA, N_HEADS * HEAD_PAD)
    wq_perm = jnp.concatenate([jnp.zeros((Q_LORA, N_HEADS, QK_NOPE), F32),
                               _rope_perm(wq[..., QK_NOPE:]), qpad],
                              axis=-1).reshape(Q_LORA, N_HEADS * HEAD_PAD)
    wkv = w_ukv[l].reshape(KV_LORA, N_HEADS, QK_NOPE + V_HEAD)
    wk = jnp.concatenate([wkv[..., :QK_NOPE],
                          jnp.zeros((KV_LORA, N_HEADS, HEAD_PAD - QK_NOPE), F32)],
                         axis=-1).reshape(KV_LORA, N_HEADS * HEAD_PAD)
    wv = wkv[..., QK_NOPE:].reshape(KV_LORA, ATT_WIDTH)
    cidx = jnp.arange(FOURIER_WIDTH, dtype=I32)
    ang = (2.0 * math.pi / FOURIER_WIDTH) * ((cidx[:, None] * cidx[None, :]) % FOURIER_WIDTH
                                             ).astype(F32)
    fc = jnp.concatenate([jnp.cos(ang), jnp.sin(ang)], axis=1).astype(BF16)
    ng = len(POOL_WINDOWS)
    wg = jnp.zeros((ng, POOL_GROUP, ng, POOL_GROUP), F32)
    for gi in range(ng):
        wg = wg.at[gi, :, gi, :].set(w_grp[l, gi])
    return {
        "wall": wall, "n1": norm1_g[l][None, :], "n2": norm2_g[l][None, :],
        "qg": q_norm_g[l][None, :], "kvg": kv_norm_g[l][None, :],
        "wq": wq_full.astype(BF16), "wqp": wq_perm.astype(BF16),
        "wk": wk.astype(BF16), "wv": wv.astype(BF16), "fc": fc,
        "bg": b_gate[l][None, :],
        "wg": wg.reshape(POOL_WIDTH, POOL_WIDTH).astype(BF16),
        "ps": pool_scale[l][None, :],
        "woa": w_oa[l].astype(BF16), "wob": w_ob[l].astype(BF16), "woc": w_oc[l].astype(BF16),
        "wout": w_out[l].astype(BF16), "wpq": w_pq[l].astype(BF16),
    }


def _bf16_bits(w):
    return lax.bitcast_convert_type(w.astype(BF16), jnp.uint16).astype(jnp.uint32)


def _rope_tables(seq_len, rope):
    zeros_n = jnp.zeros((seq_len, QK_NOPE), F32)
    zeros_p = jnp.zeros((seq_len, HEAD_PAD - QK_NOPE - QK_ROPE), F32)
    ones_n = jnp.ones((seq_len, QK_NOPE), F32)
    if rope:
        pos = jnp.arange(seq_len, dtype=I32)
        half = QK_ROPE // 2
        inv_freq = ROPE_BASE ** (-jnp.arange(0, half, 2, dtype=F32) / half)
        ang_r = (pos // GRID_W).astype(F32)[:, None] * inv_freq
        ang_c = (pos % GRID_W).astype(F32)[:, None] * inv_freq
        cos = jnp.concatenate([jnp.cos(ang_r)] * 2 + [jnp.cos(ang_c)] * 2, axis=1)
        sin = jnp.concatenate([jnp.sin(ang_r)] * 2 + [jnp.sin(ang_c)] * 2, axis=1)
    else:
        cos = jnp.ones((seq_len, QK_ROPE), F32)
        sin = jnp.zeros((seq_len, QK_ROPE), F32)
    cq = jnp.concatenate([ones_n, cos, zeros_p], axis=1) * ATT_SCALE
    sq = jnp.concatenate([zeros_n, sin, zeros_p], axis=1) * ATT_SCALE
    ck = jnp.concatenate([zeros_n, cos, zeros_p], axis=1)
    sk = jnp.concatenate([zeros_n, sin, zeros_p], axis=1)
    return cq, sq, ck, sk


def _dft(seq_len):
    n = seq_len
    a = DFT_SPLIT if n % DFT_SPLIT == 0 else 1
    b = n // a
    k = jnp.arange(n, dtype=I32)[None, :]
    ang_a = (2.0 * math.pi / a) * ((jnp.arange(a, dtype=I32)[:, None] * k) % a).astype(F32)
    ang_b = (2.0 * math.pi / n) * ((jnp.arange(b, dtype=I32)[:, None] * k) % n).astype(F32)
    ca, sa = jnp.cos(ang_a)[:, None, :], jnp.sin(ang_a)[:, None, :]
    cb, sb = jnp.cos(ang_b)[None, :, :], jnp.sin(ang_b)[None, :, :]
    cos = (ca * cb - sa * sb).reshape(n, n)
    sin = (sa * cb + ca * sb).reshape(n, n)
    return cos.astype(BF16), sin.astype(BF16)


def _peer_block(x1, h2, pq, keys, mod, mod_row, seq_len, fg, table, sc_table, n_sc, final):
    t, d = x1.shape
    tile = (SUBLANES, LANES)
    eidx, gate = _route(pq, keys)
    n_tc = t - n_sc
    mod_tiles = mod.reshape(MOD_ROWS, 6, SUBLANES, LANES)[:, 5:6]
    out = _peer(n_tc, eidx, gate, h2.reshape((t,) + tile), x1.reshape((t,) + tile), mod_tiles,
                mod_row, seq_len, fg.reshape(tile), table, final).reshape(n_tc, d)
    if n_sc == 0:
        return out, ()
    y = _peer_sc(n_tc, n_sc, eidx, gate, h2, sc_table)
    out_sc = _peer_finish(y, x1, mod, mod_row, seq_len, n_tc, fg, final)
    return jnp.concatenate([out, out_sc], axis=0), (out, eidx, gate, h2)


def kernel(x, c, ctx, c_ctx, w_mod, b_mod, norm1_g, norm2_g, w_in, b_gate, q_norm_g, w_uq,
           kv_norm_g, w_ukv, w_oa, w_ob, w_grp, pool_scale, w_oc, w_out, w_pq, peer_keys,
           peer_down, peer_up, final_g):
    batch, seq_len, d = x.shape
    ctx_len = ctx.shape[1]
    depth = w_mod.shape[0]
    assert d == SUBLANES * LANES and batch + 1 <= MOD_ROWS
    assert seq_len % TOK_TILE == 0 and ctx_len % (2 * PEER_TOK) == 0 and ctx_len % LANES == 0
    tile = (SUBLANES, LANES)

    cvec = jnp.concatenate([c, c_ctx[None, :], jnp.zeros((MOD_ROWS - batch - 1, d), F32)], axis=0)
    mod_all = _modulation(cvec, w_mod, b_mod)

    def x_row(b0):
        return lambda i, per_seq: b0 + i // per_seq

    def c_row(i, per_seq):
        return batch

    tabs_x = _rope_tables(seq_len, True)
    tabs_c = _rope_tables(ctx_len, False)
    dft_x = _dft(seq_len)
    dft_c = _dft(ctx_len)

    n_chains = PEER_CHAINS if batch % PEER_CHAINS == 0 else 1
    bpc = batch // n_chains
    use_sc = bpc * seq_len >= TOK_TILE + max(PEER_SC_TOKENS, PEER_SC_TOKENS_FIRST,
                                             PEER_SC_TOKENS_MIDDLE, PEER_SC_TOKENS_LAYER_END)

    def sc_tokens(l, j):
        if not use_sc:
            return 0
        if j == n_chains - 1:
            return PEER_SC_TOKENS if l == depth - 1 else PEER_SC_TOKENS_LAYER_END
        return PEER_SC_TOKENS_FIRST if (l, j) == (0, 0) else PEER_SC_TOKENS_MIDDLE

    tables = [jnp.concatenate([peer_down[l].reshape((-1,) + tile),
                               peer_up[l].reshape((-1,) + tile)], axis=1).astype(BF16)
              for l in range(depth)]
    sc_tables = [(_bf16_bits(peer_down[l]) << 16) | _bf16_bits(peer_up[l]) for l in range(depth)]
    x, ctx, tables, sc_tables = lax.optimization_barrier((x, ctx, tables, sc_tables))

    chains = [x[j * bpc:(j + 1) * bpc].reshape(bpc * seq_len, d) for j in range(n_chains)]
    cs = ctx.reshape(batch * ctx_len, d)
    sc_args = ()
    for l in range(depth):
        last = l == depth - 1
        lw = _prep_layer(l, d, w_in, b_gate, q_norm_g, w_uq, kv_norm_g, w_ukv, w_oa, w_ob, w_grp,
                         pool_scale, w_oc, w_out, w_pq, norm1_g, norm2_g)
        mod = mod_all[l].reshape(MOD_ROWS, 1, 6 * d)
        keys = peer_keys[l].reshape(2 * PEER_HEADS, N_KEYS, PEER_HALF).astype(BF16)
        table = tables[l]
        sc_table = sc_tables[l]

        if last:
            kc, vc = _in_proj(cs, mod, c_row, ctx_len, lw, tabs_c, False)
        else:
            kc, vc, qc, abc, zpc, gc = _in_proj(cs, mod, c_row, ctx_len, lw, tabs_c, True)

        for j in range(n_chains):
            row = x_row(j * bpc)
            sc_args, xs = lax.optimization_barrier((sc_args, chains[j]))
            ctx_rows = slice(j * bpc * ctx_len, (j + 1) * bpc * ctx_len)
            kx, vx, qx, abx, zpx, gx = _in_proj(xs, mod, row, seq_len, lw, tabs_x, True)
            att_x = _attention(qx, kx, vx, bpc, seq_len, (kc[ctx_rows], vc[ctx_rows]))
            four_x = _fourier(abx, bpc, seq_len, dft_x)
            pool_x = _pool(zpx, bpc, seq_len, lw["wg"], lw["ps"])
            x1, h2, pq = _out_proj(xs, att_x, four_x, pool_x, gx, mod, row, seq_len, lw)
            chains[j], sc_args = _peer_block(x1, h2, pq, keys, mod, row, seq_len, final_g, table,
                                             sc_table, sc_tokens(l, j), last)

        if not last:
            att_c = _attention(qc, kc, vc, batch, ctx_len, None)
            four_c = _fourier(abc, batch, ctx_len, dft_c)
            pool_c = _pool(zpc, batch, ctx_len, lw["wg"], lw["ps"])
            c1, hc2, pqc = _out_proj(cs, att_c, four_c, pool_c, gc, mod, c_row, ctx_len, lw)
            cs, _ = _peer_block(c1, hc2, pqc, keys, mod, c_row, ctx_len, final_g, table, None, 0,
                                False)
    xs = jnp.concatenate(chains, axis=0)
    return xs.reshape(batch, seq_len, d)
```

```python
import functools
import math

import jax
import jax.numpy as jnp
from jax import lax
from jax.experimental import pallas as pl
from jax.experimental.pallas import tpu as pltpu
from jax.experimental.pallas import tpu_sc as plsc

F32 = jnp.float32
BF16 = jnp.bfloat16
I32 = jnp.int32

GRID_W = 64
N_HEADS = 8
Q_LORA = 256
KV_LORA = 128
QK_NOPE = 64
QK_ROPE = 32
V_HEAD = 64
ATT_WIDTH = N_HEADS * V_HEAD
ATT_SCALE = (QK_NOPE + QK_ROPE) ** -0.5
ROPE_BASE = 10000.0
FOURIER_WIDTH = 256
POOL_WINDOWS = (2, 4, 8, 16)
POOL_GROUP = 64
POOL_WIDTH = POOL_GROUP * len(POOL_WINDOWS)
N_BRANCH = 3
PEER_HEADS = 8
N_KEYS = 128
PEER_QDIM = 256
PEER_HALF = PEER_QDIM // 2
PEER_TOPK = 16
PEER_SLOTS = PEER_HEADS * PEER_TOPK
EPS = 1e-6

LANES = 128
SUBLANES = 8
HEAD_PAD = 128
POOL_PAD = 8
MOD_ROWS = 16
VMEM_LIMIT = 48 * 1024 * 1024

STAIR = tuple((i, PEER_TOPK // (i + 1)) for i in range(PEER_TOPK))
N_CAND = sum(cnt for _, cnt in STAIR)
CAND_ROWS = -(-N_CAND // SUBLANES) * SUBLANES

TOK_TILE = 256
ATT_Q_TILE = 256
FOUR_TILE = 512
DFT_SPLIT = 64
ROUTE_TILE = 256
PEER_TOK = 16
PEER_WAIT_GROUPS = 4
PEER_VMEM_LIMIT = 56 * 1024 * 1024

SC_CORES = 2
SC_SUBCORES = 16
SC_WORKERS = SC_CORES * SC_SUBCORES
SC_LANES = 16
SC_CHUNK = 32
PEER_CHAINS = 2
PEER_SC_TOKENS = 5888
PEER_SC_TOKENS_FIRST = 8704
PEER_SC_TOKENS_MIDDLE = 8704
PEER_SC_TOKENS_LAYER_END = 9728
GELU_C = math.sqrt(2.0 / math.pi)

C_CQ = 0
C_CKV = C_CQ + Q_LORA
C_ZF = C_CKV + KV_LORA
C_ZP = C_ZF + FOURIER_WIDTH
C_ZG = C_ZP + POOL_WIDTH


def _params(sem=None):
    return pltpu.CompilerParams(dimension_semantics=sem, vmem_limit_bytes=VMEM_LIMIT)


def _rms(x):
    return x * lax.rsqrt(jnp.mean(x * x, axis=-1, keepdims=True) + EPS)


def _dot(a, b):
    return jnp.dot(a, b, preferred_element_type=F32)


def _dot_nt(a, b):
    return lax.dot_general(a, b, (((1,), (1,)), ((), ())), preferred_element_type=F32)


def _mod_kernel(c_ref, w_ref, b_ref, o_ref):
    c = c_ref[...]
    s = c * jax.nn.sigmoid(c)
    o_ref[0] = jnp.dot(s, w_ref[0], preferred_element_type=F32,
                       precision=lax.Precision.HIGHEST) + b_ref[0]


def _modulation(cvec, w_mod, b_mod):
    depth, d, n = w_mod.shape
    tn = 1536
    return pl.pallas_call(
        _mod_kernel,
        out_shape=jax.ShapeDtypeStruct((depth, MOD_ROWS, n), F32),
        grid=(depth, n // tn),
        in_specs=[pl.BlockSpec((MOD_ROWS, d), lambda l, j: (0, 0)),
                  pl.BlockSpec((1, d, tn), lambda l, j: (l, 0, j)),
                  pl.BlockSpec((1, 1, tn), lambda l, j: (l, 0, j))],
        out_specs=pl.BlockSpec((1, MOD_ROWS, tn), lambda l, j: (l, 0, j)),
        compiler_params=_params(("arbitrary", "arbitrary")),
        name="modulation",
    )(cvec, w_mod, b_mod.reshape(depth, 1, n))


def _in_kernel(d, full, x_ref, m_ref, n1_ref, wall_ref, qg_ref, wq_ref, wqp_ref, kvg_ref,
               wk_ref, wv_ref, fc_ref, bg_ref, cq_ref, sq_ref, ck_ref, sk_ref, *outs):
    x = x_ref[...]
    m = m_ref[0]
    h = _rms(x) * n1_ref[...] * (1.0 + m[:, d:2 * d]) + m[:, 0:d]
    hb = h.astype(BF16)
    c_kr = C_ZG + N_BRANCH * d
    if full:
        z = _dot(hb, wall_ref[...])
        k_ref, v_ref, q_ref, ab_ref, zp_ref, g_ref = outs
    else:
        z = None
        k_ref, v_ref = outs
    def col(lo, hi):
        if full:
            return z[:, lo:hi]
        return _dot(hb, wall_ref[:, lo:hi])

    ckv = col(C_CKV, C_ZF)
    ckvn = (_rms(ckv) * kvg_ref[...]).astype(BF16)
    kf = _dot(ckvn, wk_ref[...])
    kr = col(c_kr, c_kr + LANES) * ck_ref[...] + col(c_kr + LANES, c_kr + 2 * LANES) * sk_ref[...]
    for hh in range(N_HEADS):
        sl = slice(hh * HEAD_PAD, (hh + 1) * HEAD_PAD)
        k_ref[:, sl] = (kf[:, sl] + kr).astype(BF16)
    v_ref[...] = _dot(ckvn, wv_ref[...]).astype(BF16)
    if not full:
        return
    cq = z[:, C_CQ:C_CKV]
    cqn = (_rms(cq) * qg_ref[...]).astype(BF16)
    qf = _dot(cqn, wq_ref[...])
    qr = _dot(cqn, wqp_ref[...])
    cosq = cq_ref[...]
    sinq = sq_ref[...]
    for hh in range(N_HEADS):
        sl = slice(hh * HEAD_PAD, (hh + 1) * HEAD_PAD)
        q_ref[:, sl] = (qf[:, sl] * cosq + qr[:, sl] * sinq).astype(BF16)
    ab_ref[...] = _dot(z[:, C_ZF:C_ZP].astype(BF16), fc_ref[...]).astype(BF16)
    zp_ref[...] = z[:, C_ZP:C_ZG]
    g_ref[...] = jax.nn.sigmoid(z[:, C_ZG:c_kr] + bg_ref[...]).astype(BF16)


def _in_proj(x2d, mod, mod_row, seq_len, lw, tabs, full):
    t, d = x2d.shape
    tm = min(TOK_TILE, seq_len)
    tiles_per_seq = seq_len // tm
    wall = lw["wall"]
    nw = wall.shape[1]
    cq, sq, ck, sk = tabs

    def const(shape):
        return pl.BlockSpec(shape, lambda i: (0,) * len(shape))

    def pos(i):
        return (i % tiles_per_seq, 0)

    in_specs = [
        pl.BlockSpec((tm, d), lambda i: (i, 0)),
        pl.BlockSpec((1, 1, mod.shape[-1]), lambda i: (mod_row(i, tiles_per_seq), 0, 0)),
        const((1, d)), const((d, nw)), const((1, Q_LORA)),
        const(lw["wq"].shape), const(lw["wqp"].shape), const((1, KV_LORA)),
        const(lw["wk"].shape), const(lw["wv"].shape), const(lw["fc"].shape),
        const((1, N_BRANCH * d)),
        pl.BlockSpec((tm, LANES), pos), pl.BlockSpec((tm, LANES), pos),
        pl.BlockSpec((tm, LANES), pos), pl.BlockSpec((tm, LANES), pos),
    ]
    kw = N_HEADS * HEAD_PAD
    out_shape = [jax.ShapeDtypeStruct((t, kw), BF16), jax.ShapeDtypeStruct((t, ATT_WIDTH), BF16)]
    out_specs = [pl.BlockSpec((tm, kw), lambda i: (i, 0)),
                 pl.BlockSpec((tm, ATT_WIDTH), lambda i: (i, 0))]
    if full:
        out_shape += [jax.ShapeDtypeStruct((t, kw), BF16),
                      jax.ShapeDtypeStruct((t, 2 * FOURIER_WIDTH), BF16),
                      jax.ShapeDtypeStruct((t, POOL_WIDTH), F32),
                      jax.ShapeDtypeStruct((t, N_BRANCH * d), BF16)]
        out_specs += [pl.BlockSpec((tm, kw), lambda i: (i, 0)),
                      pl.BlockSpec((tm, 2 * FOURIER_WIDTH), lambda i: (i, 0)),
                      pl.BlockSpec((tm, POOL_WIDTH), lambda i: (i, 0)),
                      pl.BlockSpec((tm, N_BRANCH * d), lambda i: (i, 0))]
    return pl.pallas_call(
        functools.partial(_in_kernel, d, full),
        out_shape=out_shape,
        grid=(t // tm,),
        in_specs=in_specs,
        out_specs=out_specs,
        compiler_params=_params(("arbitrary",)),
        name="in_proj" if full else "ctx_kv_proj",
    )(x2d, mod, lw["n1"], wall, lw["qg"], lw["wq"], lw["wqp"], lw["kvg"], lw["wk"], lw["wv"],
      lw["fc"], lw["bg"], cq, sq, ck, sk)


def _attn_kernel(has_ctx, q_ref, k_ref, v_ref, *rest):
    if has_ctx:
        kc_ref, vc_ref, o_ref = rest
    else:
        (o_ref,) = rest
    outs = []
    for hh in range(2):
        sl = slice(hh * HEAD_PAD, (hh + 1) * HEAD_PAD)
        q = q_ref[:, sl]
        s = _dot_nt(q, k_ref[:, sl])
        mx = jnp.max(s, axis=-1, keepdims=True)
        if has_ctx:
            sc = _dot_nt(q, kc_ref[:, sl])
            mx = jnp.maximum(mx, jnp.max(sc, axis=-1, keepdims=True))
        p = jnp.exp(s - mx)
        den = jnp.sum(p, axis=-1, keepdims=True)
        o = _dot(p.astype(BF16), v_ref[...])
        if has_ctx:
            pc = jnp.exp(sc - mx)
            den = den + jnp.sum(pc, axis=-1, keepdims=True)
            o = o + _dot(pc.astype(BF16), vc_ref[...])
        outs.append(o / den)
    lane = lax.broadcasted_iota(I32, outs[0].shape, 1)
    o_ref[...] = jnp.where(lane < V_HEAD, outs[0], outs[1]).astype(BF16)


def _attention(q, k, v, batch, seq_len, ctx_kv):
    t = q.shape[0]
    tq = min(ATT_Q_TILE, seq_len)
    nq = seq_len // tq
    pair_w = 2 * HEAD_PAD
    in_specs = [pl.BlockSpec((tq, pair_w), lambda b, j, i: (b * nq + i, j)),
                pl.BlockSpec((seq_len, pair_w), lambda b, j, i: (b, j)),
                pl.BlockSpec((seq_len, 2 * V_HEAD), lambda b, j, i: (b, j))]
    args = [q, k, v]
    if ctx_kv is not None:
        kc, vc = ctx_kv
        lc = kc.shape[0] // batch
        in_specs += [pl.BlockSpec((lc, pair_w), lambda b, j, i: (b, j)),
                     pl.BlockSpec((lc, 2 * V_HEAD), lambda b, j, i: (b, j))]
        args += [kc, vc]
    return pl.pallas_call(
        functools.partial(_attn_kernel, ctx_kv is not None),
        out_shape=jax.ShapeDtypeStruct((t, ATT_WIDTH), BF16),
        grid=(batch, N_HEADS // 2, nq),
        in_specs=in_specs,
        out_specs=pl.BlockSpec((tq, 2 * V_HEAD), lambda b, j, i: (b * nq + i, j)),
        compiler_params=_params(("arbitrary", "arbitrary", "arbitrary")),
        name="attention" if ctx_kv is not None else "ctx_attention",
    )(*args)


def _fourier_kernel(norm, c_ref, s_ref, ab_ref, o_ref):
    a = ab_ref[:, 0:FOURIER_WIDTH]
    b = ab_ref[:, FOURIER_WIDTH:2 * FOURIER_WIDTH]
    o = _dot(c_ref[...], a) - _dot(s_ref[...], b)
    o_ref[...] = (o * norm).astype(BF16)


def _fourier(ab, batch, seq_len, dft):
    t = ab.shape[0]
    tm = min(FOUR_TILE, seq_len)
    nt = seq_len // tm
    cl, sl = dft
    norm = 1.0 / math.sqrt(seq_len * FOURIER_WIDTH)
    return pl.pallas_call(
        functools.partial(_fourier_kernel, norm),
        out_shape=jax.ShapeDtypeStruct((t, FOURIER_WIDTH), BF16),
        grid=(nt, batch),
        in_specs=[pl.BlockSpec((tm, seq_len), lambda i, b: (i, 0)),
                  pl.BlockSpec((tm, seq_len), lambda i, b: (i, 0)),
                  pl.BlockSpec((seq_len, 2 * FOURIER_WIDTH), lambda i, b: (b, 0))],
        out_specs=pl.BlockSpec((tm, FOURIER_WIDTH), lambda i, b: (b * nt + i, 0)),
        compiler_params=_params(("arbitrary", "arbitrary")),
        name="fourier",
    )(cl, sl, ab)


def _pool_kernel(seq_len, z_ref, wg_ref, ps_ref, o_ref, pad_ref, s_ref):
    n = seq_len
    p = n + 2 * POOL_PAD
    z = z_ref[...]
    zeros = jnp.zeros((POOL_PAD, POOL_WIDTH), F32)
    pad_ref[0:POOL_PAD, :] = zeros
    pad_ref[POOL_PAD + n:p, :] = zeros
    pad_ref[POOL_PAD:POOL_PAD + n, :] = z
    s_ref[0:p - 1, :] = pad_ref[0:p - 1, :] + pad_ref[1:p, :]
    w2 = s_ref[POOL_PAD - 1:POOL_PAD - 1 + n, :]
    pad_ref[0:p - 3, :] = s_ref[0:p - 3, :] + s_ref[2:p - 1, :]
    w4 = pad_ref[POOL_PAD - 2:POOL_PAD - 2 + n, :]
    s_ref[0:p - 7, :] = pad_ref[0:p - 7, :] + pad_ref[4:p - 3, :]
    w8 = s_ref[POOL_PAD - 4:POOL_PAD - 4 + n, :]
    pad_ref[0:p - 15, :] = s_ref[0:p - 15, :] + s_ref[8:p - 7, :]
    w16 = pad_ref[0:n, :]
    pos = lax.broadcasted_iota(I32, (n, POOL_WIDTH), 0)
    grp = lax.broadcasted_iota(I32, (n, POOL_WIDTH), 1) // POOL_GROUP
    win = jnp.where(grp == 0, w2, jnp.where(grp == 1, w4, jnp.where(grp == 2, w8, w16)))
    half = jnp.where(grp == 0, 1, jnp.where(grp == 1, 2, jnp.where(grp == 2, 4, 8)))
    lo = jnp.maximum(pos - half, 0)
    hi = jnp.minimum(pos + half, n)
    cnt = (hi - lo).astype(F32)
    pooled = win / cnt - z
    y = _dot(pooled.astype(BF16), wg_ref[...])
    o_ref[...] = (y * ps_ref[...]).astype(BF16)


def _pool(zp, batch, seq_len, wg_bd, pool_scale):
    t = zp.shape[0]
    return pl.pallas_call(
        functools.partial(_pool_kernel, seq_len),
        out_shape=jax.ShapeDtypeStruct((t, POOL_WIDTH), BF16),
        grid=(batch,),
        in_specs=[pl.BlockSpec((seq_len, POOL_WIDTH), lambda b: (b, 0)),
                  pl.BlockSpec((POOL_WIDTH, POOL_WIDTH), lambda b: (0, 0)),
                  pl.BlockSpec((1, POOL_WIDTH), lambda b: (0, 0))],
        out_specs=pl.BlockSpec((seq_len, POOL_WIDTH), lambda b: (b, 0)),
        scratch_shapes=[pltpu.VMEM((seq_len + 2 * POOL_PAD, POOL_WIDTH), F32),
                        pltpu.VMEM((seq_len + 2 * POOL_PAD, POOL_WIDTH), F32)],
        compiler_params=_params(("arbitrary",)),
        name="pool",
    )(zp, wg_bd, pool_scale)


def _out_kernel(d, x_ref, att_ref, four_ref, pool_ref, g_ref, m_ref, woa_ref, wob_ref, woc_ref,
                wout_ref, n2_ref, wpq_ref, x1_ref, h2_ref, pq_ref):
    m = m_ref[0]
    ya = _dot(att_ref[...], woa_ref[...])
    yb = _dot(four_ref[...], wob_ref[...])
    yc = _dot(pool_ref[...], woc_ref[...])
    mixp = (g_ref[:, 0:d].astype(F32) * ya + g_ref[:, d:2 * d].astype(F32) * yb
            + g_ref[:, 2 * d:3 * d].astype(F32) * yc)
    mix = _dot(mixp.astype(BF16), wout_ref[...])
    x1 = x_ref[...] + m[:, 2 * d:3 * d] * mix
    x1_ref[...] = x1
    h2 = _rms(x1) * n2_ref[...] * (1.0 + m[:, 4 * d:5 * d]) + m[:, 3 * d:4 * d]
    h2_ref[...] = h2
    pq = _dot(h2.astype(BF16), wpq_ref[...])
    for hp in range(2 * PEER_HEADS):
        pq_ref[hp] = pq[:, hp * PEER_HALF:(hp + 1) * PEER_HALF].astype(BF16)


def _out_proj(x2d, att, four, pool, g, mod, mod_row, seq_len, lw):
    t, d = x2d.shape
    tm = min(TOK_TILE, seq_len)
    tiles_per_seq = seq_len // tm

    def const(shape):
        return pl.BlockSpec(shape, lambda i: (0,) * len(shape))

    def row(w):
        return pl.BlockSpec((tm, w), lambda i: (i, 0))

    nhp = 2 * PEER_HEADS
    return pl.pallas_call(
        functools.partial(_out_kernel, d),
        out_shape=[jax.ShapeDtypeStruct((t, d), F32), jax.ShapeDtypeStruct((t, d), F32),
                   jax.ShapeDtypeStruct((nhp, t, PEER_HALF), BF16)],
        grid=(t // tm,),
        in_specs=[row(d), row(ATT_WIDTH), row(FOURIER_WIDTH), row(POOL_WIDTH), row(N_BRANCH * d),
                  pl.BlockSpec((1, 1, mod.shape[-1]), lambda i: (mod_row(i, tiles_per_seq), 0, 0)),
                  const(lw["woa"].shape), const(lw["wob"].shape), const(lw["woc"].shape),
                  const(lw["wout"].shape), const((1, d)), const(lw["wpq"].shape)],
        out_specs=[row(d), row(d), pl.BlockSpec((nhp, tm, PEER_HALF), lambda i: (0, i, 0))],
        compiler_params=_params(("arbitrary",)),
        name="out_proj",
    )(x2d, att, four, pool, g, mod, lw["woa"], lw["wob"], lw["woc"], lw["wout"], lw["n2"],
      lw["wpq"])


def _select_round(s, iota, n):
    mx = jnp.max(s, axis=0, keepdims=True)
    idx = jnp.min(jnp.where(s == mx, iota, n), axis=0, keepdims=True)
    hit = iota == idx
    return mx, idx, hit, jnp.where(hit, -jnp.inf, s)


def _route_kernel(pq_ref, keys_ref, eidx_ref, gate_ref, sv_a, si_a, sv_b, si_b, cand_ref, cidx_ref,
                  ts_ref, eidx_s, gate_s):
    tm = pq_ref.shape[1]
    iota_k = lax.broadcasted_iota(I32, (N_KEYS, tm), 0).astype(F32)
    iota_c = lax.broadcasted_iota(I32, (CAND_ROWS, tm), 0).astype(F32)
    cand_ref[N_CAND:CAND_ROWS, :] = jnp.full((CAND_ROWS - N_CAND, tm), -jnp.inf, F32)
    cidx_ref[N_CAND:CAND_ROWS, :] = jnp.zeros((CAND_ROWS - N_CAND, tm), F32)

    def sub_key_topk(hd, sv_ref, si_ref):
        s = [_dot_nt(keys_ref[2 * hd + p], pq_ref[2 * hd + p]) for p in range(2)]
        for r in range(PEER_TOPK):
            for p in range(2):
                mx, idx, _, s[p] = _select_round(s[p], iota_k, float(N_KEYS))
                sv_ref[p, r:r + 1, :] = mx
                si_ref[p, r:r + 1, :] = idx

    def pair_topk(hd, sv_ref, si_ref):
        off = 0
        for i, cnt in STAIR:
            cand_ref[off:off + cnt, :] = sv_ref[0, i:i + 1, :] + sv_ref[1, 0:cnt, :]
            cidx_ref[off:off + cnt, :] = si_ref[0, i:i + 1, :] * N_KEYS + si_ref[1, 0:cnt, :]
            off += cnt
        cidx = cidx_ref[...]
        c = cand_ref[...]
        base = pl.multiple_of(hd * PEER_TOPK, PEER_TOPK)
        for r in range(PEER_TOPK):
            mx, _, hit, c = _select_round(c, iota_c, float(CAND_ROWS))
            ts_ref[r:r + 1, :] = mx
            expert = jnp.sum(jnp.where(hit, cidx, 0.0), axis=0, keepdims=True)
            eidx_s[pl.ds(base + r, 1), :] = expert.astype(I32)
        ts = ts_ref[...]
        ex = jnp.exp(ts - ts[0:1, :])
        gate_s[pl.ds(base, PEER_TOPK), :] = ex / jnp.sum(ex, axis=0, keepdims=True)

    sub_key_topk(0, sv_a, si_a)

    def two_heads(j, carry):
        hd = 2 * j
        sub_key_topk(hd + 1, sv_b, si_b)
        pair_topk(hd, sv_a, si_a)
        sub_key_topk(hd + 2, sv_a, si_a)
        pair_topk(hd + 1, sv_b, si_b)
        return carry

    lax.fori_loop(0, PEER_HEADS // 2 - 1, two_heads, 0)
    sub_key_topk(PEER_HEADS - 1, sv_b, si_b)
    pair_topk(PEER_HEADS - 2, sv_a, si_a)
    pair_topk(PEER_HEADS - 1, sv_b, si_b)
    eidx_ref[...] = eidx_s[...].T
    gate_ref[...] = gate_s[...].T


def _route(pq, keys):
    nhp, t, _ = pq.shape
    tm = ROUTE_TILE if t % ROUTE_TILE == 0 else LANES
    lists = pltpu.VMEM((2, PEER_TOPK, tm), F32)
    return pl.pallas_call(
        _route_kernel,
        out_shape=[jax.ShapeDtypeStruct((t, PEER_SLOTS), I32),
                   jax.ShapeDtypeStruct((t, PEER_SLOTS), F32)],
        grid=(t // tm,),
        in_specs=[pl.BlockSpec((nhp, tm, PEER_HALF), lambda i: (0, i, 0)),
                  pl.BlockSpec((nhp, N_KEYS, PEER_HALF), lambda i: (0, 0, 0))],
        out_specs=[pl.BlockSpec((tm, PEER_SLOTS), lambda i: (i, 0)),
                   pl.BlockSpec((tm, PEER_SLOTS), lambda i: (i, 0))],
        scratch_shapes=[lists, lists, lists, lists,
                        pltpu.VMEM((CAND_ROWS, tm), F32),
                        pltpu.VMEM((CAND_ROWS, tm), F32),
                        pltpu.VMEM((PEER_TOPK, tm), F32),
                        pltpu.VMEM((PEER_SLOTS, tm), I32), pltpu.VMEM((PEER_SLOTS, tm), F32)],
        compiler_params=_params(("arbitrary",)),
        name="peer_route",
    )(pq, keys)


def _fold_rows(a, b, keep_a, shift):
    return jnp.where(keep_a, a + pltpu.roll(a, SUBLANES - shift, 0), b + pltpu.roll(b, shift, 0))


def _peer_kernel(final, idx_ref, idxn_ref, gate_ref, h_ref, x_ref, g2_ref, fg_ref, tab_ref,
                 o_ref, buf_a, buf_b, act_ref, sem):
    i = pl.program_id(0)
    n = pl.num_programs(0)
    grp_tok = PEER_TOK // PEER_WAIT_GROUPS
    grp_rows = grp_tok * PEER_SLOTS
    sub = lax.broadcasted_iota(I32, (SUBLANES, LANES), 0)
    keep = {sh: (sub & sh) == 0 for sh in (4, 2, 1)}
    g2 = g2_ref[0, 0]
    gate_t = gate_ref[...].T
    hi_mask = jnp.uint32(0xFFFF0000)

    def row_copy(ids, t_src, t_dst, k, buf, s):
        return pltpu.make_async_copy(tab_ref.at[ids[t_src, k]], buf.at[t_dst * PEER_SLOTS + k],
                                     sem.at[s, t_dst // grp_tok])

    def issue_token(ids, t_src, t_dst, buf, s):
        for k in range(PEER_SLOTS):
            row_copy(ids, t_src, t_dst, k, buf, s).start(priority=k % 2)

    def wait_group(buf, s, q):
        pltpu.make_async_copy(tab_ref.at[pl.ds(0, grp_rows)],
                              buf.at[pl.ds(q * grp_rows, grp_rows)], sem.at[s, q]).wait()

    def token(buf, tl, tb):
        r0 = tl * PEER_SLOTS
        hv = h_ref[tb]
        groups = []
        for g in range(PEER_SLOTS // SUBLANES):
            p = [lax.bitcast_convert_type(buf[r0 + g * SUBLANES + j] & hi_mask, F32) * hv
                 for j in range(SUBLANES)]
            for sh in (4, 2, 1):
                half = len(p) // 2
                p = [_fold_rows(p[j], p[j + half], keep[sh], sh) for j in range(half)]
            groups.append(p[0])
        part = jnp.concatenate(groups, axis=0)
        a = jnp.sum(part, axis=1, keepdims=True)
        act = gate_t[:, tb:tb + 1] * jax.nn.gelu(a)
        act_ref[...] = jnp.broadcast_to(act, (PEER_SLOTS, LANES))
        accs = [jnp.zeros((SUBLANES, LANES), F32) for _ in range(4)]
        for k in range(PEER_SLOTS):
            up = lax.bitcast_convert_type(buf[r0 + k] << 16, F32)
            accs[k % 4] = accs[k % 4] + act_ref[k:k + 1, :] * up
        xr = x_ref[tb] + g2 * ((accs[0] + accs[1]) + (accs[2] + accs[3]))
        if final:
            ms = jnp.sum(jnp.sum(xr * xr, axis=1, keepdims=True), axis=0, keepdims=True)
            xr = xr * lax.rsqrt(ms / (SUBLANES * LANES) + EPS) * fg_ref[...]
        o_ref[tb] = xr

    @pl.when(i == 0)
    def _():
        def body(t, carry):
            issue_token(idx_ref, t, t, buf_a, 0)
            return carry
        lax.fori_loop(0, PEER_TOK, body, 0)

    for t in range(PEER_TOK):
        if t % grp_tok == 0:
            wait_group(buf_a, 0, t // grp_tok)
        issue_token(idx_ref, PEER_TOK + t, t, buf_b, 1)
        token(buf_a, t, t)
    for t in range(PEER_TOK):
        if t % grp_tok == 0:
            wait_group(buf_b, 1, t // grp_tok)
        issue_token(idxn_ref, t, t, buf_a, 0)
        token(buf_b, t, PEER_TOK + t)

    @pl.when(i == n - 1)
    def _():
        for q in range(PEER_WAIT_GROUPS):
            wait_group(buf_a, 0, q)


def _peer(n_tok, eidx, gate, h3, x3, g2, mod_row, seq_len, fg, table, final):
    t = n_tok
    tb = 2 * PEER_TOK
    nb = t // tb
    steps_per_seq = seq_len // tb
    rows = PEER_TOK * PEER_SLOTS
    tile = (SUBLANES, LANES)
    return pl.pallas_call(
        functools.partial(_peer_kernel, final),
        out_shape=jax.ShapeDtypeStruct((t,) + tile, F32),
        grid=(nb,),
        in_specs=[pl.BlockSpec((tb, PEER_SLOTS), lambda i: (i, 0), memory_space=pltpu.SMEM),
                  pl.BlockSpec((tb, PEER_SLOTS), lambda i: (jnp.minimum(i + 1, nb - 1), 0),
                               memory_space=pltpu.SMEM),
                  pl.BlockSpec((tb, PEER_SLOTS), lambda i: (i, 0)),
                  pl.BlockSpec((tb,) + tile, lambda i: (i, 0, 0)),
                  pl.BlockSpec((tb,) + tile, lambda i: (i, 0, 0)),
                  pl.BlockSpec((1, 1) + tile, lambda i: (mod_row(i, steps_per_seq), 0, 0, 0)),
                  pl.BlockSpec(tile, lambda i: (0, 0)),
                  pl.BlockSpec(memory_space=pl.ANY)],
        out_specs=pl.BlockSpec((tb,) + tile, lambda i: (i, 0, 0)),
        scratch_shapes=[pltpu.VMEM((rows,) + tile, jnp.uint32),
                        pltpu.VMEM((rows,) + tile, jnp.uint32),
                        pltpu.VMEM((PEER_SLOTS, LANES), F32),
                        pltpu.SemaphoreType.DMA((2, PEER_WAIT_GROUPS))],
        compiler_params=pltpu.CompilerParams(dimension_semantics=("arbitrary",),
                                             vmem_limit_bytes=PEER_VMEM_LIMIT),
        name="peer_experts",
    )(eidx, eidx, gate, h3, x3, g2, fg, table)


def _peer_sc_body(first, n_tok, d, eidx_hbm, gate_hbm, h_hbm, tab_hbm, y_hbm, idx_v, idx_n, gate_v, gate_n,
                  h_v, h_n, rows_a, rows_b, out_v, sem_a, sem_b, sem_n):
    wid = lax.axis_index("s") * SC_CORES + lax.axis_index("c")
    per_worker = n_tok // SC_WORKERS
    tok0 = first + wid * per_worker
    nj = d // SC_LANES
    lane = lax.iota(I32, SC_LANES)
    zero = jnp.zeros((SC_LANES,), F32)
    hi_mask = jnp.full((SC_LANES,), 0xFFFF0000, jnp.uint32)

    def gather(idx_ref, c, rows, sem):
        row0 = pl.multiple_of(c * SC_CHUNK, SC_CHUNK)
        return pltpu.make_async_copy(tab_hbm.at[idx_ref.at[pl.ds(row0, SC_CHUNK)]], rows, sem)

    def copy_words(src, dst, n):
        def step(j, carry):
            off = pl.multiple_of(j * SC_LANES, SC_LANES)
            dst[pl.ds(off, SC_LANES)] = src[pl.ds(off, SC_LANES)]
            return carry
        lax.fori_loop(0, n // SC_LANES, step, 0)

    def compute(c, rows):
        row0 = pl.multiple_of(c * SC_CHUNK, SC_CHUNK)

        def down_step(j, accs):
            off = pl.multiple_of(j * SC_LANES, SC_LANES)
            hj = h_v[pl.ds(off, SC_LANES)]
            out = []
            for r in range(SC_CHUNK):
                w = rows[r, pl.ds(off, SC_LANES)]
                dn = lax.bitcast_convert_type(w & hi_mask, F32)
                out.append(accs[r] + dn * hj)
            return tuple(out)

        accs = lax.fori_loop(0, nj, down_step, tuple(zero for _ in range(SC_CHUNK)))
        acts = []
        for g in range(SC_CHUNK // SC_LANES):
            a = zero
            for r in range(SC_LANES):
                a = jnp.where(lane == r, jnp.sum(accs[g * SC_LANES + r]), a)
            gt = gate_v[pl.ds(row0 + g * SC_LANES, SC_LANES)]
            u = GELU_C * (a + 0.044715 * (a * a * a))
            th = 1.0 - 2.0 / (jnp.exp(2.0 * u) + 1.0)
            act = gt * (0.5 * a * (1.0 + th))
            for r in range(SC_LANES):
                acts.append(jnp.sum(jnp.where(lane == r, act, 0.0)))

        def up_step(j, carry):
            off = pl.multiple_of(j * SC_LANES, SC_LANES)
            o = out_v[pl.ds(off, SC_LANES)]
            for r in range(SC_CHUNK):
                w = rows[r, pl.ds(off, SC_LANES)]
                up = lax.bitcast_convert_type(w << 16, F32)
                o = o + acts[r] * up
            out_v[pl.ds(off, SC_LANES)] = o
            return carry

        lax.fori_loop(0, nj, up_step, 0)

    def token(ti, carry):
        t = tok0 + ti
        tn = jnp.minimum(t + 1, tok0 + per_worker - 1)

        def clear(j, carry2):
            out_v[pl.ds(pl.multiple_of(j * SC_LANES, SC_LANES), SC_LANES)] = zero
            return carry2

        lax.fori_loop(0, nj, clear, 0)

        def pair(pp, carry2):
            gather(idx_v, 2 * pp + 1, rows_b, sem_b).start()
            gather(idx_v, 2 * pp, rows_a, sem_a).wait()
            compute(2 * pp, rows_a)

            @pl.when(pp == 0)
            def _():
                gather(idx_v, 2, rows_a, sem_a).start()

            @pl.when(pp == 1)
            def _():
                pltpu.sync_copy(eidx_hbm.at[tn], idx_n)
                gather(idx_n, 0, rows_a, sem_a).start()
                pltpu.make_async_copy(gate_hbm.at[tn], gate_n, sem_n).start()
                pltpu.make_async_copy(h_hbm.at[tn], h_n, sem_n).start()

            gather(idx_v, 2 * pp + 1, rows_b, sem_b).wait()
            compute(2 * pp + 1, rows_b)
            return carry2

        lax.fori_loop(0, PEER_SLOTS // (2 * SC_CHUNK), pair, 0)
        pltpu.sync_copy(out_v, y_hbm.at[t - first])
        pltpu.make_async_copy(gate_hbm.at[tn], gate_n, sem_n).wait()
        pltpu.make_async_copy(h_hbm.at[tn], h_n, sem_n).wait()
        copy_words(idx_n, idx_v, PEER_SLOTS)
        copy_words(gate_n, gate_v, PEER_SLOTS)
        copy_words(h_n, h_v, d)
        return carry

    pltpu.sync_copy(eidx_hbm.at[tok0], idx_v)
    pltpu.sync_copy(gate_hbm.at[tok0], gate_v)
    pltpu.sync_copy(h_hbm.at[tok0], h_v)
    gather(idx_v, 0, rows_a, sem_a).start()
    lax.fori_loop(0, per_worker, token, 0)
    gather(idx_v, 0, rows_a, sem_a).wait()


def _peer_sc(first, n_tok, eidx, gate, h2, sc_table):
    d = h2.shape[1]
    mesh = plsc.VectorSubcoreMesh(core_axis_name="c", subcore_axis_name="s",
                                  num_cores=SC_CORES, num_subcores=SC_SUBCORES)
    return pl.kernel(
        functools.partial(_peer_sc_body, first, n_tok, d),
        out_type=jax.ShapeDtypeStruct((n_tok, d), F32),
        mesh=mesh,
        scratch_types=[pltpu.VMEM((PEER_SLOTS,), I32), pltpu.VMEM((PEER_SLOTS,), I32),
                       pltpu.VMEM((PEER_SLOTS,), F32), pltpu.VMEM((PEER_SLOTS,), F32),
                       pltpu.VMEM((d,), F32), pltpu.VMEM((d,), F32),
                       pltpu.VMEM((SC_CHUNK, d), jnp.uint32), pltpu.VMEM((SC_CHUNK, d), jnp.uint32),
                       pltpu.VMEM((d,), F32), pltpu.SemaphoreType.DMA, pltpu.SemaphoreType.DMA,
                       pltpu.SemaphoreType.DMA],
        compiler_params=pltpu.CompilerParams(needs_layout_passes=False),
        cost_estimate=pl.CostEstimate(
            flops=4 * n_tok * PEER_SLOTS * d, transcendentals=n_tok * PEER_SLOTS,
            bytes_accessed=n_tok * (PEER_SLOTS * d * 4 + 2 * d * 4 + 2 * PEER_SLOTS * 4)),
        name="peer_experts_sc",
    )(eidx, gate, h2, sc_table)


def _finish_kernel(final, d, y_ref, x_ref, m_ref, fg_ref, o_ref):
    xr = x_ref[...] + m_ref[0][:, 5 * d:6 * d] * y_ref[...]
    if final:
        xr = _rms(xr) * fg_ref[...]
    o_ref[...] = xr


def _peer_finish(y, x1, mod, mod_row, seq_len, tok0, fg, final):
    t, d = y.shape
    tm = min(TOK_TILE, seq_len)
    tiles_per_seq = seq_len // tm
    tile0 = tok0 // tm
    return pl.pallas_call(
        functools.partial(_finish_kernel, final, d),
        out_shape=jax.ShapeDtypeStruct((t, d), F32),
        grid=(t // tm,),
        in_specs=[pl.BlockSpec((tm, d), lambda i: (i, 0)),
                  pl.BlockSpec((tm, d), lambda i: (i + tile0, 0)),
                  pl.BlockSpec((1, 1, mod.shape[-1]),
                               lambda i: (mod_row(i + tile0, tiles_per_seq), 0, 0)),
                  pl.BlockSpec((1, d), lambda i: (0, 0))],
        out_specs=pl.BlockSpec((tm, d), lambda i: (i, 0)),
        compiler_params=_params(("arbitrary",)),
        name="peer_finish",
    )(y, x1, mod, fg.reshape(1, d))


def _rope_perm(w):
    q = QK_ROPE // 4
    a1, a2, b1, b2 = (w[..., j * q:(j + 1) * q] for j in range(4))
    return jnp.concatenate([-a2, a1, -b2, b1], axis=-1)


def _prep_layer(l, d, w_in, b_gate, q_norm_g, w_uq, kv_norm_g, w_ukv, w_oa, w_ob, w_grp,
                pool_scale, w_oc, w_out, w_pq, norm1_g, norm2_g):
    wi = w_in[l]
    s0 = Q_LORA
    s1 = s0 + KV_LORA
    s2 = s1 + QK_ROPE
    s3 = s2 + FOURIER_WIDTH
    s4 = s3 + POOL_WIDTH
    w_kr = wi[:, s1:s2]
    zl = jnp.zeros((d, QK_NOPE), F32)
    zr = jnp.zeros((d, HEAD_PAD - QK_NOPE - QK_ROPE), F32)
    wall = jnp.concatenate([wi[:, 0:s1], wi[:, s2:], zl, w_kr, zr, zl, _rope_perm(w_kr), zr],
                           axis=1).astype(BF16)
    wq = w_uq[l].reshape(Q_LORA, N_HEADS, QK_NOPE + QK_ROPE)
    qpad = jnp.zeros((Q_LORA, N_HEADS, HEAD_PAD - QK_NOPE - QK_ROPE), F32)
    wq_full = jnp.concatenate([wq, qpad], axis=-1).reshape(Q_LORA, N_HEADS * HEAD_PAD)
    wq_perm = jnp.concatenate([jnp.zeros((Q_LORA, N_HEADS, QK_NOPE), F32),
                               _rope_perm(wq[..., QK_NOPE:]), qpad],
                              axis=-1).reshape(Q_LORA, N_HEADS * HEAD_PAD)
    wkv = w_ukv[l].reshape(KV_LORA, N_HEADS, QK_NOPE + V_HEAD)
    wk = jnp.concatenate([wkv[..., :QK_NOPE],
                          jnp.zeros((KV_LORA, N_HEADS, HEAD_PAD - QK_NOPE), F32)],
                         axis=-1).reshape(KV_LORA, N_HEADS * HEAD_PAD)
    wv = wkv[..., QK_NOPE:].reshape(KV_LORA, ATT_WIDTH)
    cidx = jnp.arange(FOURIER_WIDTH, dtype=I32)
    ang = (2.0 * math.pi / FOURIER_WIDTH) * ((cidx[:, None] * cidx[None, :]) % FOURIER_WIDTH
                                             ).astype(F32)
    fc = jnp.concatenate([jnp.cos(ang), jnp.sin(ang)], axis=1).astype(BF16)
    ng = len(POOL_WINDOWS)
    wg = jnp.zeros((ng, POOL_GROUP, ng, POOL_GROUP), F32)
    for gi in range(ng):
        wg = wg.at[gi, :, gi, :].set(w_grp[l, gi])
    return {
        "wall": wall, "n1": norm1_g[l][None, :], "n2": norm2_g[l][None, :],
        "qg": q_norm_g[l][None, :], "kvg": kv_norm_g[l][None, :],
        "wq": wq_full.astype(BF16), "wqp": wq_perm.astype(BF16),
        "wk": wk.astype(BF16), "wv": wv.astype(BF16), "fc": fc,
        "bg": b_gate[l][None, :],
        "wg": wg.reshape(POOL_WIDTH, POOL_WIDTH).astype(BF16),
        "ps": pool_scale[l][None, :],
        "woa": w_oa[l].astype(BF16), "wob": w_ob[l].astype(BF16), "woc": w_oc[l].astype(BF16),
        "wout": w_out[l].astype(BF16), "wpq": w_pq[l].astype(BF16),
    }


def _bf16_bits(w):
    return lax.bitcast_convert_type(w.astype(BF16), jnp.uint16).astype(jnp.uint32)


def _rope_tables(seq_len, rope):
    zeros_n = jnp.zeros((seq_len, QK_NOPE), F32)
    zeros_p = jnp.zeros((seq_len, HEAD_PAD - QK_NOPE - QK_ROPE), F32)
    ones_n = jnp.ones((seq_len, QK_NOPE), F32)
    if rope:
        pos = jnp.arange(seq_len, dtype=I32)
        half = QK_ROPE // 2
        inv_freq = ROPE_BASE ** (-jnp.arange(0, half, 2, dtype=F32) / half)
        ang_r = (pos // GRID_W).astype(F32)[:, None] * inv_freq
        ang_c = (pos % GRID_W).astype(F32)[:, None] * inv_freq
        cos = jnp.concatenate([jnp.cos(ang_r)] * 2 + [jnp.cos(ang_c)] * 2, axis=1)
        sin = jnp.concatenate([jnp.sin(ang_r)] * 2 + [jnp.sin(ang_c)] * 2, axis=1)
    else:
        cos = jnp.ones((seq_len, QK_ROPE), F32)
        sin = jnp.zeros((seq_len, QK_ROPE), F32)
    cq = jnp.concatenate([ones_n, cos, zeros_p], axis=1) * ATT_SCALE
    sq = jnp.concatenate([zeros_n, sin, zeros_p], axis=1) * ATT_SCALE
    ck = jnp.concatenate([zeros_n, cos, zeros_p], axis=1)
    sk = jnp.concatenate([zeros_n, sin, zeros_p], axis=1)
    return cq, sq, ck, sk


def _dft(seq_len):
    n = seq_len
    a = DFT_SPLIT if n % DFT_SPLIT == 0 else 1
    b = n // a
    k = jnp.arange(n, dtype=I32)[None, :]
    ang_a = (2.0 * math.pi / a) * ((jnp.arange(a, dtype=I32)[:, None] * k) % a).astype(F32)
    ang_b = (2.0 * math.pi / n) * ((jnp.arange(b, dtype=I32)[:, None] * k) % n).astype(F32)
    ca, sa = jnp.cos(ang_a)[:, None, :], jnp.sin(ang_a)[:, None, :]
    cb, sb = jnp.cos(ang_b)[None, :, :], jnp.sin(ang_b)[None, :, :]
    cos = (ca * cb - sa * sb).reshape(n, n)
    sin = (sa * cb + ca * sb).reshape(n, n)
    return cos.astype(BF16), sin.astype(BF16)


def _peer_block(x1, h2, pq, keys, mod, mod_row, seq_len, fg, table, sc_table, n_sc, final):
    t, d = x1.shape
    tile = (SUBLANES, LANES)
    eidx, gate = _route(pq, keys)
    n_tc = t - n_sc
    mod_tiles = mod.reshape(MOD_ROWS, 6, SUBLANES, LANES)[:, 5:6]
    out = _peer(n_tc, eidx, gate, h2.reshape((t,) + tile), x1.reshape((t,) + tile), mod_tiles,
                mod_row, seq_len, fg.reshape(tile), table, final).reshape(n_tc, d)
    if n_sc == 0:
        return out, ()
    y = _peer_sc(n_tc, n_sc, eidx, gate, h2, sc_table)
    out_sc = _peer_finish(y, x1, mod, mod_row, seq_len, n_tc, fg, final)
    return jnp.concatenate([out, out_sc], axis=0), (out, eidx, gate, h2)


def kernel(x, c, ctx, c_ctx, w_mod, b_mod, norm1_g, norm2_g, w_in, b_gate, q_norm_g, w_uq,
           kv_norm_g, w_ukv, w_oa, w_ob, w_grp, pool_scale, w_oc, w_out, w_pq, peer_keys,
           peer_down, peer_up, final_g):
    batch, seq_len, d = x.shape
    ctx_len = ctx.shape[1]
    depth = w_mod.shape[0]
    assert d == SUBLANES * LANES and batch + 1 <= MOD_ROWS
    assert seq_len % TOK_TILE == 0 and ctx_len % (2 * PEER_TOK) == 0 and ctx_len % LANES == 0
    tile = (SUBLANES, LANES)

    cvec = jnp.concatenate([c, c_ctx[None, :], jnp.zeros((MOD_ROWS - batch - 1, d), F32)], axis=0)
    mod_all = _modulation(cvec, w_mod, b_mod)

    def x_row(b0):
        return lambda i, per_seq: b0 + i // per_seq

    def c_row(i, per_seq):
        return batch

    tabs_x = _rope_tables(seq_len, True)
    tabs_c = _rope_tables(ctx_len, False)
    dft_x = _dft(seq_len)
    dft_c = _dft(ctx_len)

    n_chains = PEER_CHAINS if batch % PEER_CHAINS == 0 else 1
    bpc = batch // n_chains
    use_sc = bpc * seq_len >= TOK_TILE + max(PEER_SC_TOKENS, PEER_SC_TOKENS_FIRST,
                                             PEER_SC_TOKENS_MIDDLE, PEER_SC_TOKENS_LAYER_END)

    def sc_tokens(l, j):
        if not use_sc:
            return 0
        if j == n_chains - 1:
            return PEER_SC_TOKENS if l == depth - 1 else PEER_SC_TOKENS_LAYER_END
        return PEER_SC_TOKENS_FIRST if (l, j) == (0, 0) else PEER_SC_TOKENS_MIDDLE

    sc_tables = [(_bf16_bits(peer_down[l]) << 16) | _bf16_bits(peer_up[l]) for l in range(depth)]
    x, ctx, sc_tables = lax.optimization_barrier((x, ctx, sc_tables))

    chains = [x[j * bpc:(j + 1) * bpc].reshape(bpc * seq_len, d) for j in range(n_chains)]
    cs = ctx.reshape(batch * ctx_len, d)
    sc_args = ()
    for l in range(depth):
        last = l == depth - 1
        lw = _prep_layer(l, d, w_in, b_gate, q_norm_g, w_uq, kv_norm_g, w_ukv, w_oa, w_ob, w_grp,
                         pool_scale, w_oc, w_out, w_pq, norm1_g, norm2_g)
        mod = mod_all[l].reshape(MOD_ROWS, 1, 6 * d)
        keys = peer_keys[l].reshape(2 * PEER_HEADS, N_KEYS, PEER_HALF).astype(BF16)
        sc_table = sc_tables[l]
        table = sc_table.reshape((-1,) + tile)

        if last:
            kc, vc = _in_proj(cs, mod, c_row, ctx_len, lw, tabs_c, False)
        else:
            kc, vc, qc, abc, zpc, gc = _in_proj(cs, mod, c_row, ctx_len, lw, tabs_c, True)

        for j in range(n_chains):
            row = x_row(j * bpc)
            sc_args, xs = lax.optimization_barrier((sc_args, chains[j]))
            ctx_rows = slice(j * bpc * ctx_len, (j + 1) * bpc * ctx_len)
            kx, vx, qx, abx, zpx, gx = _in_proj(xs, mod, row, seq_len, lw, tabs_x, True)
            att_x = _attention(qx, kx, vx, bpc, seq_len, (kc[ctx_rows], vc[ctx_rows]))
            four_x = _fourier(abx, bpc, seq_len, dft_x)
            pool_x = _pool(zpx, bpc, seq_len, lw["wg"], lw["ps"])
            x1, h2, pq = _out_proj(xs, att_x, four_x, pool_x, gx, mod, row, seq_len, lw)
            chains[j], sc_args = _peer_block(x1, h2, pq, keys, mod, row, seq_len, final_g, table,
                                             sc_table, sc_tokens(l, j), last)

        if not last:
            att_c = _attention(qc, kc, vc, batch, ctx_len, None)
            four_c = _fourier(abc, batch, ctx_len, dft_c)
            pool_c = _pool(zpc, batch, ctx_len, lw["wg"], lw["ps"])
            c1, hc2, pqc = _out_proj(cs, att_c, four_c, pool_c, gc, mod, c_row, ctx_len, lw)
            cs, _ = _peer_block(c1, hc2, pqc, keys, mod, c_row, ctx_len, final_g, table, None, 0,
                                False)
    xs = jnp.concatenate(chains, axis=0)
    return xs.reshape(batch, seq_len, d)
```
